```python
import math
import jax
import jax.numpy as jnp
from jax import lax
import numpy as np


D_MODEL = 2048
BATCH = 2
SEQ = 8192
DEPTH = 4

CTX_LEN = 256
GRID_W = 64
HEAD_DIM = 64
ROPE_BASE = 10000.0
EPS = 1e-6
NEG_INF = -1e30

A_HEADS = D_MODEL // 512
A_WIDTH = A_HEADS * 2 * HEAD_DIM
B_HEADS = D_MODEL // 256
B_KV_HEADS = B_HEADS // 4
B_WIDTH = B_HEADS * HEAD_DIM
B_KV_WIDTH = B_KV_HEADS * HEAD_DIM
WINDOW = 128
BLOCK = 128
POOL_WINDOWS = (2, 4, 8, 16)
C_WIDTH = D_MODEL // 4
C_GROUP = C_WIDTH // len(POOL_WINDOWS)
D_WIDTH = D_MODEL - A_WIDTH - B_WIDTH - C_WIDTH
HY_EMB = 33
HY_HIDDEN = 64
HY_FAST_DECAY = 0.3
HY_SLOW_DECAY = 1.5
HY_TARGET = 1e-2

MIX_WIDTH = A_WIDTH + B_WIDTH + C_WIDTH + D_WIDTH
FFN_HIDDEN = ((8 * D_MODEL // 3 + 255) // 256) * 256

OFF_QA = 0
OFF_KA = OFF_QA + A_WIDTH
OFF_VA = OFF_KA + A_WIDTH
OFF_QB = OFF_VA + A_WIDTH
OFF_KB = OFF_QB + B_WIDTH
OFF_VB = OFF_KB + B_KV_WIDTH
OFF_POOL = OFF_VB + B_KV_WIDTH
OFF_HY = OFF_POOL + C_WIDTH
IN_COLS = OFF_HY + 3 * D_WIDTH

kernel_name = 'hybrid_parallel_mixer_dit_trunk'


def rms_norm(x, g):
    xf = x.astype(jnp.float32)
    y = xf * lax.rsqrt(jnp.mean(xf * xf, axis=-1, keepdims=True) + EPS)
    return (y * g.astype(jnp.float32)).astype(x.dtype)


def axial_rope_tables(L):
    rows = L // GRID_W
    row = jnp.repeat(jnp.arange(rows, dtype=jnp.float32), GRID_W)
    col = jnp.tile(jnp.arange(GRID_W, dtype=jnp.float32), rows)
    half = HEAD_DIM // 2
    inv = ROPE_BASE ** (-jnp.arange(0, half, 2, dtype=jnp.float32) / half)
    ang = jnp.concatenate([row[:, None] * inv, col[:, None] * inv], axis=-1)
    return jnp.cos(ang), jnp.sin(ang)


def apply_axial_rope(x, cos, sin):
    shp = x.shape
    nf = HEAD_DIM // 4
    xr = x.reshape(shp[0], shp[1], -1, 2, 2, nf).astype(jnp.float32)
    x1, x2 = xr[..., 0, :], xr[..., 1, :]
    c = cos.reshape(shp[1], 1, 2, nf)
    s = sin.reshape(shp[1], 1, 2, nf)
    out = jnp.stack([x1 * c - x2 * s, x2 * c + x1 * s], axis=-2)
    return out.reshape(shp).astype(x.dtype)


def dwconv3(u, w, b):
    up = jnp.pad(u, ((0, 0), (1, 1), (0, 0)))
    return up[:, :-2] * w[0] + up[:, 1:-1] * w[1] + up[:, 2:] * w[2] + b


def band_blocks(t, nb):
    L = t.shape[1]
    tp = jnp.pad(t, [(0, 0), (BLOCK, BLOCK)] + [(0, 0)] * (t.ndim - 2))
    parts = [tp[:, i * BLOCK: i * BLOCK + L].reshape((t.shape[0], nb, BLOCK) + t.shape[2:]) for i in range(3)]
    return jnp.concatenate(parts, axis=2)


def diff_attention(qa, ka, va, qa_c, ka_c, va_c, lam, sub_g, sub_scale):
    B, L = qa.shape[:2]
    nb = L // BLOCK
    k_all = jnp.concatenate([ka_c, ka], axis=1)
    v_all = jnp.concatenate([va_c, va], axis=1)

    def attend(q, k, v):
        s = jnp.einsum('bqhid,bkhid->bhiqk', q, k).astype(jnp.float32) * (HEAD_DIM ** -0.5)
        p = jax.nn.softmax(s, axis=-1)
        a = p[:, :, 0] - lam * p[:, :, 1]
        o = jnp.einsum('bhqk,bkhe->bqhe', a.astype(v.dtype), v)
        return rms_norm(o, sub_g) * sub_scale

    q_blocks = jnp.moveaxis(qa.reshape((B, nb, BLOCK) + qa.shape[2:]), 1, 0)
    o = lax.map(lambda qb: attend(qb, k_all, v_all), q_blocks)
    o = jnp.moveaxis(o, 0, 1).reshape(B, L, A_WIDTH)
    o_c = None
    if qa_c is not None:
        o_c = attend(qa_c, ka_c, va_c).reshape(B, qa_c.shape[1], A_WIDTH)
    return o, o_c


def window_attention(qb, kb, vb, qb_c, kb_c, vb_c, sink):
    B, L = qb.shape[:2]
    C = kb_c.shape[1]
    nb = L // BLOCK
    G = B_HEADS // B_KV_HEADS
    scale = HEAD_DIM ** -0.5
    sink_f = sink.astype(jnp.float32).reshape(B_KV_HEADS, G)

    def with_sink(s):
        sb = jnp.broadcast_to(sink_f[:, :, None, None], s.shape[:-1] + (1,))
        return sb

    qblk = qb.reshape(B, nb, BLOCK, B_KV_HEADS, G, HEAD_DIM)
    kband = band_blocks(kb, nb)
    vband = band_blocks(vb, nb)
    qpos = jnp.arange(L).reshape(nb, BLOCK)
    kpos = jnp.arange(nb)[:, None] * BLOCK - BLOCK + jnp.arange(3 * BLOCK)[None, :]
    valid = (kpos[:, None, :] >= 0) & (kpos[:, None, :] < L) & (jnp.abs(kpos[:, None, :] - qpos[:, :, None]) <= WINDOW)
    s_loc = jnp.einsum('bnqkgd,bnjkd->bnkgqj', qblk, kband).astype(jnp.float32) * scale
    s_loc = jnp.where(valid[None, :, None, None], s_loc, NEG_INF)
    s_ctx = jnp.einsum('bnqkgd,bjkd->bnkgqj', qblk, kb_c).astype(jnp.float32) * scale
    p = jax.nn.softmax(jnp.concatenate([with_sink(s_loc), s_ctx, s_loc], axis=-1), axis=-1)
    o = (jnp.einsum('bnkgqj,bjkd->bnqkgd', p[..., 1:1 + C].astype(vb.dtype), vb_c)
         + jnp.einsum('bnkgqj,bnjkd->bnqkgd', p[..., 1 + C:].astype(vb.dtype), vband))
    o = o.reshape(B, L, B_WIDTH)
    o_c = None
    if qb_c is not None:
        qc = qb_c.reshape(B, C, B_KV_HEADS, G, HEAD_DIM)
        s_c = jnp.einsum('bqkgd,bjkd->bkgqj', qc, kb_c).astype(jnp.float32) * scale
        p_c = jax.nn.softmax(jnp.concatenate([with_sink(s_c), s_c], axis=-1), axis=-1)
        o_c = jnp.einsum('bkgqj,bjkd->bqkgd', p_c[..., 1:].astype(vb_c.dtype), vb_c).reshape(B, C, B_WIDTH)
    return o, o_c


def pool_mixer(u, w_lin, ls):
    B, L, _ = u.shape
    cs = jnp.concatenate([jnp.zeros_like(u[:, :1], dtype=jnp.float32),
                          jnp.cumsum(u.astype(jnp.float32), axis=1)], axis=1)
    t = jnp.arange(L)
    outs = []
    for g, w in enumerate(POOL_WINDOWS):
        lo = w // 2
        hi = w - 1 - lo
        start = jnp.clip(t - lo, 0, L)
        end = jnp.clip(t + hi + 1, 0, L)
        csg = cs[..., g * C_GROUP:(g + 1) * C_GROUP]
        mean = (csg[:, end] - csg[:, start]) / (end - start).astype(jnp.float32)[None, :, None]
        outs.append(mean.astype(u.dtype) - u[..., g * C_GROUP:(g + 1) * C_GROUP])
    d = jnp.stack(outs, axis=2)
    y = jnp.einsum('blgc,gcd->blgd', d, w_lin).reshape(B, L, C_WIDTH)
    return y * ls


def hyena_kernel(L, w1, b1, w2, b2, w3, freq):
    t01 = jnp.linspace(0.0, 1.0, L, dtype=jnp.float32)[:, None]
    bands = (HY_EMB - 1) // 2
    w_ang = 2.0 * math.pi * jnp.arange(L, dtype=jnp.float32)[:, None] / L
    f = jnp.linspace(1e-4, bands - 1, bands, dtype=jnp.float32)[None, :]
    z = jnp.concatenate([t01, jnp.cos(f * w_ang), -jnp.sin(f * w_ang)], axis=-1)
    h = jnp.sin(freq * (z @ w1 + b1))
    h = jnp.sin(freq * (h @ w2 + b2))
    h = (h @ w3).astype(jnp.float32).reshape(L, 2, D_WIDTH)
    deltas = jnp.linspace(math.log(HY_TARGET) / HY_FAST_DECAY, math.log(HY_TARGET) / HY_SLOW_DECAY, D_WIDTH, dtype=jnp.float32)
    h = h * jnp.exp(-t01 * jnp.abs(deltas)[None, :])[:, None, :]
    kern = jnp.concatenate([h[:, 0], jnp.zeros((1, D_WIDTH), jnp.float32), h[:0:-1, 1]], axis=0)
    return kern * lax.rsqrt(jnp.sum(kern * kern, axis=0, keepdims=True) + EPS)


def long_conv(u, kern):
    L = u.shape[1]
    uf = jnp.fft.rfft(u.astype(jnp.float32), n=2 * L, axis=1)
    kf = jnp.fft.rfft(kern, n=2 * L, axis=0)
    return jnp.fft.irfft(uf * kf[None], n=2 * L, axis=1)[:, :L].astype(u.dtype)


def hyena_mixer(u, conv_w, conv_b, kern, bias):
    u = dwconv3(u, conv_w, conv_b)
    v, x1, x2 = jnp.split(u, 3, axis=-1)
    z = v * x1
    z = long_conv(z, kern) + z * bias
    return x2 * z


def conv_ffn(h, w_up, conv_w, conv_b, w_down):
    u = dwconv3(h @ w_up, conv_w, conv_b)
    gate, up = jnp.split(u, 2, axis=-1)
    return (jax.nn.silu(gate) * up) @ w_down


def setup_inputs(seed: int = 0) -> dict:
    key = jax.random.key(seed)
    ks = jax.random.split(key, 29)

    def nrm(k, shape, scale):
        return jax.random.normal(k, shape, jnp.float32) * scale

    return {
        'x': nrm(ks[0], (BATCH, SEQ, D_MODEL), 1.0),
        'c': nrm(ks[1], (BATCH, D_MODEL), 1.0),
        'ctx': nrm(ks[2], (BATCH, CTX_LEN, D_MODEL), 1.0),
        'c_ctx': nrm(ks[3], (D_MODEL,), 1.0),
        'w_mod': nrm(ks[4], (DEPTH, D_MODEL, 6 * D_MODEL), 0.5 * D_MODEL ** -0.5),
        'b_mod': nrm(ks[5], (DEPTH, 6 * D_MODEL), 0.01),
        'norm1_g': 1.0 + nrm(ks[6], (DEPTH, D_MODEL), 0.05),
        'norm2_g': 1.0 + nrm(ks[7], (DEPTH, D_MODEL), 0.05),
        'w_in': nrm(ks[8], (DEPTH, D_MODEL, IN_COLS), D_MODEL ** -0.5),
        'w_out': nrm(ks[9], (DEPTH, MIX_WIDTH, D_MODEL), MIX_WIDTH ** -0.5),
        'qk_gain': 1.0 + nrm(ks[10], (DEPTH, 4, HEAD_DIM), 0.05),
        'diff_lam': nrm(ks[11], (DEPTH, 4, HEAD_DIM), 0.1),
        'diff_subln': 1.0 + nrm(ks[12], (DEPTH, 2 * HEAD_DIM), 0.05),
        'win_sink': nrm(ks[13], (DEPTH, B_HEADS), 1.0),
        'pool_w': nrm(ks[14], (DEPTH, len(POOL_WINDOWS), C_GROUP, C_GROUP), C_GROUP ** -0.5),
        'pool_scale': 1.0 + nrm(ks[15], (DEPTH, C_WIDTH), 0.1),
        'hy_conv_w': nrm(ks[16], (DEPTH, 3, 3 * D_WIDTH), 3 ** -0.5),
        'hy_conv_b': nrm(ks[17], (DEPTH, 3 * D_WIDTH), 0.01),
        'hy_w1': nrm(ks[18], (DEPTH, HY_EMB, HY_HIDDEN), HY_EMB ** -0.5),
        'hy_b1': nrm(ks[19], (DEPTH, HY_HIDDEN), 0.1),
        'hy_w2': nrm(ks[20], (DEPTH, HY_HIDDEN, HY_HIDDEN), HY_HIDDEN ** -0.5),
        'hy_b2': nrm(ks[21], (DEPTH, HY_HIDDEN), 0.1),
        'hy_w3': nrm(ks[22], (DEPTH, HY_HIDDEN, 2 * D_WIDTH), HY_HIDDEN ** -0.5),
        'hy_freq': 1.0 + nrm(ks[23], (DEPTH, HY_HIDDEN), 0.1),
        'hy_bias': nrm(ks[24], (DEPTH, D_WIDTH), 0.5),
        'ffn_w_in': nrm(ks[25], (DEPTH, D_MODEL, 2 * FFN_HIDDEN), D_MODEL ** -0.5),
        'ffn_conv_w': nrm(ks[26], (DEPTH, 3, 2 * FFN_HIDDEN), 3 ** -0.5),
        'ffn_conv_b': nrm(ks[27], (DEPTH, 2 * FFN_HIDDEN), 0.01),
        'ffn_w_out': nrm(ks[28], (DEPTH, FFN_HIDDEN, D_MODEL), FFN_HIDDEN ** -0.5),
    }


def reference(x, c, ctx, c_ctx, w_mod, b_mod, norm1_g, norm2_g, w_in, w_out, qk_gain, diff_lam, diff_subln,
              win_sink, pool_w, pool_scale, hy_conv_w, hy_conv_b, hy_w1, hy_b1, hy_w2, hy_b2, hy_w3, hy_freq,
              hy_bias, ffn_w_in, ffn_conv_w, ffn_conv_b, ffn_w_out):
    B, L, _ = x.shape
    C = ctx.shape[1]
    cos, sin = axial_rope_tables(L)
    for l in range(DEPTH):
        last = l == DEPTH - 1
        wl = w_in[l]
        mod = jax.nn.silu(c) @ w_mod[l] + b_mod[l]
        mod_c = jax.nn.silu(c_ctx) @ w_mod[l] + b_mod[l]
        sh1, sc1, g1, sh2, sc2, g2 = jnp.split(mod[:, None, :], 6, axis=-1)
        csh1, csc1, cg1, csh2, csc2, cg2 = jnp.split(mod_c, 6, axis=-1)

        h = rms_norm(x, norm1_g[l]) * (1.0 + sc1) + sh1
        hc = rms_norm(ctx, norm1_g[l]) * (1.0 + csc1) + csh1
        p = h @ wl
        if last:
            pkv_a = hc @ wl[:, OFF_KA:OFF_QB]
            pkv_b = hc @ wl[:, OFF_KB:OFF_POOL]
            ka_c_raw, va_c_raw = pkv_a[..., :A_WIDTH], pkv_a[..., A_WIDTH:]
            kb_c_raw, vb_c_raw = pkv_b[..., :B_KV_WIDTH], pkv_b[..., B_KV_WIDTH:]
            pc = None
        else:
            pc = hc @ wl
            ka_c_raw, va_c_raw = pc[..., OFF_KA:OFF_VA], pc[..., OFF_VA:OFF_QB]
            kb_c_raw, vb_c_raw = pc[..., OFF_KB:OFF_VB], pc[..., OFF_VB:OFF_POOL]

        qa = apply_axial_rope(rms_norm(p[..., OFF_QA:OFF_KA].reshape(B, L, A_HEADS, 2, HEAD_DIM), qk_gain[l, 0]), cos, sin)
        ka = apply_axial_rope(rms_norm(p[..., OFF_KA:OFF_VA].reshape(B, L, A_HEADS, 2, HEAD_DIM), qk_gain[l, 1]), cos, sin)
        va = p[..., OFF_VA:OFF_QB].reshape(B, L, A_HEADS, 2 * HEAD_DIM)
        ka_c = rms_norm(ka_c_raw.reshape(B, C, A_HEADS, 2, HEAD_DIM), qk_gain[l, 1])
        va_c = va_c_raw.reshape(B, C, A_HEADS, 2 * HEAD_DIM)
        qa_c = None if last else rms_norm(pc[..., OFF_QA:OFF_KA].reshape(B, C, A_HEADS, 2, HEAD_DIM), qk_gain[l, 0])
        lam_p = diff_lam[l].astype(jnp.float32)
        lambda_init = 0.8 - 0.6 * math.exp(-0.3 * l)
        lam = jnp.exp(jnp.sum(lam_p[0] * lam_p[1])) - jnp.exp(jnp.sum(lam_p[2] * lam_p[3])) + lambda_init
        o_a, o_a_c = diff_attention(qa, ka, va, qa_c, ka_c, va_c, lam, diff_subln[l], 1.0 - lambda_init)

        qb = apply_axial_rope(rms_norm(p[..., OFF_QB:OFF_KB].reshape(B, L, B_HEADS, HEAD_DIM), qk_gain[l, 2]), cos, sin)
        kb = apply_axial_rope(rms_norm(p[..., OFF_KB:OFF_VB].reshape(B, L, B_KV_HEADS, HEAD_DIM), qk_gain[l, 3]), cos, sin)
        vb = p[..., OFF_VB:OFF_POOL].reshape(B, L, B_KV_HEADS, HEAD_DIM)
        kb_c = rms_norm(kb_c_raw.reshape(B, C, B_KV_HEADS, HEAD_DIM), qk_gain[l, 3])
        vb_c = vb_c_raw.reshape(B, C, B_KV_HEADS, HEAD_DIM)
        qb_c = None if last else rms_norm(pc[..., OFF_QB:OFF_KB].reshape(B, C, B_HEADS, HEAD_DIM), qk_gain[l, 2])
        o_b, o_b_c = window_attention(qb, kb, vb, qb_c, kb_c, vb_c, win_sink[l])

        o_c = pool_mixer(p[..., OFF_POOL:OFF_HY], pool_w[l], pool_scale[l])

        kern_lat = hyena_kernel(L, hy_w1[l], hy_b1[l], hy_w2[l], hy_b2[l], hy_w3[l], hy_freq[l])
        o_d = hyena_mixer(p[..., OFF_HY:], hy_conv_w[l], hy_conv_b[l], kern_lat, hy_bias[l])

        x = x + g1 * (jnp.concatenate([o_a, o_b, o_c, o_d], axis=-1) @ w_out[l])
        h2 = rms_norm(x, norm2_g[l]) * (1.0 + sc2) + sh2
        x = x + g2 * conv_ffn(h2, ffn_w_in[l], ffn_conv_w[l], ffn_conv_b[l], ffn_w_out[l])

        if not last:
            o_c_c = pool_mixer(pc[..., OFF_POOL:OFF_HY], pool_w[l], pool_scale[l])
            kern_ctx = hyena_kernel(C, hy_w1[l], hy_b1[l], hy_w2[l], hy_b2[l], hy_w3[l], hy_freq[l])
            o_d_c = hyena_mixer(pc[..., OFF_HY:], hy_conv_w[l], hy_conv_b[l], kern_ctx, hy_bias[l])
            ctx = ctx + cg1 * (jnp.concatenate([o_a_c, o_b_c, o_c_c, o_d_c], axis=-1) @ w_out[l])
            hc2 = rms_norm(ctx, norm2_g[l]) * (1.0 + csc2) + csh2
            ctx = ctx + cg2 * conv_ffn(hc2, ffn_w_in[l], ffn_conv_w[l], ffn_conv_b[l], ffn_w_out[l])
    return x
```

```python
import functools
import math

import jax
import jax.numpy as jnp
from jax import lax
from jax.experimental import pallas as pl
from jax.experimental.pallas import tpu as pltpu

F32 = jnp.float32
BF16 = jnp.bfloat16

HEAD_DIM = 64
GRID_W = 64
ROPE_BASE = 10000.0
EPS = 1e-6
NEG_INF = -1e30
WINDOW = 128
BLOCK = 128
POOL_WINDOWS = (2, 4, 8, 16)
HY_EMB = 33
HY_HIDDEN = 64
HY_FAST_DECAY = 0.3
HY_SLOW_DECAY = 1.5
HY_TARGET = 1e-2

LANE = 128
SUBLANE = 8
HALO = SUBLANE
DFT_RADIX = 128
VMEM_LIMIT = 56 * 1024 * 1024


def _cp(*sem):
    return pltpu.CompilerParams(dimension_semantics=sem, vmem_limit_bytes=VMEM_LIMIT)


def _tile(n, pref):
    if n <= pref:
        return n
    t = pref
    while n % t:
        t //= 2
    assert t >= SUBLANE, (n, pref)
    return t


def _dot(a, b):
    return jnp.dot(a, b, preferred_element_type=F32)


def _dot_nt(a, b):
    return lax.dot_general(a, b, (((1,), (1,)), ((), ())), preferred_element_type=F32)


def _split_bf16(x):
    hi = x.astype(BF16)
    lo = (x - hi.astype(F32)).astype(BF16)
    return hi, lo


def _dot3(a_hi, a_lo, b):
    b_hi, b_lo = _split_bf16(b)
    return _dot(a_hi, b_hi) + _dot(a_hi, b_lo) + _dot(a_lo, b_hi)


def _mod_kernel(c_ref, w_ref, b_ref, o_ref):
    c = c_ref[...]
    a = (c * jax.nn.sigmoid(c)).astype(BF16)
    o_ref[...] = _dot(a, w_ref[...].astype(BF16)) + b_ref[...]


def _modulation(cvec, w_mod, b_mod):
    depth, d, n = w_mod.shape
    rows = cvec.shape[0]
    tn = _tile(n, 1024)
    return pl.pallas_call(
        _mod_kernel,
        grid=(depth, n // tn),
        in_specs=[
            pl.BlockSpec((rows, d), lambda l, j: (0, 0)),
            pl.BlockSpec((None, d, tn), lambda l, j: (l, 0, j)),
            pl.BlockSpec((None, 1, tn), lambda l, j: (l, 0, j)),
        ],
        out_specs=pl.BlockSpec((None, rows, tn), lambda l, j: (l, 0, j)),
        out_shape=jax.ShapeDtypeStruct((depth, rows, n), F32),
        compiler_params=_cp("parallel", "parallel"),
        name="adaln_mod",
    )(cvec, w_mod, b_mod.reshape(depth, 1, n))


def _norm_mod(x, g, sc, sh):
    ms = jnp.mean(x * x, axis=-1, keepdims=True)
    return (x * lax.rsqrt(ms + EPS) * g) * (1.0 + sc) + sh


def _inproj_kernel(x_ref, g_ref, sc_ref, sh_ref, w_ref, o_ref):
    h = _norm_mod(x_ref[...], g_ref[...], sc_ref[...], sh_ref[...])
    o_ref[...] = _dot(h.astype(BF16), w_ref[...])


def _in_projection(x2d, g, sc, sh, w, rows_per_group):
    n, d = x2d.shape
    nc = w.shape[1]
    tm = _tile(rows_per_group, 512)
    tpg = rows_per_group // tm
    ncol = 2 if (nc // 2) % LANE == 0 else 1
    tn = nc // ncol
    vec = pl.BlockSpec((None, 1, d), lambda j, i: (i // tpg, 0, 0))
    return pl.pallas_call(
        _inproj_kernel,
        grid=(ncol, n // tm),
        in_specs=[
            pl.BlockSpec((tm, d), lambda j, i: (i, 0)),
            pl.BlockSpec((1, d), lambda j, i: (0, 0)),
            vec, vec,
            pl.BlockSpec((d, tn), lambda j, i: (0, j)),
        ],
        out_specs=pl.BlockSpec((tm, tn), lambda j, i: (i, j)),
        out_shape=jax.ShapeDtypeStruct((n, nc), F32),
        compiler_params=_cp("parallel", "parallel"),
        name="in_proj",
    )(x2d, g, sc, sh, w)


def _qkprep_kernel(rope, x_ref, gain_ref, cos_ref, sin_ref, gm_ref, o_ref):
    x = x_ref[...]
    hi, lo = _split_bf16(x * x)
    gm = gm_ref[...]
    ms = _dot(hi, gm) + _dot(lo, gm)
    y = x * lax.rsqrt(ms + EPS) * gain_ref[...]
    if rope:
        lane = lax.broadcasted_iota(jnp.int32, y.shape, 1)
        first_half = (lane & 31) < 16
        partner = jnp.where(first_half, pltpu.roll(y, LANE - 16, 1), pltpu.roll(y, 16, 1))
        y = y * cos_ref[...] + partner * sin_ref[...]
    o_ref[...] = y.astype(BF16)


def _qk_prep(p, col_block0, gains, cos_t, sin_t, seq_len, rope):
    n = p.shape[0]
    nblk = gains.shape[0]
    tm = _tile(seq_len, 1024)
    tps = seq_len // tm
    lane = jnp.arange(LANE)
    gm = ((lane[:, None] // HEAD_DIM) == (lane[None, :] // HEAD_DIM)).astype(BF16) * (1.0 / HEAD_DIM)
    tab = pl.BlockSpec((tm, LANE), lambda i, j: (i % tps, 0))
    return pl.pallas_call(
        functools.partial(_qkprep_kernel, rope),
        grid=(n // tm, nblk),
        in_specs=[
            pl.BlockSpec((tm, LANE), lambda i, j: (i, col_block0 + j)),
            pl.BlockSpec((None, 1, LANE), lambda i, j: (j, 0, 0)),
            tab, tab,
            pl.BlockSpec((LANE, LANE), lambda i, j: (0, 0)),
        ],
        out_specs=pl.BlockSpec((tm, LANE), lambda i, j: (i, j)),
        out_shape=jax.ShapeDtypeStruct((n, nblk * LANE), BF16),
        compiler_params=_cp("parallel", "parallel"),
        name="qk_prep",
    )(p, gains.reshape(nblk, 1, LANE), cos_t, sin_t, gm.astype(BF16))


def _rope_tables(seq_len):
    rows = seq_len // GRID_W
    row = jnp.repeat(jnp.arange(rows, dtype=F32), GRID_W)
    col = jnp.tile(jnp.arange(GRID_W, dtype=F32), rows)
    half = HEAD_DIM // 2
    inv = ROPE_BASE ** (-jnp.arange(0, half, 2, dtype=F32) / half)
    ang = jnp.concatenate([row[:, None] * inv, col[:, None] * inv], axis=-1)
    lane = jnp.arange(LANE)
    within = lane % HEAD_DIM
    idx = (within // 32) * 16 + (within % 16)
    sign = jnp.where((lane % 32) < 16, -1.0, 1.0).astype(F32)
    return jnp.cos(ang)[:, idx], jnp.sin(ang)[:, idx] * sign


def _attn_a_kernel(has_lat, sub_scale, *refs):
    if has_lat:
        (q_ref, kc_ref, vc_ref, k_ref, v_ref, lam_ref, subg_ref, o_ref,
         m1, l1, a1, m2, l2, a2) = refs
    else:
        (q_ref, kc_ref, vc_ref, lam_ref, subg_ref, o_ref, m1, l1, a1, m2, l2, a2) = refs
    j = pl.program_id(3)

    @pl.when(j == 0)
    def _():
        for m, l, a in ((m1, l1, a1), (m2, l2, a2)):
            m[...] = jnp.full(m.shape, -jnp.inf, F32)
            l[...] = jnp.zeros(l.shape, F32)
            a[...] = jnp.zeros(a.shape, F32)

    q = q_ref[...]
    lane = lax.broadcasted_iota(jnp.int32, q.shape, 1)
    zero = jnp.zeros_like(q)
    q1 = jnp.where(lane < HEAD_DIM, q, zero)
    q2 = jnp.where(lane >= HEAD_DIM, q, zero)

    def step(k, v):
        for qm, m, l, a in ((q1, m1, l1, a1), (q2, m2, l2, a2)):
            s = _dot_nt(qm, k)
            m_prev = m[...]
            m_new = jnp.maximum(m_prev, jnp.max(s, axis=-1, keepdims=True))
            alpha = jnp.exp(m_prev - m_new)
            p = jnp.exp(s - m_new)
            l[...] = alpha * l[...] + jnp.sum(p, axis=-1, keepdims=True)
            a[...] = alpha * a[...] + _dot(p.astype(BF16), v)
            m[...] = m_new

    if has_lat:
        @pl.when(j == 0)
        def _():
            step(kc_ref[...], vc_ref[...].astype(BF16))

        @pl.when(j > 0)
        def _():
            step(k_ref[...], v_ref[...].astype(BF16))
    else:
        step(kc_ref[...], vc_ref[...].astype(BF16))

    @pl.when(j == pl.num_programs(3) - 1)
    def _():
        o = a1[...] / l1[...] - lam_ref[...] * (a2[...] / l2[...])
        ms = jnp.mean(o * o, axis=-1, keepdims=True)
        o_ref[...] = ((o * lax.rsqrt(ms + EPS) * subg_ref[...]) * sub_scale).astype(BF16)


def _diff_attention(q_arr, q_len, qk_ctx, p_ctx, ctx_len, qk_lat, p_lat, lat_len, lam, subg,
                    sub_scale, heads, k_col0, v_col0):
    has_lat = qk_lat is not None
    batch = q_arr.shape[0] // q_len
    tq = _tile(q_len, 512)
    nq = q_len // tq
    in_specs = [
        pl.BlockSpec((tq, LANE), lambda b, h, i, j: (b * nq + i, h)),
        pl.BlockSpec((ctx_len, LANE), lambda b, h, i, j: (b, k_col0 + h)),
        pl.BlockSpec((ctx_len, LANE), lambda b, h, i, j: (b, v_col0 + h)),
    ]
    args = [q_arr, qk_ctx, p_ctx]
    nsteps = 1
    if has_lat:
        tk = _tile(lat_len, 512)
        nk = lat_len // tk
        nsteps = 1 + nk
        in_specs += [
            pl.BlockSpec((tk, LANE), lambda b, h, i, j: (b * nk + jnp.maximum(j - 1, 0), k_col0 + h)),
            pl.BlockSpec((tk, LANE), lambda b, h, i, j: (b * nk + jnp.maximum(j - 1, 0), v_col0 + h)),
        ]
        args += [qk_lat, p_lat]
    in_specs += [pl.BlockSpec((1, LANE), lambda b, h, i, j: (0, 0))] * 2
    args += [lam, subg]
    return pl.pallas_call(
        functools.partial(_attn_a_kernel, has_lat, sub_scale),
        grid=(batch, heads, nq, nsteps),
        in_specs=in_specs,
        out_specs=pl.BlockSpec((tq, LANE), lambda b, h, i, j: (b * nq + i, h)),
        out_shape=jax.ShapeDtypeStruct((batch * q_len, heads * LANE), BF16),
        scratch_shapes=[pltpu.VMEM((tq, 1), F32), pltpu.VMEM((tq, 1), F32), pltpu.VMEM((tq, LANE), F32),
                        pltpu.VMEM((tq, 1), F32), pltpu.VMEM((tq, 1), F32), pltpu.VMEM((tq, LANE), F32)],
        compiler_params=_cp("parallel", "parallel", "parallel", "arbitrary"),
        name="diff_attn",
    )(*args)


def _attn_b_kernel(has_loc, seq_len, q_heads, group, *refs):
    if has_loc:
        (q_ref, kx_ref, vx_ref, kp_ref, kc_ref, kn_ref, vp_ref, vc_ref, vn_ref, sink_ref, o_ref) = refs
    else:
        (q_ref, kx_ref, vx_ref, sink_ref, o_ref) = refs
    n = pl.program_id(1)
    q = q_ref[...]
    lane = lax.broadcasted_iota(jnp.int32, (BLOCK, LANE), 1)
    low = lane < HEAD_DIM
    kx = kx_ref[...]
    kx_sw = pltpu.roll(kx, HEAD_DIM, 1)
    vx = vx_ref[...].astype(BF16)
    if has_loc:
        kb = jnp.concatenate([kp_ref[...], kc_ref[...], kn_ref[...]], axis=0)
        kb_sw = pltpu.roll(kb, HEAD_DIM, 1)
        vb = jnp.concatenate([vp_ref[...], vc_ref[...], vn_ref[...]], axis=0).astype(BF16)
        kpos = (n - 1) * BLOCK + lax.broadcasted_iota(jnp.int32, (BLOCK, 3 * BLOCK), 1)
        qpos = n * BLOCK + lax.broadcasted_iota(jnp.int32, (BLOCK, 3 * BLOCK), 0)
        valid = (kpos >= 0) & (kpos < seq_len) & (jnp.abs(kpos - qpos) <= WINDOW)
    outs = []
    for blk in range(q_heads // 2):
        q128 = q[:, blk * LANE:(blk + 1) * LANE]
        res = []
        for half in range(2):
            hq = 2 * blk + half
            kvh = hq // group
            same = (kvh % 2) == half
            qm = jnp.where(low if half == 0 else jnp.logical_not(low), q128, jnp.zeros_like(q128))
            sink = sink_ref[hq:hq + 1, 0:1]
            s_ctx = _dot_nt(qm, kx if same else kx_sw)
            m = jnp.maximum(jnp.max(s_ctx, axis=-1, keepdims=True), sink)
            if has_loc:
                s_loc = jnp.where(valid, _dot_nt(qm, kb if same else kb_sw), NEG_INF)
                m = jnp.maximum(m, jnp.max(s_loc, axis=-1, keepdims=True))
            p_ctx = jnp.exp(s_ctx - m)
            den = jnp.exp(sink - m) + jnp.sum(p_ctx, axis=-1, keepdims=True)
            o = _dot(p_ctx.astype(BF16), vx)
            if has_loc:
                p_loc = jnp.exp(s_loc - m)
                den = den + jnp.sum(p_loc, axis=-1, keepdims=True)
                o = o + _dot(p_loc.astype(BF16), vb)
            o = o / den
            res.append(o if same else pltpu.roll(o, HEAD_DIM, 1))
        outs.append(jnp.where(low, res[0], res[1]))
    o_ref[...] = jnp.concatenate(outs, axis=1).astype(BF16)


def _window_attention(q_arr, q_len, qk_ctx, p_ctx, ctx_len, p_lat, sink_tab, q_heads, kv_heads,
                      q_blk, k_blk, v_blk, local):
    assert kv_heads == 2, "key/value heads must fill exactly one 128-lane block"
    batch = q_arr.shape[0] // q_len
    nb = q_len // BLOCK
    qw = q_heads * HEAD_DIM
    in_specs = [
        pl.BlockSpec((BLOCK, qw), lambda b, n: (b * nb + n, q_blk)),
        pl.BlockSpec((ctx_len, LANE), lambda b, n: (b, k_blk)),
        pl.BlockSpec((ctx_len, LANE), lambda b, n: (b, v_blk)),
    ]
    args = [q_arr, qk_ctx, p_ctx]
    if local:
        prev = lambda b, n: b * nb + jnp.maximum(n - 1, 0)
        nxt = lambda b, n: b * nb + jnp.minimum(n + 1, nb - 1)
        for col, arr in ((k_blk, q_arr), (v_blk, p_lat)):
            in_specs += [
                pl.BlockSpec((BLOCK, LANE), lambda b, n, col=col: (prev(b, n), col)),
                pl.BlockSpec((BLOCK, LANE), lambda b, n, col=col: (b * nb + n, col)),
                pl.BlockSpec((BLOCK, LANE), lambda b, n, col=col: (nxt(b, n), col)),
            ]
            args += [arr, arr, arr]
    in_specs.append(pl.BlockSpec((q_heads, LANE), lambda b, n: (0, 0)))
    args.append(sink_tab)
    return pl.pallas_call(
        functools.partial(_attn_b_kernel, local, q_len, q_heads, q_heads // kv_heads),
        grid=(batch, nb),
        in_specs=in_specs,
        out_specs=pl.BlockSpec((BLOCK, qw), lambda b, n: (b * nb + n, 0)),
        out_shape=jax.ShapeDtypeStruct((batch * q_len, qw), BF16),
        compiler_params=_cp("parallel", "parallel"),
        name="window_attn",
    )(*args)


def _halo_specs(tile, width, col_blk, n_rows):
    r = tile // HALO
    last = n_rows // HALO - 1
    return [
        pl.BlockSpec((tile, width), lambda i: (i, col_blk)),
        pl.BlockSpec((HALO, width), lambda i: (jnp.maximum(i * r - 1, 0), col_blk)),
        pl.BlockSpec((HALO, width), lambda i: (jnp.minimum(i * r + r, last), col_blk)),
    ]


def _outside_rows(tile, tile_idx, tiles_per_seq):
    row = lax.broadcasted_iota(jnp.int32, (tile + 2 * HALO, 1), 0)
    lo = jnp.where(tile_idx == 0, HALO, 0)
    hi = jnp.where(tile_idx == tiles_per_seq - 1, HALO + tile, tile + 2 * HALO)
    return (row < lo) | (row >= hi)


def _with_halo(x, xp, xn, tiles_per_seq):
    ext = jnp.concatenate([xp, x, xn], axis=0)
    outside = _outside_rows(x.shape[0], pl.program_id(0) % tiles_per_seq, tiles_per_seq)
    return jnp.where(outside, 0.0, ext)


def _conv3(ext, w, b, tile):
    rows = ext.shape[0]
    c = (pltpu.roll(ext, 1, 0) * w[0:1] + ext * w[1:2] + pltpu.roll(ext, rows - 1, 0) * w[2:3])
    return c[HALO:HALO + tile] + b


def _pool_kernel(seq_len, tile, x_ref, xp_ref, xn_ref, w_ref, ls_ref, o_ref):
    tps = seq_len // tile
    x = x_ref[...]
    ext = _with_halo(x, xp_ref[...], xn_ref[...], tps)
    rows = tile + 2 * HALO
    t = (pl.program_id(0) % tps) * tile + lax.broadcasted_iota(jnp.int32, (tile, 1), 0)
    gw = x.shape[1] // len(POOL_WINDOWS)
    outs = []
    for g, w in enumerate(POOL_WINDOWS):
        lo = w // 2
        hi = w - 1 - lo
        a = ext[:, g * gw:(g + 1) * gw]
        span = 1
        while span < w:
            a = a + pltpu.roll(a, span, 0)
            span *= 2
        win = (pltpu.roll(a, rows - hi, 0) if hi else a)[HALO:HALO + tile]
        cnt = jnp.minimum(t + hi + 1, seq_len) - jnp.maximum(t - lo, 0)
        d = win / cnt.astype(F32) - x[:, g * gw:(g + 1) * gw]
        outs.append(_dot(d.astype(BF16), w_ref[g]))
    o_ref[...] = (jnp.concatenate(outs, axis=1) * ls_ref[...]).astype(BF16)


def _pool_mixer(p, col0, width, seq_len, w_lin, ls):
    n = p.shape[0]
    tile = _tile(seq_len, 512)
    g, gw, _ = w_lin.shape
    return pl.pallas_call(
        functools.partial(_pool_kernel, seq_len, tile),
        grid=(n // tile,),
        in_specs=_halo_specs(tile, width, col0 // width, n)
        + [pl.BlockSpec((g, gw, gw), lambda i: (0, 0, 0)),
           pl.BlockSpec((1, width), lambda i: (0, 0))],
        out_specs=pl.BlockSpec((tile, width), lambda i: (i, 0)),
        out_shape=jax.ShapeDtypeStruct((n, width), BF16),
        compiler_params=_cp("parallel"),
        name="pool_mixer",
    )(p, p, p, w_lin, ls)


def _hypre_kernel(seq_len, tile, x_ref, xp_ref, xn_ref, cw_ref, cb_ref, z_ref, x2_ref):
    ext = _with_halo(x_ref[...], xp_ref[...], xn_ref[...], seq_len // tile)
    c = _conv3(ext, cw_ref[...], cb_ref[...], tile)
    dw = c.shape[1] // 3
    z_ref[...] = c[:, :dw] * c[:, dw:2 * dw]
    x2_ref[...] = c[:, 2 * dw:]


def _hyena_pre(p, col0, width, seq_len, conv_w, conv_b):
    n = p.shape[0]
    tile = _tile(seq_len, 512)
    dw = width // 3
    out = jax.ShapeDtypeStruct((n, dw), F32)
    return pl.pallas_call(
        functools.partial(_hypre_kernel, seq_len, tile),
        grid=(n // tile,),
        in_specs=_halo_specs(tile, width, col0 // width, n)
        + [pl.BlockSpec((3, width), lambda i: (0, 0)),
           pl.BlockSpec((1, width), lambda i: (0, 0))],
        out_specs=[pl.BlockSpec((tile, dw), lambda i: (i, 0))] * 2,
        out_shape=[out, out],
        compiler_params=_cp("parallel"),
        name="hyena_pre",
    )(p, p, p, conv_w, conv_b.reshape(1, width))


def _hyfilt_kernel(seq_len, tile, zf_ref, w1_ref, b1_ref, w2_ref, b2_ref, w3_ref, fr_ref, dl_ref,
                   k_ref, ss_ref):
    i = pl.program_id(0)
    hp = functools.partial(jnp.dot, precision=lax.Precision.HIGHEST, preferred_element_type=F32)
    zf = zf_ref[...]
    fr = fr_ref[...]
    h = jnp.sin(fr * (hp(zf, w1_ref[...]) + b1_ref[...]))
    h = jnp.sin(fr * (hp(h, w2_ref[...]) + b2_ref[...]))
    k = hp(h, w3_ref[...]) * jnp.exp(-zf[:, 0:1] * dl_ref[...])
    row = i * tile + lax.broadcasted_iota(jnp.int32, (tile, 1), 0)
    k = jnp.where(row == seq_len, 0.0, k)
    k_ref[...] = k

    @pl.when(i == 0)
    def _():
        ss_ref[...] = jnp.zeros(ss_ref.shape, F32)

    ss_ref[...] += jnp.sum(k * k, axis=0, keepdims=True)


def _hyena_filter(seq_len, w1, b1, w2, b2, w3, freq):
    dw = w3.shape[1] // 2
    bands = (HY_EMB - 1) // 2
    t01 = jnp.linspace(0.0, 1.0, seq_len, dtype=F32)[:, None]
    w_ang = 2.0 * math.pi * jnp.arange(seq_len, dtype=F32)[:, None] / seq_len
    f = jnp.linspace(1e-4, bands - 1, bands, dtype=F32)[None, :]
    z = jnp.concatenate([t01, jnp.cos(f * w_ang), -jnp.sin(f * w_ang)], axis=-1)
    rev = (seq_len - jnp.arange(seq_len)) % seq_len
    zf = jnp.pad(jnp.concatenate([z, z[rev]], axis=0), ((0, 0), (0, LANE - HY_EMB)))
    pad_h = LANE - HY_HIDDEN
    w1p = jnp.pad(w1, ((0, LANE - HY_EMB), (0, pad_h)))
    w2p = jnp.pad(w2, ((0, pad_h), (0, pad_h)))
    w3p = jnp.pad(w3, ((0, pad_h), (0, 0)))
    vec = lambda v: jnp.pad(v, (0, pad_h)).reshape(1, LANE)
    deltas = jnp.abs(jnp.linspace(math.log(HY_TARGET) / HY_FAST_DECAY, math.log(HY_TARGET) / HY_SLOW_DECAY,
                                  dw, dtype=F32)).reshape(1, dw)
    tile = _tile(seq_len, 1024)
    tps = seq_len // tile
    full = lambda shape: pl.BlockSpec(shape, lambda i: (0, 0))
    return pl.pallas_call(
        functools.partial(_hyfilt_kernel, seq_len, tile),
        grid=(2 * tps,),
        in_specs=[pl.BlockSpec((tile, LANE), lambda i: (i, 0)),
                  full((LANE, LANE)), full((1, LANE)), full((LANE, LANE)), full((1, LANE)),
                  pl.BlockSpec((LANE, dw), lambda i: (0, i // tps)),
                  full((1, LANE)), full((1, dw))],
        out_specs=[pl.BlockSpec((tile, dw), lambda i: (i, 0)), full((1, dw))],
        out_shape=[jax.ShapeDtypeStruct((2 * seq_len, dw), F32), jax.ShapeDtypeStruct((1, dw), F32)],
        compiler_params=_cp("arbitrary"),
        name="hyena_filter",
    )(zf, w1p, vec(b1), w2p, vec(b2), w3p, vec(freq), deltas)


def _dft_tables(n1, n2):
    n = n1 * n2
    two_pi = 2.0 * math.pi

    def cs(num, den):
        ang = (num % den).astype(F32) * (two_pi / den)
        return jnp.cos(ang), jnp.sin(ang)

    a = jnp.arange(n1)
    c1, s1 = cs(a[:, None] * a[None, :], n1)
    fwd_a = jnp.concatenate([c1, -s1], axis=0)
    inv_a = jnp.concatenate([c1, -s1], axis=1)[: n1 // 2] / n
    k2 = jnp.arange(n2)
    num = (k2[None, :, None] * k2[None, None, :] * n1 + a[:, None, None] * k2[None, None, :])
    cb, sb = cs(num, n)
    fwd_b = jnp.concatenate([jnp.concatenate([cb, sb], axis=2),
                             jnp.concatenate([-sb, cb], axis=2)], axis=1)
    cbt, sbt = jnp.swapaxes(cb, 1, 2), jnp.swapaxes(sb, 1, 2)
    inv_b = jnp.concatenate([jnp.concatenate([cbt, -sbt], axis=2),
                             jnp.concatenate([sbt, cbt], axis=2)], axis=1)
    return tuple(_split_bf16(m) for m in (fwd_a, inv_a, fwd_b, inv_b))


def _dft_a_kernel(fh_ref, fl_ref, x_ref, re_ref, im_ref):
    y = _dot3(fh_ref[...], fl_ref[...], x_ref[...])
    half = y.shape[0] // 2
    re_ref[...] = y[:half]
    im_ref[...] = y[half:]


def _dft_stage_a(xv, fwd_a, n1):
    b, k, cols = xv.shape
    fh, fl = fwd_a
    ct = _tile(cols, 2048)
    out = jax.ShapeDtypeStruct((b, n1, cols), F32)
    return pl.pallas_call(
        _dft_a_kernel,
        grid=(b, cols // ct),
        in_specs=[pl.BlockSpec((2 * n1, k), lambda bi, j: (0, 0))] * 2
        + [pl.BlockSpec((None, k, ct), lambda bi, j: (bi, 0, j))],
        out_specs=[pl.BlockSpec((None, n1, ct), lambda bi, j: (bi, 0, j))] * 2,
        out_shape=[out, out],
        compiler_params=_cp("parallel", "parallel"),
        name="dft_stage_a",
    )(fh[:, :k], fl[:, :k], xv)


def _dft_b_kernel(mh_ref, ml_ref, re_ref, im_ref, ss_ref, kr_ref, ki_ref):
    x = _dot3(mh_ref[...], ml_ref[...], jnp.concatenate([re_ref[...], im_ref[...]], axis=0))
    nrm = lax.rsqrt(ss_ref[...] + EPS)
    half = x.shape[0] // 2
    kr_ref[...] = x[:half] * nrm
    ki_ref[...] = x[half:] * nrm


def _filter_spectrum(kern, sumsq, tabs, n1, n2):
    fwd_a, _, fwd_b, _ = tabs
    ch = kern.shape[1]
    are, aim = _dft_stage_a(kern.reshape(1, n1, n2 * ch), fwd_a, n1)
    are, aim = are.reshape(n1 * n2, ch), aim.reshape(n1 * n2, ch)
    blk = pl.BlockSpec((n2, ch), lambda i: (i, 0))
    mat = pl.BlockSpec((None, 2 * n2, 2 * n2), lambda i: (i, 0, 0))
    out = jax.ShapeDtypeStruct((n1 * n2, ch), F32)
    return pl.pallas_call(
        _dft_b_kernel,
        grid=(n1,),
        in_specs=[mat, mat, blk, blk, pl.BlockSpec((1, ch), lambda i: (0, 0))],
        out_specs=[blk, blk],
        out_shape=[out, out],
        compiler_params=_cp("parallel"),
        name="filter_spectrum",
    )(fwd_b[0], fwd_b[1], are, aim, sumsq)


def _dft_mul_kernel(fh_ref, fl_ref, gh_ref, gl_ref, re_ref, im_ref, kr_ref, ki_ref, qr_ref, qi_ref):
    x = _dot3(fh_ref[...], fl_ref[...], jnp.concatenate([re_ref[...], im_ref[...]], axis=0))
    half = x.shape[0] // 2
    xr, xi = x[:half], x[half:]
    kr, ki = kr_ref[...], ki_ref[...]
    prod = jnp.concatenate([xr * kr - xi * ki, xr * ki + xi * kr], axis=0)
    q = _dot3(gh_ref[...], gl_ref[...], prod)
    qr_ref[...] = q[:half]
    qi_ref[...] = q[half:]


def _dft_inv_kernel(gh_ref, gl_ref, qr_ref, qi_ref, z_ref, x2_ref, bias_ref, o_ref):
    y = _dot3(gh_ref[...], gl_ref[...], jnp.concatenate([qr_ref[...], qi_ref[...]], axis=0))
    z = z_ref[...]
    o_ref[...] = (x2_ref[...] * (y + z * bias_ref[...])).astype(BF16)


def _hyena_long(z, x2, kr, ki, bias, tabs, batch, seq_len, n1, n2):
    fwd_a, inv_a, fwd_b, inv_b = tabs
    ch = z.shape[1]
    cols = n2 * ch
    zv = z.reshape(batch, n1 // 2, cols)
    are, aim = _dft_stage_a(zv, fwd_a, n1)
    are, aim = are.reshape(batch, n1 * n2, ch), aim.reshape(batch, n1 * n2, ch)
    mat = pl.BlockSpec((None, 2 * n2, 2 * n2), lambda i, b: (i, 0, 0))
    dat = pl.BlockSpec((None, n2, ch), lambda i, b: (b, i, 0))
    spec = pl.BlockSpec((n2, ch), lambda i, b: (i, 0))
    out = jax.ShapeDtypeStruct((batch, n1 * n2, ch), F32)
    qr, qi = pl.pallas_call(
        _dft_mul_kernel,
        grid=(n1, batch),
        in_specs=[mat, mat, mat, mat, dat, dat, spec, spec],
        out_specs=[dat, dat],
        out_shape=[out, out],
        compiler_params=_cp("parallel", "parallel"),
        name="dft_filter_mul",
    )(fwd_b[0], fwd_b[1], inv_b[0], inv_b[1], are, aim, kr, ki)
    qr, qi = qr.reshape(batch, n1, cols), qi.reshape(batch, n1, cols)
    ct = _tile(cols, 2048)
    full = pl.BlockSpec((None, n1, ct), lambda b, j: (b, 0, j))
    half = pl.BlockSpec((None, n1 // 2, ct), lambda b, j: (b, 0, j))
    gmat = pl.BlockSpec((n1 // 2, 2 * n1), lambda b, j: (0, 0))
    o = pl.pallas_call(
        _dft_inv_kernel,
        grid=(batch, cols // ct),
        in_specs=[gmat, gmat, full, full, half, half, pl.BlockSpec((1, ct), lambda b, j: (0, j))],
        out_specs=half,
        out_shape=jax.ShapeDtypeStruct((batch, n1 // 2, cols), BF16),
        compiler_params=_cp("parallel", "parallel"),
        name="dft_inverse_gate",
    )(inv_a[0], inv_a[1], qr, qi, zv, x2.reshape(batch, n1 // 2, cols), jnp.tile(bias, n2).reshape(1, cols))
    return o.reshape(batch * seq_len, ch)


def _hyena_short_kernel(fh_ref, fl_ref, gh_ref, gl_ref, k_ref, ss_ref, z_ref, x2_ref, bias_ref, o_ref):
    fh, fl = fh_ref[...], fl_ref[...]
    z = z_ref[...]
    seq = z.shape[0]
    ks = _dot3(fh, fl, k_ref[...]) * lax.rsqrt(ss_ref[...] + EPS)
    zs = _dot3(fh[:, :seq], fl[:, :seq], z)
    n = ks.shape[0] // 2
    kr, ki, zr, zi = ks[:n], ks[n:], zs[:n], zs[n:]
    prod = jnp.concatenate([zr * kr - zi * ki, zr * ki + zi * kr], axis=0)
    y = _dot3(gh_ref[...], gl_ref[...], prod)
    o_ref[...] = (x2_ref[...] * (y + z * bias_ref[...])).astype(BF16)


def _hyena_short(z, x2, kern, sumsq, bias, batch, seq_len):
    n = 2 * seq_len
    ch = z.shape[1]
    a = jnp.arange(n)
    ang = ((a[:, None] * a[None, :]) % n).astype(F32) * (2.0 * math.pi / n)
    c, s = jnp.cos(ang), jnp.sin(ang)
    fh, fl = _split_bf16(jnp.concatenate([c, -s], axis=0))
    gh, gl = _split_bf16(jnp.concatenate([c, -s], axis=1)[:seq_len] / n)
    full = lambda shape: pl.BlockSpec(shape, lambda b: (0, 0))
    dat = pl.BlockSpec((seq_len, ch), lambda b: (b, 0))
    return pl.pallas_call(
        _hyena_short_kernel,
        grid=(batch,),
        in_specs=[full((2 * n, n)), full((2 * n, n)), full((seq_len, 2 * n)), full((seq_len, 2 * n)),
                  full((n, ch)), full((1, ch)), dat, dat, full((1, ch))],
        out_specs=dat,
        out_shape=jax.ShapeDtypeStruct((batch * seq_len, ch), BF16),
        compiler_params=_cp("parallel"),
        name="hyena_short_conv",
    )(fh, fl, gh, gl, kern, sumsq, z, x2, bias.reshape(1, ch))


def _hyena_mixer(p, col0, width, batch, seq_len, conv_w, conv_b, w1, b1, w2, b2, w3, freq, bias):
    z, x2 = _hyena_pre(p, col0, width, seq_len, conv_w, conv_b)
    kern, sumsq = _hyena_filter(seq_len, w1, b1, w2, b2, w3, freq)
    n = 2 * seq_len
    if n <= 4 * DFT_RADIX:
        return _hyena_short(z, x2, kern, sumsq, bias, batch, seq_len)
    n2 = DFT_RADIX
    n1 = n // n2
    tabs = _dft_tables(n1, n2)
    kr, ki = _filter_spectrum(kern, sumsq, tabs, n1, n2)
    return _hyena_long(z, x2, kr, ki, bias, tabs, batch, seq_len, n1, n2)


def _outproj_kernel(x_ref, a_ref, b_ref, c_ref, d_ref, w_ref, g_ref, o_ref):
    acc = None
    row = 0
    for m_ref in (a_ref, b_ref, c_ref, d_ref):
        wdt = m_ref.shape[1]
        part = _dot(m_ref[...], w_ref[row:row + wdt, :])
        acc = part if acc is None else acc + part
        row += wdt
    o_ref[...] = x_ref[...] + g_ref[...] * acc


def _out_projection(x2d, mixers, w, gate, rows_per_group):
    n, d = x2d.shape
    tm = _tile(rows_per_group, 512)
    tpg = rows_per_group // tm
    return pl.pallas_call(
        _outproj_kernel,
        grid=(n // tm,),
        in_specs=[pl.BlockSpec((tm, d), lambda i: (i, 0))]
        + [pl.BlockSpec((tm, m.shape[1]), lambda i: (i, 0)) for m in mixers]
        + [pl.BlockSpec(w.shape, lambda i: (0, 0)),
           pl.BlockSpec((None, 1, d), lambda i: (i // tpg, 0, 0))],
        out_specs=pl.BlockSpec((tm, d), lambda i: (i, 0)),
        out_shape=jax.ShapeDtypeStruct((n, d), F32),
        compiler_params=_cp("parallel"),
        name="out_proj",
    )(x2d, *mixers, w, gate)


def _ffn_kernel(seq_len, tile, x_ref, xp_ref, xn_ref, g_ref, sc_ref, sh_ref, gate_ref,
                wg_ref, wu_ref, cwg_ref, cwu_ref, cbg_ref, cbu_ref, wd_ref, o_ref, h_scr, acc):
    j = pl.program_id(1)

    @pl.when(j == 0)
    def _():
        ext = jnp.concatenate([xp_ref[...], x_ref[...], xn_ref[...]], axis=0)
        h = _norm_mod(ext, g_ref[...], sc_ref[...], sh_ref[...])
        tps = seq_len // tile
        outside = _outside_rows(tile, pl.program_id(0) % tps, tps)
        h_scr[...] = jnp.where(outside, 0.0, h).astype(BF16)
        acc[...] = jnp.zeros(acc.shape, F32)

    h = h_scr[...]
    gate = _conv3(_dot(h, wg_ref[...]), cwg_ref[...], cbg_ref[...], tile)
    up = _conv3(_dot(h, wu_ref[...]), cwu_ref[...], cbu_ref[...], tile)
    act = (gate * jax.nn.sigmoid(gate)) * up
    acc[...] += _dot(act.astype(BF16), wd_ref[...])

    @pl.when(j == pl.num_programs(1) - 1)
    def _():
        o_ref[...] = x_ref[...] + gate_ref[...] * acc[...]


def _conv_ffn(x2d, g, sc, sh, gate, w_up, conv_w, conv_b, w_down, seq_len, rows_per_group):
    n, d = x2d.shape
    f = w_down.shape[0]
    tile = _tile(seq_len, 512)
    hc = _tile(f, 512)
    nh = f // hc
    tpg = rows_per_group // tile
    r = tile // HALO
    last = n // HALO - 1
    vec = pl.BlockSpec((None, 1, d), lambda i, j: (i // tpg, 0, 0))
    cb = conv_b.reshape(1, 2 * f)
    return pl.pallas_call(
        functools.partial(_ffn_kernel, seq_len, tile),
        grid=(n // tile, nh),
        in_specs=[
            pl.BlockSpec((tile, d), lambda i, j: (i, 0)),
            pl.BlockSpec((HALO, d), lambda i, j: (jnp.maximum(i * r - 1, 0), 0)),
            pl.BlockSpec((HALO, d), lambda i, j: (jnp.minimum(i * r + r, last), 0)),
            pl.BlockSpec((1, d), lambda i, j: (0, 0)),
            vec, vec, vec,
            pl.BlockSpec((d, hc), lambda i, j: (0, j)),
            pl.BlockSpec((d, hc), lambda i, j: (0, nh + j)),
            pl.BlockSpec((3, hc), lambda i, j: (0, j)),
            pl.BlockSpec((3, hc), lambda i, j: (0, nh + j)),
            pl.BlockSpec((1, hc), lambda i, j: (0, j)),
            pl.BlockSpec((1, hc), lambda i, j: (0, nh + j)),
            pl.BlockSpec((hc, d), lambda i, j: (j, 0)),
        ],
        out_specs=pl.BlockSpec((tile, d), lambda i, j: (i, 0)),
        out_shape=jax.ShapeDtypeStruct((n, d), F32),
        scratch_shapes=[pltpu.VMEM((tile + 2 * HALO, d), BF16), pltpu.VMEM((tile, d), F32)],
        compiler_params=_cp("parallel", "arbitrary"),
        name="conv_ffn",
    )(x2d, x2d, x2d, g, sc, sh, gate, w_up, w_up, conv_w, conv_w, cb, cb, w_down)


def kernel(x, c, ctx, c_ctx, w_mod, b_mod, norm1_g, norm2_g, w_in, w_out, qk_gain, diff_lam, diff_subln,
           win_sink, pool_w, pool_scale, hy_conv_w, hy_conv_b, hy_w1, hy_b1, hy_w2, hy_b2, hy_w3, hy_freq,
           hy_bias, ffn_w_in, ffn_conv_w, ffn_conv_b, ffn_w_out):
    batch, seq, d = x.shape
    clen = ctx.shape[1]
    depth = w_in.shape[0]
    a_heads = d // 512
    a_width = a_heads * 2 * HEAD_DIM
    b_heads = d // 256
    b_kv = b_heads // 4
    b_width = b_heads * HEAD_DIM
    b_kvw = b_kv * HEAD_DIM
    c_width = d // 4
    d_width = d - a_width - b_width - c_width
    off_ka = a_width
    off_va = 2 * a_width
    off_qb = 3 * a_width
    off_kb = off_qb + b_width
    off_vb = off_kb + b_kvw
    off_pool = off_vb + b_kvw
    off_hy = off_pool + c_width
    in_cols = off_hy + 3 * d_width
    order = ((off_hy, in_cols), (off_pool, off_hy), (off_va, off_qb), (0, off_ka), (off_ka, off_va),
             (off_qb, off_kb), (off_kb, off_vb), (off_vb, off_pool))
    n_hy, n_pool, n_va, n_qa, n_ka, n_qb, n_kb, n_vb = (
        sum(e - s for s, e in order[:k]) for k in range(len(order)))
    assert a_width % LANE == 0 and b_width % LANE == 0 and b_kvw == LANE and c_width // 4 == LANE
    assert seq % BLOCK == 0 and clen % BLOCK == 0 and n_pool % c_width == 0
    assert (n_ka - n_qa) == a_width and (n_qb - n_qa) % b_width == 0 and n_kb - n_qb == b_width

    rows = -(-(batch + 1) // SUBLANE) * SUBLANE
    cvec = jnp.zeros((rows, d), F32).at[:batch].set(c).at[batch].set(c_ctx)
    mods = _modulation(cvec, w_mod, b_mod)

    w_in_b = jnp.concatenate([w_in[:, :, s:e] for s, e in order], axis=-1).astype(BF16)
    w_out_b = w_out.astype(BF16)
    ffn_in_b = ffn_w_in.astype(BF16)
    ffn_out_b = ffn_w_out.astype(BF16)
    pool_w_b = pool_w.astype(BF16)

    cos_t, sin_t = _rope_tables(seq)
    scale = HEAD_DIM ** -0.5
    tile2 = lambda v: jnp.tile(v, LANE // HEAD_DIM)

    xs = x.reshape(batch * seq, d)
    cs = ctx.reshape(batch * clen, d)
    for l in range(depth):
        last = l == depth - 1
        m = mods[l]
        sh1, sc1, g1, sh2, sc2, g2 = [m[:batch, i * d:(i + 1) * d].reshape(batch, 1, d) for i in range(6)]
        csh1, csc1, cg1, csh2, csc2, cg2 = [m[batch, i * d:(i + 1) * d].reshape(1, 1, d) for i in range(6)]
        n1g = norm1_g[l].reshape(1, d)
        n2g = norm2_g[l].reshape(1, d)

        p = _in_projection(xs, n1g, sc1, sh1, w_in_b[l], seq)
        pc = _in_projection(cs, n1g, csc1, csh1, w_in_b[l], batch * clen)

        na = a_width // LANE
        nbq = b_width // LANE
        gains = jnp.stack([tile2(qk_gain[l, 0]) * scale] * na + [tile2(qk_gain[l, 1])] * na
                          + [tile2(qk_gain[l, 2]) * scale] * nbq + [tile2(qk_gain[l, 3])])
        qk = _qk_prep(p, n_qa // LANE, gains, cos_t, sin_t, seq, True)
        qk_c = _qk_prep(pc, n_qa // LANE, gains, cos_t, sin_t, clen, False)
        kb_blk = 2 * na + nbq
        qb_blk = (n_qb - n_qa) // b_width

        lam_p = diff_lam[l].astype(F32)
        lambda_init = 0.8 - 0.6 * math.exp(-0.3 * l)
        lam = jnp.exp(jnp.sum(lam_p[0] * lam_p[1])) - jnp.exp(jnp.sum(lam_p[2] * lam_p[3])) + lambda_init
        lam_v = jnp.full((1, LANE), lam, F32)
        subg = diff_subln[l].reshape(1, LANE)
        sink_tab = jnp.broadcast_to(win_sink[l].astype(F32)[:, None], (b_heads, LANE))
        hy = (hy_conv_w[l], hy_conv_b[l], hy_w1[l], hy_b1[l], hy_w2[l], hy_b2[l], hy_w3[l], hy_freq[l],
              hy_bias[l])

        o_a = _diff_attention(qk, seq, qk_c, pc, clen, qk, p, seq, lam_v, subg, 1.0 - lambda_init,
                              a_heads, na, n_va // LANE)
        o_b = _window_attention(qk, seq, qk_c, pc, clen, p, sink_tab, b_heads, b_kv,
                                qb_blk, kb_blk, n_vb // LANE, True)
        o_c = _pool_mixer(p, n_pool, c_width, seq, pool_w_b[l], pool_scale[l].reshape(1, c_width))
        o_d = _hyena_mixer(p, n_hy, 3 * d_width, batch, seq, *hy)
        xs = _out_projection(xs, (o_a, o_b, o_c, o_d), w_out_b[l], g1, seq)
        xs = _conv_ffn(xs, n2g, sc2, sh2, g2, ffn_in_b[l], ffn_conv_w[l], ffn_conv_b[l], ffn_out_b[l],
                       seq, seq)

        if not last:
            o_a_c = _diff_attention(qk_c, clen, qk_c, pc, clen, None, None, 0, lam_v, subg,
                                    1.0 - lambda_init, a_heads, na, n_va // LANE)
            o_b_c = _window_attention(qk_c, clen, qk_c, pc, clen, None, sink_tab, b_heads, b_kv,
                                      qb_blk, kb_blk, n_vb // LANE, False)
            o_c_c = _pool_mixer(pc, n_pool, c_width, clen, pool_w_b[l], pool_scale[l].reshape(1, c_width))
            o_d_c = _hyena_mixer(pc, n_hy, 3 * d_width, batch, clen, *hy)
            cs = _out_projection(cs, (o_a_c, o_b_c, o_c_c, o_d_c), w_out_b[l], cg1, batch * clen)
            cs = _conv_ffn(cs, n2g, csc2, csh2, cg2, ffn_in_b[l], ffn_conv_w[l], ffn_conv_b[l],
                           ffn_out_b[l], clen, batch * clen)
    return xs.reshape(batch, seq, d)
```

```python
import functools
import math

import jax
import jax.numpy as jnp
from jax import lax
from jax.experimental import pallas as pl
from jax.experimental.pallas import tpu as pltpu

F32 = jnp.float32
BF16 = jnp.bfloat16

HEAD_DIM = 64
GRID_W = 64
ROPE_BASE = 10000.0
EPS = 1e-6
NEG_INF = -1e30
WINDOW = 128
BLOCK = 128
POOL_WINDOWS = (2, 4, 8, 16)
HY_EMB = 33
HY_HIDDEN = 64
HY_FAST_DECAY = 0.3
HY_SLOW_DECAY = 1.5
HY_TARGET = 1e-2

LANE = 128
SUBLANE = 8
HALO = SUBLANE
DFT_RADIX = 128
ATTN_SLAB = 256
VMEM_LIMIT = 56 * 1024 * 1024


def _cp(*sem):
    return pltpu.CompilerParams(dimension_semantics=sem, vmem_limit_bytes=VMEM_LIMIT)


def _tile(n, pref):
    if n <= pref:
        return n
    t = pref
    while n % t:
        t //= 2
    assert t >= SUBLANE, (n, pref)
    return t


def _dot(a, b):
    return jnp.dot(a, b, preferred_element_type=F32)


def _dot_nt(a, b):
    return lax.dot_general(a, b, (((1,), (1,)), ((), ())), preferred_element_type=F32)


def _split_bf16(x):
    hi = x.astype(BF16)
    lo = (x - hi.astype(F32)).astype(BF16)
    return hi, lo


def _dot3(a_hi, a_lo, b):
    b_hi, b_lo = _split_bf16(b)
    return _dot(a_hi, b_hi) + _dot(a_hi, b_lo) + _dot(a_lo, b_hi)


def _mod_kernel(c_ref, w_ref, b_ref, o_ref):
    c = c_ref[...]
    a = (c * jax.nn.sigmoid(c)).astype(BF16)
    o_ref[...] = _dot(a, w_ref[...].astype(BF16)) + b_ref[...]


def _modulation(cvec, w_mod, b_mod):
    depth, d, n = w_mod.shape
    rows = cvec.shape[0]
    tn = _tile(n, 1024)
    return pl.pallas_call(
        _mod_kernel,
        grid=(depth, n // tn),
        in_specs=[
            pl.BlockSpec((rows, d), lambda l, j: (0, 0)),
            pl.BlockSpec((None, d, tn), lambda l, j: (l, 0, j)),
            pl.BlockSpec((None, 1, tn), lambda l, j: (l, 0, j)),
        ],
        out_specs=pl.BlockSpec((None, rows, tn), lambda l, j: (l, 0, j)),
        out_shape=jax.ShapeDtypeStruct((depth, rows, n), F32),
        compiler_params=_cp("parallel", "parallel"),
        name="adaln_mod",
    )(cvec, w_mod, b_mod.reshape(depth, 1, n))


def _norm_mod(x, g, sc, sh):
    ms = jnp.mean(x * x, axis=-1, keepdims=True)
    return (x * lax.rsqrt(ms + EPS) * g) * (1.0 + sc) + sh


def _inproj_kernel(x_ref, g_ref, sc_ref, sh_ref, w_ref, o_ref):
    h = _norm_mod(x_ref[...], g_ref[...], sc_ref[...], sh_ref[...])
    o_ref[...] = _dot(h.astype(BF16), w_ref[...])


def _in_projection(x2d, g, sc, sh, w, rows_per_group):
    n, d = x2d.shape
    nc = w.shape[1]
    tm = _tile(rows_per_group, 512)
    tpg = rows_per_group // tm
    ncol = 2 if (nc // 2) % LANE == 0 else 1
    tn = nc // ncol
    vec = pl.BlockSpec((None, 1, d), lambda j, i: (i // tpg, 0, 0))
    return pl.pallas_call(
        _inproj_kernel,
        grid=(ncol, n // tm),
        in_specs=[
            pl.BlockSpec((tm, d), lambda j, i: (i, 0)),
            pl.BlockSpec((1, d), lambda j, i: (0, 0)),
            vec, vec,
            pl.BlockSpec((d, tn), lambda j, i: (0, j)),
        ],
        out_specs=pl.BlockSpec((tm, tn), lambda j, i: (i, j)),
        out_shape=jax.ShapeDtypeStruct((n, nc), F32),
        compiler_params=_cp("parallel", "parallel"),
        name="in_proj",
    )(x2d, g, sc, sh, w)


def _qkprep_kernel(rope, x_ref, gain_ref, cos_ref, sin_ref, gm_ref, o_ref):
    x = x_ref[...]
    hi, lo = _split_bf16(x * x)
    gm = gm_ref[...]
    ms = _dot(hi, gm) + _dot(lo, gm)
    y = x * lax.rsqrt(ms + EPS) * gain_ref[...]
    if rope:
        lane = lax.broadcasted_iota(jnp.int32, y.shape, 1)
        first_half = (lane & 31) < 16
        partner = jnp.where(first_half, pltpu.roll(y, LANE - 16, 1), pltpu.roll(y, 16, 1))
        y = y * cos_ref[...] + partner * sin_ref[...]
    o_ref[...] = y.astype(BF16)


def _qk_prep(p, col_block0, gains, cos_t, sin_t, seq_len, rope):
    n = p.shape[0]
    nblk = gains.shape[0]
    tm = _tile(seq_len, 1024)
    tps = seq_len // tm
    lane = jnp.arange(LANE)
    gm = ((lane[:, None] // HEAD_DIM) == (lane[None, :] // HEAD_DIM)).astype(BF16) * (1.0 / HEAD_DIM)
    tab = pl.BlockSpec((tm, LANE), lambda i, j: (i % tps, 0))
    return pl.pallas_call(
        functools.partial(_qkprep_kernel, rope),
        grid=(n // tm, nblk),
        in_specs=[
            pl.BlockSpec((tm, LANE), lambda i, j: (i, col_block0 + j)),
            pl.BlockSpec((None, 1, LANE), lambda i, j: (j, 0, 0)),
            tab, tab,
            pl.BlockSpec((LANE, LANE), lambda i, j: (0, 0)),
        ],
        out_specs=pl.BlockSpec((tm, LANE), lambda i, j: (i, j)),
        out_shape=jax.ShapeDtypeStruct((n, nblk * LANE), BF16),
        compiler_params=_cp("parallel", "parallel"),
        name="qk_prep",
    )(p, gains.reshape(nblk, 1, LANE), cos_t, sin_t, gm.astype(BF16))


def _rope_tables(seq_len):
    rows = seq_len // GRID_W
    row = jnp.repeat(jnp.arange(rows, dtype=F32), GRID_W)
    col = jnp.tile(jnp.arange(GRID_W, dtype=F32), rows)
    half = HEAD_DIM // 2
    inv = ROPE_BASE ** (-jnp.arange(0, half, 2, dtype=F32) / half)
    ang = jnp.concatenate([row[:, None] * inv, col[:, None] * inv], axis=-1)
    lane = jnp.arange(LANE)
    within = lane % HEAD_DIM
    idx = (within // 32) * 16 + (within % 16)
    sign = jnp.where((lane % 32) < 16, -1.0, 1.0).astype(F32)
    return jnp.cos(ang)[:, idx], jnp.sin(ang)[:, idx] * sign


def _vt_kernel(x_ref, o_ref):
    o_ref[...] = x_ref[...].T.astype(BF16)


def _transposed_values(p, col_block0, nblk, seq_len, chunk):
    n = p.shape[0]
    return pl.pallas_call(
        _vt_kernel,
        grid=(n // chunk, nblk),
        in_specs=[pl.BlockSpec((chunk, LANE), lambda i, j: (i, col_block0 + j))],
        out_specs=pl.BlockSpec((None, None, LANE, chunk), lambda i, j: (j, i, 0, 0)),
        out_shape=jax.ShapeDtypeStruct((nblk, n // chunk, LANE, chunk), BF16),
        compiler_params=_cp("parallel", "parallel"),
        name="values_t",
    )(p)


def _attn_a_kernel(n_lat, tk, sub_scale, *refs):
    if n_lat:
        q_ref, kc_ref, vc_ref, k_ref, v_ref, lam_ref, subg_ref, o_ref, qt, m, l, acc = refs
    else:
        q_ref, kc_ref, vc_ref, lam_ref, subg_ref, o_ref, qt, m, l, acc = refs
    tq = q_ref.shape[0]
    q_t = q_ref[...].astype(F32).T
    row = lax.broadcasted_iota(jnp.int32, q_t.shape, 0)
    qt[:, :tq] = jnp.where(row < HEAD_DIM, q_t, 0.0).astype(BF16)
    qt[:, tq:] = jnp.where(row >= HEAD_DIM, q_t, 0.0).astype(BF16)
    m[...] = jnp.full(m.shape, -jnp.inf, F32)
    l[...] = jnp.zeros(l.shape, F32)
    acc[...] = jnp.zeros(acc.shape, F32)

    def step(k, v_t):
        s = _dot(k, qt[...])
        m_prev = m[...]
        m_new = jnp.maximum(m_prev, jnp.max(s, axis=0, keepdims=True))
        alpha = jnp.exp(m_prev - m_new)
        p = jnp.exp(s - m_new)
        l[...] = alpha * l[...] + jnp.sum(p, axis=0, keepdims=True)
        acc[...] = alpha * acc[...] + _dot(v_t, p.astype(BF16))
        m[...] = m_new

    step(kc_ref[...], vc_ref[0])
    if n_lat:
        def body(c, carry):
            step(k_ref[pl.ds(pl.multiple_of(c * tk, tk), tk), :], v_ref[c])
            return carry
        lax.fori_loop(0, n_lat, body, 0, unroll=2)

    o_t = acc[...] / l[...]
    o = (o_t[:, :tq] - lam_ref[0:1, 0:1] * o_t[:, tq:]).T
    ms = jnp.mean(o * o, axis=-1, keepdims=True)
    o_ref[...] = ((o * lax.rsqrt(ms + EPS) * subg_ref[...]) * sub_scale).astype(BF16)


def _diff_attention(q_arr, q_len, qk_ctx, vt_ctx, ctx_len, qk_lat, vt_lat, lat_len, lam, subg,
                    sub_scale, heads, k_col0):
    batch = q_arr.shape[0] // q_len
    tq = _tile(q_len, 512)
    nq = q_len // tq
    in_specs = [
        pl.BlockSpec((tq, LANE), lambda b, h, i: (b * nq + i, h)),
        pl.BlockSpec((ctx_len, LANE), lambda b, h, i: (b, k_col0 + h)),
        pl.BlockSpec((None, 1, LANE, ctx_len), lambda b, h, i: (h, b, 0, 0)),
    ]
    args = [q_arr, qk_ctx, vt_ctx]
    n_lat, tk = 0, 0
    if qk_lat is not None:
        tk = vt_lat.shape[-1]
        n_lat = lat_len // tk
        in_specs += [
            pl.BlockSpec((lat_len, LANE), lambda b, h, i: (b, k_col0 + h)),
            pl.BlockSpec((None, n_lat, LANE, tk), lambda b, h, i: (h, b, 0, 0)),
        ]
        args += [qk_lat, vt_lat]
    in_specs += [pl.BlockSpec((1, LANE), lambda b, h, i: (0, 0))] * 2
    args += [lam, subg]
    return pl.pallas_call(
        functools.partial(_attn_a_kernel, n_lat, tk, sub_scale),
        grid=(batch, heads, nq),
        in_specs=in_specs,
        out_specs=pl.BlockSpec((tq, LANE), lambda b, h, i: (b * nq + i, h)),
        out_shape=jax.ShapeDtypeStruct((batch * q_len, heads * LANE), BF16),
        scratch_shapes=[pltpu.VMEM((LANE, 2 * tq), BF16), pltpu.VMEM((1, 2 * tq), F32),
                        pltpu.VMEM((1, 2 * tq), F32), pltpu.VMEM((LANE, 2 * tq), F32)],
        compiler_params=_cp("parallel", "parallel", "parallel"),
        name="diff_attn",
    )(*args)


def _attn_b_kernel(has_loc, seq_len, q_heads, group, *refs):
    if has_loc:
        (q_ref, kx_ref, vx_ref, kp_ref, kc_ref, kn_ref, vp_ref, vc_ref, vn_ref, sink_ref, o_ref) = refs
    else:
        (q_ref, kx_ref, vx_ref, sink_ref, o_ref) = refs
    n = pl.program_id(1)
    q = q_ref[...]
    lane = lax.broadcasted_iota(jnp.int32, (BLOCK, LANE), 1)
    low = lane < HEAD_DIM
    kx = kx_ref[...]
    kx_sw = pltpu.roll(kx, HEAD_DIM, 1)
    vx = vx_ref[...].astype(BF16)
    if has_loc:
        kb = jnp.concatenate([kp_ref[...], kc_ref[...], kn_ref[...]], axis=0)
        kb_sw = pltpu.roll(kb, HEAD_DIM, 1)
        vb = jnp.concatenate([vp_ref[...], vc_ref[...], vn_ref[...]], axis=0).astype(BF16)
        kpos = (n - 1) * BLOCK + lax.broadcasted_iota(jnp.int32, (BLOCK, 3 * BLOCK), 1)
        qpos = n * BLOCK + lax.broadcasted_iota(jnp.int32, (BLOCK, 3 * BLOCK), 0)
        valid = (kpos >= 0) & (kpos < seq_len) & (jnp.abs(kpos - qpos) <= WINDOW)
    outs = []
    for blk in range(q_heads // 2):
        q128 = q[:, blk * LANE:(blk + 1) * LANE]
        res = []
        for half in range(2):
            hq = 2 * blk + half
            kvh = hq // group
            same = (kvh % 2) == half
            qm = jnp.where(low if half == 0 else jnp.logical_not(low), q128, jnp.zeros_like(q128))
            sink = sink_ref[hq:hq + 1, 0:1]
            s_ctx = _dot_nt(qm, kx if same else kx_sw)
            m = jnp.maximum(jnp.max(s_ctx, axis=-1, keepdims=True), sink)
            if has_loc:
                s_loc = jnp.where(valid, _dot_nt(qm, kb if same else kb_sw), NEG_INF)
                m = jnp.maximum(m, jnp.max(s_loc, axis=-1, keepdims=True))
            p_ctx = jnp.exp(s_ctx - m)
            den = jnp.exp(sink - m) + jnp.sum(p_ctx, axis=-1, keepdims=True)
            o = _dot(p_ctx.astype(BF16), vx)
            if has_loc:
                p_loc = jnp.exp(s_loc - m)
                den = den + jnp.sum(p_loc, axis=-1, keepdims=True)
                o = o + _dot(p_loc.astype(BF16), vb)
            o = o / den
            res.append(o if same else pltpu.roll(o, HEAD_DIM, 1))
        outs.append(jnp.where(low, res[0], res[1]))
    o_ref[...] = jnp.concatenate(outs, axis=1).astype(BF16)


def _window_attention(q_arr, q_len, qk_ctx, p_ctx, ctx_len, p_lat, sink_tab, q_heads, kv_heads,
                      q_blk, k_blk, v_blk, local):
    assert kv_heads == 2, "key/value heads must fill exactly one 128-lane block"
    batch = q_arr.shape[0] // q_len
    nb = q_len // BLOCK
    qw = q_heads * HEAD_DIM
    in_specs = [
        pl.BlockSpec((BLOCK, qw), lambda b, n: (b * nb + n, q_blk)),
        pl.BlockSpec((ctx_len, LANE), lambda b, n: (b, k_blk)),
        pl.BlockSpec((ctx_len, LANE), lambda b, n: (b, v_blk)),
    ]
    args = [q_arr, qk_ctx, p_ctx]
    if local:
        prev = lambda b, n: b * nb + jnp.maximum(n - 1, 0)
        nxt = lambda b, n: b * nb + jnp.minimum(n + 1, nb - 1)
        for col, arr in ((k_blk, q_arr), (v_blk, p_lat)):
            in_specs += [
                pl.BlockSpec((BLOCK, LANE), lambda b, n, col=col: (prev(b, n), col)),
                pl.BlockSpec((BLOCK, LANE), lambda b, n, col=col: (b * nb + n, col)),
                pl.BlockSpec((BLOCK, LANE), lambda b, n, col=col: (nxt(b, n), col)),
            ]
            args += [arr, arr, arr]
    in_specs.append(pl.BlockSpec((q_heads, LANE), lambda b, n: (0, 0)))
    args.append(sink_tab)
    return pl.pallas_call(
        functools.partial(_attn_b_kernel, local, q_len, q_heads, q_heads // kv_heads),
        grid=(batch, nb),
        in_specs=in_specs,
        out_specs=pl.BlockSpec((BLOCK, qw), lambda b, n: (b * nb + n, 0)),
        out_shape=jax.ShapeDtypeStruct((batch * q_len, qw), BF16),
        compiler_params=_cp("parallel", "parallel"),
        name="window_attn",
    )(*args)


def _halo_specs(tile, width, col_blk, n_rows):
    r = tile // HALO
    last = n_rows // HALO - 1
    return [
        pl.BlockSpec((tile, width), lambda i: (i, col_blk)),
        pl.BlockSpec((HALO, width), lambda i: (jnp.maximum(i * r - 1, 0), col_blk)),
        pl.BlockSpec((HALO, width), lambda i: (jnp.minimum(i * r + r, last), col_blk)),
    ]


def _outside_rows(tile, tile_idx, tiles_per_seq):
    row = lax.broadcasted_iota(jnp.int32, (tile + 2 * HALO, 1), 0)
    lo = jnp.where(tile_idx == 0, HALO, 0)
    hi = jnp.where(tile_idx == tiles_per_seq - 1, HALO + tile, tile + 2 * HALO)
    return (row < lo) | (row >= hi)


def _with_halo(x, xp, xn, tiles_per_seq):
    ext = jnp.concatenate([xp, x, xn], axis=0)
    outside = _outside_rows(x.shape[0], pl.program_id(0) % tiles_per_seq, tiles_per_seq)
    return jnp.where(outside, 0.0, ext)


def _conv3(ext, w, b, tile):
    rows = ext.shape[0]
    c = (pltpu.roll(ext, 1, 0) * w[0:1] + ext * w[1:2] + pltpu.roll(ext, rows - 1, 0) * w[2:3])
    return c[HALO:HALO + tile] + b


def _pool_kernel(seq_len, tile, x_ref, xp_ref, xn_ref, w_ref, ls_ref, o_ref):
    tps = seq_len // tile
    x = x_ref[...]
    ext = _with_halo(x, xp_ref[...], xn_ref[...], tps)
    rows = tile + 2 * HALO
    t = (pl.program_id(0) % tps) * tile + lax.broadcasted_iota(jnp.int32, (tile, 1), 0)
    gw = x.shape[1] // len(POOL_WINDOWS)
    outs = []
    for g, w in enumerate(POOL_WINDOWS):
        lo = w // 2
        hi = w - 1 - lo
        a = ext[:, g * gw:(g + 1) * gw]
        span = 1
        while span < w:
            a = a + pltpu.roll(a, span, 0)
            span *= 2
        win = (pltpu.roll(a, rows - hi, 0) if hi else a)[HALO:HALO + tile]
        cnt = jnp.minimum(t + hi + 1, seq_len) - jnp.maximum(t - lo, 0)
        d = win / cnt.astype(F32) - x[:, g * gw:(g + 1) * gw]
        outs.append(_dot(d.astype(BF16), w_ref[g]))
    o_ref[...] = (jnp.concatenate(outs, axis=1) * ls_ref[...]).astype(BF16)


def _pool_mixer(p, col0, width, seq_len, w_lin, ls):
    n = p.shape[0]
    tile = _tile(seq_len, 512)
    g, gw, _ = w_lin.shape
    return pl.pallas_call(
        functools.partial(_pool_kernel, seq_len, tile),
        grid=(n // tile,),
        in_specs=_halo_specs(tile, width, col0 // width, n)
        + [pl.BlockSpec((g, gw, gw), lambda i: (0, 0, 0)),
           pl.BlockSpec((1, width), lambda i: (0, 0))],
        out_specs=pl.BlockSpec((tile, width), lambda i: (i, 0)),
        out_shape=jax.ShapeDtypeStruct((n, width), BF16),
        compiler_params=_cp("parallel"),
        name="pool_mixer",
    )(p, p, p, w_lin, ls)


def _hypre_kernel(seq_len, tile, x_ref, xp_ref, xn_ref, cw_ref, cb_ref, z_ref, x2_ref):
    ext = _with_halo(x_ref[...], xp_ref[...], xn_ref[...], seq_len // tile)
    c = _conv3(ext, cw_ref[...], cb_ref[...], tile)
    dw = c.shape[1] // 3
    z_ref[...] = c[:, :dw] * c[:, dw:2 * dw]
    x2_ref[...] = c[:, 2 * dw:]


def _hyena_pre(p, col0, width, seq_len, conv_w, conv_b):
    n = p.shape[0]
    tile = _tile(seq_len, 512)
    dw = width // 3
    out = jax.ShapeDtypeStruct((n, dw), F32)
    return pl.pallas_call(
        functools.partial(_hypre_kernel, seq_len, tile),
        grid=(n // tile,),
        in_specs=_halo_specs(tile, width, col0 // width, n)
        + [pl.BlockSpec((3, width), lambda i: (0, 0)),
           pl.BlockSpec((1, width), lambda i: (0, 0))],
        out_specs=[pl.BlockSpec((tile, dw), lambda i: (i, 0))] * 2,
        out_shape=[out, out],
        compiler_params=_cp("parallel"),
        name="hyena_pre",
    )(p, p, p, conv_w, conv_b.reshape(1, width))


def _hyfilt_kernel(seq_len, tile, zf_ref, w1_ref, b1_ref, w2_ref, b2_ref, w3_ref, fr_ref, dl_ref,
                   k_ref, ss_ref):
    i = pl.program_id(0)
    hp = functools.partial(jnp.dot, precision=lax.Precision.HIGHEST, preferred_element_type=F32)
    zf = zf_ref[...]
    fr = fr_ref[...]
    h = jnp.sin(fr * (hp(zf, w1_ref[...]) + b1_ref[...]))
    h = jnp.sin(fr * (hp(h, w2_ref[...]) + b2_ref[...]))
    k = hp(h, w3_ref[...]) * jnp.exp(-zf[:, 0:1] * dl_ref[...])
    row = i * tile + lax.broadcasted_iota(jnp.int32, (tile, 1), 0)
    k = jnp.where(row == seq_len, 0.0, k)
    k_ref[...] = k

    @pl.when(i == 0)
    def _():
        ss_ref[...] = jnp.zeros(ss_ref.shape, F32)

    ss_ref[...] += jnp.sum(k * k, axis=0, keepdims=True)


def _hyena_filter(seq_len, w1, b1, w2, b2, w3, freq):
    dw = w3.shape[1] // 2
    bands = (HY_EMB - 1) // 2
    t01 = jnp.linspace(0.0, 1.0, seq_len, dtype=F32)[:, None]
    w_ang = 2.0 * math.pi * jnp.arange(seq_len, dtype=F32)[:, None] / seq_len
    f = jnp.linspace(1e-4, bands - 1, bands, dtype=F32)[None, :]
    z = jnp.concatenate([t01, jnp.cos(f * w_ang), -jnp.sin(f * w_ang)], axis=-1)
    rev = (seq_len - jnp.arange(seq_len)) % seq_len
    zf = jnp.pad(jnp.concatenate([z, z[rev]], axis=0), ((0, 0), (0, LANE - HY_EMB)))
    pad_h = LANE - HY_HIDDEN
    w1p = jnp.pad(w1, ((0, LANE - HY_EMB), (0, pad_h)))
    w2p = jnp.pad(w2, ((0, pad_h), (0, pad_h)))
    w3p = jnp.pad(w3, ((0, pad_h), (0, 0)))
    vec = lambda v: jnp.pad(v, (0, pad_h)).reshape(1, LANE)
    deltas = jnp.abs(jnp.linspace(math.log(HY_TARGET) / HY_FAST_DECAY, math.log(HY_TARGET) / HY_SLOW_DECAY,
                                  dw, dtype=F32)).reshape(1, dw)
    tile = _tile(seq_len, 1024)
    tps = seq_len // tile
    full = lambda shape: pl.BlockSpec(shape, lambda i: (0, 0))
    return pl.pallas_call(
        functools.partial(_hyfilt_kernel, seq_len, tile),
        grid=(2 * tps,),
        in_specs=[pl.BlockSpec((tile, LANE), lambda i: (i, 0)),
                  full((LANE, LANE)), full((1, LANE)), full((LANE, LANE)), full((1, LANE)),
                  pl.BlockSpec((LANE, dw), lambda i: (0, i // tps)),
                  full((1, LANE)), full((1, dw))],
        out_specs=[pl.BlockSpec((tile, dw), lambda i: (i, 0)), full((1, dw))],
        out_shape=[jax.ShapeDtypeStruct((2 * seq_len, dw), F32), jax.ShapeDtypeStruct((1, dw), F32)],
        compiler_params=_cp("arbitrary"),
        name="hyena_filter",
    )(zf, w1p, vec(b1), w2p, vec(b2), w3p, vec(freq), deltas)


def _dft_tables(n1, n2):
    n = n1 * n2
    two_pi = 2.0 * math.pi

    def cs(num, den):
        ang = (num % den).astype(F32) * (two_pi / den)
        return jnp.cos(ang), jnp.sin(ang)

    a = jnp.arange(n1)
    c1, s1 = cs(a[:, None] * a[None, :], n1)
    fwd_a = jnp.concatenate([c1, -s1], axis=0)
    inv_a = jnp.concatenate([c1, -s1], axis=1)[: n1 // 2] / n
    k2 = jnp.arange(n2)
    num = (k2[None, :, None] * k2[None, None, :] * n1 + a[:, None, None] * k2[None, None, :])
    cb, sb = cs(num, n)
    fwd_b = jnp.concatenate([jnp.concatenate([cb, sb], axis=2),
                             jnp.concatenate([-sb, cb], axis=2)], axis=1)
    cbt, sbt = jnp.swapaxes(cb, 1, 2), jnp.swapaxes(sb, 1, 2)
    inv_b = jnp.concatenate([jnp.concatenate([cbt, -sbt], axis=2),
                             jnp.concatenate([sbt, cbt], axis=2)], axis=1)
    return tuple(_split_bf16(m) for m in (fwd_a, inv_a, fwd_b, inv_b))


def _dft_a_kernel(fh_ref, fl_ref, x_ref, re_ref, im_ref):
    y = _dot3(fh_ref[...], fl_ref[...], x_ref[...])
    half = y.shape[0] // 2
    re_ref[...] = y[:half]
    im_ref[...] = y[half:]


def _dft_stage_a(xv, fwd_a, n1):
    b, k, cols = xv.shape
    fh, fl = fwd_a
    ct = _tile(cols, 2048)
    out = jax.ShapeDtypeStruct((b, n1, cols), F32)
    return pl.pallas_call(
        _dft_a_kernel,
        grid=(b, cols // ct),
        in_specs=[pl.BlockSpec((2 * n1, k), lambda bi, j: (0, 0))] * 2
        + [pl.BlockSpec((None, k, ct), lambda bi, j: (bi, 0, j))],
        out_specs=[pl.BlockSpec((None, n1, ct), lambda bi, j: (bi, 0, j))] * 2,
        out_shape=[out, out],
        compiler_params=_cp("parallel", "parallel"),
        name="dft_stage_a",
    )(fh[:, :k], fl[:, :k], xv)


def _dft_b_kernel(mh_ref, ml_ref, re_ref, im_ref, ss_ref, kr_ref, ki_ref):
    x = _dot3(mh_ref[...], ml_ref[...], jnp.concatenate([re_ref[...], im_ref[...]], axis=0))
    nrm = lax.rsqrt(ss_ref[...] + EPS)
    half = x.shape[0] // 2
    kr_ref[...] = x[:half] * nrm
    ki_ref[...] = x[half:] * nrm


def _filter_spectrum(kern, sumsq, tabs, n1, n2):
    fwd_a, _, fwd_b, _ = tabs
    ch = kern.shape[1]
    are, aim = _dft_stage_a(kern.reshape(1, n1, n2 * ch), fwd_a, n1)
    are, aim = are.reshape(n1 * n2, ch), aim.reshape(n1 * n2, ch)
    blk = pl.BlockSpec((n2, ch), lambda i: (i, 0))
    mat = pl.BlockSpec((None, 2 * n2, 2 * n2), lambda i: (i, 0, 0))
    out = jax.ShapeDtypeStruct((n1 * n2, ch), F32)
    return pl.pallas_call(
        _dft_b_kernel,
        grid=(n1,),
        in_specs=[mat, mat, blk, blk, pl.BlockSpec((1, ch), lambda i: (0, 0))],
        out_specs=[blk, blk],
        out_shape=[out, out],
        compiler_params=_cp("parallel"),
        name="filter_spectrum",
    )(fwd_b[0], fwd_b[1], are, aim, sumsq)


def _dft_mul_kernel(fh_ref, fl_ref, gh_ref, gl_ref, re_ref, im_ref, kr_ref, ki_ref, qr_ref, qi_ref):
    x = _dot3(fh_ref[...], fl_ref[...], jnp.concatenate([re_ref[...], im_ref[...]], axis=0))
    half = x.shape[0] // 2
    xr, xi = x[:half], x[half:]
    kr, ki = kr_ref[...], ki_ref[...]
    prod = jnp.concatenate([xr * kr - xi * ki, xr * ki + xi * kr], axis=0)
    q = _dot3(gh_ref[...], gl_ref[...], prod)
    qr_ref[...] = q[:half]
    qi_ref[...] = q[half:]


def _dft_inv_kernel(gh_ref, gl_ref, qr_ref, qi_ref, z_ref, x2_ref, bias_ref, o_ref):
    y = _dot3(gh_ref[...], gl_ref[...], jnp.concatenate([qr_ref[...], qi_ref[...]], axis=0))
    z = z_ref[...]
    o_ref[...] = (x2_ref[...] * (y + z * bias_ref[...])).astype(BF16)


def _hyena_long(z, x2, kr, ki, bias, tabs, batch, seq_len, n1, n2):
    fwd_a, inv_a, fwd_b, inv_b = tabs
    ch = z.shape[1]
    cols = n2 * ch
    zv = z.reshape(batch, n1 // 2, cols)
    are, aim = _dft_stage_a(zv, fwd_a, n1)
    are, aim = are.reshape(batch, n1 * n2, ch), aim.reshape(batch, n1 * n2, ch)
    mat = pl.BlockSpec((None, 2 * n2, 2 * n2), lambda i, b: (i, 0, 0))
    dat = pl.BlockSpec((None, n2, ch), lambda i, b: (b, i, 0))
    spec = pl.BlockSpec((n2, ch), lambda i, b: (i, 0))
    out = jax.ShapeDtypeStruct((batch, n1 * n2, ch), F32)
    qr, qi = pl.pallas_call(
        _dft_mul_kernel,
        grid=(n1, batch),
        in_specs=[mat, mat, mat, mat, dat, dat, spec, spec],
        out_specs=[dat, dat],
        out_shape=[out, out],
        compiler_params=_cp("parallel", "parallel"),
        name="dft_filter_mul",
    )(fwd_b[0], fwd_b[1], inv_b[0], inv_b[1], are, aim, kr, ki)
    qr, qi = qr.reshape(batch, n1, cols), qi.reshape(batch, n1, cols)
    ct = _tile(cols, 2048)
    full = pl.BlockSpec((None, n1, ct), lambda b, j: (b, 0, j))
    half = pl.BlockSpec((None, n1 // 2, ct), lambda b, j: (b, 0, j))
    gmat = pl.BlockSpec((n1 // 2, 2 * n1), lambda b, j: (0, 0))
    o = pl.pallas_call(
        _dft_inv_kernel,
        grid=(batch, cols // ct),
        in_specs=[gmat, gmat, full, full, half, half, pl.BlockSpec((1, ct), lambda b, j: (0, j))],
        out_specs=half,
        out_shape=jax.ShapeDtypeStruct((batch, n1 // 2, cols), BF16),
        compiler_params=_cp("parallel", "parallel"),
        name="dft_inverse_gate",
    )(inv_a[0], inv_a[1], qr, qi, zv, x2.reshape(batch, n1 // 2, cols), jnp.tile(bias, n2).reshape(1, cols))
    return o.reshape(batch * seq_len, ch)


def _hyena_short_kernel(fh_ref, fl_ref, gh_ref, gl_ref, k_ref, ss_ref, z_ref, x2_ref, bias_ref, o_ref):
    fh, fl = fh_ref[...], fl_ref[...]
    z = z_ref[...]
    seq = z.shape[0]
    ks = _dot3(fh, fl, k_ref[...]) * lax.rsqrt(ss_ref[...] + EPS)
    zs = _dot3(fh[:, :seq], fl[:, :seq], z)
    n = ks.shape[0] // 2
    kr, ki, zr, zi = ks[:n], ks[n:], zs[:n], zs[n:]
    prod = jnp.concatenate([zr * kr - zi * ki, zr * ki + zi * kr], axis=0)
    y = _dot3(gh_ref[...], gl_ref[...], prod)
    o_ref[...] = (x2_ref[...] * (y + z * bias_ref[...])).astype(BF16)


def _hyena_short(z, x2, kern, sumsq, bias, batch, seq_len):
    n = 2 * seq_len
    ch = z.shape[1]
    a = jnp.arange(n)
    ang = ((a[:, None] * a[None, :]) % n).astype(F32) * (2.0 * math.pi / n)
    c, s = jnp.cos(ang), jnp.sin(ang)
    fh, fl = _split_bf16(jnp.concatenate([c, -s], axis=0))
    gh, gl = _split_bf16(jnp.concatenate([c, -s], axis=1)[:seq_len] / n)
    full = lambda shape: pl.BlockSpec(shape, lambda b: (0, 0))
    dat = pl.BlockSpec((seq_len, ch), lambda b: (b, 0))
    return pl.pallas_call(
        _hyena_short_kernel,
        grid=(batch,),
        in_specs=[full((2 * n, n)), full((2 * n, n)), full((seq_len, 2 * n)), full((seq_len, 2 * n)),
                  full((n, ch)), full((1, ch)), dat, dat, full((1, ch))],
        out_specs=dat,
        out_shape=jax.ShapeDtypeStruct((batch * seq_len, ch), BF16),
        compiler_params=_cp("parallel"),
        name="hyena_short_conv",
    )(fh, fl, gh, gl, kern, sumsq, z, x2, bias.reshape(1, ch))


def _hyena_mixer(p, col0, width, batch, seq_len, conv_w, conv_b, w1, b1, w2, b2, w3, freq, bias):
    z, x2 = _hyena_pre(p, col0, width, seq_len, conv_w, conv_b)
    kern, sumsq = _hyena_filter(seq_len, w1, b1, w2, b2, w3, freq)
    n = 2 * seq_len
    if n <= 4 * DFT_RADIX:
        return _hyena_short(z, x2, kern, sumsq, bias, batch, seq_len)
    n2 = DFT_RADIX
    n1 = n // n2
    tabs = _dft_tables(n1, n2)
    kr, ki = _filter_spectrum(kern, sumsq, tabs, n1, n2)
    return _hyena_long(z, x2, kr, ki, bias, tabs, batch, seq_len, n1, n2)


def _outproj_kernel(x_ref, a_ref, b_ref, c_ref, d_ref, w_ref, g_ref, o_ref):
    acc = None
    row = 0
    for m_ref in (a_ref, b_ref, c_ref, d_ref):
        wdt = m_ref.shape[1]
        part = _dot(m_ref[...], w_ref[row:row + wdt, :])
        acc = part if acc is None else acc + part
        row += wdt
    o_ref[...] = x_ref[...] + g_ref[...] * acc


def _out_projection(x2d, mixers, w, gate, rows_per_group):
    n, d = x2d.shape
    tm = _tile(rows_per_group, 512)
    tpg = rows_per_group // tm
    return pl.pallas_call(
        _outproj_kernel,
        grid=(n // tm,),
        in_specs=[pl.BlockSpec((tm, d), lambda i: (i, 0))]
        + [pl.BlockSpec((tm, m.shape[1]), lambda i: (i, 0)) for m in mixers]
        + [pl.BlockSpec(w.shape, lambda i: (0, 0)),
           pl.BlockSpec((None, 1, d), lambda i: (i // tpg, 0, 0))],
        out_specs=pl.BlockSpec((tm, d), lambda i: (i, 0)),
        out_shape=jax.ShapeDtypeStruct((n, d), F32),
        compiler_params=_cp("parallel"),
        name="out_proj",
    )(x2d, *mixers, w, gate)


def _ffn_kernel(seq_len, tile, x_ref, xp_ref, xn_ref, g_ref, sc_ref, sh_ref, gate_ref,
                wg_ref, wu_ref, cwg_ref, cwu_ref, cbg_ref, cbu_ref, wd_ref, o_ref, h_scr, acc):
    j = pl.program_id(1)

    @pl.when(j == 0)
    def _():
        ext = jnp.concatenate([xp_ref[...], x_ref[...], xn_ref[...]], axis=0)
        h = _norm_mod(ext, g_ref[...], sc_ref[...], sh_ref[...])
        tps = seq_len // tile
        outside = _outside_rows(tile, pl.program_id(0) % tps, tps)
        h_scr[...] = jnp.where(outside, 0.0, h).astype(BF16)
        acc[...] = jnp.zeros(acc.shape, F32)

    h = h_scr[...]
    gate = _conv3(_dot(h, wg_ref[...]), cwg_ref[...], cbg_ref[...], tile)
    up = _conv3(_dot(h, wu_ref[...]), cwu_ref[...], cbu_ref[...], tile)
    act = (gate * jax.nn.sigmoid(gate)) * up
    acc[...] += _dot(act.astype(BF16), wd_ref[...])

    @pl.when(j == pl.num_programs(1) - 1)
    def _():
        o_ref[...] = x_ref[...] + gate_ref[...] * acc[...]


def _conv_ffn(x2d, g, sc, sh, gate, w_up, conv_w, conv_b, w_down, seq_len, rows_per_group):
    n, d = x2d.shape
    f = w_down.shape[0]
    tile = _tile(seq_len, 512)
    hc = _tile(f, 512)
    nh = f // hc
    tpg = rows_per_group // tile
    r = tile // HALO
    last = n // HALO - 1
    vec = pl.BlockSpec((None, 1, d), lambda i, j: (i // tpg, 0, 0))
    cb = conv_b.reshape(1, 2 * f)
    return pl.pallas_call(
        functools.partial(_ffn_kernel, seq_len, tile),
        grid=(n // tile, nh),
        in_specs=[
            pl.BlockSpec((tile, d), lambda i, j: (i, 0)),
            pl.BlockSpec((HALO, d), lambda i, j: (jnp.maximum(i * r - 1, 0), 0)),
            pl.BlockSpec((HALO, d), lambda i, j: (jnp.minimum(i * r + r, last), 0)),
            pl.BlockSpec((1, d), lambda i, j: (0, 0)),
            vec, vec, vec,
            pl.BlockSpec((d, hc), lambda i, j: (0, j)),
            pl.BlockSpec((d, hc), lambda i, j: (0, nh + j)),
            pl.BlockSpec((3, hc), lambda i, j: (0, j)),
            pl.BlockSpec((3, hc), lambda i, j: (0, nh + j)),
            pl.BlockSpec((1, hc), lambda i, j: (0, j)),
            pl.BlockSpec((1, hc), lambda i, j: (0, nh + j)),
            pl.BlockSpec((hc, d), lambda i, j: (j, 0)),
        ],
        out_specs=pl.BlockSpec((tile, d), lambda i, j: (i, 0)),
        out_shape=jax.ShapeDtypeStruct((n, d), F32),
        scratch_shapes=[pltpu.VMEM((tile + 2 * HALO, d), BF16), pltpu.VMEM((tile, d), F32)],
        compiler_params=_cp("parallel", "arbitrary"),
        name="conv_ffn",
    )(x2d, x2d, x2d, g, sc, sh, gate, w_up, w_up, conv_w, conv_w, cb, cb, w_down)


def kernel(x, c, ctx, c_ctx, w_mod, b_mod, norm1_g, norm2_g, w_in, w_out, qk_gain, diff_lam, diff_subln,
           win_sink, pool_w, pool_scale, hy_conv_w, hy_conv_b, hy_w1, hy_b1, hy_w2, hy_b2, hy_w3, hy_freq,
           hy_bias, ffn_w_in, ffn_conv_w, ffn_conv_b, ffn_w_out):
    batch, seq, d = x.shape
    clen = ctx.shape[1]
    depth = w_in.shape[0]
    a_heads = d // 512
    a_width = a_heads * 2 * HEAD_DIM
    b_heads = d // 256
    b_kv = b_heads // 4
    b_width = b_heads * HEAD_DIM
    b_kvw = b_kv * HEAD_DIM
    c_width = d // 4
    d_width = d - a_width - b_width - c_width
    off_ka = a_width
    off_va = 2 * a_width
    off_qb = 3 * a_width
    off_kb = off_qb + b_width
    off_vb = off_kb + b_kvw
    off_pool = off_vb + b_kvw
    off_hy = off_pool + c_width
    in_cols = off_hy + 3 * d_width
    order = ((off_hy, in_cols), (off_pool, off_hy), (off_va, off_qb), (0, off_ka), (off_ka, off_va),
             (off_qb, off_kb), (off_kb, off_vb), (off_vb, off_pool))
    n_hy, n_pool, n_va, n_qa, n_ka, n_qb, n_kb, n_vb = (
        sum(e - s for s, e in order[:k]) for k in range(len(order)))
    assert a_width % LANE == 0 and b_width % LANE == 0 and b_kvw == LANE and c_width // 4 == LANE
    assert seq % BLOCK == 0 and clen % BLOCK == 0 and n_pool % c_width == 0
    assert (n_ka - n_qa) == a_width and (n_qb - n_qa) % b_width == 0 and n_kb - n_qb == b_width

    rows = -(-(batch + 1) // SUBLANE) * SUBLANE
    cvec = jnp.zeros((rows, d), F32).at[:batch].set(c).at[batch].set(c_ctx)
    mods = _modulation(cvec, w_mod, b_mod)

    w_in_b = jnp.concatenate([w_in[:, :, s:e] for s, e in order], axis=-1).astype(BF16)
    w_out_b = w_out.astype(BF16)
    ffn_in_b = ffn_w_in.astype(BF16)
    ffn_out_b = ffn_w_out.astype(BF16)
    pool_w_b = pool_w.astype(BF16)

    cos_t, sin_t = _rope_tables(seq)
    scale = HEAD_DIM ** -0.5
    tile2 = lambda v: jnp.tile(v, LANE // HEAD_DIM)

    xs = x.reshape(batch * seq, d)
    cs = ctx.reshape(batch * clen, d)
    for l in range(depth):
        last = l == depth - 1
        m = mods[l]
        sh1, sc1, g1, sh2, sc2, g2 = [m[:batch, i * d:(i + 1) * d].reshape(batch, 1, d) for i in range(6)]
        csh1, csc1, cg1, csh2, csc2, cg2 = [m[batch, i * d:(i + 1) * d].reshape(1, 1, d) for i in range(6)]
        n1g = norm1_g[l].reshape(1, d)
        n2g = norm2_g[l].reshape(1, d)

        p = _in_projection(xs, n1g, sc1, sh1, w_in_b[l], seq)
        pc = _in_projection(cs, n1g, csc1, csh1, w_in_b[l], batch * clen)

        na = a_width // LANE
        nbq = b_width // LANE
        gains = jnp.stack([tile2(qk_gain[l, 0]) * scale] * na + [tile2(qk_gain[l, 1])] * na
                          + [tile2(qk_gain[l, 2]) * scale] * nbq + [tile2(qk_gain[l, 3])])
        qk = _qk_prep(p, n_qa // LANE, gains, cos_t, sin_t, seq, True)
        qk_c = _qk_prep(pc, n_qa // LANE, gains, cos_t, sin_t, clen, False)
        kb_blk = 2 * na + nbq
        qb_blk = (n_qb - n_qa) // b_width

        lam_p = diff_lam[l].astype(F32)
        lambda_init = 0.8 - 0.6 * math.exp(-0.3 * l)
        lam = jnp.exp(jnp.sum(lam_p[0] * lam_p[1])) - jnp.exp(jnp.sum(lam_p[2] * lam_p[3])) + lambda_init
        lam_v = jnp.full((1, LANE), lam, F32)
        subg = diff_subln[l].reshape(1, LANE)
        sink_tab = jnp.broadcast_to(win_sink[l].astype(F32)[:, None], (b_heads, LANE))
        hy = (hy_conv_w[l], hy_conv_b[l], hy_w1[l], hy_b1[l], hy_w2[l], hy_b2[l], hy_w3[l], hy_freq[l],
              hy_bias[l])

        vt = _transposed_values(p, n_va // LANE, a_heads, seq, _tile(seq, 512))
        vt_c = _transposed_values(pc, n_va // LANE, a_heads, clen, clen)
        o_a = _diff_attention(qk, seq, qk_c, vt_c, clen, qk, vt, seq, lam_v, subg, 1.0 - lambda_init,
                              a_heads, na)
        o_b = _window_attention(qk, seq, qk_c, pc, clen, p, sink_tab, b_heads, b_kv,
                                qb_blk, kb_blk, n_vb // LANE, True)
        o_c = _pool_mixer(p, n_pool, c_width, seq, pool_w_b[l], pool_scale[l].reshape(1, c_width))
        o_d = _hyena_mixer(p, n_hy, 3 * d_width, batch, seq, *hy)
        xs = _out_projection(xs, (o_a, o_b, o_c, o_d), w_out_b[l], g1, seq)
        xs = _conv_ffn(xs, n2g, sc2, sh2, g2, ffn_in_b[l], ffn_conv_w[l], ffn_conv_b[l], ffn_out_b[l],
                       seq, seq)

        if not last:
            o_a_c = _diff_attention(qk_c, clen, qk_c, vt_c, clen, None, None, 0, lam_v, subg,
                                    1.0 - lambda_init, a_heads, na)
            o_b_c = _window_attention(qk_c, clen, qk_c, pc, clen, None, sink_tab, b_heads, b_kv,
                                      qb_blk, kb_blk, n_vb // LANE, False)
            o_c_c = _pool_mixer(pc, n_pool, c_width, clen, pool_w_b[l], pool_scale[l].reshape(1, c_width))
            o_d_c = _hyena_mixer(pc, n_hy, 3 * d_width, batch, clen, *hy)
            cs = _out_projection(cs, (o_a_c, o_b_c, o_c_c, o_d_c), w_out_b[l], cg1, batch * clen)
            cs = _conv_ffn(cs, n2g, csc2, csh2, cg2, ffn_in_b[l], ffn_conv_w[l], ffn_conv_b[l],
                           ffn_out_b[l], clen, batch * clen)
    return xs.reshape(batch, seq, d)
```

```python
import functools
import math

import jax
import jax.numpy as jnp
from jax import lax
from jax.experimental import pallas as pl
from jax.experimental.pallas import tpu as pltpu

F32 = jnp.float32
BF16 = jnp.bfloat16

HEAD_DIM = 64
GRID_W = 64
ROPE_BASE = 10000.0
EPS = 1e-6
NEG_INF = -1e30
WINDOW = 128
BLOCK = 128
POOL_WINDOWS = (2, 4, 8, 16)
HY_EMB = 33
HY_HIDDEN = 64
HY_FAST_DECAY = 0.3
HY_SLOW_DECAY = 1.5
HY_TARGET = 1e-2

LANE = 128
SUBLANE = 8
HALO = SUBLANE
DFT_RADIX = 128
VT_ROWS = LANE + 16
LOG2E = 1.4426950408889634
VMEM_LIMIT = 56 * 1024 * 1024


def _cp(*sem):
    return pltpu.CompilerParams(dimension_semantics=sem, vmem_limit_bytes=VMEM_LIMIT)


def _tile(n, pref):
    if n <= pref:
        return n
    t = pref
    while n % t:
        t //= 2
    assert t >= SUBLANE, (n, pref)
    return t


def _dot(a, b):
    return jnp.dot(a, b, preferred_element_type=F32)


def _dot_nt(a, b):
    return lax.dot_general(a, b, (((1,), (1,)), ((), ())), preferred_element_type=F32)


def _split_bf16(x):
    hi = x.astype(BF16)
    lo = (x - hi.astype(F32)).astype(BF16)
    return hi, lo


def _dot3(a_hi, a_lo, b):
    b_hi, b_lo = _split_bf16(b)
    return _dot(a_hi, b_hi) + _dot(a_hi, b_lo) + _dot(a_lo, b_hi)


def _mod_kernel(c_ref, w_ref, b_ref, o_ref):
    c = c_ref[...]
    a = (c * jax.nn.sigmoid(c)).astype(BF16)
    o_ref[...] = _dot(a, w_ref[...].astype(BF16)) + b_ref[...]


def _modulation(cvec, w_mod, b_mod):
    depth, d, n = w_mod.shape
    rows = cvec.shape[0]
    tn = _tile(n, 1024)
    return pl.pallas_call(
        _mod_kernel,
        grid=(depth, n // tn),
        in_specs=[
            pl.BlockSpec((rows, d), lambda l, j: (0, 0)),
            pl.BlockSpec((None, d, tn), lambda l, j: (l, 0, j)),
            pl.BlockSpec((None, 1, tn), lambda l, j: (l, 0, j)),
        ],
        out_specs=pl.BlockSpec((None, rows, tn), lambda l, j: (l, 0, j)),
        out_shape=jax.ShapeDtypeStruct((depth, rows, n), F32),
        compiler_params=_cp("parallel", "parallel"),
        name="adaln_mod",
    )(cvec, w_mod, b_mod.reshape(depth, 1, n))


def _norm_mod(x, g, sc, sh):
    ms = jnp.mean(x * x, axis=-1, keepdims=True)
    return (x * lax.rsqrt(ms + EPS) * g) * (1.0 + sc) + sh


def _inproj_kernel(x_ref, g_ref, sc_ref, sh_ref, w_ref, o_ref):
    h = _norm_mod(x_ref[...], g_ref[...], sc_ref[...], sh_ref[...])
    o_ref[...] = _dot(h.astype(BF16), w_ref[...])


def _in_projection(x2d, g, sc, sh, w, rows_per_group):
    n, d = x2d.shape
    nc = w.shape[1]
    tm = _tile(rows_per_group, 512)
    tpg = rows_per_group // tm
    ncol = 2 if (nc // 2) % LANE == 0 else 1
    tn = nc // ncol
    vec = pl.BlockSpec((None, 1, d), lambda j, i: (i // tpg, 0, 0))
    return pl.pallas_call(
        _inproj_kernel,
        grid=(ncol, n // tm),
        in_specs=[
            pl.BlockSpec((tm, d), lambda j, i: (i, 0)),
            pl.BlockSpec((1, d), lambda j, i: (0, 0)),
            vec, vec,
            pl.BlockSpec((d, tn), lambda j, i: (0, j)),
        ],
        out_specs=pl.BlockSpec((tm, tn), lambda j, i: (i, j)),
        out_shape=jax.ShapeDtypeStruct((n, nc), F32),
        compiler_params=_cp("parallel", "parallel"),
        name="in_proj",
    )(x2d, g, sc, sh, w)


def _qkprep_kernel(rope, x_ref, gain_ref, cos_ref, sin_ref, gm_ref, o_ref):
    x = x_ref[...]
    hi, lo = _split_bf16(x * x)
    gm = gm_ref[...]
    ms = _dot(hi, gm) + _dot(lo, gm)
    y = x * lax.rsqrt(ms + EPS) * gain_ref[...]
    if rope:
        lane = lax.broadcasted_iota(jnp.int32, y.shape, 1)
        first_half = (lane & 31) < 16
        partner = jnp.where(first_half, pltpu.roll(y, LANE - 16, 1), pltpu.roll(y, 16, 1))
        y = y * cos_ref[...] + partner * sin_ref[...]
    o_ref[...] = y.astype(BF16)


def _qk_prep(p, col_block0, gains, cos_t, sin_t, seq_len, rope):
    n = p.shape[0]
    nblk = gains.shape[0]
    tm = _tile(seq_len, 1024)
    tps = seq_len // tm
    lane = jnp.arange(LANE)
    gm = ((lane[:, None] // HEAD_DIM) == (lane[None, :] // HEAD_DIM)).astype(BF16) * (1.0 / HEAD_DIM)
    tab = pl.BlockSpec((tm, LANE), lambda i, j: (i % tps, 0))
    return pl.pallas_call(
        functools.partial(_qkprep_kernel, rope),
        grid=(n // tm, nblk),
        in_specs=[
            pl.BlockSpec((tm, LANE), lambda i, j: (i, col_block0 + j)),
            pl.BlockSpec((None, 1, LANE), lambda i, j: (j, 0, 0)),
            tab, tab,
            pl.BlockSpec((LANE, LANE), lambda i, j: (0, 0)),
        ],
        out_specs=pl.BlockSpec((tm, LANE), lambda i, j: (i, j)),
        out_shape=jax.ShapeDtypeStruct((n, nblk * LANE), BF16),
        compiler_params=_cp("parallel", "parallel"),
        name="qk_prep",
    )(p, gains.reshape(nblk, 1, LANE), cos_t, sin_t, gm.astype(BF16))


def _rope_tables(seq_len):
    rows = seq_len // GRID_W
    row = jnp.repeat(jnp.arange(rows, dtype=F32), GRID_W)
    col = jnp.tile(jnp.arange(GRID_W, dtype=F32), rows)
    half = HEAD_DIM // 2
    inv = ROPE_BASE ** (-jnp.arange(0, half, 2, dtype=F32) / half)
    ang = jnp.concatenate([row[:, None] * inv, col[:, None] * inv], axis=-1)
    lane = jnp.arange(LANE)
    within = lane % HEAD_DIM
    idx = (within // 32) * 16 + (within % 16)
    sign = jnp.where((lane % 32) < 16, -1.0, 1.0).astype(F32)
    return jnp.cos(ang)[:, idx], jnp.sin(ang)[:, idx] * sign


def _vt_kernel(x_ref, o_ref):
    chunk = x_ref.shape[0]
    row = lax.broadcasted_iota(jnp.int32, (VT_ROWS - LANE, chunk), 0)
    ones_row = jnp.where(row == 0, 1.0, 0.0).astype(BF16)
    o_ref[...] = jnp.concatenate([x_ref[...].T.astype(BF16), ones_row], axis=0)


def _transposed_values(p, col_block0, nblk, seq_len, chunk):
    n = p.shape[0]
    return pl.pallas_call(
        _vt_kernel,
        grid=(n // chunk, nblk),
        in_specs=[pl.BlockSpec((chunk, LANE), lambda i, j: (i, col_block0 + j))],
        out_specs=pl.BlockSpec((None, None, VT_ROWS, chunk), lambda i, j: (j, i, 0, 0)),
        out_shape=jax.ShapeDtypeStruct((nblk, n // chunk, VT_ROWS, chunk), BF16),
        compiler_params=_cp("parallel", "parallel"),
        name="values_t",
    )(p)


def _attn_a_kernel(n_lat, tk, sub_scale, *refs):
    if n_lat:
        (q_ref, kc_ref, vc_ref, k_ref, v_ref, lam_ref, subg_ref, o_ref, qt, m, acc,
         s0, s1, x0, x1, p0, p1, al0, al1) = refs
    else:
        q_ref, kc_ref, vc_ref, lam_ref, subg_ref, o_ref, qt, m, acc = refs
    tq = q_ref.shape[0]
    q_t = q_ref[...].astype(F32).T
    row = lax.broadcasted_iota(jnp.int32, q_t.shape, 0)
    qt[:, :tq] = jnp.where(row < HEAD_DIM, q_t, 0.0).astype(BF16)
    qt[:, tq:] = jnp.where(row >= HEAD_DIM, q_t, 0.0).astype(BF16)
    m[...] = jnp.full(m.shape, -jnp.inf, F32)
    acc[...] = jnp.zeros(acc.shape, F32)

    def softmax_update(s, s_max):
        m_prev = m[...]
        m_new = jnp.maximum(m_prev, s_max)
        m[...] = m_new
        return jnp.exp2(s - m_new).astype(BF16), jnp.exp2(m_prev - m_new)

    s_c = _dot(kc_ref[...], qt[...])
    p_c, alpha_c = softmax_update(s_c, jnp.max(s_c, axis=0, keepdims=True))
    acc[...] = alpha_c * acc[...] + _dot(vc_ref[0], p_c)

    if n_lat:
        def scores(c, s_buf, x_buf):
            s = _dot(k_ref[pl.ds(pl.multiple_of(c * tk, tk), tk), :], qt[...])
            s_buf[...] = s
            x_buf[...] = jnp.max(s, axis=0, keepdims=True)

        def softmax(s_buf, x_buf, p_buf, al_buf):
            p_buf[...], al_buf[...] = softmax_update(s_buf[...], x_buf[...])

        def values(c, p_buf, al_buf):
            acc[...] = al_buf[...] * acc[...] + _dot(v_ref[c], p_buf[...])

        scores(0, s0, x0)
        if n_lat == 1:
            softmax(s0, x0, p0, al0)
            values(0, p0, al0)
        else:
            assert n_lat % 2 == 0
            scores(1, s1, x1)
            softmax(s0, x0, p0, al0)

            def body(t, carry):
                c = 2 * t
                scores(c + 2, s0, x0)
                softmax(s1, x1, p1, al1)
                values(c, p0, al0)
                scores(c + 3, s1, x1)
                softmax(s0, x0, p0, al0)
                values(c + 1, p1, al1)
                return carry
            lax.fori_loop(0, n_lat // 2 - 1, body, 0)
            softmax(s1, x1, p1, al1)
            values(n_lat - 2, p0, al0)
            values(n_lat - 1, p1, al1)

    o_t = acc[:LANE] / acc[LANE:LANE + 1]
    o = (o_t[:, :tq] - lam_ref[0:1, 0:1] * o_t[:, tq:]).T
    ms = jnp.mean(o * o, axis=-1, keepdims=True)
    o_ref[...] = ((o * lax.rsqrt(ms + EPS) * subg_ref[...]) * sub_scale).astype(BF16)


def _diff_attention(q_arr, q_len, qk_ctx, vt_ctx, ctx_len, qk_lat, vt_lat, lat_len, lam, subg,
                    sub_scale, heads, k_col0):
    batch = q_arr.shape[0] // q_len
    tq = _tile(q_len, 512)
    nq = q_len // tq
    in_specs = [
        pl.BlockSpec((tq, LANE), lambda b, h, i: (b * nq + i, h)),
        pl.BlockSpec((ctx_len, LANE), lambda b, h, i: (b, k_col0 + h)),
        pl.BlockSpec((None, 1, VT_ROWS, ctx_len), lambda b, h, i: (h, b, 0, 0)),
    ]
    args = [q_arr, qk_ctx, vt_ctx]
    n_lat, tk = 0, 0
    if qk_lat is not None:
        tk = vt_lat.shape[-1]
        n_lat = lat_len // tk
        in_specs += [
            pl.BlockSpec((lat_len, LANE), lambda b, h, i: (b, k_col0 + h)),
            pl.BlockSpec((None, n_lat, VT_ROWS, tk), lambda b, h, i: (h, b, 0, 0)),
        ]
        args += [qk_lat, vt_lat]
    in_specs += [pl.BlockSpec((1, LANE), lambda b, h, i: (0, 0))] * 2
    args += [lam, subg]
    stat = pltpu.VMEM((1, 2 * tq), F32)
    scratch = [pltpu.VMEM((LANE, 2 * tq), BF16), stat, pltpu.VMEM((VT_ROWS, 2 * tq), F32)]
    if n_lat:
        scratch += [pltpu.VMEM((tk, 2 * tq), F32)] * 2 + [stat] * 2
        scratch += [pltpu.VMEM((tk, 2 * tq), BF16)] * 2 + [stat] * 2
    return pl.pallas_call(
        functools.partial(_attn_a_kernel, n_lat, tk, sub_scale),
        grid=(batch, heads, nq),
        in_specs=in_specs,
        out_specs=pl.BlockSpec((tq, LANE), lambda b, h, i: (b * nq + i, h)),
        out_shape=jax.ShapeDtypeStruct((batch * q_len, heads * LANE), BF16),
        scratch_shapes=scratch,
        compiler_params=_cp("parallel", "parallel", "parallel"),
        name="diff_attn",
    )(*args)


def _attn_b_kernel(has_loc, seq_len, q_heads, group, *refs):
    if has_loc:
        (q_ref, kx_ref, vx_ref, kp_ref, kc_ref, kn_ref, vp_ref, vc_ref, vn_ref, sink_ref, o_ref) = refs
    else:
        (q_ref, kx_ref, vx_ref, sink_ref, o_ref) = refs
    n = pl.program_id(1)
    q = q_ref[...]
    lane = lax.broadcasted_iota(jnp.int32, (BLOCK, LANE), 1)
    low = lane < HEAD_DIM
    kx = kx_ref[...]
    kx_sw = pltpu.roll(kx, HEAD_DIM, 1)
    vx = vx_ref[...].astype(BF16)
    if has_loc:
        kb = jnp.concatenate([kp_ref[...], kc_ref[...], kn_ref[...]], axis=0)
        kb_sw = pltpu.roll(kb, HEAD_DIM, 1)
        vb = jnp.concatenate([vp_ref[...], vc_ref[...], vn_ref[...]], axis=0).astype(BF16)
        kpos = (n - 1) * BLOCK + lax.broadcasted_iota(jnp.int32, (BLOCK, 3 * BLOCK), 1)
        qpos = n * BLOCK + lax.broadcasted_iota(jnp.int32, (BLOCK, 3 * BLOCK), 0)
        valid = (kpos >= 0) & (kpos < seq_len) & (jnp.abs(kpos - qpos) <= WINDOW)
    outs = []
    for blk in range(q_heads // 2):
        q128 = q[:, blk * LANE:(blk + 1) * LANE]
        res = []
        for half in range(2):
            hq = 2 * blk + half
            kvh = hq // group
            same = (kvh % 2) == half
            qm = jnp.where(low if half == 0 else jnp.logical_not(low), q128, jnp.zeros_like(q128))
            sink = sink_ref[hq:hq + 1, 0:1]
            s_ctx = _dot_nt(qm, kx if same else kx_sw)
            m = jnp.maximum(jnp.max(s_ctx, axis=-1, keepdims=True), sink)
            if has_loc:
                s_loc = jnp.where(valid, _dot_nt(qm, kb if same else kb_sw), NEG_INF)
                m = jnp.maximum(m, jnp.max(s_loc, axis=-1, keepdims=True))
            p_ctx = jnp.exp(s_ctx - m)
            den = jnp.exp(sink - m) + jnp.sum(p_ctx, axis=-1, keepdims=True)
            o = _dot(p_ctx.astype(BF16), vx)
            if has_loc:
                p_loc = jnp.exp(s_loc - m)
                den = den + jnp.sum(p_loc, axis=-1, keepdims=True)
                o = o + _dot(p_loc.astype(BF16), vb)
            o = o / den
            res.append(o if same else pltpu.roll(o, HEAD_DIM, 1))
        outs.append(jnp.where(low, res[0], res[1]))
    o_ref[...] = jnp.concatenate(outs, axis=1).astype(BF16)


def _window_attention(q_arr, q_len, qk_ctx, p_ctx, ctx_len, p_lat, sink_tab, q_heads, kv_heads,
                      q_blk, k_blk, v_blk, local):
    assert kv_heads == 2, "key/value heads must fill exactly one 128-lane block"
    batch = q_arr.shape[0] // q_len
    nb = q_len // BLOCK
    qw = q_heads * HEAD_DIM
    in_specs = [
        pl.BlockSpec((BLOCK, qw), lambda b, n: (b * nb + n, q_blk)),
        pl.BlockSpec((ctx_len, LANE), lambda b, n: (b, k_blk)),
        pl.BlockSpec((ctx_len, LANE), lambda b, n: (b, v_blk)),
    ]
    args = [q_arr, qk_ctx, p_ctx]
    if local:
        prev = lambda b, n: b * nb + jnp.maximum(n - 1, 0)
        nxt = lambda b, n: b * nb + jnp.minimum(n + 1, nb - 1)
        for col, arr in ((k_blk, q_arr), (v_blk, p_lat)):
            in_specs += [
                pl.BlockSpec((BLOCK, LANE), lambda b, n, col=col: (prev(b, n), col)),
                pl.BlockSpec((BLOCK, LANE), lambda b, n, col=col: (b * nb + n, col)),
                pl.BlockSpec((BLOCK, LANE), lambda b, n, col=col: (nxt(b, n), col)),
            ]
            args += [arr, arr, arr]
    in_specs.append(pl.BlockSpec((q_heads, LANE), lambda b, n: (0, 0)))
    args.append(sink_tab)
    return pl.pallas_call(
        functools.partial(_attn_b_kernel, local, q_len, q_heads, q_heads // kv_heads),
        grid=(batch, nb),
        in_specs=in_specs,
        out_specs=pl.BlockSpec((BLOCK, qw), lambda b, n: (b * nb + n, 0)),
        out_shape=jax.ShapeDtypeStruct((batch * q_len, qw), BF16),
        compiler_params=_cp("parallel", "parallel"),
        name="window_attn",
    )(*args)


def _halo_specs(tile, width, col_blk, n_rows):
    r = tile // HALO
    last = n_rows // HALO - 1
    return [
        pl.BlockSpec((tile, width), lambda i: (i, col_blk)),
        pl.BlockSpec((HALO, width), lambda i: (jnp.maximum(i * r - 1, 0), col_blk)),
        pl.BlockSpec((HALO, width), lambda i: (jnp.minimum(i * r + r, last), col_blk)),
    ]


def _outside_rows(tile, tile_idx, tiles_per_seq):
    row = lax.broadcasted_iota(jnp.int32, (tile + 2 * HALO, 1), 0)
    lo = jnp.where(tile_idx == 0, HALO, 0)
    hi = jnp.where(tile_idx == tiles_per_seq - 1, HALO + tile, tile + 2 * HALO)
    return (row < lo) | (row >= hi)


def _with_halo(x, xp, xn, tiles_per_seq):
    ext = jnp.concatenate([xp, x, xn], axis=0)
    outside = _outside_rows(x.shape[0], pl.program_id(0) % tiles_per_seq, tiles_per_seq)
    return jnp.where(outside, 0.0, ext)


def _conv3(ext, w, b, tile):
    rows = ext.shape[0]
    c = (pltpu.roll(ext, 1, 0) * w[0:1] + ext * w[1:2] + pltpu.roll(ext, rows - 1, 0) * w[2:3])
    return c[HALO:HALO + tile] + b


def _pool_kernel(seq_len, tile, x_ref, xp_ref, xn_ref, w_ref, ls_ref, o_ref):
    tps = seq_len // tile
    x = x_ref[...]
    ext = _with_halo(x, xp_ref[...], xn_ref[...], tps)
    rows = tile + 2 * HALO
    t = (pl.program_id(0) % tps) * tile + lax.broadcasted_iota(jnp.int32, (tile, 1), 0)
    gw = x.shape[1] // len(POOL_WINDOWS)
    outs = []
    for g, w in enumerate(POOL_WINDOWS):
        lo = w // 2
        hi = w - 1 - lo
        a = ext[:, g * gw:(g + 1) * gw]
        span = 1
        while span < w:
            a = a + pltpu.roll(a, span, 0)
            span *= 2
        win = (pltpu.roll(a, rows - hi, 0) if hi else a)[HALO:HALO + tile]
        cnt = jnp.minimum(t + hi + 1, seq_len) - jnp.maximum(t - lo, 0)
        d = win / cnt.astype(F32) - x[:, g * gw:(g + 1) * gw]
        outs.append(_dot(d.astype(BF16), w_ref[g]))
    o_ref[...] = (jnp.concatenate(outs, axis=1) * ls_ref[...]).astype(BF16)


def _pool_mixer(p, col0, width, seq_len, w_lin, ls):
    n = p.shape[0]
    tile = _tile(seq_len, 512)
    g, gw, _ = w_lin.shape
    return pl.pallas_call(
        functools.partial(_pool_kernel, seq_len, tile),
        grid=(n // tile,),
        in_specs=_halo_specs(tile, width, col0 // width, n)
        + [pl.BlockSpec((g, gw, gw), lambda i: (0, 0, 0)),
           pl.BlockSpec((1, width), lambda i: (0, 0))],
        out_specs=pl.BlockSpec((tile, width), lambda i: (i, 0)),
        out_shape=jax.ShapeDtypeStruct((n, width), BF16),
        compiler_params=_cp("parallel"),
        name="pool_mixer",
    )(p, p, p, w_lin, ls)


def _hypre_kernel(seq_len, tile, x_ref, xp_ref, xn_ref, cw_ref, cb_ref, z_ref, x2_ref):
    ext = _with_halo(x_ref[...], xp_ref[...], xn_ref[...], seq_len // tile)
    c = _conv3(ext, cw_ref[...], cb_ref[...], tile)
    dw = c.shape[1] // 3
    z_ref[...] = c[:, :dw] * c[:, dw:2 * dw]
    x2_ref[...] = c[:, 2 * dw:]


def _hyena_pre(p, col0, width, seq_len, conv_w, conv_b):
    n = p.shape[0]
    tile = _tile(seq_len, 512)
    dw = width // 3
    out = jax.ShapeDtypeStruct((n, dw), F32)
    return pl.pallas_call(
        functools.partial(_hypre_kernel, seq_len, tile),
        grid=(n // tile,),
        in_specs=_halo_specs(tile, width, col0 // width, n)
        + [pl.BlockSpec((3, width), lambda i: (0, 0)),
           pl.BlockSpec((1, width), lambda i: (0, 0))],
        out_specs=[pl.BlockSpec((tile, dw), lambda i: (i, 0))] * 2,
        out_shape=[out, out],
        compiler_params=_cp("parallel"),
        name="hyena_pre",
    )(p, p, p, conv_w, conv_b.reshape(1, width))


def _hyfilt_kernel(seq_len, tile, zf_ref, w1_ref, b1_ref, w2_ref, b2_ref, w3_ref, fr_ref, dl_ref,
                   k_ref, ss_ref):
    i = pl.program_id(0)
    hp = functools.partial(jnp.dot, precision=lax.Precision.HIGHEST, preferred_element_type=F32)
    zf = zf_ref[...]
    fr = fr_ref[...]
    h = jnp.sin(fr * (hp(zf, w1_ref[...]) + b1_ref[...]))
    h = jnp.sin(fr * (hp(h, w2_ref[...]) + b2_ref[...]))
    k = hp(h, w3_ref[...]) * jnp.exp(-zf[:, 0:1] * dl_ref[...])
    row = i * tile + lax.broadcasted_iota(jnp.int32, (tile, 1), 0)
    k = jnp.where(row == seq_len, 0.0, k)
    k_ref[...] = k

    @pl.when(i == 0)
    def _():
        ss_ref[...] = jnp.zeros(ss_ref.shape, F32)

    ss_ref[...] += jnp.sum(k * k, axis=0, keepdims=True)


def _hyena_filter(seq_len, w1, b1, w2, b2, w3, freq):
    dw = w3.shape[1] // 2
    bands = (HY_EMB - 1) // 2
    t01 = jnp.linspace(0.0, 1.0, seq_len, dtype=F32)[:, None]
    w_ang = 2.0 * math.pi * jnp.arange(seq_len, dtype=F32)[:, None] / seq_len
    f = jnp.linspace(1e-4, bands - 1, bands, dtype=F32)[None, :]
    z = jnp.concatenate([t01, jnp.cos(f * w_ang), -jnp.sin(f * w_ang)], axis=-1)
    rev = (seq_len - jnp.arange(seq_len)) % seq_len
    zf = jnp.pad(jnp.concatenate([z, z[rev]], axis=0), ((0, 0), (0, LANE - HY_EMB)))
    pad_h = LANE - HY_HIDDEN
    w1p = jnp.pad(w1, ((0, LANE - HY_EMB), (0, pad_h)))
    w2p = jnp.pad(w2, ((0, pad_h), (0, pad_h)))
    w3p = jnp.pad(w3, ((0, pad_h), (0, 0)))
    vec = lambda v: jnp.pad(v, (0, pad_h)).reshape(1, LANE)
    deltas = jnp.abs(jnp.linspace(math.log(HY_TARGET) / HY_FAST_DECAY, math.log(HY_TARGET) / HY_SLOW_DECAY,
                                  dw, dtype=F32)).reshape(1, dw)
    tile = _tile(seq_len, 1024)
    tps = seq_len // tile
    full = lambda shape: pl.BlockSpec(shape, lambda i: (0, 0))
    return pl.pallas_call(
        functools.partial(_hyfilt_kernel, seq_len, tile),
        grid=(2 * tps,),
        in_specs=[pl.BlockSpec((tile, LANE), lambda i: (i, 0)),
                  full((LANE, LANE)), full((1, LANE)), full((LANE, LANE)), full((1, LANE)),
                  pl.BlockSpec((LANE, dw), lambda i: (0, i // tps)),
                  full((1, LANE)), full((1, dw))],
        out_specs=[pl.BlockSpec((tile, dw), lambda i: (i, 0)), full((1, dw))],
        out_shape=[jax.ShapeDtypeStruct((2 * seq_len, dw), F32), jax.ShapeDtypeStruct((1, dw), F32)],
        compiler_params=_cp("arbitrary"),
        name="hyena_filter",
    )(zf, w1p, vec(b1), w2p, vec(b2), w3p, vec(freq), deltas)


def _dft_tables(n1, n2):
    n = n1 * n2
    two_pi = 2.0 * math.pi

    def cs(num, den):
        ang = (num % den).astype(F32) * (two_pi / den)
        return jnp.cos(ang), jnp.sin(ang)

    a = jnp.arange(n1)
    c1, s1 = cs(a[:, None] * a[None, :], n1)
    fwd_a = jnp.concatenate([c1, -s1], axis=0)
    inv_a = jnp.concatenate([c1, -s1], axis=1)[: n1 // 2] / n
    k2 = jnp.arange(n2)
    num = (k2[None, :, None] * k2[None, None, :] * n1 + a[:, None, None] * k2[None, None, :])
    cb, sb = cs(num, n)
    fwd_b = jnp.concatenate([jnp.concatenate([cb, sb], axis=2),
                             jnp.concatenate([-sb, cb], axis=2)], axis=1)
    cbt, sbt = jnp.swapaxes(cb, 1, 2), jnp.swapaxes(sb, 1, 2)
    inv_b = jnp.concatenate([jnp.concatenate([cbt, -sbt], axis=2),
                             jnp.concatenate([sbt, cbt], axis=2)], axis=1)
    return tuple(_split_bf16(m) for m in (fwd_a, inv_a, fwd_b, inv_b))


def _dft_a_kernel(fh_ref, fl_ref, x_ref, re_ref, im_ref):
    y = _dot3(fh_ref[...], fl_ref[...], x_ref[...])
    half = y.shape[0] // 2
    re_ref[...] = y[:half]
    im_ref[...] = y[half:]


def _dft_stage_a(xv, fwd_a, n1):
    b, k, cols = xv.shape
    fh, fl = fwd_a
    ct = _tile(cols, 2048)
    out = jax.ShapeDtypeStruct((b, n1, cols), F32)
    return pl.pallas_call(
        _dft_a_kernel,
        grid=(b, cols // ct),
        in_specs=[pl.BlockSpec((2 * n1, k), lambda bi, j: (0, 0))] * 2
        + [pl.BlockSpec((None, k, ct), lambda bi, j: (bi, 0, j))],
        out_specs=[pl.BlockSpec((None, n1, ct), lambda bi, j: (bi, 0, j))] * 2,
        out_shape=[out, out],
        compiler_params=_cp("parallel", "parallel"),
        name="dft_stage_a",
    )(fh[:, :k], fl[:, :k], xv)


def _dft_b_kernel(mh_ref, ml_ref, re_ref, im_ref, ss_ref, kr_ref, ki_ref):
    x = _dot3(mh_ref[...], ml_ref[...], jnp.concatenate([re_ref[...], im_ref[...]], axis=0))
    nrm = lax.rsqrt(ss_ref[...] + EPS)
    half = x.shape[0] // 2
    kr_ref[...] = x[:half] * nrm
    ki_ref[...] = x[half:] * nrm


def _filter_spectrum(kern, sumsq, tabs, n1, n2):
    fwd_a, _, fwd_b, _ = tabs
    ch = kern.shape[1]
    are, aim = _dft_stage_a(kern.reshape(1, n1, n2 * ch), fwd_a, n1)
    are, aim = are.reshape(n1 * n2, ch), aim.reshape(n1 * n2, ch)
    blk = pl.BlockSpec((n2, ch), lambda i: (i, 0))
    mat = pl.BlockSpec((None, 2 * n2, 2 * n2), lambda i: (i, 0, 0))
    out = jax.ShapeDtypeStruct((n1 * n2, ch), F32)
    return pl.pallas_call(
        _dft_b_kernel,
        grid=(n1,),
        in_specs=[mat, mat, blk, blk, pl.BlockSpec((1, ch), lambda i: (0, 0))],
        out_specs=[blk, blk],
        out_shape=[out, out],
        compiler_params=_cp("parallel"),
        name="filter_spectrum",
    )(fwd_b[0], fwd_b[1], are, aim, sumsq)


def _dft_mul_kernel(fh_ref, fl_ref, gh_ref, gl_ref, re_ref, im_ref, kr_ref, ki_ref, qr_ref, qi_ref):
    x = _dot3(fh_ref[...], fl_ref[...], jnp.concatenate([re_ref[...], im_ref[...]], axis=0))
    half = x.shape[0] // 2
    xr, xi = x[:half], x[half:]
    kr, ki = kr_ref[...], ki_ref[...]
    prod = jnp.concatenate([xr * kr - xi * ki, xr * ki + xi * kr], axis=0)
    q = _dot3(gh_ref[...], gl_ref[...], prod)
    qr_ref[...] = q[:half]
    qi_ref[...] = q[half:]


def _dft_inv_kernel(gh_ref, gl_ref, qr_ref, qi_ref, z_ref, x2_ref, bias_ref, o_ref):
    y = _dot3(gh_ref[...], gl_ref[...], jnp.concatenate([qr_ref[...], qi_ref[...]], axis=0))
    z = z_ref[...]
    o_ref[...] = (x2_ref[...] * (y + z * bias_ref[...])).astype(BF16)


def _hyena_long(z, x2, kr, ki, bias, tabs, batch, seq_len, n1, n2):
    fwd_a, inv_a, fwd_b, inv_b = tabs
    ch = z.shape[1]
    cols = n2 * ch
    zv = z.reshape(batch, n1 // 2, cols)
    are, aim = _dft_stage_a(zv, fwd_a, n1)
    are, aim = are.reshape(batch, n1 * n2, ch), aim.reshape(batch, n1 * n2, ch)
    mat = pl.BlockSpec((None, 2 * n2, 2 * n2), lambda i, b: (i, 0, 0))
    dat = pl.BlockSpec((None, n2, ch), lambda i, b: (b, i, 0))
    spec = pl.BlockSpec((n2, ch), lambda i, b: (i, 0))
    out = jax.ShapeDtypeStruct((batch, n1 * n2, ch), F32)
    qr, qi = pl.pallas_call(
        _dft_mul_kernel,
        grid=(n1, batch),
        in_specs=[mat, mat, mat, mat, dat, dat, spec, spec],
        out_specs=[dat, dat],
        out_shape=[out, out],
        compiler_params=_cp("parallel", "parallel"),
        name="dft_filter_mul",
    )(fwd_b[0], fwd_b[1], inv_b[0], inv_b[1], are, aim, kr, ki)
    qr, qi = qr.reshape(batch, n1, cols), qi.reshape(batch, n1, cols)
    ct = _tile(cols, 2048)
    full = pl.BlockSpec((None, n1, ct), lambda b, j: (b, 0, j))
    half = pl.BlockSpec((None, n1 // 2, ct), lambda b, j: (b, 0, j))
    gmat = pl.BlockSpec((n1 // 2, 2 * n1), lambda b, j: (0, 0))
    o = pl.pallas_call(
        _dft_inv_kernel,
        grid=(batch, cols // ct),
        in_specs=[gmat, gmat, full, full, half, half, pl.BlockSpec((1, ct), lambda b, j: (0, j))],
        out_specs=half,
        out_shape=jax.ShapeDtypeStruct((batch, n1 // 2, cols), BF16),
        compiler_params=_cp("parallel", "parallel"),
        name="dft_inverse_gate",
    )(inv_a[0], inv_a[1], qr, qi, zv, x2.reshape(batch, n1 // 2, cols), jnp.tile(bias, n2).reshape(1, cols))
    return o.reshape(batch * seq_len, ch)


def _hyena_short_kernel(fh_ref, fl_ref, gh_ref, gl_ref, k_ref, ss_ref, z_ref, x2_ref, bias_ref, o_ref):
    fh, fl = fh_ref[...], fl_ref[...]
    z = z_ref[...]
    seq = z.shape[0]
    ks = _dot3(fh, fl, k_ref[...]) * lax.rsqrt(ss_ref[...] + EPS)
    zs = _dot3(fh[:, :seq], fl[:, :seq], z)
    n = ks.shape[0] // 2
    kr, ki, zr, zi = ks[:n], ks[n:], zs[:n], zs[n:]
    prod = jnp.concatenate([zr * kr - zi * ki, zr * ki + zi * kr], axis=0)
    y = _dot3(gh_ref[...], gl_ref[...], prod)
    o_ref[...] = (x2_ref[...] * (y + z * bias_ref[...])).astype(BF16)


def _hyena_short(z, x2, kern, sumsq, bias, batch, seq_len):
    n = 2 * seq_len
    ch = z.shape[1]
    a = jnp.arange(n)
    ang = ((a[:, None] * a[None, :]) % n).astype(F32) * (2.0 * math.pi / n)
    c, s = jnp.cos(ang), jnp.sin(ang)
    fh, fl = _split_bf16(jnp.concatenate([c, -s], axis=0))
    gh, gl = _split_bf16(jnp.concatenate([c, -s], axis=1)[:seq_len] / n)
    full = lambda shape: pl.BlockSpec(shape, lambda b: (0, 0))
    dat = pl.BlockSpec((seq_len, ch), lambda b: (b, 0))
    return pl.pallas_call(
        _hyena_short_kernel,
        grid=(batch,),
        in_specs=[full((2 * n, n)), full((2 * n, n)), full((seq_len, 2 * n)), full((seq_len, 2 * n)),
                  full((n, ch)), full((1, ch)), dat, dat, full((1, ch))],
        out_specs=dat,
        out_shape=jax.ShapeDtypeStruct((batch * seq_len, ch), BF16),
        compiler_params=_cp("parallel"),
        name="hyena_short_conv",
    )(fh, fl, gh, gl, kern, sumsq, z, x2, bias.reshape(1, ch))


def _hyena_mixer(p, col0, width, batch, seq_len, conv_w, conv_b, w1, b1, w2, b2, w3, freq, bias):
    z, x2 = _hyena_pre(p, col0, width, seq_len, conv_w, conv_b)
    kern, sumsq = _hyena_filter(seq_len, w1, b1, w2, b2, w3, freq)
    n = 2 * seq_len
    if n <= 4 * DFT_RADIX:
        return _hyena_short(z, x2, kern, sumsq, bias, batch, seq_len)
    n2 = DFT_RADIX
    n1 = n // n2
    tabs = _dft_tables(n1, n2)
    kr, ki = _filter_spectrum(kern, sumsq, tabs, n1, n2)
    return _hyena_long(z, x2, kr, ki, bias, tabs, batch, seq_len, n1, n2)


def _outproj_kernel(x_ref, a_ref, b_ref, c_ref, d_ref, w_ref, g_ref, o_ref):
    acc = None
    row = 0
    for m_ref in (a_ref, b_ref, c_ref, d_ref):
        wdt = m_ref.shape[1]
        part = _dot(m_ref[...], w_ref[row:row + wdt, :])
        acc = part if acc is None else acc + part
        row += wdt
    o_ref[...] = x_ref[...] + g_ref[...] * acc


def _out_projection(x2d, mixers, w, gate, rows_per_group):
    n, d = x2d.shape
    tm = _tile(rows_per_group, 512)
    tpg = rows_per_group // tm
    return pl.pallas_call(
        _outproj_kernel,
        grid=(n // tm,),
        in_specs=[pl.BlockSpec((tm, d), lambda i: (i, 0))]
        + [pl.BlockSpec((tm, m.shape[1]), lambda i: (i, 0)) for m in mixers]
        + [pl.BlockSpec(w.shape, lambda i: (0, 0)),
           pl.BlockSpec((None, 1, d), lambda i: (i // tpg, 0, 0))],
        out_specs=pl.BlockSpec((tm, d), lambda i: (i, 0)),
        out_shape=jax.ShapeDtypeStruct((n, d), F32),
        compiler_params=_cp("parallel"),
        name="out_proj",
    )(x2d, *mixers, w, gate)


def _ffn_kernel(seq_len, tile, x_ref, xp_ref, xn_ref, g_ref, sc_ref, sh_ref, gate_ref,
                wg_ref, wu_ref, cwg_ref, cwu_ref, cbg_ref, cbu_ref, wd_ref, o_ref, h_scr, acc):
    j = pl.program_id(1)

    @pl.when(j == 0)
    def _():
        ext = jnp.concatenate([xp_ref[...], x_ref[...], xn_ref[...]], axis=0)
        h = _norm_mod(ext, g_ref[...], sc_ref[...], sh_ref[...])
        tps = seq_len // tile
        outside = _outside_rows(tile, pl.program_id(0) % tps, tps)
        h_scr[...] = jnp.where(outside, 0.0, h).astype(BF16)
        acc[...] = jnp.zeros(acc.shape, F32)

    h = h_scr[...]
    gate = _conv3(_dot(h, wg_ref[...]), cwg_ref[...], cbg_ref[...], tile)
    up = _conv3(_dot(h, wu_ref[...]), cwu_ref[...], cbu_ref[...], tile)
    act = (gate * jax.nn.sigmoid(gate)) * up
    acc[...] += _dot(act.astype(BF16), wd_ref[...])

    @pl.when(j == pl.num_programs(1) - 1)
    def _():
        o_ref[...] = x_ref[...] + gate_ref[...] * acc[...]


def _conv_ffn(x2d, g, sc, sh, gate, w_up, conv_w, conv_b, w_down, seq_len, rows_per_group):
    n, d = x2d.shape
    f = w_down.shape[0]
    tile = _tile(seq_len, 512)
    hc = _tile(f, 512)
    nh = f // hc
    tpg = rows_per_group // tile
    r = tile // HALO
    last = n // HALO - 1
    vec = pl.BlockSpec((None, 1, d), lambda i, j: (i // tpg, 0, 0))
    cb = conv_b.reshape(1, 2 * f)
    return pl.pallas_call(
        functools.partial(_ffn_kernel, seq_len, tile),
        grid=(n // tile, nh),
        in_specs=[
            pl.BlockSpec((tile, d), lambda i, j: (i, 0)),
            pl.BlockSpec((HALO, d), lambda i, j: (jnp.maximum(i * r - 1, 0), 0)),
            pl.BlockSpec((HALO, d), lambda i, j: (jnp.minimum(i * r + r, last), 0)),
            pl.BlockSpec((1, d), lambda i, j: (0, 0)),
            vec, vec, vec,
            pl.BlockSpec((d, hc), lambda i, j: (0, j)),
            pl.BlockSpec((d, hc), lambda i, j: (0, nh + j)),
            pl.BlockSpec((3, hc), lambda i, j: (0, j)),
            pl.BlockSpec((3, hc), lambda i, j: (0, nh + j)),
            pl.BlockSpec((1, hc), lambda i, j: (0, j)),
            pl.BlockSpec((1, hc), lambda i, j: (0, nh + j)),
            pl.BlockSpec((hc, d), lambda i, j: (j, 0)),
        ],
        out_specs=pl.BlockSpec((tile, d), lambda i, j: (i, 0)),
        out_shape=jax.ShapeDtypeStruct((n, d), F32),
        scratch_shapes=[pltpu.VMEM((tile + 2 * HALO, d), BF16), pltpu.VMEM((tile, d), F32)],
        compiler_params=_cp("parallel", "arbitrary"),
        name="conv_ffn",
    )(x2d, x2d, x2d, g, sc, sh, gate, w_up, w_up, conv_w, conv_w, cb, cb, w_down)


def kernel(x, c, ctx, c_ctx, w_mod, b_mod, norm1_g, norm2_g, w_in, w_out, qk_gain, diff_lam, diff_subln,
           win_sink, pool_w, pool_scale, hy_conv_w, hy_conv_b, hy_w1, hy_b1, hy_w2, hy_b2, hy_w3, hy_freq,
           hy_bias, ffn_w_in, ffn_conv_w, ffn_conv_b, ffn_w_out):
    batch, seq, d = x.shape
    clen = ctx.shape[1]
    depth = w_in.shape[0]
    a_heads = d // 512
    a_width = a_heads * 2 * HEAD_DIM
    b_heads = d // 256
    b_kv = b_heads // 4
    b_width = b_heads * HEAD_DIM
    b_kvw = b_kv * HEAD_DIM
    c_width = d // 4
    d_width = d - a_width - b_width - c_width
    off_ka = a_width
    off_va = 2 * a_width
    off_qb = 3 * a_width
    off_kb = off_qb + b_width
    off_vb = off_kb + b_kvw
    off_pool = off_vb + b_kvw
    off_hy = off_pool + c_width
    in_cols = off_hy + 3 * d_width
    order = ((off_hy, in_cols), (off_pool, off_hy), (off_va, off_qb), (0, off_ka), (off_ka, off_va),
             (off_qb, off_kb), (off_kb, off_vb), (off_vb, off_pool))
    n_hy, n_pool, n_va, n_qa, n_ka, n_qb, n_kb, n_vb = (
        sum(e - s for s, e in order[:k]) for k in range(len(order)))
    assert a_width % LANE == 0 and b_width % LANE == 0 and b_kvw == LANE and c_width // 4 == LANE
    assert seq % BLOCK == 0 and clen % BLOCK == 0 and n_pool % c_width == 0
    assert (n_ka - n_qa) == a_width and (n_qb - n_qa) % b_width == 0 and n_kb - n_qb == b_width

    rows = -(-(batch + 1) // SUBLANE) * SUBLANE
    cvec = jnp.zeros((rows, d), F32).at[:batch].set(c).at[batch].set(c_ctx)
    mods = _modulation(cvec, w_mod, b_mod)

    w_in_b = jnp.concatenate([w_in[:, :, s:e] for s, e in order], axis=-1).astype(BF16)
    w_out_b = w_out.astype(BF16)
    ffn_in_b = ffn_w_in.astype(BF16)
    ffn_out_b = ffn_w_out.astype(BF16)
    pool_w_b = pool_w.astype(BF16)

    cos_t, sin_t = _rope_tables(seq)
    scale = HEAD_DIM ** -0.5
    tile2 = lambda v: jnp.tile(v, LANE // HEAD_DIM)

    xs = x.reshape(batch * seq, d)
    cs = ctx.reshape(batch * clen, d)
    for l in range(depth):
        last = l == depth - 1
        m = mods[l]
        sh1, sc1, g1, sh2, sc2, g2 = [m[:batch, i * d:(i + 1) * d].reshape(batch, 1, d) for i in range(6)]
        csh1, csc1, cg1, csh2, csc2, cg2 = [m[batch, i * d:(i + 1) * d].reshape(1, 1, d) for i in range(6)]
        n1g = norm1_g[l].reshape(1, d)
        n2g = norm2_g[l].reshape(1, d)

        p = _in_projection(xs, n1g, sc1, sh1, w_in_b[l], seq)
        pc = _in_projection(cs, n1g, csc1, csh1, w_in_b[l], batch * clen)

        na = a_width // LANE
        nbq = b_width // LANE
        gains = jnp.stack([tile2(qk_gain[l, 0]) * (scale * LOG2E)] * na + [tile2(qk_gain[l, 1])] * na
                          + [tile2(qk_gain[l, 2]) * scale] * nbq + [tile2(qk_gain[l, 3])])
        qk = _qk_prep(p, n_qa // LANE, gains, cos_t, sin_t, seq, True)
        qk_c = _qk_prep(pc, n_qa // LANE, gains, cos_t, sin_t, clen, False)
        kb_blk = 2 * na + nbq
        qb_blk = (n_qb - n_qa) // b_width

        lam_p = diff_lam[l].astype(F32)
        lambda_init = 0.8 - 0.6 * math.exp(-0.3 * l)
        lam = jnp.exp(jnp.sum(lam_p[0] * lam_p[1])) - jnp.exp(jnp.sum(lam_p[2] * lam_p[3])) + lambda_init
        lam_v = jnp.full((1, LANE), lam, F32)
        subg = diff_subln[l].reshape(1, LANE)
        sink_tab = jnp.broadcast_to(win_sink[l].astype(F32)[:, None], (b_heads, LANE))
        hy = (hy_conv_w[l], hy_conv_b[l], hy_w1[l], hy_b1[l], hy_w2[l], hy_b2[l], hy_w3[l], hy_freq[l],
              hy_bias[l])

        vt = _transposed_values(p, n_va // LANE, a_heads, seq, _tile(seq, 512))
        vt_c = _transposed_values(pc, n_va // LANE, a_heads, clen, clen)
        o_a = _diff_attention(qk, seq, qk_c, vt_c, clen, qk, vt, seq, lam_v, subg, 1.0 - lambda_init,
                              a_heads, na)
        o_b = _window_attention(qk, seq, qk_c, pc, clen, p, sink_tab, b_heads, b_kv,
                                qb_blk, kb_blk, n_vb // LANE, True)
        o_c = _pool_mixer(p, n_pool, c_width, seq, pool_w_b[l], pool_scale[l].reshape(1, c_width))
        o_d = _hyena_mixer(p, n_hy, 3 * d_width, batch, seq, *hy)
        xs = _out_projection(xs, (o_a, o_b, o_c, o_d), w_out_b[l], g1, seq)
        xs = _conv_ffn(xs, n2g, sc2, sh2, g2, ffn_in_b[l], ffn_conv_w[l], ffn_conv_b[l], ffn_out_b[l],
                       seq, seq)

        if not last:
            o_a_c = _diff_attention(qk_c, clen, qk_c, vt_c, clen, None, None, 0, lam_v, subg,
                                    1.0 - lambda_init, a_heads, na)
            o_b_c = _window_attention(qk_c, clen, qk_c, pc, clen, None, sink_tab, b_heads, b_kv,
                                      qb_blk, kb_blk, n_vb // LANE, False)
            o_c_c = _pool_mixer(pc, n_pool, c_width, clen, pool_w_b[l], pool_scale[l].reshape(1, c_width))
            o_d_c = _hyena_mixer(pc, n_hy, 3 * d_width, batch, clen, *hy)
            cs = _out_projection(cs, (o_a_c, o_b_c, o_c_c, o_d_c), w_out_b[l], cg1, batch * clen)
            cs = _conv_ffn(cs, n2g, csc2, csh2, cg2, ffn_in_b[l], ffn_conv_w[l], ffn_conv_b[l],
                           ffn_out_b[l], clen, batch * clen)
    return xs.reshape(batch, seq, d)
```

```python
import functools
import math

import jax
import jax.numpy as jnp
from jax import lax
from jax.experimental import pallas as pl
from jax.experimental.pallas import tpu as pltpu

F32 = jnp.float32
BF16 = jnp.bfloat16

HEAD_DIM = 64
GRID_W = 64
ROPE_BASE = 10000.0
EPS = 1e-6
NEG_INF = -1e30
WINDOW = 128
BLOCK = 128
POOL_WINDOWS = (2, 4, 8, 16)
HY_EMB = 33
HY_HIDDEN = 64
HY_FAST_DECAY = 0.3
HY_SLOW_DECAY = 1.5
HY_TARGET = 1e-2

LANE = 128
SUBLANE = 8
HALO = SUBLANE
DFT_RADIX = 128
DFT_PASSES = 1
VT_ROWS = LANE + 16
LOG2E = 1.4426950408889634
VMEM_LIMIT = 56 * 1024 * 1024


def _cp(*sem):
    return pltpu.CompilerParams(dimension_semantics=sem, vmem_limit_bytes=VMEM_LIMIT)


def _tile(n, pref):
    if n <= pref:
        return n
    t = pref
    while n % t:
        t //= 2
    assert t >= SUBLANE, (n, pref)
    return t


def _dot(a, b):
    return jnp.dot(a, b, preferred_element_type=F32)


def _dot_nt(a, b):
    return lax.dot_general(a, b, (((1,), (1,)), ((), ())), preferred_element_type=F32)


def _split_bf16(x):
    hi = x.astype(BF16)
    lo = (x - hi.astype(F32)).astype(BF16)
    return hi, lo


def _dot3(a_hi, a_lo, b):
    if DFT_PASSES == 1:
        return _dot(a_hi, b.astype(BF16))
    b_hi, b_lo = _split_bf16(b)
    return _dot(a_hi, b_hi) + _dot(a_hi, b_lo) + _dot(a_lo, b_hi)


def _mod_kernel(c_ref, w_ref, b_ref, o_ref):
    c = c_ref[...]
    a = (c * jax.nn.sigmoid(c)).astype(BF16)
    o_ref[...] = _dot(a, w_ref[...].astype(BF16)) + b_ref[...]


def _modulation(cvec, w_mod, b_mod):
    depth, d, n = w_mod.shape
    rows = cvec.shape[0]
    tn = _tile(n, 1024)
    return pl.pallas_call(
        _mod_kernel,
        grid=(depth, n // tn),
        in_specs=[
            pl.BlockSpec((rows, d), lambda l, j: (0, 0)),
            pl.BlockSpec((None, d, tn), lambda l, j: (l, 0, j)),
            pl.BlockSpec((None, 1, tn), lambda l, j: (l, 0, j)),
        ],
        out_specs=pl.BlockSpec((None, rows, tn), lambda l, j: (l, 0, j)),
        out_shape=jax.ShapeDtypeStruct((depth, rows, n), F32),
        compiler_params=_cp("parallel", "parallel"),
        name="adaln_mod",
    )(cvec, w_mod, b_mod.reshape(depth, 1, n))


def _norm_mod(x, g, sc, sh):
    ms = jnp.mean(x * x, axis=-1, keepdims=True)
    return (x * lax.rsqrt(ms + EPS) * g) * (1.0 + sc) + sh


def _inproj_kernel(x_ref, g_ref, sc_ref, sh_ref, w_ref, o_ref):
    h = _norm_mod(x_ref[...], g_ref[...], sc_ref[...], sh_ref[...])
    o_ref[...] = _dot(h.astype(BF16), w_ref[...])


def _in_projection(x2d, g, sc, sh, w, rows_per_group):
    n, d = x2d.shape
    nc = w.shape[1]
    tm = _tile(rows_per_group, 512)
    tpg = rows_per_group // tm
    ncol = 2 if (nc // 2) % LANE == 0 else 1
    tn = nc // ncol
    vec = pl.BlockSpec((None, 1, d), lambda j, i: (i // tpg, 0, 0))
    return pl.pallas_call(
        _inproj_kernel,
        grid=(ncol, n // tm),
        in_specs=[
            pl.BlockSpec((tm, d), lambda j, i: (i, 0)),
            pl.BlockSpec((1, d), lambda j, i: (0, 0)),
            vec, vec,
            pl.BlockSpec((d, tn), lambda j, i: (0, j)),
        ],
        out_specs=pl.BlockSpec((tm, tn), lambda j, i: (i, j)),
        out_shape=jax.ShapeDtypeStruct((n, nc), F32),
        compiler_params=_cp("parallel", "parallel"),
        name="in_proj",
    )(x2d, g, sc, sh, w)


def _qkprep_kernel(rope, x_ref, gain_ref, cos_ref, sin_ref, gm_ref, o_ref):
    x = x_ref[...]
    hi, lo = _split_bf16(x * x)
    gm = gm_ref[...]
    ms = _dot(hi, gm) + _dot(lo, gm)
    y = x * lax.rsqrt(ms + EPS) * gain_ref[...]
    if rope:
        lane = lax.broadcasted_iota(jnp.int32, y.shape, 1)
        first_half = (lane & 31) < 16
        partner = jnp.where(first_half, pltpu.roll(y, LANE - 16, 1), pltpu.roll(y, 16, 1))
        y = y * cos_ref[...] + partner * sin_ref[...]
    o_ref[...] = y.astype(BF16)


def _qk_prep(p, col_block0, gains, cos_t, sin_t, seq_len, rope):
    n = p.shape[0]
    nblk = gains.shape[0]
    tm = _tile(seq_len, 1024)
    tps = seq_len // tm
    lane = jnp.arange(LANE)
    gm = ((lane[:, None] // HEAD_DIM) == (lane[None, :] // HEAD_DIM)).astype(BF16) * (1.0 / HEAD_DIM)
    tab = pl.BlockSpec((tm, LANE), lambda i, j: (i % tps, 0))
    return pl.pallas_call(
        functools.partial(_qkprep_kernel, rope),
        grid=(n // tm, nblk),
        in_specs=[
            pl.BlockSpec((tm, LANE), lambda i, j: (i, col_block0 + j)),
            pl.BlockSpec((None, 1, LANE), lambda i, j: (j, 0, 0)),
            tab, tab,
            pl.BlockSpec((LANE, LANE), lambda i, j: (0, 0)),
        ],
        out_specs=pl.BlockSpec((tm, LANE), lambda i, j: (i, j)),
        out_shape=jax.ShapeDtypeStruct((n, nblk * LANE), BF16),
        compiler_params=_cp("parallel", "parallel"),
        name="qk_prep",
    )(p, gains.reshape(nblk, 1, LANE), cos_t, sin_t, gm.astype(BF16))


def _rope_tables(seq_len):
    rows = seq_len // GRID_W
    row = jnp.repeat(jnp.arange(rows, dtype=F32), GRID_W)
    col = jnp.tile(jnp.arange(GRID_W, dtype=F32), rows)
    half = HEAD_DIM // 2
    inv = ROPE_BASE ** (-jnp.arange(0, half, 2, dtype=F32) / half)
    ang = jnp.concatenate([row[:, None] * inv, col[:, None] * inv], axis=-1)
    lane = jnp.arange(LANE)
    within = lane % HEAD_DIM
    idx = (within // 32) * 16 + (within % 16)
    sign = jnp.where((lane % 32) < 16, -1.0, 1.0).astype(F32)
    return jnp.cos(ang)[:, idx], jnp.sin(ang)[:, idx] * sign


def _vt_kernel(x_ref, o_ref):
    chunk = x_ref.shape[0]
    row = lax.broadcasted_iota(jnp.int32, (VT_ROWS - LANE, chunk), 0)
    ones_row = jnp.where(row == 0, 1.0, 0.0).astype(BF16)
    o_ref[...] = jnp.concatenate([x_ref[...].T.astype(BF16), ones_row], axis=0)


def _transposed_values(p, col_block0, nblk, seq_len, chunk):
    n = p.shape[0]
    return pl.pallas_call(
        _vt_kernel,
        grid=(n // chunk, nblk),
        in_specs=[pl.BlockSpec((chunk, LANE), lambda i, j: (i, col_block0 + j))],
        out_specs=pl.BlockSpec((None, None, VT_ROWS, chunk), lambda i, j: (j, i, 0, 0)),
        out_shape=jax.ShapeDtypeStruct((nblk, n // chunk, VT_ROWS, chunk), BF16),
        compiler_params=_cp("parallel", "parallel"),
        name="values_t",
    )(p)


def _attn_a_kernel(n_lat, tk, sub_scale, *refs):
    if n_lat:
        (q_ref, kc_ref, vc_ref, k_ref, v_ref, lam_ref, subg_ref, o_ref, qt, m, acc,
         s0, s1, x0, x1, p0, p1, al0, al1) = refs
    else:
        q_ref, kc_ref, vc_ref, lam_ref, subg_ref, o_ref, qt, m, acc = refs
    tq = q_ref.shape[0]
    q_t = q_ref[...].astype(F32).T
    row = lax.broadcasted_iota(jnp.int32, q_t.shape, 0)
    qt[:, :tq] = jnp.where(row < HEAD_DIM, q_t, 0.0).astype(BF16)
    qt[:, tq:] = jnp.where(row >= HEAD_DIM, q_t, 0.0).astype(BF16)
    m[...] = jnp.full(m.shape, -jnp.inf, F32)
    acc[...] = jnp.zeros(acc.shape, F32)

    def softmax_update(s, s_max):
        m_prev = m[...]
        m_new = jnp.maximum(m_prev, s_max)
        m[...] = m_new
        return jnp.exp2(s - m_new).astype(BF16), jnp.exp2(m_prev - m_new)

    s_c = _dot(kc_ref[...], qt[...])
    p_c, alpha_c = softmax_update(s_c, jnp.max(s_c, axis=0, keepdims=True))
    acc[...] = alpha_c * acc[...] + _dot(vc_ref[0], p_c)

    if n_lat:
        def scores(c, s_buf, x_buf):
            s = _dot(k_ref[pl.ds(pl.multiple_of(c * tk, tk), tk), :], qt[...])
            s_buf[...] = s
            x_buf[...] = jnp.max(s, axis=0, keepdims=True)

        def softmax(s_buf, x_buf, p_buf, al_buf):
            p_buf[...], al_buf[...] = softmax_update(s_buf[...], x_buf[...])

        def values(c, p_buf, al_buf):
            acc[...] = al_buf[...] * acc[...] + _dot(v_ref[c], p_buf[...])

        scores(0, s0, x0)
        if n_lat == 1:
            softmax(s0, x0, p0, al0)
            values(0, p0, al0)
        else:
            assert n_lat % 2 == 0
            scores(1, s1, x1)
            softmax(s0, x0, p0, al0)

            def body(t, carry):
                c = 2 * t
                scores(c + 2, s0, x0)
                softmax(s1, x1, p1, al1)
                values(c, p0, al0)
                scores(c + 3, s1, x1)
                softmax(s0, x0, p0, al0)
                values(c + 1, p1, al1)
                return carry
            lax.fori_loop(0, n_lat // 2 - 1, body, 0)
            softmax(s1, x1, p1, al1)
            values(n_lat - 2, p0, al0)
            values(n_lat - 1, p1, al1)

    o_t = acc[:LANE] / acc[LANE:LANE + 1]
    o = (o_t[:, :tq] - lam_ref[0:1, 0:1] * o_t[:, tq:]).T
    ms = jnp.mean(o * o, axis=-1, keepdims=True)
    o_ref[...] = ((o * lax.rsqrt(ms + EPS) * subg_ref[...]) * sub_scale).astype(BF16)


def _diff_attention(q_arr, q_len, qk_ctx, vt_ctx, ctx_len, qk_lat, vt_lat, lat_len, lam, subg,
                    sub_scale, heads, k_col0):
    batch = q_arr.shape[0] // q_len
    tq = _tile(q_len, 512)
    nq = q_len // tq
    in_specs = [
        pl.BlockSpec((tq, LANE), lambda b, h, i: (b * nq + i, h)),
        pl.BlockSpec((ctx_len, LANE), lambda b, h, i: (b, k_col0 + h)),
        pl.BlockSpec((None, 1, VT_ROWS, ctx_len), lambda b, h, i: (h, b, 0, 0)),
    ]
    args = [q_arr, qk_ctx, vt_ctx]
    n_lat, tk = 0, 0
    if qk_lat is not None:
        tk = vt_lat.shape[-1]
        n_lat = lat_len // tk
        in_specs += [
            pl.BlockSpec((lat_len, LANE), lambda b, h, i: (b, k_col0 + h)),
            pl.BlockSpec((None, n_lat, VT_ROWS, tk), lambda b, h, i: (h, b, 0, 0)),
        ]
        args += [qk_lat, vt_lat]
    in_specs += [pl.BlockSpec((1, LANE), lambda b, h, i: (0, 0))] * 2
    args += [lam, subg]
    stat = pltpu.VMEM((1, 2 * tq), F32)
    scratch = [pltpu.VMEM((LANE, 2 * tq), BF16), stat, pltpu.VMEM((VT_ROWS, 2 * tq), F32)]
    if n_lat:
        scratch += [pltpu.VMEM((tk, 2 * tq), F32)] * 2 + [stat] * 2
        scratch += [pltpu.VMEM((tk, 2 * tq), BF16)] * 2 + [stat] * 2
    return pl.pallas_call(
        functools.partial(_attn_a_kernel, n_lat, tk, sub_scale),
        grid=(batch, heads, nq),
        in_specs=in_specs,
        out_specs=pl.BlockSpec((tq, LANE), lambda b, h, i: (b * nq + i, h)),
        out_shape=jax.ShapeDtypeStruct((batch * q_len, heads * LANE), BF16),
        scratch_shapes=scratch,
        compiler_params=_cp("parallel", "parallel", "parallel"),
        name="diff_attn",
    )(*args)


def _attn_b_kernel(has_loc, seq_len, q_heads, group, *refs):
    if has_loc:
        (q_ref, kx_ref, vx_ref, kp_ref, kc_ref, kn_ref, vp_ref, vc_ref, vn_ref, sink_ref, o_ref) = refs
    else:
        (q_ref, kx_ref, vx_ref, sink_ref, o_ref) = refs
    n = pl.program_id(1)
    clen = kx_ref.shape[0]
    q_t = q_ref[...].astype(F32).T
    keys = [kx_ref[...]]
    vals = [vx_ref[0]]
    if has_loc:
        keys += [kp_ref[...], kc_ref[...], kn_ref[...]]
        vals += [vp_ref[0], vc_ref[0], vn_ref[0]]
        kpos = (n - 1) * BLOCK + lax.broadcasted_iota(jnp.int32, (3 * BLOCK, BLOCK), 0)
        qpos = n * BLOCK + lax.broadcasted_iota(jnp.int32, (3 * BLOCK, BLOCK), 1)
        valid = (kpos >= 0) & (kpos < seq_len) & (jnp.abs(kpos - qpos) <= WINDOW)
        valid = jnp.concatenate([valid] * group, axis=1)
    k_all = jnp.concatenate(keys, axis=0)
    v_all = jnp.concatenate(vals, axis=1)
    zeros = jnp.zeros((HEAD_DIM, BLOCK), F32)
    heads_out = []
    for kvh in range(q_heads // group):
        cols = []
        for g in range(group):
            hq = kvh * group + g
            blk = q_t[hq * HEAD_DIM:(hq + 1) * HEAD_DIM]
            cols.append(jnp.concatenate([blk, zeros] if kvh == 0 else [zeros, blk], axis=0))
        q_kv = jnp.concatenate(cols, axis=1).astype(BF16)
        sink = jnp.concatenate([sink_ref[kvh * group + g:kvh * group + g + 1, :] for g in range(group)],
                               axis=1) * LOG2E
        s = _dot(k_all, q_kv)
        if has_loc:
            s = jnp.concatenate([s[:clen], jnp.where(valid, s[clen:], NEG_INF)], axis=0)
        m = jnp.maximum(jnp.max(s, axis=0, keepdims=True), sink)
        o_t = _dot(v_all, jnp.exp2(s - m).astype(BF16))
        den = o_t[LANE:LANE + 1] + jnp.exp2(sink - m)
        o_t = o_t[kvh * HEAD_DIM:(kvh + 1) * HEAD_DIM] / den
        heads_out += [o_t[:, g * BLOCK:(g + 1) * BLOCK] for g in range(group)]
    o_ref[...] = jnp.concatenate(heads_out, axis=0).T.astype(BF16)


def _window_attention(q_arr, q_len, qk_ctx, vt_ctx, ctx_len, vt_lat, sink_tab, q_heads, kv_heads,
                      q_blk, k_blk, local):
    assert kv_heads == 2, "key/value heads must fill exactly one 128-lane block"
    batch = q_arr.shape[0] // q_len
    nb = q_len // BLOCK
    qw = q_heads * HEAD_DIM
    in_specs = [
        pl.BlockSpec((BLOCK, qw), lambda b, n: (b * nb + n, q_blk)),
        pl.BlockSpec((ctx_len, LANE), lambda b, n: (b, k_blk)),
        pl.BlockSpec((None, 1, VT_ROWS, ctx_len), lambda b, n: (0, b, 0, 0)),
    ]
    args = [q_arr, qk_ctx, vt_ctx]
    if local:
        prev = lambda b, n: b * nb + jnp.maximum(n - 1, 0)
        nxt = lambda b, n: b * nb + jnp.minimum(n + 1, nb - 1)
        in_specs += [
            pl.BlockSpec((BLOCK, LANE), lambda b, n: (prev(b, n), k_blk)),
            pl.BlockSpec((BLOCK, LANE), lambda b, n: (b * nb + n, k_blk)),
            pl.BlockSpec((BLOCK, LANE), lambda b, n: (nxt(b, n), k_blk)),
            pl.BlockSpec((None, 1, VT_ROWS, BLOCK), lambda b, n: (0, prev(b, n), 0, 0)),
            pl.BlockSpec((None, 1, VT_ROWS, BLOCK), lambda b, n: (0, b * nb + n, 0, 0)),
            pl.BlockSpec((None, 1, VT_ROWS, BLOCK), lambda b, n: (0, nxt(b, n), 0, 0)),
        ]
        args += [q_arr] * 3 + [vt_lat] * 3
    in_specs.append(pl.BlockSpec((q_heads, LANE), lambda b, n: (0, 0)))
    args.append(sink_tab)
    return pl.pallas_call(
        functools.partial(_attn_b_kernel, local, q_len, q_heads, q_heads // kv_heads),
        grid=(batch, nb),
        in_specs=in_specs,
        out_specs=pl.BlockSpec((BLOCK, qw), lambda b, n: (b * nb + n, 0)),
        out_shape=jax.ShapeDtypeStruct((batch * q_len, qw), BF16),
        compiler_params=_cp("parallel", "parallel"),
        name="window_attn",
    )(*args)


def _halo_specs(tile, width, col_blk, n_rows):
    r = tile // HALO
    last = n_rows // HALO - 1
    return [
        pl.BlockSpec((tile, width), lambda i: (i, col_blk)),
        pl.BlockSpec((HALO, width), lambda i: (jnp.maximum(i * r - 1, 0), col_blk)),
        pl.BlockSpec((HALO, width), lambda i: (jnp.minimum(i * r + r, last), col_blk)),
    ]


def _outside_rows(tile, tile_idx, tiles_per_seq):
    row = lax.broadcasted_iota(jnp.int32, (tile + 2 * HALO, 1), 0)
    lo = jnp.where(tile_idx == 0, HALO, 0)
    hi = jnp.where(tile_idx == tiles_per_seq - 1, HALO + tile, tile + 2 * HALO)
    return (row < lo) | (row >= hi)


def _with_halo(x, xp, xn, tiles_per_seq):
    ext = jnp.concatenate([xp, x, xn], axis=0)
    outside = _outside_rows(x.shape[0], pl.program_id(0) % tiles_per_seq, tiles_per_seq)
    return jnp.where(outside, 0.0, ext)


def _conv3(ext, w, b, tile):
    rows = ext.shape[0]
    c = (pltpu.roll(ext, 1, 0) * w[0:1] + ext * w[1:2] + pltpu.roll(ext, rows - 1, 0) * w[2:3])
    return c[HALO:HALO + tile] + b


def _pool_kernel(seq_len, tile, x_ref, xp_ref, xn_ref, w_ref, ls_ref, o_ref):
    tps = seq_len // tile
    x = x_ref[...]
    ext = _with_halo(x, xp_ref[...], xn_ref[...], tps)
    rows = tile + 2 * HALO
    t = (pl.program_id(0) % tps) * tile + lax.broadcasted_iota(jnp.int32, (tile, 1), 0)
    gw = x.shape[1] // len(POOL_WINDOWS)
    outs = []
    for g, w in enumerate(POOL_WINDOWS):
        lo = w // 2
        hi = w - 1 - lo
        a = ext[:, g * gw:(g + 1) * gw]
        span = 1
        while span < w:
            a = a + pltpu.roll(a, span, 0)
            span *= 2
        win = (pltpu.roll(a, rows - hi, 0) if hi else a)[HALO:HALO + tile]
        cnt = jnp.minimum(t + hi + 1, seq_len) - jnp.maximum(t - lo, 0)
        d = win / cnt.astype(F32) - x[:, g * gw:(g + 1) * gw]
        outs.append(_dot(d.astype(BF16), w_ref[g]))
    o_ref[...] = (jnp.concatenate(outs, axis=1) * ls_ref[...]).astype(BF16)


def _pool_mixer(p, col0, width, seq_len, w_lin, ls):
    n = p.shape[0]
    tile = _tile(seq_len, 512)
    g, gw, _ = w_lin.shape
    return pl.pallas_call(
        functools.partial(_pool_kernel, seq_len, tile),
        grid=(n // tile,),
        in_specs=_halo_specs(tile, width, col0 // width, n)
        + [pl.BlockSpec((g, gw, gw), lambda i: (0, 0, 0)),
           pl.BlockSpec((1, width), lambda i: (0, 0))],
        out_specs=pl.BlockSpec((tile, width), lambda i: (i, 0)),
        out_shape=jax.ShapeDtypeStruct((n, width), BF16),
        compiler_params=_cp("parallel"),
        name="pool_mixer",
    )(p, p, p, w_lin, ls)


def _hypre_kernel(seq_len, tile, x_ref, xp_ref, xn_ref, cw_ref, cb_ref, z_ref, x2_ref):
    ext = _with_halo(x_ref[...], xp_ref[...], xn_ref[...], seq_len // tile)
    c = _conv3(ext, cw_ref[...], cb_ref[...], tile)
    dw = c.shape[1] // 3
    z_ref[...] = c[:, :dw] * c[:, dw:2 * dw]
    x2_ref[...] = c[:, 2 * dw:]


def _hyena_pre(p, col0, width, seq_len, conv_w, conv_b):
    n = p.shape[0]
    tile = _tile(seq_len, 512)
    dw = width // 3
    out = jax.ShapeDtypeStruct((n, dw), F32)
    return pl.pallas_call(
        functools.partial(_hypre_kernel, seq_len, tile),
        grid=(n // tile,),
        in_specs=_halo_specs(tile, width, col0 // width, n)
        + [pl.BlockSpec((3, width), lambda i: (0, 0)),
           pl.BlockSpec((1, width), lambda i: (0, 0))],
        out_specs=[pl.BlockSpec((tile, dw), lambda i: (i, 0))] * 2,
        out_shape=[out, out],
        compiler_params=_cp("parallel"),
        name="hyena_pre",
    )(p, p, p, conv_w, conv_b.reshape(1, width))


def _hyfilt_kernel(seq_len, tile, zf_ref, w1_ref, b1_ref, w2_ref, b2_ref, w3_ref, fr_ref, dl_ref,
                   k_ref, ss_ref):
    i = pl.program_id(0)
    hp = functools.partial(jnp.dot, precision=lax.Precision.HIGHEST, preferred_element_type=F32)
    zf = zf_ref[...]
    fr = fr_ref[...]
    h = jnp.sin(fr * (hp(zf, w1_ref[...]) + b1_ref[...]))
    h = jnp.sin(fr * (hp(h, w2_ref[...]) + b2_ref[...]))
    k = hp(h, w3_ref[...]) * jnp.exp(-zf[:, 0:1] * dl_ref[...])
    row = i * tile + lax.broadcasted_iota(jnp.int32, (tile, 1), 0)
    k = jnp.where(row == seq_len, 0.0, k)
    k_ref[...] = k

    @pl.when(i == 0)
    def _():
        ss_ref[...] = jnp.zeros(ss_ref.shape, F32)

    ss_ref[...] += jnp.sum(k * k, axis=0, keepdims=True)


def _hyena_filter(seq_len, w1, b1, w2, b2, w3, freq):
    dw = w3.shape[1] // 2
    bands = (HY_EMB - 1) // 2
    t01 = jnp.linspace(0.0, 1.0, seq_len, dtype=F32)[:, None]
    w_ang = 2.0 * math.pi * jnp.arange(seq_len, dtype=F32)[:, None] / seq_len
    f = jnp.linspace(1e-4, bands - 1, bands, dtype=F32)[None, :]
    z = jnp.concatenate([t01, jnp.cos(f * w_ang), -jnp.sin(f * w_ang)], axis=-1)
    rev = (seq_len - jnp.arange(seq_len)) % seq_len
    zf = jnp.pad(jnp.concatenate([z, z[rev]], axis=0), ((0, 0), (0, LANE - HY_EMB)))
    pad_h = LANE - HY_HIDDEN
    w1p = jnp.pad(w1, ((0, LANE - HY_EMB), (0, pad_h)))
    w2p = jnp.pad(w2, ((0, pad_h), (0, pad_h)))
    w3p = jnp.pad(w3, ((0, pad_h), (0, 0)))
    vec = lambda v: jnp.pad(v, (0, pad_h)).reshape(1, LANE)
    deltas = jnp.abs(jnp.linspace(math.log(HY_TARGET) / HY_FAST_DECAY, math.log(HY_TARGET) / HY_SLOW_DECAY,
                                  dw, dtype=F32)).reshape(1, dw)
    tile = _tile(seq_len, 1024)
    tps = seq_len // tile
    full = lambda shape: pl.BlockSpec(shape, lambda i: (0, 0))
    return pl.pallas_call(
        functools.partial(_hyfilt_kernel, seq_len, tile),
        grid=(2 * tps,),
        in_specs=[pl.BlockSpec((tile, LANE), lambda i: (i, 0)),
                  full((LANE, LANE)), full((1, LANE)), full((LANE, LANE)), full((1, LANE)),
                  pl.BlockSpec((LANE, dw), lambda i: (0, i // tps)),
                  full((1, LANE)), full((1, dw))],
        out_specs=[pl.BlockSpec((tile, dw), lambda i: (i, 0)), full((1, dw))],
        out_shape=[jax.ShapeDtypeStruct((2 * seq_len, dw), F32), jax.ShapeDtypeStruct((1, dw), F32)],
        compiler_params=_cp("arbitrary"),
        name="hyena_filter",
    )(zf, w1p, vec(b1), w2p, vec(b2), w3p, vec(freq), deltas)


def _dft_tables(n1, n2):
    n = n1 * n2
    two_pi = 2.0 * math.pi

    def cs(num, den):
        ang = (num % den).astype(F32) * (two_pi / den)
        return jnp.cos(ang), jnp.sin(ang)

    a = jnp.arange(n1)
    c1, s1 = cs(a[:, None] * a[None, :], n1)
    fwd_a = jnp.concatenate([c1, -s1], axis=0)
    inv_a = jnp.concatenate([c1, -s1], axis=1)[: n1 // 2] / n
    k2 = jnp.arange(n2)
    num = (k2[None, :, None] * k2[None, None, :] * n1 + a[:, None, None] * k2[None, None, :])
    cb, sb = cs(num, n)
    fwd_b = jnp.concatenate([jnp.concatenate([cb, sb], axis=2),
                             jnp.concatenate([-sb, cb], axis=2)], axis=1)
    cbt, sbt = jnp.swapaxes(cb, 1, 2), jnp.swapaxes(sb, 1, 2)
    inv_b = jnp.concatenate([jnp.concatenate([cbt, -sbt], axis=2),
                             jnp.concatenate([sbt, cbt], axis=2)], axis=1)
    return tuple(_split_bf16(m) for m in (fwd_a, inv_a, fwd_b, inv_b))


def _dft_a_kernel(fh_ref, fl_ref, x_ref, re_ref, im_ref):
    y = _dot3(fh_ref[...], fl_ref[...], x_ref[...])
    half = y.shape[0] // 2
    re_ref[...] = y[:half]
    im_ref[...] = y[half:]


def _dft_stage_a(xv, fwd_a, n1):
    b, k, cols = xv.shape
    fh, fl = fwd_a
    ct = _tile(cols, 2048)
    out = jax.ShapeDtypeStruct((b, n1, cols), F32)
    return pl.pallas_call(
        _dft_a_kernel,
        grid=(b, cols // ct),
        in_specs=[pl.BlockSpec((2 * n1, k), lambda bi, j: (0, 0))] * 2
        + [pl.BlockSpec((None, k, ct), lambda bi, j: (bi, 0, j))],
        out_specs=[pl.BlockSpec((None, n1, ct), lambda bi, j: (bi, 0, j))] * 2,
        out_shape=[out, out],
        compiler_params=_cp("parallel", "parallel"),
        name="dft_stage_a",
    )(fh[:, :k], fl[:, :k], xv)


def _dft_b_kernel(mh_ref, ml_ref, re_ref, im_ref, ss_ref, kr_ref, ki_ref):
    x = _dot3(mh_ref[...], ml_ref[...], jnp.concatenate([re_ref[...], im_ref[...]], axis=0))
    nrm = lax.rsqrt(ss_ref[...] + EPS)
    half = x.shape[0] // 2
    kr_ref[...] = x[:half] * nrm
    ki_ref[...] = x[half:] * nrm


def _filter_spectrum(kern, sumsq, tabs, n1, n2):
    fwd_a, _, fwd_b, _ = tabs
    ch = kern.shape[1]
    are, aim = _dft_stage_a(kern.reshape(1, n1, n2 * ch), fwd_a, n1)
    are, aim = are.reshape(n1 * n2, ch), aim.reshape(n1 * n2, ch)
    blk = pl.BlockSpec((n2, ch), lambda i: (i, 0))
    mat = pl.BlockSpec((None, 2 * n2, 2 * n2), lambda i: (i, 0, 0))
    out = jax.ShapeDtypeStruct((n1 * n2, ch), F32)
    return pl.pallas_call(
        _dft_b_kernel,
        grid=(n1,),
        in_specs=[mat, mat, blk, blk, pl.BlockSpec((1, ch), lambda i: (0, 0))],
        out_specs=[blk, blk],
        out_shape=[out, out],
        compiler_params=_cp("parallel"),
        name="filter_spectrum",
    )(fwd_b[0], fwd_b[1], are, aim, sumsq)


def _dft_mul_kernel(fh_ref, fl_ref, gh_ref, gl_ref, re_ref, im_ref, kr_ref, ki_ref, qr_ref, qi_ref):
    x = _dot3(fh_ref[...], fl_ref[...], jnp.concatenate([re_ref[...], im_ref[...]], axis=0))
    half = x.shape[0] // 2
    xr, xi = x[:half], x[half:]
    kr, ki = kr_ref[...], ki_ref[...]
    prod = jnp.concatenate([xr * kr - xi * ki, xr * ki + xi * kr], axis=0)
    q = _dot3(gh_ref[...], gl_ref[...], prod)
    qr_ref[...] = q[:half]
    qi_ref[...] = q[half:]


def _dft_inv_kernel(gh_ref, gl_ref, qr_ref, qi_ref, z_ref, x2_ref, bias_ref, o_ref):
    y = _dot3(gh_ref[...], gl_ref[...], jnp.concatenate([qr_ref[...], qi_ref[...]], axis=0))
    z = z_ref[...]
    o_ref[...] = (x2_ref[...] * (y + z * bias_ref[...])).astype(BF16)


def _hyena_long(z, x2, kr, ki, bias, tabs, batch, seq_len, n1, n2):
    fwd_a, inv_a, fwd_b, inv_b = tabs
    ch = z.shape[1]
    cols = n2 * ch
    zv = z.reshape(batch, n1 // 2, cols)
    are, aim = _dft_stage_a(zv, fwd_a, n1)
    are, aim = are.reshape(batch, n1 * n2, ch), aim.reshape(batch, n1 * n2, ch)
    mat = pl.BlockSpec((None, 2 * n2, 2 * n2), lambda i, b: (i, 0, 0))
    dat = pl.BlockSpec((None, n2, ch), lambda i, b: (b, i, 0))
    spec = pl.BlockSpec((n2, ch), lambda i, b: (i, 0))
    out = jax.ShapeDtypeStruct((batch, n1 * n2, ch), F32)
    qr, qi = pl.pallas_call(
        _dft_mul_kernel,
        grid=(n1, batch),
        in_specs=[mat, mat, mat, mat, dat, dat, spec, spec],
        out_specs=[dat, dat],
        out_shape=[out, out],
        compiler_params=_cp("parallel", "parallel"),
        name="dft_filter_mul",
    )(fwd_b[0], fwd_b[1], inv_b[0], inv_b[1], are, aim, kr, ki)
    qr, qi = qr.reshape(batch, n1, cols), qi.reshape(batch, n1, cols)
    ct = _tile(cols, 2048)
    full = pl.BlockSpec((None, n1, ct), lambda b, j: (b, 0, j))
    half = pl.BlockSpec((None, n1 // 2, ct), lambda b, j: (b, 0, j))
    gmat = pl.BlockSpec((n1 // 2, 2 * n1), lambda b, j: (0, 0))
    o = pl.pallas_call(
        _dft_inv_kernel,
        grid=(batch, cols // ct),
        in_specs=[gmat, gmat, full, full, half, half, pl.BlockSpec((1, ct), lambda b, j: (0, j))],
        out_specs=half,
        out_shape=jax.ShapeDtypeStruct((batch, n1 // 2, cols), BF16),
        compiler_params=_cp("parallel", "parallel"),
        name="dft_inverse_gate",
    )(inv_a[0], inv_a[1], qr, qi, zv, x2.reshape(batch, n1 // 2, cols), jnp.tile(bias, n2).reshape(1, cols))
    return o.reshape(batch * seq_len, ch)


def _hyena_short_kernel(fh_ref, fl_ref, gh_ref, gl_ref, k_ref, ss_ref, z_ref, x2_ref, bias_ref, o_ref):
    fh, fl = fh_ref[...], fl_ref[...]
    z = z_ref[...]
    seq = z.shape[0]
    ks = _dot3(fh, fl, k_ref[...]) * lax.rsqrt(ss_ref[...] + EPS)
    zs = _dot3(fh[:, :seq], fl[:, :seq], z)
    n = ks.shape[0] // 2
    kr, ki, zr, zi = ks[:n], ks[n:], zs[:n], zs[n:]
    prod = jnp.concatenate([zr * kr - zi * ki, zr * ki + zi * kr], axis=0)
    y = _dot3(gh_ref[...], gl_ref[...], prod)
    o_ref[...] = (x2_ref[...] * (y + z * bias_ref[...])).astype(BF16)


def _hyena_short(z, x2, kern, sumsq, bias, batch, seq_len):
    n = 2 * seq_len
    ch = z.shape[1]
    a = jnp.arange(n)
    ang = ((a[:, None] * a[None, :]) % n).astype(F32) * (2.0 * math.pi / n)
    c, s = jnp.cos(ang), jnp.sin(ang)
    fh, fl = _split_bf16(jnp.concatenate([c, -s], axis=0))
    gh, gl = _split_bf16(jnp.concatenate([c, -s], axis=1)[:seq_len] / n)
    full = lambda shape: pl.BlockSpec(shape, lambda b: (0, 0))
    dat = pl.BlockSpec((seq_len, ch), lambda b: (b, 0))
    return pl.pallas_call(
        _hyena_short_kernel,
        grid=(batch,),
        in_specs=[full((2 * n, n)), full((2 * n, n)), full((seq_len, 2 * n)), full((seq_len, 2 * n)),
                  full((n, ch)), full((1, ch)), dat, dat, full((1, ch))],
        out_specs=dat,
        out_shape=jax.ShapeDtypeStruct((batch * seq_len, ch), BF16),
        compiler_params=_cp("parallel"),
        name="hyena_short_conv",
    )(fh, fl, gh, gl, kern, sumsq, z, x2, bias.reshape(1, ch))


def _hyena_mixer(p, col0, width, batch, seq_len, conv_w, conv_b, w1, b1, w2, b2, w3, freq, bias):
    z, x2 = _hyena_pre(p, col0, width, seq_len, conv_w, conv_b)
    kern, sumsq = _hyena_filter(seq_len, w1, b1, w2, b2, w3, freq)
    n = 2 * seq_len
    if n <= 4 * DFT_RADIX:
        return _hyena_short(z, x2, kern, sumsq, bias, batch, seq_len)
    n2 = DFT_RADIX
    n1 = n // n2
    tabs = _dft_tables(n1, n2)
    kr, ki = _filter_spectrum(kern, sumsq, tabs, n1, n2)
    return _hyena_long(z, x2, kr, ki, bias, tabs, batch, seq_len, n1, n2)


def _outproj_kernel(x_ref, a_ref, b_ref, c_ref, d_ref, w_ref, g_ref, o_ref):
    acc = None
    row = 0
    for m_ref in (a_ref, b_ref, c_ref, d_ref):
        wdt = m_ref.shape[1]
        part = _dot(m_ref[...], w_ref[row:row + wdt, :])
        acc = part if acc is None else acc + part
        row += wdt
    o_ref[...] = x_ref[...] + g_ref[...] * acc


def _out_projection(x2d, mixers, w, gate, rows_per_group):
    n, d = x2d.shape
    tm = _tile(rows_per_group, 512)
    tpg = rows_per_group // tm
    return pl.pallas_call(
        _outproj_kernel,
        grid=(n // tm,),
        in_specs=[pl.BlockSpec((tm, d), lambda i: (i, 0))]
        + [pl.BlockSpec((tm, m.shape[1]), lambda i: (i, 0)) for m in mixers]
        + [pl.BlockSpec(w.shape, lambda i: (0, 0)),
           pl.BlockSpec((None, 1, d), lambda i: (i // tpg, 0, 0))],
        out_specs=pl.BlockSpec((tm, d), lambda i: (i, 0)),
        out_shape=jax.ShapeDtypeStruct((n, d), F32),
        compiler_params=_cp("parallel"),
        name="out_proj",
    )(x2d, *mixers, w, gate)


def _ffn_kernel(seq_len, tile, x_ref, xp_ref, xn_ref, g_ref, sc_ref, sh_ref, gate_ref,
                wg_ref, wu_ref, cwg_ref, cwu_ref, cbg_ref, cbu_ref, wd_ref, o_ref, h_scr, acc):
    j = pl.program_id(1)

    @pl.when(j == 0)
    def _():
        ext = jnp.concatenate([xp_ref[...], x_ref[...], xn_ref[...]], axis=0)
        h = _norm_mod(ext, g_ref[...], sc_ref[...], sh_ref[...])
        tps = seq_len // tile
        outside = _outside_rows(tile, pl.program_id(0) % tps, tps)
        h_scr[...] = jnp.where(outside, 0.0, h).astype(BF16)
        acc[...] = jnp.zeros(acc.shape, F32)

    h = h_scr[...]
    gate = _conv3(_dot(h, wg_ref[...]), cwg_ref[...], cbg_ref[...], tile)
    up = _conv3(_dot(h, wu_ref[...]), cwu_ref[...], cbu_ref[...], tile)
    act = (gate * jax.nn.sigmoid(gate)) * up
    acc[...] += _dot(act.astype(BF16), wd_ref[...])

    @pl.when(j == pl.num_programs(1) - 1)
    def _():
        o_ref[...] = x_ref[...] + gate_ref[...] * acc[...]


def _conv_ffn(x2d, g, sc, sh, gate, w_up, conv_w, conv_b, w_down, seq_len, rows_per_group):
    n, d = x2d.shape
    f = w_down.shape[0]
    tile = _tile(seq_len, 512)
    hc = _tile(f, 512)
    nh = f // hc
    tpg = rows_per_group // tile
    r = tile // HALO
    last = n // HALO - 1
    vec = pl.BlockSpec((None, 1, d), lambda i, j: (i // tpg, 0, 0))
    cb = conv_b.reshape(1, 2 * f)
    return pl.pallas_call(
        functools.partial(_ffn_kernel, seq_len, tile),
        grid=(n // tile, nh),
        in_specs=[
            pl.BlockSpec((tile, d), lambda i, j: (i, 0)),
            pl.BlockSpec((HALO, d), lambda i, j: (jnp.maximum(i * r - 1, 0), 0)),
            pl.BlockSpec((HALO, d), lambda i, j: (jnp.minimum(i * r + r, last), 0)),
            pl.BlockSpec((1, d), lambda i, j: (0, 0)),
            vec, vec, vec,
            pl.BlockSpec((d, hc), lambda i, j: (0, j)),
            pl.BlockSpec((d, hc), lambda i, j: (0, nh + j)),
            pl.BlockSpec((3, hc), lambda i, j: (0, j)),
            pl.BlockSpec((3, hc), lambda i, j: (0, nh + j)),
            pl.BlockSpec((1, hc), lambda i, j: (0, j)),
            pl.BlockSpec((1, hc), lambda i, j: (0, nh + j)),
            pl.BlockSpec((hc, d), lambda i, j: (j, 0)),
        ],
        out_specs=pl.BlockSpec((tile, d), lambda i, j: (i, 0)),
        out_shape=jax.ShapeDtypeStruct((n, d), F32),
        scratch_shapes=[pltpu.VMEM((tile + 2 * HALO, d), BF16), pltpu.VMEM((tile, d), F32)],
        compiler_params=_cp("parallel", "arbitrary"),
        name="conv_ffn",
    )(x2d, x2d, x2d, g, sc, sh, gate, w_up, w_up, conv_w, conv_w, cb, cb, w_down)


def kernel(x, c, ctx, c_ctx, w_mod, b_mod, norm1_g, norm2_g, w_in, w_out, qk_gain, diff_lam, diff_subln,
           win_sink, pool_w, pool_scale, hy_conv_w, hy_conv_b, hy_w1, hy_b1, hy_w2, hy_b2, hy_w3, hy_freq,
           hy_bias, ffn_w_in, ffn_conv_w, ffn_conv_b, ffn_w_out):
    batch, seq, d = x.shape
    clen = ctx.shape[1]
    depth = w_in.shape[0]
    a_heads = d // 512
    a_width = a_heads * 2 * HEAD_DIM
    b_heads = d // 256
    b_kv = b_heads // 4
    b_width = b_heads * HEAD_DIM
    b_kvw = b_kv * HEAD_DIM
    c_width = d // 4
    d_width = d - a_width - b_width - c_width
    off_ka = a_width
    off_va = 2 * a_width
    off_qb = 3 * a_width
    off_kb = off_qb + b_width
    off_vb = off_kb + b_kvw
    off_pool = off_vb + b_kvw
    off_hy = off_pool + c_width
    in_cols = off_hy + 3 * d_width
    order = ((off_hy, in_cols), (off_pool, off_hy), (off_va, off_qb), (0, off_ka), (off_ka, off_va),
             (off_qb, off_kb), (off_kb, off_vb), (off_vb, off_pool))
    n_hy, n_pool, n_va, n_qa, n_ka, n_qb, n_kb, n_vb = (
        sum(e - s for s, e in order[:k]) for k in range(len(order)))
    assert a_width % LANE == 0 and b_width % LANE == 0 and b_kvw == LANE and c_width // 4 == LANE
    assert seq % BLOCK == 0 and clen % BLOCK == 0 and n_pool % c_width == 0
    assert (n_ka - n_qa) == a_width and (n_qb - n_qa) % b_width == 0 and n_kb - n_qb == b_width

    rows = -(-(batch + 1) // SUBLANE) * SUBLANE
    cvec = jnp.zeros((rows, d), F32).at[:batch].set(c).at[batch].set(c_ctx)
    mods = _modulation(cvec, w_mod, b_mod)

    w_in_b = jnp.concatenate([w_in[:, :, s:e] for s, e in order], axis=-1).astype(BF16)
    w_out_b = w_out.astype(BF16)
    ffn_in_b = ffn_w_in.astype(BF16)
    ffn_out_b = ffn_w_out.astype(BF16)
    pool_w_b = pool_w.astype(BF16)

    cos_t, sin_t = _rope_tables(seq)
    scale = HEAD_DIM ** -0.5
    tile2 = lambda v: jnp.tile(v, LANE // HEAD_DIM)

    xs = x.reshape(batch * seq, d)
    cs = ctx.reshape(batch * clen, d)
    for l in range(depth):
        last = l == depth - 1
        m = mods[l]
        sh1, sc1, g1, sh2, sc2, g2 = [m[:batch, i * d:(i + 1) * d].reshape(batch, 1, d) for i in range(6)]
        csh1, csc1, cg1, csh2, csc2, cg2 = [m[batch, i * d:(i + 1) * d].reshape(1, 1, d) for i in range(6)]
        n1g = norm1_g[l].reshape(1, d)
        n2g = norm2_g[l].reshape(1, d)

        p = _in_projection(xs, n1g, sc1, sh1, w_in_b[l], seq)
        pc = _in_projection(cs, n1g, csc1, csh1, w_in_b[l], batch * clen)

        na = a_width // LANE
        nbq = b_width // LANE
        gains = jnp.stack([tile2(qk_gain[l, 0]) * (scale * LOG2E)] * na + [tile2(qk_gain[l, 1])] * na
                          + [tile2(qk_gain[l, 2]) * (scale * LOG2E)] * nbq + [tile2(qk_gain[l, 3])])
        qk = _qk_prep(p, n_qa // LANE, gains, cos_t, sin_t, seq, True)
        qk_c = _qk_prep(pc, n_qa // LANE, gains, cos_t, sin_t, clen, False)
        kb_blk = 2 * na + nbq
        qb_blk = (n_qb - n_qa) // b_width

        lam_p = diff_lam[l].astype(F32)
        lambda_init = 0.8 - 0.6 * math.exp(-0.3 * l)
        lam = jnp.exp(jnp.sum(lam_p[0] * lam_p[1])) - jnp.exp(jnp.sum(lam_p[2] * lam_p[3])) + lambda_init
        lam_v = jnp.full((1, LANE), lam, F32)
        subg = diff_subln[l].reshape(1, LANE)
        sink_tab = jnp.broadcast_to(win_sink[l].astype(F32)[:, None], (b_heads, LANE))
        hy = (hy_conv_w[l], hy_conv_b[l], hy_w1[l], hy_b1[l], hy_w2[l], hy_b2[l], hy_w3[l], hy_freq[l],
              hy_bias[l])

        vt = _transposed_values(p, n_va // LANE, a_heads, seq, _tile(seq, 512))
        vt_c = _transposed_values(pc, n_va // LANE, a_heads, clen, clen)
        o_a = _diff_attention(qk, seq, qk_c, vt_c, clen, qk, vt, seq, lam_v, subg, 1.0 - lambda_init,
                              a_heads, na)
        vtb = _transposed_values(p, n_vb // LANE, 1, seq, BLOCK)
        vtb_c = _transposed_values(pc, n_vb // LANE, 1, clen, clen)
        o_b = _window_attention(qk, seq, qk_c, vtb_c, clen, vtb, sink_tab, b_heads, b_kv,
                                qb_blk, kb_blk, True)
        o_c = _pool_mixer(p, n_pool, c_width, seq, pool_w_b[l], pool_scale[l].reshape(1, c_width))
        o_d = _hyena_mixer(p, n_hy, 3 * d_width, batch, seq, *hy)
        xs = _out_projection(xs, (o_a, o_b, o_c, o_d), w_out_b[l], g1, seq)
        xs = _conv_ffn(xs, n2g, sc2, sh2, g2, ffn_in_b[l], ffn_conv_w[l], ffn_conv_b[l], ffn_out_b[l],
                       seq, seq)

        if not last:
            o_a_c = _diff_attention(qk_c, clen, qk_c, vt_c, clen, None, None, 0, lam_v, subg,
                                    1.0 - lambda_init, a_heads, na)
            o_b_c = _window_attention(qk_c, clen, qk_c, vtb_c, clen, None, sink_tab, b_heads, b_kv,
                                      qb_blk, kb_blk, False)
            o_c_c = _pool_mixer(pc, n_pool, c_width, clen, pool_w_b[l], pool_scale[l].reshape(1, c_width))
            o_d_c = _hyena_mixer(pc, n_hy, 3 * d_width, batch, clen, *hy)
            cs = _out_projection(cs, (o_a_c, o_b_c, o_c_c, o_d_c), w_out_b[l], cg1, batch * clen)
            cs = _conv_ffn(cs, n2g, csc2, csh2, cg2, ffn_in_b[l], ffn_conv_w[l], ffn_conv_b[l],
                           ffn_out_b[l], clen, batch * clen)
    return xs.reshape(batch, seq, d)
```

```python
import functools
import math

import jax
import jax.numpy as jnp
from jax import lax
from jax.experimental import pallas as pl
from jax.experimental.pallas import tpu as pltpu

F32 = jnp.float32
BF16 = jnp.bfloat16

HEAD_DIM = 64
GRID_W = 64
ROPE_BASE = 10000.0
EPS = 1e-6
NEG_INF = -1e30
WINDOW = 128
BLOCK = 128
POOL_WINDOWS = (2, 4, 8, 16)
HY_EMB = 33
HY_HIDDEN = 64
HY_FAST_DECAY = 0.3
HY_SLOW_DECAY = 1.5
HY_TARGET = 1e-2

LANE = 128
SUBLANE = 8
HALO = SUBLANE
DFT_RADIX = 128
DFT_BLOCKS = 8
VT_ROWS = LANE + 16
LOG2E = 1.4426950408889634
VMEM_LIMIT = 56 * 1024 * 1024


def _cp(*sem):
    return pltpu.CompilerParams(dimension_semantics=sem, vmem_limit_bytes=VMEM_LIMIT)


def _tile(n, pref):
    if n <= pref:
        return n
    t = pref
    while n % t:
        t //= 2
    assert t >= SUBLANE, (n, pref)
    return t


def _dot(a, b):
    return jnp.dot(a, b, preferred_element_type=F32)


def _dot_nt(a, b):
    return lax.dot_general(a, b, (((1,), (1,)), ((), ())), preferred_element_type=F32)


def _split_bf16(x):
    hi = x.astype(BF16)
    lo = (x - hi.astype(F32)).astype(BF16)
    return hi, lo


def _mod_kernel(c_ref, w_ref, b_ref, o_ref):
    c = c_ref[...]
    a = (c * jax.nn.sigmoid(c)).astype(BF16)
    o_ref[...] = _dot(a, w_ref[...].astype(BF16)) + b_ref[...]


def _modulation(cvec, w_mod, b_mod):
    depth, d, n = w_mod.shape
    rows = cvec.shape[0]
    tn = _tile(n, 1024)
    return pl.pallas_call(
        _mod_kernel,
        grid=(depth, n // tn),
        in_specs=[
            pl.BlockSpec((rows, d), lambda l, j: (0, 0)),
            pl.BlockSpec((None, d, tn), lambda l, j: (l, 0, j)),
            pl.BlockSpec((None, 1, tn), lambda l, j: (l, 0, j)),
        ],
        out_specs=pl.BlockSpec((None, rows, tn), lambda l, j: (l, 0, j)),
        out_shape=jax.ShapeDtypeStruct((depth, rows, n), F32),
        compiler_params=_cp("parallel", "parallel"),
        name="adaln_mod",
    )(cvec, w_mod, b_mod.reshape(depth, 1, n))


def _norm_mod(x, g, sc, sh):
    ms = jnp.mean(x * x, axis=-1, keepdims=True)
    return (x * lax.rsqrt(ms + EPS) * g) * (1.0 + sc) + sh


def _inproj_kernel(x_ref, g_ref, sc_ref, sh_ref, w_ref, o_ref):
    h = _norm_mod(x_ref[...], g_ref[...], sc_ref[...], sh_ref[...])
    o_ref[...] = _dot(h.astype(BF16), w_ref[...])


def _in_projection(x2d, g, sc, sh, w, rows_per_group):
    n, d = x2d.shape
    nc = w.shape[1]
    tm = _tile(rows_per_group, 512)
    tpg = rows_per_group // tm
    ncol = 2 if (nc // 2) % LANE == 0 else 1
    tn = nc // ncol
    vec = pl.BlockSpec((None, 1, d), lambda j, i: (i // tpg, 0, 0))
    return pl.pallas_call(
        _inproj_kernel,
        grid=(ncol, n // tm),
        in_specs=[
            pl.BlockSpec((tm, d), lambda j, i: (i, 0)),
            pl.BlockSpec((1, d), lambda j, i: (0, 0)),
            vec, vec,
            pl.BlockSpec((d, tn), lambda j, i: (0, j)),
        ],
        out_specs=pl.BlockSpec((tm, tn), lambda j, i: (i, j)),
        out_shape=jax.ShapeDtypeStruct((n, nc), F32),
        compiler_params=_cp("parallel", "parallel"),
        name="in_proj",
    )(x2d, g, sc, sh, w)


def _qkprep_kernel(rope, x_ref, gain_ref, cos_ref, sin_ref, gm_ref, o_ref):
    x = x_ref[...]
    hi, lo = _split_bf16(x * x)
    gm = gm_ref[...]
    ms = _dot(hi, gm) + _dot(lo, gm)
    y = x * lax.rsqrt(ms + EPS) * gain_ref[...]
    if rope:
        lane = lax.broadcasted_iota(jnp.int32, y.shape, 1)
        first_half = (lane & 31) < 16
        partner = jnp.where(first_half, pltpu.roll(y, LANE - 16, 1), pltpu.roll(y, 16, 1))
        y = y * cos_ref[...] + partner * sin_ref[...]
    o_ref[...] = y.astype(BF16)


def _qk_prep(p, col_block0, gains, cos_t, sin_t, seq_len, rope):
    n = p.shape[0]
    nblk = gains.shape[0]
    tm = _tile(seq_len, 1024)
    tps = seq_len // tm
    lane = jnp.arange(LANE)
    gm = ((lane[:, None] // HEAD_DIM) == (lane[None, :] // HEAD_DIM)).astype(BF16) * (1.0 / HEAD_DIM)
    tab = pl.BlockSpec((tm, LANE), lambda i, j: (i % tps, 0))
    return pl.pallas_call(
        functools.partial(_qkprep_kernel, rope),
        grid=(n // tm, nblk),
        in_specs=[
            pl.BlockSpec((tm, LANE), lambda i, j: (i, col_block0 + j)),
            pl.BlockSpec((None, 1, LANE), lambda i, j: (j, 0, 0)),
            tab, tab,
            pl.BlockSpec((LANE, LANE), lambda i, j: (0, 0)),
        ],
        out_specs=pl.BlockSpec((tm, LANE), lambda i, j: (i, j)),
        out_shape=jax.ShapeDtypeStruct((n, nblk * LANE), BF16),
        compiler_params=_cp("parallel", "parallel"),
        name="qk_prep",
    )(p, gains.reshape(nblk, 1, LANE), cos_t, sin_t, gm.astype(BF16))


def _rope_tables(seq_len):
    rows = seq_len // GRID_W
    row = jnp.repeat(jnp.arange(rows, dtype=F32), GRID_W)
    col = jnp.tile(jnp.arange(GRID_W, dtype=F32), rows)
    half = HEAD_DIM // 2
    inv = ROPE_BASE ** (-jnp.arange(0, half, 2, dtype=F32) / half)
    ang = jnp.concatenate([row[:, None] * inv, col[:, None] * inv], axis=-1)
    lane = jnp.arange(LANE)
    within = lane % HEAD_DIM
    idx = (within // 32) * 16 + (within % 16)
    sign = jnp.where((lane % 32) < 16, -1.0, 1.0).astype(F32)
    return jnp.cos(ang)[:, idx], jnp.sin(ang)[:, idx] * sign


def _vt_kernel(x_ref, o_ref):
    chunks, _, chunk = o_ref.shape
    row = lax.broadcasted_iota(jnp.int32, (VT_ROWS - LANE, chunk), 0)
    ones_row = jnp.where(row == 0, 1.0, 0.0).astype(BF16)
    for c in range(chunks):
        x_t = x_ref[c * chunk:(c + 1) * chunk, :].T.astype(BF16)
        o_ref[c] = jnp.concatenate([x_t, ones_row], axis=0)


def _transposed_values(p, col_block0, nblk, seq_len, chunk):
    n = p.shape[0]
    rows = max(chunk, _tile(seq_len, 512))
    return pl.pallas_call(
        _vt_kernel,
        grid=(n // rows, nblk),
        in_specs=[pl.BlockSpec((rows, LANE), lambda i, j: (i, col_block0 + j))],
        out_specs=pl.BlockSpec((None, rows // chunk, VT_ROWS, chunk), lambda i, j: (j, i, 0, 0)),
        out_shape=jax.ShapeDtypeStruct((nblk, n // chunk, VT_ROWS, chunk), BF16),
        compiler_params=_cp("parallel", "parallel"),
        name="values_t",
    )(p)


def _attn_a_kernel(n_lat, tk, sub_scale, *refs):
    if n_lat:
        (q_ref, kc_ref, vc_ref, k_ref, v_ref, lam_ref, subg_ref, o_ref, qt, m, acc,
         s0, s1, x0, x1, p0, p1, al0, al1) = refs
    else:
        q_ref, kc_ref, vc_ref, lam_ref, subg_ref, o_ref, qt, m, acc = refs
    tq = q_ref.shape[0]
    q_t = q_ref[...].astype(F32).T
    row = lax.broadcasted_iota(jnp.int32, q_t.shape, 0)
    qt[:, :tq] = jnp.where(row < HEAD_DIM, q_t, 0.0).astype(BF16)
    qt[:, tq:] = jnp.where(row >= HEAD_DIM, q_t, 0.0).astype(BF16)
    m[...] = jnp.full(m.shape, -jnp.inf, F32)
    acc[...] = jnp.zeros(acc.shape, F32)

    def softmax_update(s, s_max):
        m_prev = m[...]
        m_new = jnp.maximum(m_prev, s_max)
        m[...] = m_new
        return jnp.exp2(s - m_new).astype(BF16), jnp.exp2(m_prev - m_new)

    s_c = _dot(kc_ref[...], qt[...])
    p_c, alpha_c = softmax_update(s_c, jnp.max(s_c, axis=0, keepdims=True))
    acc[...] = alpha_c * acc[...] + _dot(vc_ref[0], p_c)

    if n_lat:
        def scores(c, s_buf, x_buf):
            s = _dot(k_ref[pl.ds(pl.multiple_of(c * tk, tk), tk), :], qt[...])
            s_buf[...] = s
            x_buf[...] = jnp.max(s, axis=0, keepdims=True)

        def softmax(s_buf, x_buf, p_buf, al_buf):
            p_buf[...], al_buf[...] = softmax_update(s_buf[...], x_buf[...])

        def values(c, p_buf, al_buf):
            acc[...] = al_buf[...] * acc[...] + _dot(v_ref[c], p_buf[...])

        scores(0, s0, x0)
        if n_lat == 1:
            softmax(s0, x0, p0, al0)
            values(0, p0, al0)
        else:
            assert n_lat % 2 == 0
            scores(1, s1, x1)
            softmax(s0, x0, p0, al0)

            def body(t, carry):
                c = 2 * t
                scores(c + 2, s0, x0)
                softmax(s1, x1, p1, al1)
                values(c, p0, al0)
                scores(c + 3, s1, x1)
                softmax(s0, x0, p0, al0)
                values(c + 1, p1, al1)
                return carry
            lax.fori_loop(0, n_lat // 2 - 1, body, 0)
            softmax(s1, x1, p1, al1)
            values(n_lat - 2, p0, al0)
            values(n_lat - 1, p1, al1)

    o_t = acc[:LANE] / acc[LANE:LANE + 1]
    o = (o_t[:, :tq] - lam_ref[0:1, 0:1] * o_t[:, tq:]).T
    ms = jnp.mean(o * o, axis=-1, keepdims=True)
    o_ref[...] = ((o * lax.rsqrt(ms + EPS) * subg_ref[...]) * sub_scale).astype(BF16)


def _diff_attention(q_arr, q_len, qk_ctx, vt_ctx, ctx_len, qk_lat, vt_lat, lat_len, lam, subg,
                    sub_scale, heads, k_col0):
    batch = q_arr.shape[0] // q_len
    tq = _tile(q_len, 512)
    nq = q_len // tq
    in_specs = [
        pl.BlockSpec((tq, LANE), lambda b, h, i: (b * nq + i, h)),
        pl.BlockSpec((ctx_len, LANE), lambda b, h, i: (b, k_col0 + h)),
        pl.BlockSpec((None, 1, VT_ROWS, ctx_len), lambda b, h, i: (h, b, 0, 0)),
    ]
    args = [q_arr, qk_ctx, vt_ctx]
    n_lat, tk = 0, 0
    if qk_lat is not None:
        tk = vt_lat.shape[-1]
        n_lat = lat_len // tk
        in_specs += [
            pl.BlockSpec((lat_len, LANE), lambda b, h, i: (b, k_col0 + h)),
            pl.BlockSpec((None, n_lat, VT_ROWS, tk), lambda b, h, i: (h, b, 0, 0)),
        ]
        args += [qk_lat, vt_lat]
    in_specs += [pl.BlockSpec((1, LANE), lambda b, h, i: (0, 0))] * 2
    args += [lam, subg]
    stat = pltpu.VMEM((1, 2 * tq), F32)
    scratch = [pltpu.VMEM((LANE, 2 * tq), BF16), stat, pltpu.VMEM((VT_ROWS, 2 * tq), F32)]
    if n_lat:
        scratch += [pltpu.VMEM((tk, 2 * tq), F32)] * 2 + [stat] * 2
        scratch += [pltpu.VMEM((tk, 2 * tq), BF16)] * 2 + [stat] * 2
    return pl.pallas_call(
        functools.partial(_attn_a_kernel, n_lat, tk, sub_scale),
        grid=(batch, heads, nq),
        in_specs=in_specs,
        out_specs=pl.BlockSpec((tq, LANE), lambda b, h, i: (b * nq + i, h)),
        out_shape=jax.ShapeDtypeStruct((batch * q_len, heads * LANE), BF16),
        scratch_shapes=scratch,
        compiler_params=_cp("parallel", "parallel", "parallel"),
        name="diff_attn",
    )(*args)


def _attn_b_kernel(has_loc, seq_len, q_heads, group, *refs):
    if has_loc:
        (q_ref, kx_ref, vx_ref, kp_ref, kc_ref, kn_ref, vp_ref, vc_ref, vn_ref, sink_ref, o_ref) = refs
    else:
        (q_ref, kx_ref, vx_ref, sink_ref, o_ref) = refs
    n = pl.program_id(1)
    clen = kx_ref.shape[0]
    q_t = q_ref[...].astype(F32).T
    keys = [kx_ref[...]]
    vals = [vx_ref[0]]
    if has_loc:
        keys += [kp_ref[...], kc_ref[...], kn_ref[...]]
        vals += [vp_ref[0], vc_ref[0], vn_ref[0]]
        kpos = (n - 1) * BLOCK + lax.broadcasted_iota(jnp.int32, (3 * BLOCK, BLOCK), 0)
        qpos = n * BLOCK + lax.broadcasted_iota(jnp.int32, (3 * BLOCK, BLOCK), 1)
        valid = (kpos >= 0) & (kpos < seq_len) & (jnp.abs(kpos - qpos) <= WINDOW)
        valid = jnp.concatenate([valid] * group, axis=1)
    k_all = jnp.concatenate(keys, axis=0)
    v_all = jnp.concatenate(vals, axis=1)
    zeros = jnp.zeros((HEAD_DIM, BLOCK), F32)
    heads_out = []
    for kvh in range(q_heads // group):
        cols = []
        for g in range(group):
            hq = kvh * group + g
            blk = q_t[hq * HEAD_DIM:(hq + 1) * HEAD_DIM]
            cols.append(jnp.concatenate([blk, zeros] if kvh == 0 else [zeros, blk], axis=0))
        q_kv = jnp.concatenate(cols, axis=1).astype(BF16)
        sink = jnp.concatenate([sink_ref[kvh * group + g:kvh * group + g + 1, :] for g in range(group)],
                               axis=1) * LOG2E
        s = _dot(k_all, q_kv)
        if has_loc:
            s = jnp.concatenate([s[:clen], jnp.where(valid, s[clen:], NEG_INF)], axis=0)
        m = jnp.maximum(jnp.max(s, axis=0, keepdims=True), sink)
        o_t = _dot(v_all, jnp.exp2(s - m).astype(BF16))
        den = o_t[LANE:LANE + 1] + jnp.exp2(sink - m)
        o_t = o_t[kvh * HEAD_DIM:(kvh + 1) * HEAD_DIM] / den
        heads_out += [o_t[:, g * BLOCK:(g + 1) * BLOCK] for g in range(group)]
    o_ref[...] = jnp.concatenate(heads_out, axis=0).T.astype(BF16)


def _window_attention(q_arr, q_len, qk_ctx, vt_ctx, ctx_len, vt_lat, sink_tab, q_heads, kv_heads,
                      q_blk, k_blk, local):
    assert kv_heads == 2, "key/value heads must fill exactly one 128-lane block"
    batch = q_arr.shape[0] // q_len
    nb = q_len // BLOCK
    qw = q_heads * HEAD_DIM
    in_specs = [
        pl.BlockSpec((BLOCK, qw), lambda b, n: (b * nb + n, q_blk)),
        pl.BlockSpec((ctx_len, LANE), lambda b, n: (b, k_blk)),
        pl.BlockSpec((None, 1, VT_ROWS, ctx_len), lambda b, n: (0, b, 0, 0)),
    ]
    args = [q_arr, qk_ctx, vt_ctx]
    if local:
        prev = lambda b, n: b * nb + jnp.maximum(n - 1, 0)
        nxt = lambda b, n: b * nb + jnp.minimum(n + 1, nb - 1)
        in_specs += [
            pl.BlockSpec((BLOCK, LANE), lambda b, n: (prev(b, n), k_blk)),
            pl.BlockSpec((BLOCK, LANE), lambda b, n: (b * nb + n, k_blk)),
            pl.BlockSpec((BLOCK, LANE), lambda b, n: (nxt(b, n), k_blk)),
            pl.BlockSpec((None, 1, VT_ROWS, BLOCK), lambda b, n: (0, prev(b, n), 0, 0)),
            pl.BlockSpec((None, 1, VT_ROWS, BLOCK), lambda b, n: (0, b * nb + n, 0, 0)),
            pl.BlockSpec((None, 1, VT_ROWS, BLOCK), lambda b, n: (0, nxt(b, n), 0, 0)),
        ]
        args += [q_arr] * 3 + [vt_lat] * 3
    in_specs.append(pl.BlockSpec((q_heads, LANE), lambda b, n: (0, 0)))
    args.append(sink_tab)
    return pl.pallas_call(
        functools.partial(_attn_b_kernel, local, q_len, q_heads, q_heads // kv_heads),
        grid=(batch, nb),
        in_specs=in_specs,
        out_specs=pl.BlockSpec((BLOCK, qw), lambda b, n: (b * nb + n, 0)),
        out_shape=jax.ShapeDtypeStruct((batch * q_len, qw), BF16),
        compiler_params=_cp("parallel", "parallel"),
        name="window_attn",
    )(*args)


def _halo_specs(tile, width, col_blk, n_rows):
    r = tile // HALO
    last = n_rows // HALO - 1
    return [
        pl.BlockSpec((tile, width), lambda i: (i, col_blk)),
        pl.BlockSpec((HALO, width), lambda i: (jnp.maximum(i * r - 1, 0), col_blk)),
        pl.BlockSpec((HALO, width), lambda i: (jnp.minimum(i * r + r, last), col_blk)),
    ]


def _outside_rows(tile, tile_idx, tiles_per_seq):
    row = lax.broadcasted_iota(jnp.int32, (tile + 2 * HALO, 1), 0)
    lo = jnp.where(tile_idx == 0, HALO, 0)
    hi = jnp.where(tile_idx == tiles_per_seq - 1, HALO + tile, tile + 2 * HALO)
    return (row < lo) | (row >= hi)


def _with_halo(x, xp, xn, tiles_per_seq):
    ext = jnp.concatenate([xp, x, xn], axis=0)
    outside = _outside_rows(x.shape[0], pl.program_id(0) % tiles_per_seq, tiles_per_seq)
    return jnp.where(outside, 0.0, ext)


def _conv3(ext, w, b, tile):
    rows = ext.shape[0]
    c = (pltpu.roll(ext, 1, 0) * w[0:1] + ext * w[1:2] + pltpu.roll(ext, rows - 1, 0) * w[2:3])
    return c[HALO:HALO + tile] + b


def _pool_kernel(seq_len, tile, x_ref, xp_ref, xn_ref, w_ref, ls_ref, o_ref):
    tps = seq_len // tile
    x = x_ref[...]
    ext = _with_halo(x, xp_ref[...], xn_ref[...], tps)
    rows = tile + 2 * HALO
    t = (pl.program_id(0) % tps) * tile + lax.broadcasted_iota(jnp.int32, (tile, 1), 0)
    gw = x.shape[1] // len(POOL_WINDOWS)
    outs = []
    for g, w in enumerate(POOL_WINDOWS):
        lo = w // 2
        hi = w - 1 - lo
        a = ext[:, g * gw:(g + 1) * gw]
        span = 1
        while span < w:
            a = a + pltpu.roll(a, span, 0)
            span *= 2
        win = (pltpu.roll(a, rows - hi, 0) if hi else a)[HALO:HALO + tile]
        cnt = jnp.minimum(t + hi + 1, seq_len) - jnp.maximum(t - lo, 0)
        d = win / cnt.astype(F32) - x[:, g * gw:(g + 1) * gw]
        outs.append(_dot(d.astype(BF16), w_ref[g]))
    o_ref[...] = (jnp.concatenate(outs, axis=1) * ls_ref[...]).astype(BF16)


def _pool_mixer(p, col0, width, seq_len, w_lin, ls):
    n = p.shape[0]
    tile = _tile(seq_len, 512)
    g, gw, _ = w_lin.shape
    return pl.pallas_call(
        functools.partial(_pool_kernel, seq_len, tile),
        grid=(n // tile,),
        in_specs=_halo_specs(tile, width, col0 // width, n)
        + [pl.BlockSpec((g, gw, gw), lambda i: (0, 0, 0)),
           pl.BlockSpec((1, width), lambda i: (0, 0))],
        out_specs=pl.BlockSpec((tile, width), lambda i: (i, 0)),
        out_shape=jax.ShapeDtypeStruct((n, width), BF16),
        compiler_params=_cp("parallel"),
        name="pool_mixer",
    )(p, p, p, w_lin, ls)


def _hypre_kernel(seq_len, tile, x_ref, xp_ref, xn_ref, cw_ref, cb_ref, z_ref, x2_ref):
    ext = _with_halo(x_ref[...], xp_ref[...], xn_ref[...], seq_len // tile)
    c = _conv3(ext, cw_ref[...], cb_ref[...], tile)
    dw = c.shape[1] // 3
    z_ref[...] = c[:, :dw] * c[:, dw:2 * dw]
    x2_ref[...] = c[:, 2 * dw:]


def _hyena_pre(p, col0, width, seq_len, conv_w, conv_b):
    n = p.shape[0]
    tile = _tile(seq_len, 512)
    dw = width // 3
    out = jax.ShapeDtypeStruct((n, dw), F32)
    return pl.pallas_call(
        functools.partial(_hypre_kernel, seq_len, tile),
        grid=(n // tile,),
        in_specs=_halo_specs(tile, width, col0 // width, n)
        + [pl.BlockSpec((3, width), lambda i: (0, 0)),
           pl.BlockSpec((1, width), lambda i: (0, 0))],
        out_specs=[pl.BlockSpec((tile, dw), lambda i: (i, 0))] * 2,
        out_shape=[out, out],
        compiler_params=_cp("parallel"),
        name="hyena_pre",
    )(p, p, p, conv_w, conv_b.reshape(1, width))


def _hyfilt_kernel(seq_len, tile, zf_ref, w1_ref, b1_ref, w2_ref, b2_ref, w3_ref, fr_ref, dl_ref,
                   k_ref, ss_ref):
    i = pl.program_id(0)
    hp = functools.partial(jnp.dot, precision=lax.Precision.HIGHEST, preferred_element_type=F32)
    zf = zf_ref[...]
    fr = fr_ref[...]
    h = jnp.sin(fr * (hp(zf, w1_ref[...]) + b1_ref[...]))
    h = jnp.sin(fr * (hp(h, w2_ref[...]) + b2_ref[...]))
    k = hp(h, w3_ref[...]) * jnp.exp(-zf[:, 0:1] * dl_ref[...])
    row = i * tile + lax.broadcasted_iota(jnp.int32, (tile, 1), 0)
    k = jnp.where(row == seq_len, 0.0, k)
    k_ref[...] = k

    @pl.when(i == 0)
    def _():
        ss_ref[...] = jnp.zeros(ss_ref.shape, F32)

    ss_ref[...] += jnp.sum(k * k, axis=0, keepdims=True)


def _hyena_filter(seq_len, w1, b1, w2, b2, w3, freq):
    dw = w3.shape[1] // 2
    bands = (HY_EMB - 1) // 2
    t01 = jnp.linspace(0.0, 1.0, seq_len, dtype=F32)[:, None]
    w_ang = 2.0 * math.pi * jnp.arange(seq_len, dtype=F32)[:, None] / seq_len
    f = jnp.linspace(1e-4, bands - 1, bands, dtype=F32)[None, :]
    z = jnp.concatenate([t01, jnp.cos(f * w_ang), -jnp.sin(f * w_ang)], axis=-1)
    rev = (seq_len - jnp.arange(seq_len)) % seq_len
    zf = jnp.pad(jnp.concatenate([z, z[rev]], axis=0), ((0, 0), (0, LANE - HY_EMB)))
    pad_h = LANE - HY_HIDDEN
    w1p = jnp.pad(w1, ((0, LANE - HY_EMB), (0, pad_h)))
    w2p = jnp.pad(w2, ((0, pad_h), (0, pad_h)))
    w3p = jnp.pad(w3, ((0, pad_h), (0, 0)))
    vec = lambda v: jnp.pad(v, (0, pad_h)).reshape(1, LANE)
    deltas = jnp.abs(jnp.linspace(math.log(HY_TARGET) / HY_FAST_DECAY, math.log(HY_TARGET) / HY_SLOW_DECAY,
                                  dw, dtype=F32)).reshape(1, dw)
    tile = _tile(seq_len, 1024)
    tps = seq_len // tile
    full = lambda shape: pl.BlockSpec(shape, lambda i: (0, 0))
    return pl.pallas_call(
        functools.partial(_hyfilt_kernel, seq_len, tile),
        grid=(2 * tps,),
        in_specs=[pl.BlockSpec((tile, LANE), lambda i: (i, 0)),
                  full((LANE, LANE)), full((1, LANE)), full((LANE, LANE)), full((1, LANE)),
                  pl.BlockSpec((LANE, dw), lambda i: (0, i // tps)),
                  full((1, LANE)), full((1, dw))],
        out_specs=[pl.BlockSpec((tile, dw), lambda i: (i, 0)), full((1, dw))],
        out_shape=[jax.ShapeDtypeStruct((2 * seq_len, dw), F32), jax.ShapeDtypeStruct((1, dw), F32)],
        compiler_params=_cp("arbitrary"),
        name="hyena_filter",
    )(zf, w1p, vec(b1), w2p, vec(b2), w3p, vec(freq), deltas)


def _dft_tables(n1, n2):
    n = n1 * n2
    two_pi = 2.0 * math.pi

    def cs(num, den):
        ang = (num % den).astype(F32) * (two_pi / den)
        return jnp.cos(ang), jnp.sin(ang)

    a = jnp.arange(n1)
    c1, s1 = cs(a[:, None] * a[None, :], n1)
    kron = lambda mat: jnp.kron(mat, jnp.eye(SUBLANE, dtype=F32))
    fwd_a = jnp.concatenate([kron(c1), kron(-s1)], axis=0)
    inv_a = jnp.concatenate([kron(c1[: n1 // 2]), kron(-s1[: n1 // 2])], axis=1) / n
    k2 = jnp.arange(n2)
    num = (k2[None, :, None] * k2[None, None, :] * n1 + a[:, None, None] * k2[None, None, :])
    cb, sb = cs(num, n)
    fwd_b = jnp.concatenate([jnp.concatenate([cb, sb], axis=2),
                             jnp.concatenate([-sb, cb], axis=2)], axis=1)
    cbt, sbt = jnp.swapaxes(cb, 1, 2), jnp.swapaxes(sb, 1, 2)
    inv_b = jnp.concatenate([jnp.concatenate([cbt, -sbt], axis=2),
                             jnp.concatenate([sbt, cbt], axis=2)], axis=1)
    return tuple(m.astype(BF16) for m in (fwd_a, inv_a, fwd_b, inv_b))


def _rows(x):
    a, g, c = x.shape
    return x.reshape(a * g, c)


def _dft_a_kernel(f_ref, x_ref, re_ref, im_ref):
    y = _dot(f_ref[...], _rows(x_ref[...]).astype(BF16))
    half = y.shape[0] // 2
    re_ref[...] = y[:half].reshape(re_ref.shape)
    im_ref[...] = y[half:].reshape(im_ref.shape)


def _dft_stage_a(xv, fwd_a, n1):
    b, k, n2, ch = xv.shape
    groups = n2 // SUBLANE
    out = jax.ShapeDtypeStruct((b, n1, n2, ch), F32)
    return pl.pallas_call(
        _dft_a_kernel,
        grid=(b, groups),
        in_specs=[pl.BlockSpec((2 * n1 * SUBLANE, k * SUBLANE), lambda bi, j: (0, 0)),
                  pl.BlockSpec((None, k, SUBLANE, ch), lambda bi, j: (bi, 0, j, 0))],
        out_specs=[pl.BlockSpec((None, n1, SUBLANE, ch), lambda bi, j: (bi, 0, j, 0))] * 2,
        out_shape=[out, out],
        compiler_params=_cp("parallel", "parallel"),
        name="dft_stage_a",
    )(fwd_a[:, :k * SUBLANE], xv)


def _dft_b_kernel(m_ref, re_ref, im_ref, ss_ref, kr_ref, ki_ref):
    nrm = lax.rsqrt(ss_ref[...] + EPS)
    n2 = m_ref.shape[1] // 2
    for k in range(m_ref.shape[0]):
        rows = slice(k * n2, (k + 1) * n2)
        x = _dot(m_ref[k], jnp.concatenate([re_ref[rows], im_ref[rows]], axis=0).astype(BF16))
        kr_ref[rows] = x[:n2] * nrm
        ki_ref[rows] = x[n2:] * nrm


def _filter_spectrum(kern, sumsq, tabs, n1, n2):
    fwd_a, _, fwd_b, _ = tabs
    ch = kern.shape[1]
    are, aim = _dft_stage_a(kern.reshape(1, n1, n2, ch), fwd_a, n1)
    are, aim = are.reshape(n1 * n2, ch), aim.reshape(n1 * n2, ch)
    kb = _tile(n1, DFT_BLOCKS)
    blk = pl.BlockSpec((kb * n2, ch), lambda i: (i, 0))
    out = jax.ShapeDtypeStruct((n1 * n2, ch), F32)
    return pl.pallas_call(
        _dft_b_kernel,
        grid=(n1 // kb,),
        in_specs=[pl.BlockSpec((kb, 2 * n2, 2 * n2), lambda i: (i, 0, 0)), blk, blk,
                  pl.BlockSpec((1, ch), lambda i: (0, 0))],
        out_specs=[blk, blk],
        out_shape=[out, out],
        compiler_params=_cp("parallel"),
        name="filter_spectrum",
    )(fwd_b, are, aim, sumsq)


def _dft_mul_kernel(f_ref, g_ref, re_ref, im_ref, kr_ref, ki_ref, qr_ref, qi_ref):
    n2 = f_ref.shape[1] // 2
    for k in range(f_ref.shape[0]):
        rows = slice(k * n2, (k + 1) * n2)
        x = _dot(f_ref[k], jnp.concatenate([re_ref[rows], im_ref[rows]], axis=0).astype(BF16))
        xr, xi = x[:n2], x[n2:]
        kr, ki = kr_ref[rows], ki_ref[rows]
        prod = jnp.concatenate([xr * kr - xi * ki, xr * ki + xi * kr], axis=0)
        q = _dot(g_ref[k], prod.astype(BF16))
        qr_ref[rows] = q[:n2]
        qi_ref[rows] = q[n2:]


def _dft_inv_kernel(g_ref, qr_ref, qi_ref, z_ref, x2_ref, bias_ref, o_ref):
    q = jnp.concatenate([_rows(qr_ref[...]), _rows(qi_ref[...])], axis=0)
    y = _dot(g_ref[...], q.astype(BF16)).reshape(z_ref.shape)
    o_ref[...] = x2_ref[...] * (y + z_ref[...] * bias_ref[...])


def _hyena_long(z, x2, kr, ki, bias, tabs, batch, seq_len, n1, n2):
    fwd_a, inv_a, fwd_b, inv_b = tabs
    ch = z.shape[1]
    zv = z.reshape(batch, n1 // 2, n2, ch)
    are, aim = _dft_stage_a(zv, fwd_a, n1)
    are, aim = are.reshape(batch, n1 * n2, ch), aim.reshape(batch, n1 * n2, ch)
    kb = _tile(n1, DFT_BLOCKS)
    mat = pl.BlockSpec((kb, 2 * n2, 2 * n2), lambda i, b: (i, 0, 0))
    dat = pl.BlockSpec((None, kb * n2, ch), lambda i, b: (b, i, 0))
    spec = pl.BlockSpec((kb * n2, ch), lambda i, b: (i, 0))
    out = jax.ShapeDtypeStruct((batch, n1 * n2, ch), F32)
    qr, qi = pl.pallas_call(
        _dft_mul_kernel,
        grid=(n1 // kb, batch),
        in_specs=[mat, mat, dat, dat, spec, spec],
        out_specs=[dat, dat],
        out_shape=[out, out],
        compiler_params=_cp("parallel", "parallel"),
        name="dft_filter_mul",
    )(fwd_b, inv_b, are, aim, kr, ki)
    qr, qi = qr.reshape(batch, n1, n2, ch), qi.reshape(batch, n1, n2, ch)
    full = pl.BlockSpec((None, n1, SUBLANE, ch), lambda b, j: (b, 0, j, 0))
    half = pl.BlockSpec((None, n1 // 2, SUBLANE, ch), lambda b, j: (b, 0, j, 0))
    o = pl.pallas_call(
        _dft_inv_kernel,
        grid=(batch, n2 // SUBLANE),
        in_specs=[pl.BlockSpec(inv_a.shape, lambda b, j: (0, 0)), full, full, half, half,
                  pl.BlockSpec((1, ch), lambda b, j: (0, 0))],
        out_specs=half,
        out_shape=jax.ShapeDtypeStruct((batch, n1 // 2, n2, ch), F32),
        compiler_params=_cp("parallel", "parallel"),
        name="dft_inverse_gate",
    )(inv_a, qr, qi, zv, x2.reshape(batch, n1 // 2, n2, ch), bias.reshape(1, ch))
    return o.reshape(batch * seq_len, ch)


def _hyena_short_kernel(f_ref, g_ref, k_ref, ss_ref, z_ref, x2_ref, bias_ref, o_ref):
    f = f_ref[...]
    z = z_ref[...]
    seq = z.shape[0]
    ks = _dot(f, k_ref[...].astype(BF16)) * lax.rsqrt(ss_ref[...] + EPS)
    zs = _dot(f[:, :seq], z.astype(BF16))
    n = ks.shape[0] // 2
    kr, ki, zr, zi = ks[:n], ks[n:], zs[:n], zs[n:]
    prod = jnp.concatenate([zr * kr - zi * ki, zr * ki + zi * kr], axis=0)
    y = _dot(g_ref[...], prod.astype(BF16))
    o_ref[...] = x2_ref[...] * (y + z * bias_ref[...])


def _hyena_short(z, x2, kern, sumsq, bias, batch, seq_len):
    n = 2 * seq_len
    ch = z.shape[1]
    a = jnp.arange(n)
    ang = ((a[:, None] * a[None, :]) % n).astype(F32) * (2.0 * math.pi / n)
    c, s = jnp.cos(ang), jnp.sin(ang)
    f = jnp.concatenate([c, -s], axis=0).astype(BF16)
    g = (jnp.concatenate([c, -s], axis=1)[:seq_len] / n).astype(BF16)
    full = lambda shape: pl.BlockSpec(shape, lambda b: (0, 0))
    dat = pl.BlockSpec((seq_len, ch), lambda b: (b, 0))
    return pl.pallas_call(
        _hyena_short_kernel,
        grid=(batch,),
        in_specs=[full((2 * n, n)), full((seq_len, 2 * n)),
                  full((n, ch)), full((1, ch)), dat, dat, full((1, ch))],
        out_specs=dat,
        out_shape=jax.ShapeDtypeStruct((batch * seq_len, ch), F32),
        compiler_params=_cp("parallel"),
        name="hyena_short_conv",
    )(f, g, kern, sumsq, z, x2, bias.reshape(1, ch))


def _hyena_mixer(p, col0, width, batch, seq_len, conv_w, conv_b, w1, b1, w2, b2, w3, freq, bias):
    z, x2 = _hyena_pre(p, col0, width, seq_len, conv_w, conv_b)
    kern, sumsq = _hyena_filter(seq_len, w1, b1, w2, b2, w3, freq)
    n = 2 * seq_len
    if n <= 4 * DFT_RADIX:
        return _hyena_short(z, x2, kern, sumsq, bias, batch, seq_len)
    n2 = DFT_RADIX
    n1 = n // n2
    tabs = _dft_tables(n1, n2)
    kr, ki = _filter_spectrum(kern, sumsq, tabs, n1, n2)
    return _hyena_long(z, x2, kr, ki, bias, tabs, batch, seq_len, n1, n2)


def _outproj_kernel(x_ref, a_ref, b_ref, c_ref, d_ref, w_ref, g_ref, o_ref):
    acc = None
    row = 0
    for m_ref in (a_ref, b_ref, c_ref, d_ref):
        wdt = m_ref.shape[1]
        part = _dot(m_ref[...].astype(BF16), w_ref[row:row + wdt, :])
        acc = part if acc is None else acc + part
        row += wdt
    o_ref[...] = x_ref[...] + g_ref[...] * acc


def _out_projection(x2d, mixers, w, gate, rows_per_group):
    n, d = x2d.shape
    tm = _tile(rows_per_group, 512)
    tpg = rows_per_group // tm
    return pl.pallas_call(
        _outproj_kernel,
        grid=(n // tm,),
        in_specs=[pl.BlockSpec((tm, d), lambda i: (i, 0))]
        + [pl.BlockSpec((tm, m.shape[1]), lambda i: (i, 0)) for m in mixers]
        + [pl.BlockSpec(w.shape, lambda i: (0, 0)),
           pl.BlockSpec((None, 1, d), lambda i: (i // tpg, 0, 0))],
        out_specs=pl.BlockSpec((tm, d), lambda i: (i, 0)),
        out_shape=jax.ShapeDtypeStruct((n, d), F32),
        compiler_params=_cp("parallel"),
        name="out_proj",
    )(x2d, *mixers, w, gate)


def _ffn_kernel(seq_len, tile, x_ref, xp_ref, xn_ref, g_ref, sc_ref, sh_ref, gate_ref,
                wg_ref, wu_ref, cwg_ref, cwu_ref, cbg_ref, cbu_ref, wd_ref, o_ref, h_scr, acc):
    j = pl.program_id(1)

    @pl.when(j == 0)
    def _():
        ext = jnp.concatenate([xp_ref[...], x_ref[...], xn_ref[...]], axis=0)
        h = _norm_mod(ext, g_ref[...], sc_ref[...], sh_ref[...])
        tps = seq_len // tile
        outside = _outside_rows(tile, pl.program_id(0) % tps, tps)
        h_scr[...] = jnp.where(outside, 0.0, h).astype(BF16)
        acc[...] = jnp.zeros(acc.shape, F32)

    h = h_scr[...]
    gate = _conv3(_dot(h, wg_ref[...]), cwg_ref[...], cbg_ref[...], tile)
    up = _conv3(_dot(h, wu_ref[...]), cwu_ref[...], cbu_ref[...], tile)
    act = (gate * jax.nn.sigmoid(gate)) * up
    acc[...] += _dot(act.astype(BF16), wd_ref[...])

    @pl.when(j == pl.num_programs(1) - 1)
    def _():
        o_ref[...] = x_ref[...] + gate_ref[...] * acc[...]


def _conv_ffn(x2d, g, sc, sh, gate, w_up, conv_w, conv_b, w_down, seq_len, rows_per_group):
    n, d = x2d.shape
    f = w_down.shape[0]
    tile = _tile(seq_len, 512)
    hc = _tile(f, 512)
    nh = f // hc
    tpg = rows_per_group // tile
    r = tile // HALO
    last = n // HALO - 1
    vec = pl.BlockSpec((None, 1, d), lambda i, j: (i // tpg, 0, 0))
    cb = conv_b.reshape(1, 2 * f)
    return pl.pallas_call(
        functools.partial(_ffn_kernel, seq_len, tile),
        grid=(n // tile, nh),
        in_specs=[
            pl.BlockSpec((tile, d), lambda i, j: (i, 0)),
            pl.BlockSpec((HALO, d), lambda i, j: (jnp.maximum(i * r - 1, 0), 0)),
            pl.BlockSpec((HALO, d), lambda i, j: (jnp.minimum(i * r + r, last), 0)),
            pl.BlockSpec((1, d), lambda i, j: (0, 0)),
            vec, vec, vec,
            pl.BlockSpec((d, hc), lambda i, j: (0, j)),
            pl.BlockSpec((d, hc), lambda i, j: (0, nh + j)),
            pl.BlockSpec((3, hc), lambda i, j: (0, j)),
            pl.BlockSpec((3, hc), lambda i, j: (0, nh + j)),
            pl.BlockSpec((1, hc), lambda i, j: (0, j)),
            pl.BlockSpec((1, hc), lambda i, j: (0, nh + j)),
            pl.BlockSpec((hc, d), lambda i, j: (j, 0)),
        ],
        out_specs=pl.BlockSpec((tile, d), lambda i, j: (i, 0)),
        out_shape=jax.ShapeDtypeStruct((n, d), F32),
        scratch_shapes=[pltpu.VMEM((tile + 2 * HALO, d), BF16), pltpu.VMEM((tile, d), F32)],
        compiler_params=_cp("parallel", "arbitrary"),
        name="conv_ffn",
    )(x2d, x2d, x2d, g, sc, sh, gate, w_up, w_up, conv_w, conv_w, cb, cb, w_down)


def kernel(x, c, ctx, c_ctx, w_mod, b_mod, norm1_g, norm2_g, w_in, w_out, qk_gain, diff_lam, diff_subln,
           win_sink, pool_w, pool_scale, hy_conv_w, hy_conv_b, hy_w1, hy_b1, hy_w2, hy_b2, hy_w3, hy_freq,
           hy_bias, ffn_w_in, ffn_conv_w, ffn_conv_b, ffn_w_out):
    batch, seq, d = x.shape
    clen = ctx.shape[1]
    depth = w_in.shape[0]
    a_heads = d // 512
    a_width = a_heads * 2 * HEAD_DIM
    b_heads = d // 256
    b_kv = b_heads // 4
    b_width = b_heads * HEAD_DIM
    b_kvw = b_kv * HEAD_DIM
    c_width = d // 4
    d_width = d - a_width - b_width - c_width
    off_ka = a_width
    off_va = 2 * a_width
    off_qb = 3 * a_width
    off_kb = off_qb + b_width
    off_vb = off_kb + b_kvw
    off_pool = off_vb + b_kvw
    off_hy = off_pool + c_width
    in_cols = off_hy + 3 * d_width
    order = ((off_hy, in_cols), (off_pool, off_hy), (off_va, off_qb), (0, off_ka), (off_ka, off_va),
             (off_qb, off_kb), (off_kb, off_vb), (off_vb, off_pool))
    n_hy, n_pool, n_va, n_qa, n_ka, n_qb, n_kb, n_vb = (
        sum(e - s for s, e in order[:k]) for k in range(len(order)))
    assert a_width % LANE == 0 and b_width % LANE == 0 and b_kvw == LANE and c_width // 4 == LANE
    assert seq % BLOCK == 0 and clen % BLOCK == 0 and n_pool % c_width == 0
    assert (n_ka - n_qa) == a_width and (n_qb - n_qa) % b_width == 0 and n_kb - n_qb == b_width

    rows = -(-(batch + 1) // SUBLANE) * SUBLANE
    cvec = jnp.zeros((rows, d), F32).at[:batch].set(c).at[batch].set(c_ctx)
    mods = _modulation(cvec, w_mod, b_mod)

    w_in_b = jnp.concatenate([w_in[:, :, s:e] for s, e in order], axis=-1).astype(BF16)
    w_out_b = w_out.astype(BF16)
    ffn_in_b = ffn_w_in.astype(BF16)
    ffn_out_b = ffn_w_out.astype(BF16)
    pool_w_b = pool_w.astype(BF16)

    cos_t, sin_t = _rope_tables(seq)
    scale = HEAD_DIM ** -0.5
    tile2 = lambda v: jnp.tile(v, LANE // HEAD_DIM)

    xs = x.reshape(batch * seq, d)
    cs = ctx.reshape(batch * clen, d)
    for l in range(depth):
        last = l == depth - 1
        m = mods[l]
        sh1, sc1, g1, sh2, sc2, g2 = [m[:batch, i * d:(i + 1) * d].reshape(batch, 1, d) for i in range(6)]
        csh1, csc1, cg1, csh2, csc2, cg2 = [m[batch, i * d:(i + 1) * d].reshape(1, 1, d) for i in range(6)]
        n1g = norm1_g[l].reshape(1, d)
        n2g = norm2_g[l].reshape(1, d)

        p = _in_projection(xs, n1g, sc1, sh1, w_in_b[l], seq)
        pc = _in_projection(cs, n1g, csc1, csh1, w_in_b[l], batch * clen)

        na = a_width // LANE
        nbq = b_width // LANE
        gains = jnp.stack([tile2(qk_gain[l, 0]) * (scale * LOG2E)] * na + [tile2(qk_gain[l, 1])] * na
                          + [tile2(qk_gain[l, 2]) * (scale * LOG2E)] * nbq + [tile2(qk_gain[l, 3])])
        qk = _qk_prep(p, n_qa // LANE, gains, cos_t, sin_t, seq, True)
        qk_c = _qk_prep(pc, n_qa // LANE, gains, cos_t, sin_t, clen, False)
        kb_blk = 2 * na + nbq
        qb_blk = (n_qb - n_qa) // b_width

        lam_p = diff_lam[l].astype(F32)
        lambda_init = 0.8 - 0.6 * math.exp(-0.3 * l)
        lam = jnp.exp(jnp.sum(lam_p[0] * lam_p[1])) - jnp.exp(jnp.sum(lam_p[2] * lam_p[3])) + lambda_init
        lam_v = jnp.full((1, LANE), lam, F32)
        subg = diff_subln[l].reshape(1, LANE)
        sink_tab = jnp.broadcast_to(win_sink[l].astype(F32)[:, None], (b_heads, LANE))
        hy = (hy_conv_w[l], hy_conv_b[l], hy_w1[l], hy_b1[l], hy_w2[l], hy_b2[l], hy_w3[l], hy_freq[l],
              hy_bias[l])

        vt = _transposed_values(p, n_va // LANE, a_heads, seq, _tile(seq, 512))
        vt_c = _transposed_values(pc, n_va // LANE, a_heads, clen, clen)
        o_a = _diff_attention(qk, seq, qk_c, vt_c, clen, qk, vt, seq, lam_v, subg, 1.0 - lambda_init,
                              a_heads, na)
        vtb = _transposed_values(p, n_vb // LANE, 1, seq, BLOCK)
        vtb_c = _transposed_values(pc, n_vb // LANE, 1, clen, clen)
        o_b = _window_attention(qk, seq, qk_c, vtb_c, clen, vtb, sink_tab, b_heads, b_kv,
                                qb_blk, kb_blk, True)
        o_c = _pool_mixer(p, n_pool, c_width, seq, pool_w_b[l], pool_scale[l].reshape(1, c_width))
        o_d = _hyena_mixer(p, n_hy, 3 * d_width, batch, seq, *hy)
        xs = _out_projection(xs, (o_a, o_b, o_c, o_d), w_out_b[l], g1, seq)
        xs = _conv_ffn(xs, n2g, sc2, sh2, g2, ffn_in_b[l], ffn_conv_w[l], ffn_conv_b[l], ffn_out_b[l],
                       seq, seq)

        if not last:
            o_a_c = _diff_attention(qk_c, clen, qk_c, vt_c, clen, None, None, 0, lam_v, subg,
                                    1.0 - lambda_init, a_heads, na)
            o_b_c = _window_attention(qk_c, clen, qk_c, vtb_c, clen, None, sink_tab, b_heads, b_kv,
                                      qb_blk, kb_blk, False)
            o_c_c = _pool_mixer(pc, n_pool, c_width, clen, pool_w_b[l], pool_scale[l].reshape(1, c_width))
            o_d_c = _hyena_mixer(pc, n_hy, 3 * d_width, batch, clen, *hy)
            cs = _out_projection(cs, (o_a_c, o_b_c, o_c_c, o_d_c), w_out_b[l], cg1, batch * clen)
            cs = _conv_ffn(cs, n2g, csc2, csh2, cg2, ffn_in_b[l], ffn_conv_w[l], ffn_conv_b[l],
                           ffn_out_b[l], clen, batch * clen)
    return xs.reshape(batch, seq, d)
```

```python
import functools
import math

import jax
import jax.numpy as jnp
from jax import lax
from jax.experimental import pallas as pl
from jax.experimental.pallas import tpu as pltpu

F32 = jnp.float32
BF16 = jnp.bfloat16

HEAD_DIM = 64
GRID_W = 64
ROPE_BASE = 10000.0
EPS = 1e-6
NEG_INF = -1e30
WINDOW = 128
BLOCK = 128
POOL_WINDOWS = (2, 4, 8, 16)
HY_EMB = 33
HY_HIDDEN = 64
HY_FAST_DECAY = 0.3
HY_SLOW_DECAY = 1.5
HY_TARGET = 1e-2

LANE = 128
SUBLANE = 8
HALO = SUBLANE
DFT_RADIX = 128
DFT_BLOCKS = 8
VT_ROWS = LANE + 16
LOG2E = 1.4426950408889634
VMEM_LIMIT = 56 * 1024 * 1024


def _cp(*sem):
    return pltpu.CompilerParams(dimension_semantics=sem, vmem_limit_bytes=VMEM_LIMIT)


def _tile(n, pref):
    if n <= pref:
        return n
    t = pref
    while n % t:
        t //= 2
    assert t >= SUBLANE, (n, pref)
    return t


def _dot(a, b):
    return jnp.dot(a, b, preferred_element_type=F32)


def _dot_nt(a, b):
    return lax.dot_general(a, b, (((1,), (1,)), ((), ())), preferred_element_type=F32)


def _split_bf16(x):
    hi = x.astype(BF16)
    lo = (x - hi.astype(F32)).astype(BF16)
    return hi, lo


def _mod_kernel(c_ref, w_ref, b_ref, o_ref):
    c = c_ref[...]
    a = (c * jax.nn.sigmoid(c)).astype(BF16)
    o_ref[...] = _dot(a, w_ref[...].astype(BF16)) + b_ref[...]


def _modulation(cvec, w_mod, b_mod):
    depth, d, n = w_mod.shape
    rows = cvec.shape[0]
    tn = _tile(n, 1024)
    return pl.pallas_call(
        _mod_kernel,
        grid=(depth, n // tn),
        in_specs=[
            pl.BlockSpec((rows, d), lambda l, j: (0, 0)),
            pl.BlockSpec((None, d, tn), lambda l, j: (l, 0, j)),
            pl.BlockSpec((None, 1, tn), lambda l, j: (l, 0, j)),
        ],
        out_specs=pl.BlockSpec((None, rows, tn), lambda l, j: (l, 0, j)),
        out_shape=jax.ShapeDtypeStruct((depth, rows, n), F32),
        compiler_params=_cp("parallel", "parallel"),
        name="adaln_mod",
    )(cvec, w_mod, b_mod.reshape(depth, 1, n))


def _norm_mod(x, g, sc, sh):
    ms = jnp.mean(x * x, axis=-1, keepdims=True)
    return (x * lax.rsqrt(ms + EPS) * g) * (1.0 + sc) + sh


def _inproj_kernel(x_ref, g_ref, sc_ref, sh_ref, w_ref, o_ref):
    h = _norm_mod(x_ref[...], g_ref[...], sc_ref[...], sh_ref[...])
    o_ref[...] = _dot(h.astype(BF16), w_ref[...])


def _in_projection(x2d, g, sc, sh, w, rows_per_group):
    n, d = x2d.shape
    nc = w.shape[1]
    tm = _tile(rows_per_group, 512)
    tpg = rows_per_group // tm
    ncol = 2 if (nc // 2) % LANE == 0 else 1
    tn = nc // ncol
    vec = pl.BlockSpec((None, 1, d), lambda j, i: (i // tpg, 0, 0))
    return pl.pallas_call(
        _inproj_kernel,
        grid=(ncol, n // tm),
        in_specs=[
            pl.BlockSpec((tm, d), lambda j, i: (i, 0)),
            pl.BlockSpec((1, d), lambda j, i: (0, 0)),
            vec, vec,
            pl.BlockSpec((d, tn), lambda j, i: (0, j)),
        ],
        out_specs=pl.BlockSpec((tm, tn), lambda j, i: (i, j)),
        out_shape=jax.ShapeDtypeStruct((n, nc), F32),
        compiler_params=_cp("parallel", "parallel"),
        name="in_proj",
    )(x2d, g, sc, sh, w)


def _qkprep_kernel(rope, x_ref, gain_ref, cos_ref, sin_ref, gm_ref, o_ref):
    x = x_ref[...]
    hi, lo = _split_bf16(x * x)
    gm = gm_ref[...]
    ms = _dot(hi, gm) + _dot(lo, gm)
    y = x * lax.rsqrt(ms + EPS) * gain_ref[...]
    if rope:
        lane = lax.broadcasted_iota(jnp.int32, y.shape, 1)
        first_half = (lane & 31) < 16
        partner = jnp.where(first_half, pltpu.roll(y, LANE - 16, 1), pltpu.roll(y, 16, 1))
        y = y * cos_ref[...] + partner * sin_ref[...]
    o_ref[...] = y.astype(BF16)


def _qk_prep(p, col_block0, gains, cos_t, sin_t, seq_len, rope):
    n = p.shape[0]
    nblk = gains.shape[0]
    tm = _tile(seq_len, 1024)
    tps = seq_len // tm
    lane = jnp.arange(LANE)
    gm = ((lane[:, None] // HEAD_DIM) == (lane[None, :] // HEAD_DIM)).astype(BF16) * (1.0 / HEAD_DIM)
    tab = pl.BlockSpec((tm, LANE), lambda i, j: (i % tps, 0))
    return pl.pallas_call(
        functools.partial(_qkprep_kernel, rope),
        grid=(n // tm, nblk),
        in_specs=[
            pl.BlockSpec((tm, LANE), lambda i, j: (i, col_block0 + j)),
            pl.BlockSpec((None, 1, LANE), lambda i, j: (j, 0, 0)),
            tab, tab,
            pl.BlockSpec((LANE, LANE), lambda i, j: (0, 0)),
        ],
        out_specs=pl.BlockSpec((tm, LANE), lambda i, j: (i, j)),
        out_shape=jax.ShapeDtypeStruct((n, nblk * LANE), BF16),
        compiler_params=_cp("parallel", "parallel"),
        name="qk_prep",
    )(p, gains.reshape(nblk, 1, LANE), cos_t, sin_t, gm.astype(BF16))


def _rope_tables(seq_len):
    rows = seq_len // GRID_W
    row = jnp.repeat(jnp.arange(rows, dtype=F32), GRID_W)
    col = jnp.tile(jnp.arange(GRID_W, dtype=F32), rows)
    half = HEAD_DIM // 2
    inv = ROPE_BASE ** (-jnp.arange(0, half, 2, dtype=F32) / half)
    ang = jnp.concatenate([row[:, None] * inv, col[:, None] * inv], axis=-1)
    lane = jnp.arange(LANE)
    within = lane % HEAD_DIM
    idx = (within // 32) * 16 + (within % 16)
    sign = jnp.where((lane % 32) < 16, -1.0, 1.0).astype(F32)
    return jnp.cos(ang)[:, idx], jnp.sin(ang)[:, idx] * sign


def _vt_kernel(x_ref, o_ref):
    chunks, _, chunk = o_ref.shape
    row = lax.broadcasted_iota(jnp.int32, (VT_ROWS - LANE, chunk), 0)
    ones_row = jnp.where(row == 0, 1.0, 0.0).astype(BF16)
    for c in range(chunks):
        x_t = x_ref[c * chunk:(c + 1) * chunk, :].T.astype(BF16)
        o_ref[c] = jnp.concatenate([x_t, ones_row], axis=0)


def _transposed_values(p, col_block0, nblk, seq_len, chunk):
    n = p.shape[0]
    rows = max(chunk, _tile(seq_len, 512))
    return pl.pallas_call(
        _vt_kernel,
        grid=(n // rows, nblk),
        in_specs=[pl.BlockSpec((rows, LANE), lambda i, j: (i, col_block0 + j))],
        out_specs=pl.BlockSpec((None, rows // chunk, VT_ROWS, chunk), lambda i, j: (j, i, 0, 0)),
        out_shape=jax.ShapeDtypeStruct((nblk, n // chunk, VT_ROWS, chunk), BF16),
        compiler_params=_cp("parallel", "parallel"),
        name="values_t",
    )(p)


def _attn_a_kernel(n_lat, tk, sub_scale, *refs):
    if n_lat:
        (q_ref, kc_ref, vc_ref, k_ref, v_ref, lam_ref, subg_ref, o_ref, qt, m, acc,
         s0, s1, x0, x1, p0, p1, al0, al1) = refs
    else:
        q_ref, kc_ref, vc_ref, lam_ref, subg_ref, o_ref, qt, m, acc = refs
    tq = q_ref.shape[0]
    q_t = q_ref[...].astype(F32).T
    row = lax.broadcasted_iota(jnp.int32, q_t.shape, 0)
    qt[:, :tq] = jnp.where(row < HEAD_DIM, q_t, 0.0).astype(BF16)
    qt[:, tq:] = jnp.where(row >= HEAD_DIM, q_t, 0.0).astype(BF16)
    m[...] = jnp.full(m.shape, -jnp.inf, F32)
    acc[...] = jnp.zeros(acc.shape, F32)

    def softmax_update(s, s_max):
        m_prev = m[...]
        m_new = jnp.maximum(m_prev, s_max)
        m[...] = m_new
        return jnp.exp2(s - m_new).astype(BF16), jnp.exp2(m_prev - m_new)

    s_c = _dot(kc_ref[...], qt[...])

    def context_update():
        p_c, alpha_c = softmax_update(s_c, jnp.max(s_c, axis=0, keepdims=True))
        acc[...] = alpha_c * acc[...] + _dot(vc_ref[0], p_c)

    if not n_lat:
        context_update()
    else:
        def scores(c, s_buf, x_buf):
            s = _dot(k_ref[pl.ds(pl.multiple_of(c * tk, tk), tk), :], qt[...])
            s_buf[...] = s
            x_buf[...] = jnp.max(s, axis=0, keepdims=True)

        def softmax(s_buf, x_buf, p_buf, al_buf):
            p_buf[...], al_buf[...] = softmax_update(s_buf[...], x_buf[...])

        def values(c, p_buf, al_buf):
            acc[...] = al_buf[...] * acc[...] + _dot(v_ref[c], p_buf[...])

        scores(0, s0, x0)
        if n_lat == 1:
            context_update()
            softmax(s0, x0, p0, al0)
            values(0, p0, al0)
        else:
            assert n_lat % 2 == 0
            scores(1, s1, x1)
            context_update()
            softmax(s0, x0, p0, al0)

            def body(t, carry):
                c = 2 * t
                scores(c + 2, s0, x0)
                softmax(s1, x1, p1, al1)
                values(c, p0, al0)
                scores(c + 3, s1, x1)
                softmax(s0, x0, p0, al0)
                values(c + 1, p1, al1)
                return carry
            lax.fori_loop(0, n_lat // 2 - 1, body, 0)
            softmax(s1, x1, p1, al1)
            values(n_lat - 2, p0, al0)
            values(n_lat - 1, p1, al1)

    o_t = acc[:LANE] / acc[LANE:LANE + 1]
    o = (o_t[:, :tq] - lam_ref[0:1, 0:1] * o_t[:, tq:]).T
    ms = jnp.mean(o * o, axis=-1, keepdims=True)
    o_ref[...] = ((o * lax.rsqrt(ms + EPS) * subg_ref[...]) * sub_scale).astype(BF16)


def _diff_attention(q_arr, q_len, qk_ctx, vt_ctx, ctx_len, qk_lat, vt_lat, lat_len, lam, subg,
                    sub_scale, heads, k_col0):
    batch = q_arr.shape[0] // q_len
    tq = _tile(q_len, 1024)
    nq = q_len // tq
    in_specs = [
        pl.BlockSpec((tq, LANE), lambda b, h, i: (b * nq + i, h)),
        pl.BlockSpec((ctx_len, LANE), lambda b, h, i: (b, k_col0 + h)),
        pl.BlockSpec((None, 1, VT_ROWS, ctx_len), lambda b, h, i: (h, b, 0, 0)),
    ]
    args = [q_arr, qk_ctx, vt_ctx]
    n_lat, tk = 0, 0
    if qk_lat is not None:
        tk = vt_lat.shape[-1]
        n_lat = lat_len // tk
        in_specs += [
            pl.BlockSpec((lat_len, LANE), lambda b, h, i: (b, k_col0 + h)),
            pl.BlockSpec((None, n_lat, VT_ROWS, tk), lambda b, h, i: (h, b, 0, 0)),
        ]
        args += [qk_lat, vt_lat]
    in_specs += [pl.BlockSpec((1, LANE), lambda b, h, i: (0, 0))] * 2
    args += [lam, subg]
    stat = pltpu.VMEM((1, 2 * tq), F32)
    scratch = [pltpu.VMEM((LANE, 2 * tq), BF16), stat, pltpu.VMEM((VT_ROWS, 2 * tq), F32)]
    if n_lat:
        scratch += [pltpu.VMEM((tk, 2 * tq), F32)] * 2 + [stat] * 2
        scratch += [pltpu.VMEM((tk, 2 * tq), BF16)] * 2 + [stat] * 2
    return pl.pallas_call(
        functools.partial(_attn_a_kernel, n_lat, tk, sub_scale),
        grid=(batch, heads, nq),
        in_specs=in_specs,
        out_specs=pl.BlockSpec((tq, LANE), lambda b, h, i: (b * nq + i, h)),
        out_shape=jax.ShapeDtypeStruct((batch * q_len, heads * LANE), BF16),
        scratch_shapes=scratch,
        compiler_params=_cp("parallel", "parallel", "parallel"),
        name="diff_attn",
    )(*args)


def _attn_b_kernel(has_loc, seq_len, q_heads, group, *refs):
    if has_loc:
        (q_ref, kx_ref, vx_ref, kp_ref, kc_ref, kn_ref, vp_ref, vc_ref, vn_ref, sink_ref, o_ref) = refs
    else:
        (q_ref, kx_ref, vx_ref, sink_ref, o_ref) = refs
    n = pl.program_id(1)
    clen = kx_ref.shape[0]
    q_t = q_ref[...].astype(F32).T
    keys = [kx_ref[...]]
    vals = [vx_ref[0]]
    if has_loc:
        keys += [kp_ref[...], kc_ref[...], kn_ref[...]]
        vals += [vp_ref[0], vc_ref[0], vn_ref[0]]
        kpos = (n - 1) * BLOCK + lax.broadcasted_iota(jnp.int32, (3 * BLOCK, BLOCK), 0)
        qpos = n * BLOCK + lax.broadcasted_iota(jnp.int32, (3 * BLOCK, BLOCK), 1)
        valid = (kpos >= 0) & (kpos < seq_len) & (jnp.abs(kpos - qpos) <= WINDOW)
        valid = jnp.concatenate([valid] * group, axis=1)
    k_all = jnp.concatenate(keys, axis=0)
    v_all = jnp.concatenate(vals, axis=1)
    zeros = jnp.zeros((HEAD_DIM, BLOCK), F32)
    heads_out = []
    for kvh in range(q_heads // group):
        cols = []
        for g in range(group):
            hq = kvh * group + g
            blk = q_t[hq * HEAD_DIM:(hq + 1) * HEAD_DIM]
            cols.append(jnp.concatenate([blk, zeros] if kvh == 0 else [zeros, blk], axis=0))
        q_kv = jnp.concatenate(cols, axis=1).astype(BF16)
        sink = jnp.concatenate([sink_ref[kvh * group + g:kvh * group + g + 1, :] for g in range(group)],
                               axis=1) * LOG2E
        s = _dot(k_all, q_kv)
        if has_loc:
            s = jnp.concatenate([s[:clen], jnp.where(valid, s[clen:], NEG_INF)], axis=0)
        m = jnp.maximum(jnp.max(s, axis=0, keepdims=True), sink)
        o_t = _dot(v_all, jnp.exp2(s - m).astype(BF16))
        den = o_t[LANE:LANE + 1] + jnp.exp2(sink - m)
        o_t = o_t[kvh * HEAD_DIM:(kvh + 1) * HEAD_DIM] / den
        heads_out += [o_t[:, g * BLOCK:(g + 1) * BLOCK] for g in range(group)]
    o_ref[...] = jnp.concatenate(heads_out, axis=0).T.astype(BF16)


def _window_attention(q_arr, q_len, qk_ctx, vt_ctx, ctx_len, vt_lat, sink_tab, q_heads, kv_heads,
                      q_blk, k_blk, local):
    assert kv_heads == 2, "key/value heads must fill exactly one 128-lane block"
    batch = q_arr.shape[0] // q_len
    nb = q_len // BLOCK
    qw = q_heads * HEAD_DIM
    in_specs = [
        pl.BlockSpec((BLOCK, qw), lambda b, n: (b * nb + n, q_blk)),
        pl.BlockSpec((ctx_len, LANE), lambda b, n: (b, k_blk)),
        pl.BlockSpec((None, 1, VT_ROWS, ctx_len), lambda b, n: (0, b, 0, 0)),
    ]
    args = [q_arr, qk_ctx, vt_ctx]
    if local:
        prev = lambda b, n: b * nb + jnp.maximum(n - 1, 0)
        nxt = lambda b, n: b * nb + jnp.minimum(n + 1, nb - 1)
        in_specs += [
            pl.BlockSpec((BLOCK, LANE), lambda b, n: (prev(b, n), k_blk)),
            pl.BlockSpec((BLOCK, LANE), lambda b, n: (b * nb + n, k_blk)),
            pl.BlockSpec((BLOCK, LANE), lambda b, n: (nxt(b, n), k_blk)),
            pl.BlockSpec((None, 1, VT_ROWS, BLOCK), lambda b, n: (0, prev(b, n), 0, 0)),
            pl.BlockSpec((None, 1, VT_ROWS, BLOCK), lambda b, n: (0, b * nb + n, 0, 0)),
            pl.BlockSpec((None, 1, VT_ROWS, BLOCK), lambda b, n: (0, nxt(b, n), 0, 0)),
        ]
        args += [q_arr] * 3 + [vt_lat] * 3
    in_specs.append(pl.BlockSpec((q_heads, LANE), lambda b, n: (0, 0)))
    args.append(sink_tab)
    return pl.pallas_call(
        functools.partial(_attn_b_kernel, local, q_len, q_heads, q_heads // kv_heads),
        grid=(batch, nb),
        in_specs=in_specs,
        out_specs=pl.BlockSpec((BLOCK, qw), lambda b, n: (b * nb + n, 0)),
        out_shape=jax.ShapeDtypeStruct((batch * q_len, qw), BF16),
        compiler_params=_cp("parallel", "parallel"),
        name="window_attn",
    )(*args)


def _halo_specs(tile, width, col_blk, n_rows):
    r = tile // HALO
    last = n_rows // HALO - 1
    return [
        pl.BlockSpec((tile, width), lambda i: (i, col_blk)),
        pl.BlockSpec((HALO, width), lambda i: (jnp.maximum(i * r - 1, 0), col_blk)),
        pl.BlockSpec((HALO, width), lambda i: (jnp.minimum(i * r + r, last), col_blk)),
    ]


def _outside_rows(tile, tile_idx, tiles_per_seq):
    row = lax.broadcasted_iota(jnp.int32, (tile + 2 * HALO, 1), 0)
    lo = jnp.where(tile_idx == 0, HALO, 0)
    hi = jnp.where(tile_idx == tiles_per_seq - 1, HALO + tile, tile + 2 * HALO)
    return (row < lo) | (row >= hi)


def _with_halo(x, xp, xn, tiles_per_seq):
    ext = jnp.concatenate([xp, x, xn], axis=0)
    outside = _outside_rows(x.shape[0], pl.program_id(0) % tiles_per_seq, tiles_per_seq)
    return jnp.where(outside, 0.0, ext)


def _conv3(ext, w, b, tile):
    rows = ext.shape[0]
    c = (pltpu.roll(ext, 1, 0) * w[0:1] + ext * w[1:2] + pltpu.roll(ext, rows - 1, 0) * w[2:3])
    return c[HALO:HALO + tile] + b


def _pool_kernel(seq_len, tile, x_ref, xp_ref, xn_ref, w_ref, ls_ref, o_ref):
    tps = seq_len // tile
    x = x_ref[...]
    ext = _with_halo(x, xp_ref[...], xn_ref[...], tps)
    rows = tile + 2 * HALO
    t = (pl.program_id(0) % tps) * tile + lax.broadcasted_iota(jnp.int32, (tile, 1), 0)
    gw = x.shape[1] // len(POOL_WINDOWS)
    outs = []
    for g, w in enumerate(POOL_WINDOWS):
        lo = w // 2
        hi = w - 1 - lo
        a = ext[:, g * gw:(g + 1) * gw]
        span = 1
        while span < w:
            a = a + pltpu.roll(a, span, 0)
            span *= 2
        win = (pltpu.roll(a, rows - hi, 0) if hi else a)[HALO:HALO + tile]
        cnt = jnp.minimum(t + hi + 1, seq_len) - jnp.maximum(t - lo, 0)
        d = win / cnt.astype(F32) - x[:, g * gw:(g + 1) * gw]
        outs.append(_dot(d.astype(BF16), w_ref[g]))
    o_ref[...] = (jnp.concatenate(outs, axis=1) * ls_ref[...]).astype(BF16)


def _pool_mixer(p, col0, width, seq_len, w_lin, ls):
    n = p.shape[0]
    tile = _tile(seq_len, 512)
    g, gw, _ = w_lin.shape
    return pl.pallas_call(
        functools.partial(_pool_kernel, seq_len, tile),
        grid=(n // tile,),
        in_specs=_halo_specs(tile, width, col0 // width, n)
        + [pl.BlockSpec((g, gw, gw), lambda i: (0, 0, 0)),
           pl.BlockSpec((1, width), lambda i: (0, 0))],
        out_specs=pl.BlockSpec((tile, width), lambda i: (i, 0)),
        out_shape=jax.ShapeDtypeStruct((n, width), BF16),
        compiler_params=_cp("parallel"),
        name="pool_mixer",
    )(p, p, p, w_lin, ls)


def _hypre_kernel(seq_len, tile, x_ref, xp_ref, xn_ref, cw_ref, cb_ref, z_ref, x2_ref):
    ext = _with_halo(x_ref[...], xp_ref[...], xn_ref[...], seq_len // tile)
    c = _conv3(ext, cw_ref[...], cb_ref[...], tile)
    dw = c.shape[1] // 3
    z_ref[...] = c[:, :dw] * c[:, dw:2 * dw]
    x2_ref[...] = c[:, 2 * dw:]


def _hyena_pre(p, col0, width, seq_len, conv_w, conv_b):
    n = p.shape[0]
    tile = _tile(seq_len, 512)
    dw = width // 3
    out = jax.ShapeDtypeStruct((n, dw), F32)
    return pl.pallas_call(
        functools.partial(_hypre_kernel, seq_len, tile),
        grid=(n // tile,),
        in_specs=_halo_specs(tile, width, col0 // width, n)
        + [pl.BlockSpec((3, width), lambda i: (0, 0)),
           pl.BlockSpec((1, width), lambda i: (0, 0))],
        out_specs=[pl.BlockSpec((tile, dw), lambda i: (i, 0))] * 2,
        out_shape=[out, out],
        compiler_params=_cp("parallel"),
        name="hyena_pre",
    )(p, p, p, conv_w, conv_b.reshape(1, width))


def _hyfilt_kernel(seq_len, tile, zf_ref, w1_ref, b1_ref, w2_ref, b2_ref, w3_ref, fr_ref, dl_ref,
                   k_ref, ss_ref):
    i = pl.program_id(0)
    hp = functools.partial(jnp.dot, precision=lax.Precision.HIGHEST, preferred_element_type=F32)
    zf = zf_ref[...]
    fr = fr_ref[...]
    h = jnp.sin(fr * (hp(zf, w1_ref[...]) + b1_ref[...]))
    h = jnp.sin(fr * (hp(h, w2_ref[...]) + b2_ref[...]))
    k = hp(h, w3_ref[...]) * jnp.exp(-zf[:, 0:1] * dl_ref[...])
    row = i * tile + lax.broadcasted_iota(jnp.int32, (tile, 1), 0)
    k = jnp.where(row == seq_len, 0.0, k)
    k_ref[...] = k

    @pl.when(i == 0)
    def _():
        ss_ref[...] = jnp.zeros(ss_ref.shape, F32)

    ss_ref[...] += jnp.sum(k * k, axis=0, keepdims=True)


def _hyena_filter(seq_len, w1, b1, w2, b2, w3, freq):
    dw = w3.shape[1] // 2
    bands = (HY_EMB - 1) // 2
    t01 = jnp.linspace(0.0, 1.0, seq_len, dtype=F32)[:, None]
    w_ang = 2.0 * math.pi * jnp.arange(seq_len, dtype=F32)[:, None] / seq_len
    f = jnp.linspace(1e-4, bands - 1, bands, dtype=F32)[None, :]
    z = jnp.concatenate([t01, jnp.cos(f * w_ang), -jnp.sin(f * w_ang)], axis=-1)
    rev = (seq_len - jnp.arange(seq_len)) % seq_len
    zf = jnp.pad(jnp.concatenate([z, z[rev]], axis=0), ((0, 0), (0, LANE - HY_EMB)))
    pad_h = LANE - HY_HIDDEN
    w1p = jnp.pad(w1, ((0, LANE - HY_EMB), (0, pad_h)))
    w2p = jnp.pad(w2, ((0, pad_h), (0, pad_h)))
    w3p = jnp.pad(w3, ((0, pad_h), (0, 0)))
    vec = lambda v: jnp.pad(v, (0, pad_h)).reshape(1, LANE)
    deltas = jnp.abs(jnp.linspace(math.log(HY_TARGET) / HY_FAST_DECAY, math.log(HY_TARGET) / HY_SLOW_DECAY,
                                  dw, dtype=F32)).reshape(1, dw)
    tile = _tile(seq_len, 1024)
    tps = seq_len // tile
    full = lambda shape: pl.BlockSpec(shape, lambda i: (0, 0))
    return pl.pallas_call(
        functools.partial(_hyfilt_kernel, seq_len, tile),
        grid=(2 * tps,),
        in_specs=[pl.BlockSpec((tile, LANE), lambda i: (i, 0)),
                  full((LANE, LANE)), full((1, LANE)), full((LANE, LANE)), full((1, LANE)),
                  pl.BlockSpec((LANE, dw), lambda i: (0, i // tps)),
                  full((1, LANE)), full((1, dw))],
        out_specs=[pl.BlockSpec((tile, dw), lambda i: (i, 0)), full((1, dw))],
        out_shape=[jax.ShapeDtypeStruct((2 * seq_len, dw), F32), jax.ShapeDtypeStruct((1, dw), F32)],
        compiler_params=_cp("arbitrary"),
        name="hyena_filter",
    )(zf, w1p, vec(b1), w2p, vec(b2), w3p, vec(freq), deltas)


def _dft_tables(n1, n2):
    n = n1 * n2
    two_pi = 2.0 * math.pi

    def cs(num, den):
        ang = (num % den).astype(F32) * (two_pi / den)
        return jnp.cos(ang), jnp.sin(ang)

    a = jnp.arange(n1)
    c1, s1 = cs(a[:, None] * a[None, :], n1)
    kron = lambda mat: jnp.kron(mat, jnp.eye(SUBLANE, dtype=F32))
    fwd_a = jnp.concatenate([kron(c1), kron(-s1)], axis=0)
    inv_a = jnp.concatenate([kron(c1[: n1 // 2]), kron(-s1[: n1 // 2])], axis=1) / n
    k2 = jnp.arange(n2)
    num = (k2[None, :, None] * k2[None, None, :] * n1 + a[:, None, None] * k2[None, None, :])
    cb, sb = cs(num, n)
    fwd_b = jnp.concatenate([jnp.concatenate([cb, sb], axis=2),
                             jnp.concatenate([-sb, cb], axis=2)], axis=1)
    cbt, sbt = jnp.swapaxes(cb, 1, 2), jnp.swapaxes(sb, 1, 2)
    inv_b = jnp.concatenate([jnp.concatenate([cbt, -sbt], axis=2),
                             jnp.concatenate([sbt, cbt], axis=2)], axis=1)
    return tuple(m.astype(BF16) for m in (fwd_a, inv_a, fwd_b, inv_b))


def _rows(x):
    a, g, c = x.shape
    return x.reshape(a * g, c)


def _dft_a_kernel(f_ref, x_ref, re_ref, im_ref):
    y = _dot(f_ref[...], _rows(x_ref[...]).astype(BF16))
    half = y.shape[0] // 2
    re_ref[...] = y[:half].reshape(re_ref.shape)
    im_ref[...] = y[half:].reshape(im_ref.shape)


def _dft_stage_a(xv, fwd_a, n1):
    b, k, n2, ch = xv.shape
    groups = n2 // SUBLANE
    out = jax.ShapeDtypeStruct((b, n1, n2, ch), F32)
    return pl.pallas_call(
        _dft_a_kernel,
        grid=(b, groups),
        in_specs=[pl.BlockSpec((2 * n1 * SUBLANE, k * SUBLANE), lambda bi, j: (0, 0)),
                  pl.BlockSpec((None, k, SUBLANE, ch), lambda bi, j: (bi, 0, j, 0))],
        out_specs=[pl.BlockSpec((None, n1, SUBLANE, ch), lambda bi, j: (bi, 0, j, 0))] * 2,
        out_shape=[out, out],
        compiler_params=_cp("parallel", "parallel"),
        name="dft_stage_a",
    )(fwd_a[:, :k * SUBLANE], xv)


def _dft_b_kernel(m_ref, re_ref, im_ref, ss_ref, kr_ref, ki_ref):
    nrm = lax.rsqrt(ss_ref[...] + EPS)
    n2 = m_ref.shape[1] // 2
    for k in range(m_ref.shape[0]):
        rows = slice(k * n2, (k + 1) * n2)
        x = _dot(m_ref[k], jnp.concatenate([re_ref[rows], im_ref[rows]], axis=0).astype(BF16))
        kr_ref[rows] = x[:n2] * nrm
        ki_ref[rows] = x[n2:] * nrm


def _filter_spectrum(kern, sumsq, tabs, n1, n2):
    fwd_a, _, fwd_b, _ = tabs
    ch = kern.shape[1]
    are, aim = _dft_stage_a(kern.reshape(1, n1, n2, ch), fwd_a, n1)
    are, aim = are.reshape(n1 * n2, ch), aim.reshape(n1 * n2, ch)
    kb = _tile(n1, DFT_BLOCKS)
    blk = pl.BlockSpec((kb * n2, ch), lambda i: (i, 0))
    out = jax.ShapeDtypeStruct((n1 * n2, ch), F32)
    return pl.pallas_call(
        _dft_b_kernel,
        grid=(n1 // kb,),
        in_specs=[pl.BlockSpec((kb, 2 * n2, 2 * n2), lambda i: (i, 0, 0)), blk, blk,
                  pl.BlockSpec((1, ch), lambda i: (0, 0))],
        out_specs=[blk, blk],
        out_shape=[out, out],
        compiler_params=_cp("parallel"),
        name="filter_spectrum",
    )(fwd_b, are, aim, sumsq)


def _dft_mul_kernel(f_ref, g_ref, re_ref, im_ref, kr_ref, ki_ref, qr_ref, qi_ref):
    n2 = f_ref.shape[1] // 2
    for k in range(f_ref.shape[0]):
        rows = slice(k * n2, (k + 1) * n2)
        x = _dot(f_ref[k], jnp.concatenate([re_ref[rows], im_ref[rows]], axis=0).astype(BF16))
        xr, xi = x[:n2], x[n2:]
        kr, ki = kr_ref[rows], ki_ref[rows]
        prod = jnp.concatenate([xr * kr - xi * ki, xr * ki + xi * kr], axis=0)
        q = _dot(g_ref[k], prod.astype(BF16))
        qr_ref[rows] = q[:n2]
        qi_ref[rows] = q[n2:]


def _dft_inv_kernel(g_ref, qr_ref, qi_ref, z_ref, x2_ref, bias_ref, o_ref):
    q = jnp.concatenate([_rows(qr_ref[...]), _rows(qi_ref[...])], axis=0)
    y = _dot(g_ref[...], q.astype(BF16)).reshape(z_ref.shape)
    o_ref[...] = x2_ref[...] * (y + z_ref[...] * bias_ref[...])


def _hyena_long(z, x2, kr, ki, bias, tabs, batch, seq_len, n1, n2):
    fwd_a, inv_a, fwd_b, inv_b = tabs
    ch = z.shape[1]
    zv = z.reshape(batch, n1 // 2, n2, ch)
    are, aim = _dft_stage_a(zv, fwd_a, n1)
    are, aim = are.reshape(batch, n1 * n2, ch), aim.reshape(batch, n1 * n2, ch)
    kb = _tile(n1, DFT_BLOCKS)
    mat = pl.BlockSpec((kb, 2 * n2, 2 * n2), lambda i, b: (i, 0, 0))
    dat = pl.BlockSpec((None, kb * n2, ch), lambda i, b: (b, i, 0))
    spec = pl.BlockSpec((kb * n2, ch), lambda i, b: (i, 0))
    out = jax.ShapeDtypeStruct((batch, n1 * n2, ch), F32)
    qr, qi = pl.pallas_call(
        _dft_mul_kernel,
        grid=(n1 // kb, batch),
        in_specs=[mat, mat, dat, dat, spec, spec],
        out_specs=[dat, dat],
        out_shape=[out, out],
        compiler_params=_cp("parallel", "parallel"),
        name="dft_filter_mul",
    )(fwd_b, inv_b, are, aim, kr, ki)
    qr, qi = qr.reshape(batch, n1, n2, ch), qi.reshape(batch, n1, n2, ch)
    full = pl.BlockSpec((None, n1, SUBLANE, ch), lambda b, j: (b, 0, j, 0))
    half = pl.BlockSpec((None, n1 // 2, SUBLANE, ch), lambda b, j: (b, 0, j, 0))
    o = pl.pallas_call(
        _dft_inv_kernel,
        grid=(batch, n2 // SUBLANE),
        in_specs=[pl.BlockSpec(inv_a.shape, lambda b, j: (0, 0)), full, full, half, half,
                  pl.BlockSpec((1, ch), lambda b, j: (0, 0))],
        out_specs=half,
        out_shape=jax.ShapeDtypeStruct((batch, n1 // 2, n2, ch), F32),
        compiler_params=_cp("parallel", "parallel"),
        name="dft_inverse_gate",
    )(inv_a, qr, qi, zv, x2.reshape(batch, n1 // 2, n2, ch), bias.reshape(1, ch))
    return o.reshape(batch * seq_len, ch)


def _hyena_short_kernel(f_ref, g_ref, k_ref, ss_ref, z_ref, x2_ref, bias_ref, o_ref):
    f = f_ref[...]
    z = z_ref[...]
    seq = z.shape[0]
    ks = _dot(f, k_ref[...].astype(BF16)) * lax.rsqrt(ss_ref[...] + EPS)
    zs = _dot(f[:, :seq], z.astype(BF16))
    n = ks.shape[0] // 2
    kr, ki, zr, zi = ks[:n], ks[n:], zs[:n], zs[n:]
    prod = jnp.concatenate([zr * kr - zi * ki, zr * ki + zi * kr], axis=0)
    y = _dot(g_ref[...], prod.astype(BF16))
    o_ref[...] = x2_ref[...] * (y + z * bias_ref[...])


def _hyena_short(z, x2, kern, sumsq, bias, batch, seq_len):
    n = 2 * seq_len
    ch = z.shape[1]
    a = jnp.arange(n)
    ang = ((a[:, None] * a[None, :]) % n).astype(F32) * (2.0 * math.pi / n)
    c, s = jnp.cos(ang), jnp.sin(ang)
    f = jnp.concatenate([c, -s], axis=0).astype(BF16)
    g = (jnp.concatenate([c, -s], axis=1)[:seq_len] / n).astype(BF16)
    full = lambda shape: pl.BlockSpec(shape, lambda b: (0, 0))
    dat = pl.BlockSpec((seq_len, ch), lambda b: (b, 0))
    return pl.pallas_call(
        _hyena_short_kernel,
        grid=(batch,),
        in_specs=[full((2 * n, n)), full((seq_len, 2 * n)),
                  full((n, ch)), full((1, ch)), dat, dat, full((1, ch))],
        out_specs=dat,
        out_shape=jax.ShapeDtypeStruct((batch * seq_len, ch), F32),
        compiler_params=_cp("parallel"),
        name="hyena_short_conv",
    )(f, g, kern, sumsq, z, x2, bias.reshape(1, ch))


def _hyena_mixer(p, col0, width, batch, seq_len, conv_w, conv_b, w1, b1, w2, b2, w3, freq, bias):
    z, x2 = _hyena_pre(p, col0, width, seq_len, conv_w, conv_b)
    kern, sumsq = _hyena_filter(seq_len, w1, b1, w2, b2, w3, freq)
    n = 2 * seq_len
    if n <= 4 * DFT_RADIX:
        return _hyena_short(z, x2, kern, sumsq, bias, batch, seq_len)
    n2 = DFT_RADIX
    n1 = n // n2
    tabs = _dft_tables(n1, n2)
    kr, ki = _filter_spectrum(kern, sumsq, tabs, n1, n2)
    return _hyena_long(z, x2, kr, ki, bias, tabs, batch, seq_len, n1, n2)


def _outproj_kernel(x_ref, a_ref, b_ref, c_ref, d_ref, w_ref, g_ref, o_ref):
    acc = None
    row = 0
    for m_ref in (a_ref, b_ref, c_ref, d_ref):
        wdt = m_ref.shape[1]
        part = _dot(m_ref[...].astype(BF16), w_ref[row:row + wdt, :])
        acc = part if acc is None else acc + part
        row += wdt
    o_ref[...] = x_ref[...] + g_ref[...] * acc


def _out_projection(x2d, mixers, w, gate, rows_per_group):
    n, d = x2d.shape
    tm = _tile(rows_per_group, 512)
    tpg = rows_per_group // tm
    return pl.pallas_call(
        _outproj_kernel,
        grid=(n // tm,),
        in_specs=[pl.BlockSpec((tm, d), lambda i: (i, 0))]
        + [pl.BlockSpec((tm, m.shape[1]), lambda i: (i, 0)) for m in mixers]
        + [pl.BlockSpec(w.shape, lambda i: (0, 0)),
           pl.BlockSpec((None, 1, d), lambda i: (i // tpg, 0, 0))],
        out_specs=pl.BlockSpec((tm, d), lambda i: (i, 0)),
        out_shape=jax.ShapeDtypeStruct((n, d), F32),
        compiler_params=_cp("parallel"),
        name="out_proj",
    )(x2d, *mixers, w, gate)


def _ffn_kernel(seq_len, tile, x_ref, xp_ref, xn_ref, g_ref, sc_ref, sh_ref, gate_ref,
                wg_ref, wu_ref, cwg_ref, cwu_ref, cbg_ref, cbu_ref, wd_ref, o_ref, h_scr, acc):
    j = pl.program_id(1)

    @pl.when(j == 0)
    def _():
        ext = jnp.concatenate([xp_ref[...], x_ref[...], xn_ref[...]], axis=0)
        h = _norm_mod(ext, g_ref[...], sc_ref[...], sh_ref[...])
        tps = seq_len // tile
        outside = _outside_rows(tile, pl.program_id(0) % tps, tps)
        h_scr[...] = jnp.where(outside, 0.0, h).astype(BF16)
        acc[...] = jnp.zeros(acc.shape, F32)

    h = h_scr[...]
    gate = _conv3(_dot(h, wg_ref[...]), cwg_ref[...], cbg_ref[...], tile)
    up = _conv3(_dot(h, wu_ref[...]), cwu_ref[...], cbu_ref[...], tile)
    act = (gate * jax.nn.sigmoid(gate)) * up
    acc[...] += _dot(act.astype(BF16), wd_ref[...])

    @pl.when(j == pl.num_programs(1) - 1)
    def _():
        o_ref[...] = x_ref[...] + gate_ref[...] * acc[...]


def _conv_ffn(x2d, g, sc, sh, gate, w_up, conv_w, conv_b, w_down, seq_len, rows_per_group):
    n, d = x2d.shape
    f = w_down.shape[0]
    tile = _tile(seq_len, 512)
    hc = _tile(f, 512)
    nh = f // hc
    tpg = rows_per_group // tile
    r = tile // HALO
    last = n // HALO - 1
    vec = pl.BlockSpec((None, 1, d), lambda i, j: (i // tpg, 0, 0))
    cb = conv_b.reshape(1, 2 * f)
    return pl.pallas_call(
        functools.partial(_ffn_kernel, seq_len, tile),
        grid=(n // tile, nh),
        in_specs=[
            pl.BlockSpec((tile, d), lambda i, j: (i, 0)),
            pl.BlockSpec((HALO, d), lambda i, j: (jnp.maximum(i * r - 1, 0), 0)),
            pl.BlockSpec((HALO, d), lambda i, j: (jnp.minimum(i * r + r, last), 0)),
            pl.BlockSpec((1, d), lambda i, j: (0, 0)),
            vec, vec, vec,
            pl.BlockSpec((d, hc), lambda i, j: (0, j)),
            pl.BlockSpec((d, hc), lambda i, j: (0, nh + j)),
            pl.BlockSpec((3, hc), lambda i, j: (0, j)),
            pl.BlockSpec((3, hc), lambda i, j: (0, nh + j)),
            pl.BlockSpec((1, hc), lambda i, j: (0, j)),
            pl.BlockSpec((1, hc), lambda i, j: (0, nh + j)),
            pl.BlockSpec((hc, d), lambda i, j: (j, 0)),
        ],
        out_specs=pl.BlockSpec((tile, d), lambda i, j: (i, 0)),
        out_shape=jax.ShapeDtypeStruct((n, d), F32),
        scratch_shapes=[pltpu.VMEM((tile + 2 * HALO, d), BF16), pltpu.VMEM((tile, d), F32)],
        compiler_params=_cp("parallel", "arbitrary"),
        name="conv_ffn",
    )(x2d, x2d, x2d, g, sc, sh, gate, w_up, w_up, conv_w, conv_w, cb, cb, w_down)


def kernel(x, c, ctx, c_ctx, w_mod, b_mod, norm1_g, norm2_g, w_in, w_out, qk_gain, diff_lam, diff_subln,
           win_sink, pool_w, pool_scale, hy_conv_w, hy_conv_b, hy_w1, hy_b1, hy_w2, hy_b2, hy_w3, hy_freq,
           hy_bias, ffn_w_in, ffn_conv_w, ffn_conv_b, ffn_w_out):
    batch, seq, d = x.shape
    clen = ctx.shape[1]
    depth = w_in.shape[0]
    a_heads = d // 512
    a_width = a_heads * 2 * HEAD_DIM
    b_heads = d // 256
    b_kv = b_heads // 4
    b_width = b_heads * HEAD_DIM
    b_kvw = b_kv * HEAD_DIM
    c_width = d // 4
    d_width = d - a_width - b_width - c_width
    off_ka = a_width
    off_va = 2 * a_width
    off_qb = 3 * a_width
    off_kb = off_qb + b_width
    off_vb = off_kb + b_kvw
    off_pool = off_vb + b_kvw
    off_hy = off_pool + c_width
    in_cols = off_hy + 3 * d_width
    order = ((off_hy, in_cols), (off_pool, off_hy), (off_va, off_qb), (0, off_ka), (off_ka, off_va),
             (off_qb, off_kb), (off_kb, off_vb), (off_vb, off_pool))
    n_hy, n_pool, n_va, n_qa, n_ka, n_qb, n_kb, n_vb = (
        sum(e - s for s, e in order[:k]) for k in range(len(order)))
    assert a_width % LANE == 0 and b_width % LANE == 0 and b_kvw == LANE and c_width // 4 == LANE
    assert seq % BLOCK == 0 and clen % BLOCK == 0 and n_pool % c_width == 0
    assert (n_ka - n_qa) == a_width and (n_qb - n_qa) % b_width == 0 and n_kb - n_qb == b_width

    rows = -(-(batch + 1) // SUBLANE) * SUBLANE
    cvec = jnp.zeros((rows, d), F32).at[:batch].set(c).at[batch].set(c_ctx)
    mods = _modulation(cvec, w_mod, b_mod)

    w_in_b = jnp.concatenate([w_in[:, :, s:e] for s, e in order], axis=-1).astype(BF16)
    w_out_b = w_out.astype(BF16)
    ffn_in_b = ffn_w_in.astype(BF16)
    ffn_out_b = ffn_w_out.astype(BF16)
    pool_w_b = pool_w.astype(BF16)

    cos_t, sin_t = _rope_tables(seq)
    scale = HEAD_DIM ** -0.5
    tile2 = lambda v: jnp.tile(v, LANE // HEAD_DIM)

    xs = x.reshape(batch * seq, d)
    cs = ctx.reshape(batch * clen, d)
    for l in range(depth):
        last = l == depth - 1
        m = mods[l]
        sh1, sc1, g1, sh2, sc2, g2 = [m[:batch, i * d:(i + 1) * d].reshape(batch, 1, d) for i in range(6)]
        csh1, csc1, cg1, csh2, csc2, cg2 = [m[batch, i * d:(i + 1) * d].reshape(1, 1, d) for i in range(6)]
        n1g = norm1_g[l].reshape(1, d)
        n2g = norm2_g[l].reshape(1, d)

        p = _in_projection(xs, n1g, sc1, sh1, w_in_b[l], seq)
        pc = _in_projection(cs, n1g, csc1, csh1, w_in_b[l], batch * clen)

        na = a_width // LANE
        nbq = b_width // LANE
        gains = jnp.stack([tile2(qk_gain[l, 0]) * (scale * LOG2E)] * na + [tile2(qk_gain[l, 1])] * na
                          + [tile2(qk_gain[l, 2]) * (scale * LOG2E)] * nbq + [tile2(qk_gain[l, 3])])
        qk = _qk_prep(p, n_qa // LANE, gains, cos_t, sin_t, seq, True)
        qk_c = _qk_prep(pc, n_qa // LANE, gains, cos_t, sin_t, clen, False)
        kb_blk = 2 * na + nbq
        qb_blk = (n_qb - n_qa) // b_width

        lam_p = diff_lam[l].astype(F32)
        lambda_init = 0.8 - 0.6 * math.exp(-0.3 * l)
        lam = jnp.exp(jnp.sum(lam_p[0] * lam_p[1])) - jnp.exp(jnp.sum(lam_p[2] * lam_p[3])) + lambda_init
        lam_v = jnp.full((1, LANE), lam, F32)
        subg = diff_subln[l].reshape(1, LANE)
        sink_tab = jnp.broadcast_to(win_sink[l].astype(F32)[:, None], (b_heads, LANE))
        hy = (hy_conv_w[l], hy_conv_b[l], hy_w1[l], hy_b1[l], hy_w2[l], hy_b2[l], hy_w3[l], hy_freq[l],
              hy_bias[l])

        vt = _transposed_values(p, n_va // LANE, a_heads, seq, _tile(seq, 512))
        vt_c = _transposed_values(pc, n_va // LANE, a_heads, clen, clen)
        o_a = _diff_attention(qk, seq, qk_c, vt_c, clen, qk, vt, seq, lam_v, subg, 1.0 - lambda_init,
                              a_heads, na)
        vtb = _transposed_values(p, n_vb // LANE, 1, seq, BLOCK)
        vtb_c = _transposed_values(pc, n_vb // LANE, 1, clen, clen)
        o_b = _window_attention(qk, seq, qk_c, vtb_c, clen, vtb, sink_tab, b_heads, b_kv,
                                qb_blk, kb_blk, True)
        o_c = _pool_mixer(p, n_pool, c_width, seq, pool_w_b[l], pool_scale[l].reshape(1, c_width))
        o_d = _hyena_mixer(p, n_hy, 3 * d_width, batch, seq, *hy)
        xs = _out_projection(xs, (o_a, o_b, o_c, o_d), w_out_b[l], g1, seq)
        xs = _conv_ffn(xs, n2g, sc2, sh2, g2, ffn_in_b[l], ffn_conv_w[l], ffn_conv_b[l], ffn_out_b[l],
                       seq, seq)

        if not last:
            o_a_c = _diff_attention(qk_c, clen, qk_c, vt_c, clen, None, None, 0, lam_v, subg,
                                    1.0 - lambda_init, a_heads, na)
            o_b_c = _window_attention(qk_c, clen, qk_c, vtb_c, clen, None, sink_tab, b_heads, b_kv,
                                      qb_blk, kb_blk, False)
            o_c_c = _pool_mixer(pc, n_pool, c_width, clen, pool_w_b[l], pool_scale[l].reshape(1, c_width))
            o_d_c = _hyena_mixer(pc, n_hy, 3 * d_width, batch, clen, *hy)
            cs = _out_projection(cs, (o_a_c, o_b_c, o_c_c, o_d_c), w_out_b[l], cg1, batch * clen)
            cs = _conv_ffn(cs, n2g, csc2, csh2, cg2, ffn_in_b[l], ffn_conv_w[l], ffn_conv_b[l],
                           ffn_out_b[l], clen, batch * clen)
    return xs.reshape(batch, seq, d)
```

```python
import functools
import math

import jax
import jax.numpy as jnp
from jax import lax
from jax.experimental import pallas as pl
from jax.experimental.pallas import tpu as pltpu

F32 = jnp.float32
BF16 = jnp.bfloat16

HEAD_DIM = 64
GRID_W = 64
ROPE_BASE = 10000.0
EPS = 1e-6
NEG_INF = -1e30
WINDOW = 128
BLOCK = 128
POOL_WINDOWS = (2, 4, 8, 16)
HY_EMB = 33
HY_HIDDEN = 64
HY_FAST_DECAY = 0.3
HY_SLOW_DECAY = 1.5
HY_TARGET = 1e-2

LANE = 128
SUBLANE = 8
HALO = SUBLANE
DFT_RADIX = 128
DFT_BLOCKS = 8
VT_ROWS = LANE + 16
LOG2E = 1.4426950408889634
VMEM_LIMIT = 56 * 1024 * 1024


def _cp(*sem):
    return pltpu.CompilerParams(dimension_semantics=sem, vmem_limit_bytes=VMEM_LIMIT)


def _tile(n, pref):
    if n <= pref:
        return n
    t = pref
    while n % t:
        t //= 2
    assert t >= SUBLANE, (n, pref)
    return t


def _dot(a, b):
    return jnp.dot(a, b, preferred_element_type=F32)


def _dot_nt(a, b):
    return lax.dot_general(a, b, (((1,), (1,)), ((), ())), preferred_element_type=F32)


def _split_bf16(x):
    hi = x.astype(BF16)
    lo = (x - hi.astype(F32)).astype(BF16)
    return hi, lo


def _mod_kernel(c_ref, w_ref, b_ref, o_ref):
    c = c_ref[...]
    a = (c * jax.nn.sigmoid(c)).astype(BF16)
    o_ref[...] = _dot(a, w_ref[...].astype(BF16)) + b_ref[...]


def _modulation(cvec, w_mod, b_mod):
    depth, d, n = w_mod.shape
    rows = cvec.shape[0]
    tn = _tile(n, 1024)
    return pl.pallas_call(
        _mod_kernel,
        grid=(depth, n // tn),
        in_specs=[
            pl.BlockSpec((rows, d), lambda l, j: (0, 0)),
            pl.BlockSpec((None, d, tn), lambda l, j: (l, 0, j)),
            pl.BlockSpec((None, 1, tn), lambda l, j: (l, 0, j)),
        ],
        out_specs=pl.BlockSpec((None, rows, tn), lambda l, j: (l, 0, j)),
        out_shape=jax.ShapeDtypeStruct((depth, rows, n), F32),
        compiler_params=_cp("parallel", "parallel"),
        name="adaln_mod",
    )(cvec, w_mod, b_mod.reshape(depth, 1, n))


def _norm_mod(x, g, sc, sh):
    ms = jnp.mean(x * x, axis=-1, keepdims=True)
    return (x * lax.rsqrt(ms + EPS) * g) * (1.0 + sc) + sh


def _inproj_kernel(x_ref, g_ref, sc_ref, sh_ref, w_ref, o_ref):
    h = _norm_mod(x_ref[...], g_ref[...], sc_ref[...], sh_ref[...])
    o_ref[...] = _dot(h.astype(BF16), w_ref[...])


def _in_projection(x2d, g, sc, sh, w, rows_per_group):
    n, d = x2d.shape
    nc = w.shape[1]
    tm = _tile(rows_per_group, 512)
    tpg = rows_per_group // tm
    ncol = 2 if (nc // 2) % LANE == 0 else 1
    tn = nc // ncol
    vec = pl.BlockSpec((None, 1, d), lambda j, i: (i // tpg, 0, 0))
    return pl.pallas_call(
        _inproj_kernel,
        grid=(ncol, n // tm),
        in_specs=[
            pl.BlockSpec((tm, d), lambda j, i: (i, 0)),
            pl.BlockSpec((1, d), lambda j, i: (0, 0)),
            vec, vec,
            pl.BlockSpec((d, tn), lambda j, i: (0, j)),
        ],
        out_specs=pl.BlockSpec((tm, tn), lambda j, i: (i, j)),
        out_shape=jax.ShapeDtypeStruct((n, nc), F32),
        compiler_params=_cp("parallel", "parallel"),
        name="in_proj",
    )(x2d, g, sc, sh, w)


def _qkprep_kernel(rope, x_ref, gain_ref, cos_ref, sin_ref, gm_ref, o_ref):
    x = x_ref[...]
    hi, lo = _split_bf16(x * x)
    gm = gm_ref[...]
    ms = _dot(hi, gm) + _dot(lo, gm)
    y = x * lax.rsqrt(ms + EPS) * gain_ref[...]
    if rope:
        lane = lax.broadcasted_iota(jnp.int32, y.shape, 1)
        first_half = (lane & 31) < 16
        partner = jnp.where(first_half, pltpu.roll(y, LANE - 16, 1), pltpu.roll(y, 16, 1))
        y = y * cos_ref[...] + partner * sin_ref[...]
    o_ref[...] = y.astype(BF16)


def _qk_prep(p, col_block0, gains, cos_t, sin_t, seq_len, rope):
    n = p.shape[0]
    nblk = gains.shape[0]
    tm = _tile(seq_len, 1024)
    tps = seq_len // tm
    lane = jnp.arange(LANE)
    gm = ((lane[:, None] // HEAD_DIM) == (lane[None, :] // HEAD_DIM)).astype(BF16) * (1.0 / HEAD_DIM)
    tab = pl.BlockSpec((tm, LANE), lambda i, j: (i % tps, 0))
    return pl.pallas_call(
        functools.partial(_qkprep_kernel, rope),
        grid=(n // tm, nblk),
        in_specs=[
            pl.BlockSpec((tm, LANE), lambda i, j: (i, col_block0 + j)),
            pl.BlockSpec((None, 1, LANE), lambda i, j: (j, 0, 0)),
            tab, tab,
            pl.BlockSpec((LANE, LANE), lambda i, j: (0, 0)),
        ],
        out_specs=pl.BlockSpec((tm, LANE), lambda i, j: (i, j)),
        out_shape=jax.ShapeDtypeStruct((n, nblk * LANE), BF16),
        compiler_params=_cp("parallel", "parallel"),
        name="qk_prep",
    )(p, gains.reshape(nblk, 1, LANE), cos_t, sin_t, gm.astype(BF16))


def _rope_tables(seq_len):
    rows = seq_len // GRID_W
    row = jnp.repeat(jnp.arange(rows, dtype=F32), GRID_W)
    col = jnp.tile(jnp.arange(GRID_W, dtype=F32), rows)
    half = HEAD_DIM // 2
    inv = ROPE_BASE ** (-jnp.arange(0, half, 2, dtype=F32) / half)
    ang = jnp.concatenate([row[:, None] * inv, col[:, None] * inv], axis=-1)
    lane = jnp.arange(LANE)
    within = lane % HEAD_DIM
    idx = (within // 32) * 16 + (within % 16)
    sign = jnp.where((lane % 32) < 16, -1.0, 1.0).astype(F32)
    return jnp.cos(ang)[:, idx], jnp.sin(ang)[:, idx] * sign


def _vt_kernel(x_ref, o_ref):
    chunks, _, chunk = o_ref.shape
    row = lax.broadcasted_iota(jnp.int32, (VT_ROWS - LANE, chunk), 0)
    ones_row = jnp.where(row == 0, 1.0, 0.0).astype(BF16)
    for c in range(chunks):
        x_t = x_ref[c * chunk:(c + 1) * chunk, :].T.astype(BF16)
        o_ref[c] = jnp.concatenate([x_t, ones_row], axis=0)


def _transposed_values(p, col_block0, nblk, seq_len, chunk):
    n = p.shape[0]
    rows = max(chunk, _tile(seq_len, 512))
    return pl.pallas_call(
        _vt_kernel,
        grid=(n // rows, nblk),
        in_specs=[pl.BlockSpec((rows, LANE), lambda i, j: (i, col_block0 + j))],
        out_specs=pl.BlockSpec((None, rows // chunk, VT_ROWS, chunk), lambda i, j: (j, i, 0, 0)),
        out_shape=jax.ShapeDtypeStruct((nblk, n // chunk, VT_ROWS, chunk), BF16),
        compiler_params=_cp("parallel", "parallel"),
        name="values_t",
    )(p)


def _attn_a_kernel(n_lat, tk, sub_scale, *refs):
    if n_lat:
        (q_ref, kc_ref, vc_ref, k_ref, v_ref, lam_ref, subg_ref, o_ref, qt, m, acc,
         s0, s1, x0, x1, p0, p1, al0, al1) = refs
    else:
        q_ref, kc_ref, vc_ref, lam_ref, subg_ref, o_ref, qt, m, acc = refs
    tq = q_ref.shape[0]
    q_t = q_ref[...].astype(F32).T
    row = lax.broadcasted_iota(jnp.int32, q_t.shape, 0)
    qt[:, :tq] = jnp.where(row < HEAD_DIM, q_t, 0.0).astype(BF16)
    qt[:, tq:] = jnp.where(row >= HEAD_DIM, q_t, 0.0).astype(BF16)
    m[...] = jnp.full(m.shape, -jnp.inf, F32)
    acc[...] = jnp.zeros(acc.shape, F32)

    def softmax_update(s, s_max):
        m_prev = m[...]
        m_new = jnp.maximum(m_prev, s_max)
        m[...] = m_new
        return jnp.exp2(s - m_new).astype(BF16), jnp.exp2(m_prev - m_new)

    s_c = _dot(kc_ref[...], qt[...])

    def context_update():
        p_c, alpha_c = softmax_update(s_c, jnp.max(s_c, axis=0, keepdims=True))
        acc[...] = alpha_c * acc[...] + _dot(vc_ref[0], p_c)

    if not n_lat:
        context_update()
    else:
        def scores(c, s_buf, x_buf):
            s = _dot(k_ref[pl.ds(pl.multiple_of(c * tk, tk), tk), :], qt[...])
            s_buf[...] = s
            x_buf[...] = jnp.max(s, axis=0, keepdims=True)

        def softmax(s_buf, x_buf, p_buf, al_buf):
            p_buf[...], al_buf[...] = softmax_update(s_buf[...], x_buf[...])

        def values(c, p_buf, al_buf):
            acc[...] = al_buf[...] * acc[...] + _dot(v_ref[c], p_buf[...])

        scores(0, s0, x0)
        if n_lat == 1:
            context_update()
            softmax(s0, x0, p0, al0)
            values(0, p0, al0)
        else:
            assert n_lat % 2 == 0
            scores(1, s1, x1)
            context_update()
            softmax(s0, x0, p0, al0)

            def body(t, carry):
                c = 2 * t
                scores(c + 2, s0, x0)
                softmax(s1, x1, p1, al1)
                values(c, p0, al0)
                scores(c + 3, s1, x1)
                softmax(s0, x0, p0, al0)
                values(c + 1, p1, al1)
                return carry
            lax.fori_loop(0, n_lat // 2 - 1, body, 0)
            softmax(s1, x1, p1, al1)
            values(n_lat - 2, p0, al0)
            values(n_lat - 1, p1, al1)

    o_t = acc[:LANE] / acc[LANE:LANE + 1]
    o = (o_t[:, :tq] - lam_ref[0:1, 0:1] * o_t[:, tq:]).T
    ms = jnp.mean(o * o, axis=-1, keepdims=True)
    o_ref[...] = ((o * lax.rsqrt(ms + EPS) * subg_ref[...]) * sub_scale).astype(BF16)


def _diff_attention(q_arr, q_len, qk_ctx, vt_ctx, ctx_len, qk_lat, vt_lat, lat_len, lam, subg,
                    sub_scale, heads, k_col0):
    batch = q_arr.shape[0] // q_len
    tq = _tile(q_len, 1024)
    nq = q_len // tq
    in_specs = [
        pl.BlockSpec((tq, LANE), lambda b, h, i: (b * nq + i, h)),
        pl.BlockSpec((ctx_len, LANE), lambda b, h, i: (b, k_col0 + h)),
        pl.BlockSpec((None, 1, VT_ROWS, ctx_len), lambda b, h, i: (h, b, 0, 0)),
    ]
    args = [q_arr, qk_ctx, vt_ctx]
    n_lat, tk = 0, 0
    if qk_lat is not None:
        tk = vt_lat.shape[-1]
        n_lat = lat_len // tk
        in_specs += [
            pl.BlockSpec((lat_len, LANE), lambda b, h, i: (b, k_col0 + h)),
            pl.BlockSpec((None, n_lat, VT_ROWS, tk), lambda b, h, i: (h, b, 0, 0)),
        ]
        args += [qk_lat, vt_lat]
    in_specs += [pl.BlockSpec((1, LANE), lambda b, h, i: (0, 0))] * 2
    args += [lam, subg]
    stat = pltpu.VMEM((1, 2 * tq), F32)
    scratch = [pltpu.VMEM((LANE, 2 * tq), BF16), stat, pltpu.VMEM((VT_ROWS, 2 * tq), F32)]
    if n_lat:
        scratch += [pltpu.VMEM((tk, 2 * tq), F32)] * 2 + [stat] * 2
        scratch += [pltpu.VMEM((tk, 2 * tq), BF16)] * 2 + [stat] * 2
    return pl.pallas_call(
        functools.partial(_attn_a_kernel, n_lat, tk, sub_scale),
        grid=(batch, heads, nq),
        in_specs=in_specs,
        out_specs=pl.BlockSpec((tq, LANE), lambda b, h, i: (b * nq + i, h)),
        out_shape=jax.ShapeDtypeStruct((batch * q_len, heads * LANE), BF16),
        scratch_shapes=scratch,
        compiler_params=_cp("parallel", "parallel", "parallel"),
        name="diff_attn",
    )(*args)


def _attn_b_kernel(has_loc, seq_len, q_heads, group, *refs):
    if has_loc:
        (q_ref, kx_ref, vx_ref, kp_ref, kc_ref, kn_ref, vp_ref, vc_ref, vn_ref, sink_ref, o_ref) = refs
    else:
        (q_ref, kx_ref, vx_ref, sink_ref, o_ref) = refs
    n = pl.program_id(1)
    clen = kx_ref.shape[0]
    q_t = q_ref[...].astype(F32).T
    keys = [kx_ref[...]]
    vals = [vx_ref[0]]
    if has_loc:
        keys += [kp_ref[...], kc_ref[...], kn_ref[...]]
        vals += [vp_ref[0], vc_ref[0], vn_ref[0]]
        kpos = (n - 1) * BLOCK + lax.broadcasted_iota(jnp.int32, (3 * BLOCK, BLOCK), 0)
        qpos = n * BLOCK + lax.broadcasted_iota(jnp.int32, (3 * BLOCK, BLOCK), 1)
        valid = (kpos >= 0) & (kpos < seq_len) & (jnp.abs(kpos - qpos) <= WINDOW)
        valid = jnp.concatenate([valid] * group, axis=1)
    k_all = jnp.concatenate(keys, axis=0)
    v_all = jnp.concatenate(vals, axis=1)
    zeros = jnp.zeros((HEAD_DIM, BLOCK), F32)
    heads_out = []
    for kvh in range(q_heads // group):
        cols = []
        for g in range(group):
            hq = kvh * group + g
            blk = q_t[hq * HEAD_DIM:(hq + 1) * HEAD_DIM]
            cols.append(jnp.concatenate([blk, zeros] if kvh == 0 else [zeros, blk], axis=0))
        q_kv = jnp.concatenate(cols, axis=1).astype(BF16)
        sink = jnp.concatenate([sink_ref[kvh * group + g:kvh * group + g + 1, :] for g in range(group)],
                               axis=1) * LOG2E
        s = _dot(k_all, q_kv)
        if has_loc:
            s = jnp.concatenate([s[:clen], jnp.where(valid, s[clen:], NEG_INF)], axis=0)
        m = jnp.maximum(jnp.max(s, axis=0, keepdims=True), sink)
        o_t = _dot(v_all, jnp.exp2(s - m).astype(BF16))
        den = o_t[LANE:LANE + 1] + jnp.exp2(sink - m)
        o_t = o_t[kvh * HEAD_DIM:(kvh + 1) * HEAD_DIM] / den
        heads_out += [o_t[:, g * BLOCK:(g + 1) * BLOCK] for g in range(group)]
    o_ref[...] = jnp.concatenate(heads_out, axis=0).T.astype(BF16)


def _window_attention(q_arr, q_len, qk_ctx, vt_ctx, ctx_len, vt_lat, sink_tab, q_heads, kv_heads,
                      q_blk, k_blk, local):
    assert kv_heads == 2, "key/value heads must fill exactly one 128-lane block"
    batch = q_arr.shape[0] // q_len
    nb = q_len // BLOCK
    qw = q_heads * HEAD_DIM
    in_specs = [
        pl.BlockSpec((BLOCK, qw), lambda b, n: (b * nb + n, q_blk)),
        pl.BlockSpec((ctx_len, LANE), lambda b, n: (b, k_blk)),
        pl.BlockSpec((None, 1, VT_ROWS, ctx_len), lambda b, n: (0, b, 0, 0)),
    ]
    args = [q_arr, qk_ctx, vt_ctx]
    if local:
        prev = lambda b, n: b * nb + jnp.maximum(n - 1, 0)
        nxt = lambda b, n: b * nb + jnp.minimum(n + 1, nb - 1)
        in_specs += [
            pl.BlockSpec((BLOCK, LANE), lambda b, n: (prev(b, n), k_blk)),
            pl.BlockSpec((BLOCK, LANE), lambda b, n: (b * nb + n, k_blk)),
            pl.BlockSpec((BLOCK, LANE), lambda b, n: (nxt(b, n), k_blk)),
            pl.BlockSpec((None, 1, VT_ROWS, BLOCK), lambda b, n: (0, prev(b, n), 0, 0)),
            pl.BlockSpec((None, 1, VT_ROWS, BLOCK), lambda b, n: (0, b * nb + n, 0, 0)),
            pl.BlockSpec((None, 1, VT_ROWS, BLOCK), lambda b, n: (0, nxt(b, n), 0, 0)),
        ]
        args += [q_arr] * 3 + [vt_lat] * 3
    in_specs.append(pl.BlockSpec((q_heads, LANE), lambda b, n: (0, 0)))
    args.append(sink_tab)
    return pl.pallas_call(
        functools.partial(_attn_b_kernel, local, q_len, q_heads, q_heads // kv_heads),
        grid=(batch, nb),
        in_specs=in_specs,
        out_specs=pl.BlockSpec((BLOCK, qw), lambda b, n: (b * nb + n, 0)),
        out_shape=jax.ShapeDtypeStruct((batch * q_len, qw), BF16),
        compiler_params=_cp("parallel", "parallel"),
        name="window_attn",
    )(*args)


def _halo_specs(tile, width, col_blk, n_rows):
    r = tile // HALO
    last = n_rows // HALO - 1
    return [
        pl.BlockSpec((tile, width), lambda i: (i, col_blk)),
        pl.BlockSpec((HALO, width), lambda i: (jnp.maximum(i * r - 1, 0), col_blk)),
        pl.BlockSpec((HALO, width), lambda i: (jnp.minimum(i * r + r, last), col_blk)),
    ]


def _outside_rows(tile, tile_idx, tiles_per_seq):
    row = lax.broadcasted_iota(jnp.int32, (tile + 2 * HALO, 1), 0)
    lo = jnp.where(tile_idx == 0, HALO, 0)
    hi = jnp.where(tile_idx == tiles_per_seq - 1, HALO + tile, tile + 2 * HALO)
    return (row < lo) | (row >= hi)


def _with_halo(x, xp, xn, tiles_per_seq):
    ext = jnp.concatenate([xp, x, xn], axis=0)
    outside = _outside_rows(x.shape[0], pl.program_id(0) % tiles_per_seq, tiles_per_seq)
    return jnp.where(outside, 0.0, ext)


def _conv3(ext, w, b, tile):
    rows = ext.shape[0]
    c = (pltpu.roll(ext, 1, 0) * w[0:1] + ext * w[1:2] + pltpu.roll(ext, rows - 1, 0) * w[2:3])
    return c[HALO:HALO + tile] + b


def _pool_kernel(seq_len, tile, x_ref, xp_ref, xn_ref, w_ref, ls_ref, o_ref):
    tps = seq_len // tile
    x = x_ref[...]
    ext = _with_halo(x, xp_ref[...], xn_ref[...], tps)
    rows = tile + 2 * HALO
    t = (pl.program_id(0) % tps) * tile + lax.broadcasted_iota(jnp.int32, (tile, 1), 0)
    gw = x.shape[1] // len(POOL_WINDOWS)
    outs = []
    for g, w in enumerate(POOL_WINDOWS):
        lo = w // 2
        hi = w - 1 - lo
        a = ext[:, g * gw:(g + 1) * gw]
        span = 1
        while span < w:
            a = a + pltpu.roll(a, span, 0)
            span *= 2
        win = (pltpu.roll(a, rows - hi, 0) if hi else a)[HALO:HALO + tile]
        cnt = jnp.minimum(t + hi + 1, seq_len) - jnp.maximum(t - lo, 0)
        d = win / cnt.astype(F32) - x[:, g * gw:(g + 1) * gw]
        outs.append(_dot(d.astype(BF16), w_ref[g]))
    o_ref[...] = (jnp.concatenate(outs, axis=1) * ls_ref[...]).astype(BF16)


def _pool_mixer(p, col0, width, seq_len, w_lin, ls):
    n = p.shape[0]
    tile = _tile(seq_len, 512)
    g, gw, _ = w_lin.shape
    return pl.pallas_call(
        functools.partial(_pool_kernel, seq_len, tile),
        grid=(n // tile,),
        in_specs=_halo_specs(tile, width, col0 // width, n)
        + [pl.BlockSpec((g, gw, gw), lambda i: (0, 0, 0)),
           pl.BlockSpec((1, width), lambda i: (0, 0))],
        out_specs=pl.BlockSpec((tile, width), lambda i: (i, 0)),
        out_shape=jax.ShapeDtypeStruct((n, width), BF16),
        compiler_params=_cp("parallel"),
        name="pool_mixer",
    )(p, p, p, w_lin, ls)


def _hypre_kernel(seq_len, tile, x_ref, xp_ref, xn_ref, cw_ref, cb_ref, z_ref, x2_ref):
    ext = _with_halo(x_ref[...], xp_ref[...], xn_ref[...], seq_len // tile)
    c = _conv3(ext, cw_ref[...], cb_ref[...], tile)
    dw = c.shape[1] // 3
    z_ref[...] = c[:, :dw] * c[:, dw:2 * dw]
    x2_ref[...] = c[:, 2 * dw:]


def _hyena_pre(p, col0, width, seq_len, conv_w, conv_b):
    n = p.shape[0]
    tile = _tile(seq_len, 512)
    dw = width // 3
    out = jax.ShapeDtypeStruct((n, dw), F32)
    return pl.pallas_call(
        functools.partial(_hypre_kernel, seq_len, tile),
        grid=(n // tile,),
        in_specs=_halo_specs(tile, width, col0 // width, n)
        + [pl.BlockSpec((3, width), lambda i: (0, 0)),
           pl.BlockSpec((1, width), lambda i: (0, 0))],
        out_specs=[pl.BlockSpec((tile, dw), lambda i: (i, 0))] * 2,
        out_shape=[out, out],
        compiler_params=_cp("parallel"),
        name="hyena_pre",
    )(p, p, p, conv_w, conv_b.reshape(1, width))


def _hyfilt_kernel(seq_len, tile, zf_ref, w1_ref, b1_ref, w2_ref, b2_ref, w3_ref, fr_ref, dl_ref,
                   k_ref, ss_ref):
    i = pl.program_id(0)
    hp = functools.partial(jnp.dot, precision=lax.Precision.HIGHEST, preferred_element_type=F32)
    zf = zf_ref[...]
    fr = fr_ref[...]
    h = jnp.sin(fr * (hp(zf, w1_ref[...]) + b1_ref[...]))
    h = jnp.sin(fr * (hp(h, w2_ref[...]) + b2_ref[...]))
    k = hp(h, w3_ref[...]) * jnp.exp(-zf[:, 0:1] * dl_ref[...])
    row = i * tile + lax.broadcasted_iota(jnp.int32, (tile, 1), 0)
    k = jnp.where(row == seq_len, 0.0, k)
    k_ref[...] = k

    @pl.when(i == 0)
    def _():
        ss_ref[...] = jnp.zeros(ss_ref.shape, F32)

    ss_ref[...] += jnp.sum(k * k, axis=0, keepdims=True)


def _hyena_filter(seq_len, w1, b1, w2, b2, w3, freq):
    dw = w3.shape[1] // 2
    bands = (HY_EMB - 1) // 2
    t01 = jnp.linspace(0.0, 1.0, seq_len, dtype=F32)[:, None]
    w_ang = 2.0 * math.pi * jnp.arange(seq_len, dtype=F32)[:, None] / seq_len
    f = jnp.linspace(1e-4, bands - 1, bands, dtype=F32)[None, :]
    z = jnp.concatenate([t01, jnp.cos(f * w_ang), -jnp.sin(f * w_ang)], axis=-1)
    rev = (seq_len - jnp.arange(seq_len)) % seq_len
    zf = jnp.pad(jnp.concatenate([z, z[rev]], axis=0), ((0, 0), (0, LANE - HY_EMB)))
    pad_h = LANE - HY_HIDDEN
    w1p = jnp.pad(w1, ((0, LANE - HY_EMB), (0, pad_h)))
    w2p = jnp.pad(w2, ((0, pad_h), (0, pad_h)))
    w3p = jnp.pad(w3, ((0, pad_h), (0, 0)))
    vec = lambda v: jnp.pad(v, (0, pad_h)).reshape(1, LANE)
    deltas = jnp.abs(jnp.linspace(math.log(HY_TARGET) / HY_FAST_DECAY, math.log(HY_TARGET) / HY_SLOW_DECAY,
                                  dw, dtype=F32)).reshape(1, dw)
    tile = _tile(seq_len, 1024)
    tps = seq_len // tile
    full = lambda shape: pl.BlockSpec(shape, lambda i: (0, 0))
    return pl.pallas_call(
        functools.partial(_hyfilt_kernel, seq_len, tile),
        grid=(2 * tps,),
        in_specs=[pl.BlockSpec((tile, LANE), lambda i: (i, 0)),
                  full((LANE, LANE)), full((1, LANE)), full((LANE, LANE)), full((1, LANE)),
                  pl.BlockSpec((LANE, dw), lambda i: (0, i // tps)),
                  full((1, LANE)), full((1, dw))],
        out_specs=[pl.BlockSpec((tile, dw), lambda i: (i, 0)), full((1, dw))],
        out_shape=[jax.ShapeDtypeStruct((2 * seq_len, dw), F32), jax.ShapeDtypeStruct((1, dw), F32)],
        compiler_params=_cp("arbitrary"),
        name="hyena_filter",
    )(zf, w1p, vec(b1), w2p, vec(b2), w3p, vec(freq), deltas)


def _dft_tables(n1, n2):
    n = n1 * n2
    two_pi = 2.0 * math.pi

    def cs(num, den):
        ang = (num % den).astype(F32) * (two_pi / den)
        return jnp.cos(ang), jnp.sin(ang)

    a = jnp.arange(n1)
    c1, s1 = cs(a[:, None] * a[None, :], n1)
    kron = lambda mat: jnp.kron(mat, jnp.eye(SUBLANE, dtype=F32))
    fwd_a = jnp.concatenate([kron(c1), kron(-s1)], axis=0)
    inv_a = jnp.concatenate([kron(c1[: n1 // 2]), kron(-s1[: n1 // 2])], axis=1) / n
    k2 = jnp.arange(n2)
    num = (k2[None, :, None] * k2[None, None, :] * n1 + a[:, None, None] * k2[None, None, :])
    cb, sb = cs(num, n)
    fwd_b = jnp.concatenate([jnp.concatenate([cb, sb], axis=2),
                             jnp.concatenate([-sb, cb], axis=2)], axis=1)
    cbt, sbt = jnp.swapaxes(cb, 1, 2), jnp.swapaxes(sb, 1, 2)
    inv_b = jnp.concatenate([jnp.concatenate([cbt, -sbt], axis=2),
                             jnp.concatenate([sbt, cbt], axis=2)], axis=1)
    return tuple(m.astype(BF16) for m in (fwd_a, inv_a, fwd_b, inv_b))


def _rows(x):
    a, g, c = x.shape
    return x.reshape(a * g, c)


def _dft_a_kernel(f_ref, x_ref, re_ref, im_ref):
    y = _dot(f_ref[...], _rows(x_ref[...]).astype(BF16))
    half = y.shape[0] // 2
    re_ref[...] = y[:half].reshape(re_ref.shape)
    im_ref[...] = y[half:].reshape(im_ref.shape)


def _dft_stage_a(xv, fwd_a, n1):
    b, k, n2, ch = xv.shape
    groups = n2 // SUBLANE
    out = jax.ShapeDtypeStruct((b, n1, n2, ch), F32)
    return pl.pallas_call(
        _dft_a_kernel,
        grid=(b, groups),
        in_specs=[pl.BlockSpec((2 * n1 * SUBLANE, k * SUBLANE), lambda bi, j: (0, 0)),
                  pl.BlockSpec((None, k, SUBLANE, ch), lambda bi, j: (bi, 0, j, 0))],
        out_specs=[pl.BlockSpec((None, n1, SUBLANE, ch), lambda bi, j: (bi, 0, j, 0))] * 2,
        out_shape=[out, out],
        compiler_params=_cp("parallel", "parallel"),
        name="dft_stage_a",
    )(fwd_a[:, :k * SUBLANE], xv)


def _dft_b_kernel(m_ref, re_ref, im_ref, ss_ref, kr_ref, ki_ref):
    nrm = lax.rsqrt(ss_ref[...] + EPS)
    n2 = m_ref.shape[1] // 2
    for k in range(m_ref.shape[0]):
        rows = slice(k * n2, (k + 1) * n2)
        x = _dot(m_ref[k], jnp.concatenate([re_ref[rows], im_ref[rows]], axis=0).astype(BF16))
        kr_ref[rows] = x[:n2] * nrm
        ki_ref[rows] = x[n2:] * nrm


def _filter_spectrum(kern, sumsq, tabs, n1, n2):
    fwd_a, _, fwd_b, _ = tabs
    ch = kern.shape[1]
    are, aim = _dft_stage_a(kern.reshape(1, n1, n2, ch), fwd_a, n1)
    are, aim = are.reshape(n1 * n2, ch), aim.reshape(n1 * n2, ch)
    kb = _tile(n1, DFT_BLOCKS)
    blk = pl.BlockSpec((kb * n2, ch), lambda i: (i, 0))
    out = jax.ShapeDtypeStruct((n1 * n2, ch), F32)
    return pl.pallas_call(
        _dft_b_kernel,
        grid=(n1 // kb,),
        in_specs=[pl.BlockSpec((kb, 2 * n2, 2 * n2), lambda i: (i, 0, 0)), blk, blk,
                  pl.BlockSpec((1, ch), lambda i: (0, 0))],
        out_specs=[blk, blk],
        out_shape=[out, out],
        compiler_params=_cp("parallel"),
        name="filter_spectrum",
    )(fwd_b, are, aim, sumsq)


def _dft_mul_kernel(f_ref, g_ref, re_ref, im_ref, kr_ref, ki_ref, qr_ref, qi_ref):
    n2 = f_ref.shape[1] // 2
    for k in range(f_ref.shape[0]):
        rows = slice(k * n2, (k + 1) * n2)
        x = _dot(f_ref[k], jnp.concatenate([re_ref[rows], im_ref[rows]], axis=0).astype(BF16))
        xr, xi = x[:n2], x[n2:]
        kr, ki = kr_ref[rows], ki_ref[rows]
        prod = jnp.concatenate([xr * kr - xi * ki, xr * ki + xi * kr], axis=0)
        q = _dot(g_ref[k], prod.astype(BF16))
        qr_ref[rows] = q[:n2]
        qi_ref[rows] = q[n2:]


def _dft_inv_kernel(g_ref, qr_ref, qi_ref, z_ref, x2_ref, bias_ref, o_ref):
    q = jnp.concatenate([_rows(qr_ref[...]), _rows(qi_ref[...])], axis=0)
    y = _dot(g_ref[...], q.astype(BF16)).reshape(z_ref.shape)
    o_ref[...] = x2_ref[...] * (y + z_ref[...] * bias_ref[...])


def _hyena_long(z, x2, kr, ki, bias, tabs, batch, seq_len, n1, n2):
    fwd_a, inv_a, fwd_b, inv_b = tabs
    ch = z.shape[1]
    zv = z.reshape(batch, n1 // 2, n2, ch)
    are, aim = _dft_stage_a(zv, fwd_a, n1)
    are, aim = are.reshape(batch, n1 * n2, ch), aim.reshape(batch, n1 * n2, ch)
    kb = _tile(n1, DFT_BLOCKS)
    mat = pl.BlockSpec((kb, 2 * n2, 2 * n2), lambda i, b: (i, 0, 0))
    dat = pl.BlockSpec((None, kb * n2, ch), lambda i, b: (b, i, 0))
    spec = pl.BlockSpec((kb * n2, ch), lambda i, b: (i, 0))
    out = jax.ShapeDtypeStruct((batch, n1 * n2, ch), F32)
    qr, qi = pl.pallas_call(
        _dft_mul_kernel,
        grid=(n1 // kb, batch),
        in_specs=[mat, mat, dat, dat, spec, spec],
        out_specs=[dat, dat],
        out_shape=[out, out],
        compiler_params=_cp("parallel", "parallel"),
        name="dft_filter_mul",
    )(fwd_b, inv_b, are, aim, kr, ki)
    qr, qi = qr.reshape(batch, n1, n2, ch), qi.reshape(batch, n1, n2, ch)
    full = pl.BlockSpec((None, n1, SUBLANE, ch), lambda b, j: (b, 0, j, 0))
    half = pl.BlockSpec((None, n1 // 2, SUBLANE, ch), lambda b, j: (b, 0, j, 0))
    o = pl.pallas_call(
        _dft_inv_kernel,
        grid=(batch, n2 // SUBLANE),
        in_specs=[pl.BlockSpec(inv_a.shape, lambda b, j: (0, 0)), full, full, half, half,
                  pl.BlockSpec((1, ch), lambda b, j: (0, 0))],
        out_specs=half,
        out_shape=jax.ShapeDtypeStruct((batch, n1 // 2, n2, ch), F32),
        compiler_params=_cp("parallel", "parallel"),
        name="dft_inverse_gate",
    )(inv_a, qr, qi, zv, x2.reshape(batch, n1 // 2, n2, ch), bias.reshape(1, ch))
    return o.reshape(batch * seq_len, ch)


def _hyena_short_kernel(f_ref, g_ref, k_ref, ss_ref, z_ref, x2_ref, bias_ref, o_ref):
    f = f_ref[...]
    z = z_ref[...]
    seq = z.shape[0]
    ks = _dot(f, k_ref[...].astype(BF16)) * lax.rsqrt(ss_ref[...] + EPS)
    zs = _dot(f[:, :seq], z.astype(BF16))
    n = ks.shape[0] // 2
    kr, ki, zr, zi = ks[:n], ks[n:], zs[:n], zs[n:]
    prod = jnp.concatenate([zr * kr - zi * ki, zr * ki + zi * kr], axis=0)
    y = _dot(g_ref[...], prod.astype(BF16))
    o_ref[...] = x2_ref[...] * (y + z * bias_ref[...])


def _hyena_short(z, x2, kern, sumsq, bias, batch, seq_len):
    n = 2 * seq_len
    ch = z.shape[1]
    a = jnp.arange(n)
    ang = ((a[:, None] * a[None, :]) % n).astype(F32) * (2.0 * math.pi / n)
    c, s = jnp.cos(ang), jnp.sin(ang)
    f = jnp.concatenate([c, -s], axis=0).astype(BF16)
    g = (jnp.concatenate([c, -s], axis=1)[:seq_len] / n).astype(BF16)
    full = lambda shape: pl.BlockSpec(shape, lambda b: (0, 0))
    dat = pl.BlockSpec((seq_len, ch), lambda b: (b, 0))
    return pl.pallas_call(
        _hyena_short_kernel,
        grid=(batch,),
        in_specs=[full((2 * n, n)), full((seq_len, 2 * n)),
                  full((n, ch)), full((1, ch)), dat, dat, full((1, ch))],
        out_specs=dat,
        out_shape=jax.ShapeDtypeStruct((batch * seq_len, ch), F32),
        compiler_params=_cp("parallel"),
        name="hyena_short_conv",
    )(f, g, kern, sumsq, z, x2, bias.reshape(1, ch))


def _hyena_mixer(p, col0, width, batch, seq_len, conv_w, conv_b, w1, b1, w2, b2, w3, freq, bias):
    z, x2 = _hyena_pre(p, col0, width, seq_len, conv_w, conv_b)
    kern, sumsq = _hyena_filter(seq_len, w1, b1, w2, b2, w3, freq)
    n = 2 * seq_len
    if n <= 4 * DFT_RADIX:
        return _hyena_short(z, x2, kern, sumsq, bias, batch, seq_len)
    n2 = DFT_RADIX
    n1 = n // n2
    tabs = _dft_tables(n1, n2)
    kr, ki = _filter_spectrum(kern, sumsq, tabs, n1, n2)
    return _hyena_long(z, x2, kr, ki, bias, tabs, batch, seq_len, n1, n2)


def _outproj_kernel(x_ref, a_ref, b_ref, c_ref, d_ref, w_ref, g_ref, o_ref):
    acc = None
    row = 0
    for m_ref in (a_ref, b_ref, c_ref, d_ref):
        wdt = m_ref.shape[1]
        part = _dot(m_ref[...].astype(BF16), w_ref[row:row + wdt, :])
        acc = part if acc is None else acc + part
        row += wdt
    o_ref[...] = x_ref[...] + g_ref[...] * acc


def _out_projection(x2d, mixers, w, gate, rows_per_group):
    n, d = x2d.shape
    tm = _tile(rows_per_group, 512)
    tpg = rows_per_group // tm
    return pl.pallas_call(
        _outproj_kernel,
        grid=(n // tm,),
        in_specs=[pl.BlockSpec((tm, d), lambda i: (i, 0))]
        + [pl.BlockSpec((tm, m.shape[1]), lambda i: (i, 0)) for m in mixers]
        + [pl.BlockSpec(w.shape, lambda i: (0, 0)),
           pl.BlockSpec((None, 1, d), lambda i: (i // tpg, 0, 0))],
        out_specs=pl.BlockSpec((tm, d), lambda i: (i, 0)),
        out_shape=jax.ShapeDtypeStruct((n, d), F32),
        compiler_params=_cp("parallel"),
        name="out_proj",
    )(x2d, *mixers, w, gate)


def _ffn_kernel(seq_len, tile, x_ref, xp_ref, xn_ref, g_ref, sc_ref, sh_ref, gate_ref,
                wg_ref, wu_ref, cwg_ref, cwu_ref, cbg_ref, cbu_ref, wd_ref, o_ref, h_scr, acc):
    j = pl.program_id(1)

    @pl.when(j == 0)
    def _():
        ext = jnp.concatenate([xp_ref[...], x_ref[...], xn_ref[...]], axis=0)
        h = _norm_mod(ext, g_ref[...], sc_ref[...], sh_ref[...])
        tps = seq_len // tile
        outside = _outside_rows(tile, pl.program_id(0) % tps, tps)
        h_scr[...] = jnp.where(outside, 0.0, h).astype(BF16)
        acc[...] = jnp.zeros(acc.shape, F32)

    h = h_scr[...]
    gate = _conv3(_dot(h, wg_ref[...]), cwg_ref[...], cbg_ref[...], tile)
    up = _conv3(_dot(h, wu_ref[...]), cwu_ref[...], cbu_ref[...], tile)
    act = (gate * jax.nn.sigmoid(gate)) * up
    acc[...] += _dot(act.astype(BF16), wd_ref[...])

    @pl.when(j == pl.num_programs(1) - 1)
    def _():
        o_ref[...] = x_ref[...] + gate_ref[...] * acc[...]


def _conv_ffn(x2d, g, sc, sh, gate, w_up, conv_w, conv_b, w_down, seq_len, rows_per_group):
    n, d = x2d.shape
    f = w_down.shape[0]
    tile = _tile(seq_len, 1024)
    hc = _tile(f, 512)
    nh = f // hc
    once = pl.Buffered(1)
    tpg = rows_per_group // tile
    r = tile // HALO
    last = n // HALO - 1
    vec = pl.BlockSpec((None, 1, d), lambda i, j: (i // tpg, 0, 0))
    cb = conv_b.reshape(1, 2 * f)
    return pl.pallas_call(
        functools.partial(_ffn_kernel, seq_len, tile),
        grid=(n // tile, nh),
        in_specs=[
            pl.BlockSpec((tile, d), lambda i, j: (i, 0), pipeline_mode=once),
            pl.BlockSpec((HALO, d), lambda i, j: (jnp.maximum(i * r - 1, 0), 0)),
            pl.BlockSpec((HALO, d), lambda i, j: (jnp.minimum(i * r + r, last), 0)),
            pl.BlockSpec((1, d), lambda i, j: (0, 0)),
            vec, vec, vec,
            pl.BlockSpec((d, hc), lambda i, j: (0, j)),
            pl.BlockSpec((d, hc), lambda i, j: (0, nh + j)),
            pl.BlockSpec((3, hc), lambda i, j: (0, j)),
            pl.BlockSpec((3, hc), lambda i, j: (0, nh + j)),
            pl.BlockSpec((1, hc), lambda i, j: (0, j)),
            pl.BlockSpec((1, hc), lambda i, j: (0, nh + j)),
            pl.BlockSpec((hc, d), lambda i, j: (j, 0)),
        ],
        out_specs=pl.BlockSpec((tile, d), lambda i, j: (i, 0), pipeline_mode=once),
        out_shape=jax.ShapeDtypeStruct((n, d), F32),
        scratch_shapes=[pltpu.VMEM((tile + 2 * HALO, d), BF16), pltpu.VMEM((tile, d), F32)],
        compiler_params=_cp("parallel", "arbitrary"),
        name="conv_ffn",
    )(x2d, x2d, x2d, g, sc, sh, gate, w_up, w_up, conv_w, conv_w, cb, cb, w_down)


def kernel(x, c, ctx, c_ctx, w_mod, b_mod, norm1_g, norm2_g, w_in, w_out, qk_gain, diff_lam, diff_subln,
           win_sink, pool_w, pool_scale, hy_conv_w, hy_conv_b, hy_w1, hy_b1, hy_w2, hy_b2, hy_w3, hy_freq,
           hy_bias, ffn_w_in, ffn_conv_w, ffn_conv_b, ffn_w_out):
    batch, seq, d = x.shape
    clen = ctx.shape[1]
    depth = w_in.shape[0]
    a_heads = d // 512
    a_width = a_heads * 2 * HEAD_DIM
    b_heads = d // 256
    b_kv = b_heads // 4
    b_width = b_heads * HEAD_DIM
    b_kvw = b_kv * HEAD_DIM
    c_width = d // 4
    d_width = d - a_width - b_width - c_width
    off_ka = a_width
    off_va = 2 * a_width
    off_qb = 3 * a_width
    off_kb = off_qb + b_width
    off_vb = off_kb + b_kvw
    off_pool = off_vb + b_kvw
    off_hy = off_pool + c_width
    in_cols = off_hy + 3 * d_width
    order = ((off_hy, in_cols), (off_pool, off_hy), (off_va, off_qb), (0, off_ka), (off_ka, off_va),
             (off_qb, off_kb), (off_kb, off_vb), (off_vb, off_pool))
    n_hy, n_pool, n_va, n_qa, n_ka, n_qb, n_kb, n_vb = (
        sum(e - s for s, e in order[:k]) for k in range(len(order)))
    assert a_width % LANE == 0 and b_width % LANE == 0 and b_kvw == LANE and c_width // 4 == LANE
    assert seq % BLOCK == 0 and clen % BLOCK == 0 and n_pool % c_width == 0
    assert (n_ka - n_qa) == a_width and (n_qb - n_qa) % b_width == 0 and n_kb - n_qb == b_width

    rows = -(-(batch + 1) // SUBLANE) * SUBLANE
    cvec = jnp.zeros((rows, d), F32).at[:batch].set(c).at[batch].set(c_ctx)
    mods = _modulation(cvec, w_mod, b_mod)


    cos_t, sin_t = _rope_tables(seq)
    scale = HEAD_DIM ** -0.5
    tile2 = lambda v: jnp.tile(v, LANE // HEAD_DIM)

    xs = x.reshape(batch * seq, d)
    cs = ctx.reshape(batch * clen, d)
    for l in range(depth):
        last = l == depth - 1
        m = mods[l]
        sh1, sc1, g1, sh2, sc2, g2 = [m[:batch, i * d:(i + 1) * d].reshape(batch, 1, d) for i in range(6)]
        csh1, csc1, cg1, csh2, csc2, cg2 = [m[batch, i * d:(i + 1) * d].reshape(1, 1, d) for i in range(6)]
        n1g = norm1_g[l].reshape(1, d)
        n2g = norm2_g[l].reshape(1, d)
        w_in_l = jnp.concatenate([w_in[l, :, s:e] for s, e in order], axis=-1).astype(BF16)
        w_out_l = w_out[l].astype(BF16)
        ffn_in_l = ffn_w_in[l].astype(BF16)
        ffn_out_l = ffn_w_out[l].astype(BF16)
        pool_w_l = pool_w[l].astype(BF16)

        p = _in_projection(xs, n1g, sc1, sh1, w_in_l, seq)
        pc = _in_projection(cs, n1g, csc1, csh1, w_in_l, batch * clen)

        na = a_width // LANE
        nbq = b_width // LANE
        gains = jnp.stack([tile2(qk_gain[l, 0]) * (scale * LOG2E)] * na + [tile2(qk_gain[l, 1])] * na
                          + [tile2(qk_gain[l, 2]) * (scale * LOG2E)] * nbq + [tile2(qk_gain[l, 3])])
        qk = _qk_prep(p, n_qa // LANE, gains, cos_t, sin_t, seq, True)
        qk_c = _qk_prep(pc, n_qa // LANE, gains, cos_t, sin_t, clen, False)
        kb_blk = 2 * na + nbq
        qb_blk = (n_qb - n_qa) // b_width

        lam_p = diff_lam[l].astype(F32)
        lambda_init = 0.8 - 0.6 * math.exp(-0.3 * l)
        lam = jnp.exp(jnp.sum(lam_p[0] * lam_p[1])) - jnp.exp(jnp.sum(lam_p[2] * lam_p[3])) + lambda_init
        lam_v = jnp.full((1, LANE), lam, F32)
        subg = diff_subln[l].reshape(1, LANE)
        sink_tab = jnp.broadcast_to(win_sink[l].astype(F32)[:, None], (b_heads, LANE))
        hy = (hy_conv_w[l], hy_conv_b[l], hy_w1[l], hy_b1[l], hy_w2[l], hy_b2[l], hy_w3[l], hy_freq[l],
              hy_bias[l])

        vt = _transposed_values(p, n_va // LANE, a_heads, seq, _tile(seq, 512))
        vt_c = _transposed_values(pc, n_va // LANE, a_heads, clen, clen)
        o_a = _diff_attention(qk, seq, qk_c, vt_c, clen, qk, vt, seq, lam_v, subg, 1.0 - lambda_init,
                              a_heads, na)
        vtb = _transposed_values(p, n_vb // LANE, 1, seq, BLOCK)
        vtb_c = _transposed_values(pc, n_vb // LANE, 1, clen, clen)
        o_b = _window_attention(qk, seq, qk_c, vtb_c, clen, vtb, sink_tab, b_heads, b_kv,
                                qb_blk, kb_blk, True)
        o_c = _pool_mixer(p, n_pool, c_width, seq, pool_w_l, pool_scale[l].reshape(1, c_width))
        o_d = _hyena_mixer(p, n_hy, 3 * d_width, batch, seq, *hy)
        xs = _out_projection(xs, (o_a, o_b, o_c, o_d), w_out_l, g1, seq)
        xs = _conv_ffn(xs, n2g, sc2, sh2, g2, ffn_in_l, ffn_conv_w[l], ffn_conv_b[l], ffn_out_l,
                       seq, seq)

        if not last:
            o_a_c = _diff_attention(qk_c, clen, qk_c, vt_c, clen, None, None, 0, lam_v, subg,
                                    1.0 - lambda_init, a_heads, na)
            o_b_c = _window_attention(qk_c, clen, qk_c, vtb_c, clen, None, sink_tab, b_heads, b_kv,
                                      qb_blk, kb_blk, False)
            o_c_c = _pool_mixer(pc, n_pool, c_width, clen, pool_w_l, pool_scale[l].reshape(1, c_width))
            o_d_c = _hyena_mixer(pc, n_hy, 3 * d_width, batch, clen, *hy)
            cs = _out_projection(cs, (o_a_c, o_b_c, o_c_c, o_d_c), w_out_l, cg1, batch * clen)
            cs = _conv_ffn(cs, n2g, csc2, csh2, cg2, ffn_in_l, ffn_conv_w[l], ffn_conv_b[l],
                           ffn_out_l, clen, batch * clen)
    return xs.reshape(batch, seq, d)
```

```python
import functools
import math

import jax
import jax.numpy as jnp
from jax import lax
from jax.experimental import pallas as pl
from jax.experimental.pallas import tpu as pltpu

F32 = jnp.float32
BF16 = jnp.bfloat16

HEAD_DIM = 64
GRID_W = 64
ROPE_BASE = 10000.0
EPS = 1e-6
NEG_INF = -1e30
WINDOW = 128
BLOCK = 128
POOL_WINDOWS = (2, 4, 8, 16)
HY_EMB = 33
HY_HIDDEN = 64
HY_FAST_DECAY = 0.3
HY_SLOW_DECAY = 1.5
HY_TARGET = 1e-2

LANE = 128
SUBLANE = 8
HALO = SUBLANE
DFT_RADIX = 128
DFT_BLOCKS = 8
VT_ROWS = LANE + 16
LOG2E = 1.4426950408889634
HY_PACK = LANE // 2
VMEM_LIMIT = 56 * 1024 * 1024


def _cp(*sem):
    return pltpu.CompilerParams(dimension_semantics=sem, vmem_limit_bytes=VMEM_LIMIT)


def _tile(n, pref):
    if n <= pref:
        return n
    t = pref
    while n % t:
        t //= 2
    assert t >= SUBLANE, (n, pref)
    return t


def _dot(a, b):
    return jnp.dot(a, b, preferred_element_type=F32)


def _dot_nt(a, b):
    return lax.dot_general(a, b, (((1,), (1,)), ((), ())), preferred_element_type=F32)


def _mod_kernel(c_ref, w_ref, b_ref, o_ref):
    c = c_ref[...]
    a = (c * jax.nn.sigmoid(c)).astype(BF16)
    o_ref[...] = _dot(a, w_ref[...].astype(BF16)) + b_ref[...]


def _modulation(cvec, w_mod, b_mod):
    depth, d, n = w_mod.shape
    rows = cvec.shape[0]
    tn = _tile(n, 1024)
    return pl.pallas_call(
        _mod_kernel,
        grid=(depth, n // tn),
        in_specs=[
            pl.BlockSpec((rows, d), lambda l, j: (0, 0)),
            pl.BlockSpec((None, d, tn), lambda l, j: (l, 0, j)),
            pl.BlockSpec((None, 1, tn), lambda l, j: (l, 0, j)),
        ],
        out_specs=pl.BlockSpec((None, rows, tn), lambda l, j: (l, 0, j)),
        out_shape=jax.ShapeDtypeStruct((depth, rows, n), F32),
        compiler_params=_cp("parallel", "parallel"),
        name="adaln_mod",
    )(cvec, w_mod, b_mod.reshape(depth, 1, n))


def _norm_mod(x, g, sc, sh):
    ms = jnp.mean(x * x, axis=-1, keepdims=True)
    return (x * lax.rsqrt(ms + EPS) * g) * (1.0 + sc) + sh


def _inproj_kernel(x_ref, g_ref, sc_ref, sh_ref, w_ref, o_ref):
    h = _norm_mod(x_ref[...], g_ref[...], sc_ref[...], sh_ref[...])
    o_ref[...] = _dot(h.astype(BF16), w_ref[...])


def _in_projection(x2d, g, sc, sh, w, rows_per_group):
    n, d = x2d.shape
    nc = w.shape[1]
    tm = _tile(rows_per_group, 512)
    tpg = rows_per_group // tm
    ncol = 2 if (nc // 2) % LANE == 0 else 1
    tn = nc // ncol
    vec = pl.BlockSpec((None, 1, d), lambda j, i: (i // tpg, 0, 0))
    return pl.pallas_call(
        _inproj_kernel,
        grid=(ncol, n // tm),
        in_specs=[
            pl.BlockSpec((tm, d), lambda j, i: (i, 0)),
            pl.BlockSpec((1, d), lambda j, i: (0, 0)),
            vec, vec,
            pl.BlockSpec((d, tn), lambda j, i: (0, j)),
        ],
        out_specs=pl.BlockSpec((tm, tn), lambda j, i: (i, j)),
        out_shape=jax.ShapeDtypeStruct((n, nc), F32),
        compiler_params=_cp("parallel", "parallel"),
        name="in_proj",
    )(x2d, g, sc, sh, w)


def _qkprep_kernel(rope, x_ref, gain_ref, cos_ref, sin_ref, gm_ref, pm_ref, o_ref):
    x = x_ref[...]
    ms = _dot((x * x).astype(BF16), gm_ref[...])
    y = x * lax.rsqrt(ms + EPS) * gain_ref[...]
    if rope:
        cos = jnp.concatenate([cos_ref[...]] * 2, axis=1)
        sin = jnp.concatenate([sin_ref[...]] * 2, axis=1)
        y = y * cos + _dot(y.astype(BF16), pm_ref[...]) * sin
    o_ref[...] = y.astype(BF16)


def _qk_prep(p, col_block0, gains, cos_t, sin_t, seq_len, rope):
    n = p.shape[0]
    nblk = gains.shape[0]
    wide = 2 * LANE
    nw = -(-nblk // 2)
    assert col_block0 % 2 == 0 and (col_block0 + 2 * nw) * LANE <= p.shape[1]
    gains = jnp.concatenate([gains, jnp.ones((2 * nw - nblk, LANE), F32)], axis=0).reshape(nw, 1, wide)
    tm = _tile(seq_len, 2048)
    tps = seq_len // tm
    lane = jnp.arange(wide)
    gm = ((lane[:, None] // HEAD_DIM) == (lane[None, :] // HEAD_DIM)).astype(BF16) * (1.0 / HEAD_DIM)
    partner = jnp.where((lane % 32) < 16, lane + 16, lane - 16)
    pm = (lane[:, None] == partner[None, :]).astype(BF16)
    mat = pl.BlockSpec((wide, wide), lambda i, j: (0, 0))
    tab = pl.BlockSpec((tm, LANE), lambda i, j: (i % tps, 0))
    return pl.pallas_call(
        functools.partial(_qkprep_kernel, rope),
        grid=(n // tm, nw),
        in_specs=[
            pl.BlockSpec((tm, wide), lambda i, j: (i, col_block0 // 2 + j)),
            pl.BlockSpec((None, 1, wide), lambda i, j: (j, 0, 0)),
            tab, tab, mat, mat,
        ],
        out_specs=pl.BlockSpec((tm, wide), lambda i, j: (i, j)),
        out_shape=jax.ShapeDtypeStruct((n, nw * wide), BF16),
        compiler_params=_cp("parallel", "parallel"),
        name="qk_prep",
    )(p, gains, cos_t, sin_t, gm.astype(BF16), pm)


def _rope_tables(seq_len):
    rows = seq_len // GRID_W
    row = jnp.repeat(jnp.arange(rows, dtype=F32), GRID_W)
    col = jnp.tile(jnp.arange(GRID_W, dtype=F32), rows)
    half = HEAD_DIM // 2
    inv = ROPE_BASE ** (-jnp.arange(0, half, 2, dtype=F32) / half)
    ang = jnp.concatenate([row[:, None] * inv, col[:, None] * inv], axis=-1)
    lane = jnp.arange(LANE)
    within = lane % HEAD_DIM
    idx = (within // 32) * 16 + (within % 16)
    sign = jnp.where((lane % 32) < 16, -1.0, 1.0).astype(F32)
    return jnp.cos(ang)[:, idx], jnp.sin(ang)[:, idx] * sign


def _vt_kernel(x_ref, o_ref):
    chunks, _, chunk = o_ref.shape
    row = lax.broadcasted_iota(jnp.int32, (VT_ROWS - LANE, chunk), 0)
    ones_row = jnp.where(row == 0, 1.0, 0.0).astype(BF16)
    for c in range(chunks):
        x_t = x_ref[c * chunk:(c + 1) * chunk, :].T.astype(BF16)
        o_ref[c] = jnp.concatenate([x_t, ones_row], axis=0)


def _transposed_values(p, col_block0, nblk, seq_len, chunk):
    n = p.shape[0]
    rows = max(chunk, _tile(seq_len, 512))
    return pl.pallas_call(
        _vt_kernel,
        grid=(n // rows, nblk),
        in_specs=[pl.BlockSpec((rows, LANE), lambda i, j: (i, col_block0 + j))],
        out_specs=pl.BlockSpec((None, rows // chunk, VT_ROWS, chunk), lambda i, j: (j, i, 0, 0)),
        out_shape=jax.ShapeDtypeStruct((nblk, n // chunk, VT_ROWS, chunk), BF16),
        compiler_params=_cp("parallel", "parallel"),
        name="values_t",
    )(p)


def _attn_a_kernel(n_lat, tk, sub_scale, *refs):
    if n_lat:
        (q_ref, kc_ref, vc_ref, k_ref, v_ref, lam_ref, subg_ref, o_ref, qt, m, acc,
         s0, s1, x0, x1, p0, p1, al0, al1) = refs
    else:
        q_ref, kc_ref, vc_ref, lam_ref, subg_ref, o_ref, qt, m, acc = refs
    tq = q_ref.shape[0]
    q_t = q_ref[...].astype(F32).T
    row = lax.broadcasted_iota(jnp.int32, q_t.shape, 0)
    qt[:, :tq] = jnp.where(row < HEAD_DIM, q_t, 0.0).astype(BF16)
    qt[:, tq:] = jnp.where(row >= HEAD_DIM, q_t, 0.0).astype(BF16)
    m[...] = jnp.full(m.shape, -jnp.inf, F32)
    acc[...] = jnp.zeros(acc.shape, F32)

    def softmax_update(s, s_max):
        m_prev = m[...]
        m_new = jnp.maximum(m_prev, s_max)
        m[...] = m_new
        return jnp.exp2(s - m_new).astype(BF16), jnp.exp2(m_prev - m_new)

    s_c = _dot(kc_ref[...], qt[...])

    def context_update():
        p_c, alpha_c = softmax_update(s_c, jnp.max(s_c, axis=0, keepdims=True))
        acc[...] = alpha_c * acc[...] + _dot(vc_ref[0], p_c)

    if not n_lat:
        context_update()
    else:
        def scores(c, s_buf, x_buf):
            s = _dot(k_ref[pl.ds(pl.multiple_of(c * tk, tk), tk), :], qt[...])
            s_buf[...] = s
            x_buf[...] = jnp.max(s, axis=0, keepdims=True)

        def softmax(s_buf, x_buf, p_buf, al_buf):
            p_buf[...], al_buf[...] = softmax_update(s_buf[...], x_buf[...])

        def values(c, p_buf, al_buf):
            acc[...] = al_buf[...] * acc[...] + _dot(v_ref[c], p_buf[...])

        scores(0, s0, x0)
        if n_lat == 1:
            context_update()
            softmax(s0, x0, p0, al0)
            values(0, p0, al0)
        else:
            assert n_lat % 2 == 0
            scores(1, s1, x1)
            context_update()
            softmax(s0, x0, p0, al0)

            def body(t, carry):
                c = 2 * t
                scores(c + 2, s0, x0)
                softmax(s1, x1, p1, al1)
                values(c, p0, al0)
                scores(c + 3, s1, x1)
                softmax(s0, x0, p0, al0)
                values(c + 1, p1, al1)
                return carry
            lax.fori_loop(0, n_lat // 2 - 1, body, 0)
            softmax(s1, x1, p1, al1)
            values(n_lat - 2, p0, al0)
            values(n_lat - 1, p1, al1)

    o_t = acc[:LANE] / acc[LANE:LANE + 1]
    o = (o_t[:, :tq] - lam_ref[0:1, 0:1] * o_t[:, tq:]).T
    ms = jnp.mean(o * o, axis=-1, keepdims=True)
    o_ref[...] = ((o * lax.rsqrt(ms + EPS) * subg_ref[...]) * sub_scale).astype(BF16)


def _diff_attention(q_arr, q_len, qk_ctx, vt_ctx, ctx_len, qk_lat, vt_lat, lat_len, lam, subg,
                    sub_scale, heads, k_col0):
    batch = q_arr.shape[0] // q_len
    tq = _tile(q_len, 1024)
    nq = q_len // tq
    in_specs = [
        pl.BlockSpec((tq, LANE), lambda b, h, i: (b * nq + i, h)),
        pl.BlockSpec((ctx_len, LANE), lambda b, h, i: (b, k_col0 + h)),
        pl.BlockSpec((None, 1, VT_ROWS, ctx_len), lambda b, h, i: (h, b, 0, 0)),
    ]
    args = [q_arr, qk_ctx, vt_ctx]
    n_lat, tk = 0, 0
    if qk_lat is not None:
        tk = vt_lat.shape[-1]
        n_lat = lat_len // tk
        in_specs += [
            pl.BlockSpec((lat_len, LANE), lambda b, h, i: (b, k_col0 + h)),
            pl.BlockSpec((None, n_lat, VT_ROWS, tk), lambda b, h, i: (h, b, 0, 0)),
        ]
        args += [qk_lat, vt_lat]
    in_specs += [pl.BlockSpec((1, LANE), lambda b, h, i: (0, 0))] * 2
    args += [lam, subg]
    stat = pltpu.VMEM((1, 2 * tq), F32)
    scratch = [pltpu.VMEM((LANE, 2 * tq), BF16), stat, pltpu.VMEM((VT_ROWS, 2 * tq), F32)]
    if n_lat:
        scratch += [pltpu.VMEM((tk, 2 * tq), F32)] * 2 + [stat] * 2
        scratch += [pltpu.VMEM((tk, 2 * tq), BF16)] * 2 + [stat] * 2
    return pl.pallas_call(
        functools.partial(_attn_a_kernel, n_lat, tk, sub_scale),
        grid=(batch, heads, nq),
        in_specs=in_specs,
        out_specs=pl.BlockSpec((tq, LANE), lambda b, h, i: (b * nq + i, h)),
        out_shape=jax.ShapeDtypeStruct((batch * q_len, heads * LANE), BF16),
        scratch_shapes=scratch,
        compiler_params=_cp("parallel", "parallel", "parallel"),
        name="diff_attn",
    )(*args)


def _attn_b_kernel(has_loc, seq_len, q_heads, group, *refs):
    if has_loc:
        (q_ref, kx_ref, vx_ref, kp_ref, kc_ref, kn_ref, vp_ref, vc_ref, vn_ref, sink_ref, o_ref) = refs
    else:
        (q_ref, kx_ref, vx_ref, sink_ref, o_ref) = refs
    n = pl.program_id(1)
    clen = kx_ref.shape[0]
    q_t = q_ref[...].astype(F32).T
    keys = [kx_ref[...]]
    vals = [vx_ref[0]]
    if has_loc:
        keys += [kp_ref[...], kc_ref[...], kn_ref[...]]
        vals += [vp_ref[0], vc_ref[0], vn_ref[0]]
        kpos = (n - 1) * BLOCK + lax.broadcasted_iota(jnp.int32, (3 * BLOCK, BLOCK), 0)
        qpos = n * BLOCK + lax.broadcasted_iota(jnp.int32, (3 * BLOCK, BLOCK), 1)
        valid = (kpos >= 0) & (kpos < seq_len) & (jnp.abs(kpos - qpos) <= WINDOW)
        valid = jnp.concatenate([valid] * group, axis=1)
    k_all = jnp.concatenate(keys, axis=0)
    v_all = jnp.concatenate(vals, axis=1)
    zeros = jnp.zeros((HEAD_DIM, BLOCK), F32)
    heads_out = []
    for kvh in range(q_heads // group):
        cols = []
        for g in range(group):
            hq = kvh * group + g
            blk = q_t[hq * HEAD_DIM:(hq + 1) * HEAD_DIM]
            cols.append(jnp.concatenate([blk, zeros] if kvh == 0 else [zeros, blk], axis=0))
        q_kv = jnp.concatenate(cols, axis=1).astype(BF16)
        sink = jnp.concatenate([sink_ref[kvh * group + g:kvh * group + g + 1, :] for g in range(group)],
                               axis=1) * LOG2E
        s = _dot(k_all, q_kv)
        if has_loc:
            s = jnp.concatenate([s[:clen], jnp.where(valid, s[clen:], NEG_INF)], axis=0)
        m = jnp.maximum(jnp.max(s, axis=0, keepdims=True), sink)
        o_t = _dot(v_all, jnp.exp2(s - m).astype(BF16))
        den = o_t[LANE:LANE + 1] + jnp.exp2(sink - m)
        o_t = o_t[kvh * HEAD_DIM:(kvh + 1) * HEAD_DIM] / den
        heads_out += [o_t[:, g * BLOCK:(g + 1) * BLOCK] for g in range(group)]
    o_ref[...] = jnp.concatenate(heads_out, axis=0).T.astype(BF16)


def _window_attention(q_arr, q_len, qk_ctx, vt_ctx, ctx_len, vt_lat, sink_tab, q_heads, kv_heads,
                      q_blk, k_blk, local):
    assert kv_heads == 2, "key/value heads must fill exactly one 128-lane block"
    batch = q_arr.shape[0] // q_len
    nb = q_len // BLOCK
    qw = q_heads * HEAD_DIM
    in_specs = [
        pl.BlockSpec((BLOCK, qw), lambda b, n: (b * nb + n, q_blk)),
        pl.BlockSpec((ctx_len, LANE), lambda b, n: (b, k_blk)),
        pl.BlockSpec((None, 1, VT_ROWS, ctx_len), lambda b, n: (0, b, 0, 0)),
    ]
    args = [q_arr, qk_ctx, vt_ctx]
    if local:
        prev = lambda b, n: b * nb + jnp.maximum(n - 1, 0)
        nxt = lambda b, n: b * nb + jnp.minimum(n + 1, nb - 1)
        in_specs += [
            pl.BlockSpec((BLOCK, LANE), lambda b, n: (prev(b, n), k_blk)),
            pl.BlockSpec((BLOCK, LANE), lambda b, n: (b * nb + n, k_blk)),
            pl.BlockSpec((BLOCK, LANE), lambda b, n: (nxt(b, n), k_blk)),
            pl.BlockSpec((None, 1, VT_ROWS, BLOCK), lambda b, n: (0, prev(b, n), 0, 0)),
            pl.BlockSpec((None, 1, VT_ROWS, BLOCK), lambda b, n: (0, b * nb + n, 0, 0)),
            pl.BlockSpec((None, 1, VT_ROWS, BLOCK), lambda b, n: (0, nxt(b, n), 0, 0)),
        ]
        args += [q_arr] * 3 + [vt_lat] * 3
    in_specs.append(pl.BlockSpec((q_heads, LANE), lambda b, n: (0, 0)))
    args.append(sink_tab)
    return pl.pallas_call(
        functools.partial(_attn_b_kernel, local, q_len, q_heads, q_heads // kv_heads),
        grid=(batch, nb),
        in_specs=in_specs,
        out_specs=pl.BlockSpec((BLOCK, qw), lambda b, n: (b * nb + n, 0)),
        out_shape=jax.ShapeDtypeStruct((batch * q_len, qw), BF16),
        compiler_params=_cp("parallel", "parallel"),
        name="window_attn",
    )(*args)


def _halo_specs(tile, width, col_blk, n_rows):
    r = tile // HALO
    last = n_rows // HALO - 1
    return [
        pl.BlockSpec((tile, width), lambda i: (i, col_blk)),
        pl.BlockSpec((HALO, width), lambda i: (jnp.maximum(i * r - 1, 0), col_blk)),
        pl.BlockSpec((HALO, width), lambda i: (jnp.minimum(i * r + r, last), col_blk)),
    ]


def _outside_rows(tile, tile_idx, tiles_per_seq):
    row = lax.broadcasted_iota(jnp.int32, (tile + 2 * HALO, 1), 0)
    lo = jnp.where(tile_idx == 0, HALO, 0)
    hi = jnp.where(tile_idx == tiles_per_seq - 1, HALO + tile, tile + 2 * HALO)
    return (row < lo) | (row >= hi)


def _with_halo(x, xp, xn, tiles_per_seq):
    ext = jnp.concatenate([xp, x, xn], axis=0)
    outside = _outside_rows(x.shape[0], pl.program_id(0) % tiles_per_seq, tiles_per_seq)
    return jnp.where(outside, 0.0, ext)


def _conv3(ext, w, b, tile):
    rows = ext.shape[0]
    c = (pltpu.roll(ext, 1, 0) * w[0:1] + ext * w[1:2] + pltpu.roll(ext, rows - 1, 0) * w[2:3])
    return c[HALO:HALO + tile] + b


def _pool_kernel(seq_len, tile, x_ref, xp_ref, xn_ref, w_ref, ls_ref, o_ref):
    tps = seq_len // tile
    x = x_ref[...]
    ext = _with_halo(x, xp_ref[...], xn_ref[...], tps)
    rows = tile + 2 * HALO
    t = (pl.program_id(0) % tps) * tile + lax.broadcasted_iota(jnp.int32, (tile, 1), 0)
    gw = x.shape[1] // len(POOL_WINDOWS)
    outs = []
    for g, w in enumerate(POOL_WINDOWS):
        lo = w // 2
        hi = w - 1 - lo
        a = ext[:, g * gw:(g + 1) * gw]
        span = 1
        while span < w:
            a = a + pltpu.roll(a, span, 0)
            span *= 2
        win = (pltpu.roll(a, rows - hi, 0) if hi else a)[HALO:HALO + tile]
        cnt = jnp.minimum(t + hi + 1, seq_len) - jnp.maximum(t - lo, 0)
        d = win / cnt.astype(F32) - x[:, g * gw:(g + 1) * gw]
        outs.append(_dot(d.astype(BF16), w_ref[g]))
    o_ref[...] = (jnp.concatenate(outs, axis=1) * ls_ref[...]).astype(BF16)


def _pool_mixer(p, col0, width, seq_len, w_lin, ls):
    n = p.shape[0]
    tile = _tile(seq_len, 512)
    g, gw, _ = w_lin.shape
    return pl.pallas_call(
        functools.partial(_pool_kernel, seq_len, tile),
        grid=(n // tile,),
        in_specs=_halo_specs(tile, width, col0 // width, n)
        + [pl.BlockSpec((g, gw, gw), lambda i: (0, 0, 0)),
           pl.BlockSpec((1, width), lambda i: (0, 0))],
        out_specs=pl.BlockSpec((tile, width), lambda i: (i, 0)),
        out_shape=jax.ShapeDtypeStruct((n, width), BF16),
        compiler_params=_cp("parallel"),
        name="pool_mixer",
    )(p, p, p, w_lin, ls)


def _hypre_kernel(seq_len, tile, x_ref, xp_ref, xn_ref, cw_ref, cb_ref, z_ref, x2_ref):
    ext = _with_halo(x_ref[...], xp_ref[...], xn_ref[...], seq_len // tile)
    c = _conv3(ext, cw_ref[...], cb_ref[...], tile)
    dw = c.shape[1] // 3
    z_ref[...] = c[:, :dw] * c[:, dw:2 * dw]
    x2_ref[...] = c[:, 2 * dw:]


def _hyena_pre(p, col0, width, seq_len, conv_w, conv_b):
    n = p.shape[0]
    tile = _tile(seq_len, 512)
    dw = width // 3
    out = jax.ShapeDtypeStruct((n, dw), F32)
    return pl.pallas_call(
        functools.partial(_hypre_kernel, seq_len, tile),
        grid=(n // tile,),
        in_specs=_halo_specs(tile, width, col0 // width, n)
        + [pl.BlockSpec((3, width), lambda i: (0, 0)),
           pl.BlockSpec((1, width), lambda i: (0, 0))],
        out_specs=[pl.BlockSpec((tile, dw), lambda i: (i, 0))] * 2,
        out_shape=[out, out],
        compiler_params=_cp("parallel"),
        name="hyena_pre",
    )(p, p, p, conv_w, conv_b.reshape(1, width))


def _hyfilt_kernel(seq_len, tile, zf_ref, w1_ref, b1_ref, w2_ref, b2_ref, w3_ref, fr_ref, dl_ref,
                   k_ref, ss_ref):
    i = pl.program_id(0)
    half = tile // 2
    hp = functools.partial(jnp.dot, precision=lax.Precision.HIGHEST, preferred_element_type=F32)
    zf = zf_ref[...]
    fr = fr_ref[...]
    h = jnp.sin(fr * (hp(zf, w1_ref[...]) + b1_ref[...]))
    h = jnp.sin(fr * (hp(h, w2_ref[...]) + b2_ref[...]))
    w3 = w3_ref[...]
    dl = dl_ref[...]
    ss = jnp.zeros(ss_ref.shape, F32)
    for part in range(2):
        k = hp(h, w3[part]) * jnp.exp(-zf[:, part * HY_PACK:part * HY_PACK + 1] * dl)
        row = i * tile + part * half + lax.broadcasted_iota(jnp.int32, (half, 1), 0)
        k = jnp.where(row == seq_len, 0.0, k)
        k_ref[part * half:(part + 1) * half, :] = k
        ss = ss + jnp.sum(k * k, axis=0, keepdims=True)

    @pl.when(i == 0)
    def _():
        ss_ref[...] = jnp.zeros(ss_ref.shape, F32)

    ss_ref[...] += ss


def _hyena_filter(seq_len, w1, b1, w2, b2, w3, freq):
    dw = w3.shape[1] // 2
    bands = (HY_EMB - 1) // 2
    t01 = jnp.linspace(0.0, 1.0, seq_len, dtype=F32)[:, None]
    w_ang = 2.0 * math.pi * jnp.arange(seq_len, dtype=F32)[:, None] / seq_len
    f = jnp.linspace(1e-4, bands - 1, bands, dtype=F32)[None, :]
    z = jnp.concatenate([t01, jnp.cos(f * w_ang), -jnp.sin(f * w_ang)], axis=-1)
    rev = (seq_len - jnp.arange(seq_len)) % seq_len
    assert HY_HIDDEN == HY_PACK and HY_EMB <= HY_PACK
    tile = _tile(seq_len, 1024)
    tps = seq_len // tile
    half = tile // 2
    zf = jnp.pad(jnp.concatenate([z, z[rev]], axis=0), ((0, 0), (0, HY_PACK - HY_EMB)))
    zf = zf.reshape(2 * tps, 2, half, HY_PACK).transpose(0, 2, 1, 3).reshape(2 * tps * half, LANE)
    zero = jnp.zeros((HY_PACK, HY_PACK), F32)
    blockdiag = lambda w: jnp.block([[w, zero], [zero, w]])
    w1p = blockdiag(jnp.pad(w1, ((0, HY_PACK - HY_EMB), (0, 0))))
    w2p = blockdiag(w2)
    w3d = w3.reshape(HY_HIDDEN, 2, dw).transpose(1, 0, 2)
    zrows = jnp.zeros_like(w3d)
    w3p = jnp.stack([jnp.concatenate([w3d, zrows], axis=1), jnp.concatenate([zrows, w3d], axis=1)], axis=1)
    vec = lambda v: jnp.concatenate([v, v]).reshape(1, LANE)
    deltas = jnp.abs(jnp.linspace(math.log(HY_TARGET) / HY_FAST_DECAY, math.log(HY_TARGET) / HY_SLOW_DECAY,
                                  dw, dtype=F32)).reshape(1, dw)
    full = lambda shape: pl.BlockSpec(shape, lambda i: (0, 0))
    return pl.pallas_call(
        functools.partial(_hyfilt_kernel, seq_len, tile),
        grid=(2 * tps,),
        in_specs=[pl.BlockSpec((half, LANE), lambda i: (i, 0)),
                  full((LANE, LANE)), full((1, LANE)), full((LANE, LANE)), full((1, LANE)),
                  pl.BlockSpec((None, 2, LANE, dw), lambda i: (i // tps, 0, 0, 0)),
                  full((1, LANE)), full((1, dw))],
        out_specs=[pl.BlockSpec((tile, dw), lambda i: (i, 0)), full((1, dw))],
        out_shape=[jax.ShapeDtypeStruct((2 * seq_len, dw), F32), jax.ShapeDtypeStruct((1, dw), F32)],
        compiler_params=_cp("arbitrary"),
        name="hyena_filter",
    )(zf, w1p, vec(b1), w2p, vec(b2), w3p, vec(freq), deltas)


def _dft_tables(n1, n2):
    n = n1 * n2
    two_pi = 2.0 * math.pi

    def cs(num, den):
        ang = (num % den).astype(F32) * (two_pi / den)
        return jnp.cos(ang), jnp.sin(ang)

    a = jnp.arange(n1)
    c1, s1 = cs(a[:, None] * a[None, :], n1)
    kron = lambda mat: jnp.kron(mat, jnp.eye(SUBLANE, dtype=F32))
    fwd_a = jnp.concatenate([kron(c1), kron(-s1)], axis=0)
    inv_a = jnp.concatenate([kron(c1[: n1 // 2]), kron(-s1[: n1 // 2])], axis=1) / n
    k2 = jnp.arange(n2)
    num = (k2[None, :, None] * k2[None, None, :] * n1 + a[:, None, None] * k2[None, None, :])
    cb, sb = cs(num, n)
    fwd_b = jnp.concatenate([jnp.concatenate([cb, sb], axis=2),
                             jnp.concatenate([-sb, cb], axis=2)], axis=1)
    cbt, sbt = jnp.swapaxes(cb, 1, 2), jnp.swapaxes(sb, 1, 2)
    inv_b = jnp.concatenate([jnp.concatenate([cbt, -sbt], axis=2),
                             jnp.concatenate([sbt, cbt], axis=2)], axis=1)
    return tuple(m.astype(BF16) for m in (fwd_a, inv_a, fwd_b, inv_b))


def _rows(x):
    a, g, c = x.shape
    return x.reshape(a * g, c)


def _dft_a_kernel(f_ref, x_ref, re_ref, im_ref):
    y = _dot(f_ref[...], _rows(x_ref[...]).astype(BF16))
    half = y.shape[0] // 2
    re_ref[...] = y[:half].reshape(re_ref.shape)
    im_ref[...] = y[half:].reshape(im_ref.shape)


def _dft_stage_a(xv, fwd_a, n1):
    b, k, n2, ch = xv.shape
    groups = n2 // SUBLANE
    out = jax.ShapeDtypeStruct((b, n1, n2, ch), F32)
    return pl.pallas_call(
        _dft_a_kernel,
        grid=(b, groups),
        in_specs=[pl.BlockSpec((2 * n1 * SUBLANE, k * SUBLANE), lambda bi, j: (0, 0)),
                  pl.BlockSpec((None, k, SUBLANE, ch), lambda bi, j: (bi, 0, j, 0))],
        out_specs=[pl.BlockSpec((None, n1, SUBLANE, ch), lambda bi, j: (bi, 0, j, 0))] * 2,
        out_shape=[out, out],
        compiler_params=_cp("parallel", "parallel"),
        name="dft_stage_a",
    )(fwd_a[:, :k * SUBLANE], xv)


def _dft_b_kernel(m_ref, re_ref, im_ref, ss_ref, kr_ref, ki_ref):
    nrm = lax.rsqrt(ss_ref[...] + EPS)
    n2 = m_ref.shape[1] // 2
    for k in range(m_ref.shape[0]):
        rows = slice(k * n2, (k + 1) * n2)
        x = _dot(m_ref[k], jnp.concatenate([re_ref[rows], im_ref[rows]], axis=0).astype(BF16))
        kr_ref[rows] = x[:n2] * nrm
        ki_ref[rows] = x[n2:] * nrm


def _filter_spectrum(kern, sumsq, tabs, n1, n2):
    fwd_a, _, fwd_b, _ = tabs
    ch = kern.shape[1]
    are, aim = _dft_stage_a(kern.reshape(1, n1, n2, ch), fwd_a, n1)
    are, aim = are.reshape(n1 * n2, ch), aim.reshape(n1 * n2, ch)
    kb = _tile(n1, DFT_BLOCKS)
    blk = pl.BlockSpec((kb * n2, ch), lambda i: (i, 0))
    out = jax.ShapeDtypeStruct((n1 * n2, ch), F32)
    return pl.pallas_call(
        _dft_b_kernel,
        grid=(n1 // kb,),
        in_specs=[pl.BlockSpec((kb, 2 * n2, 2 * n2), lambda i: (i, 0, 0)), blk, blk,
                  pl.BlockSpec((1, ch), lambda i: (0, 0))],
        out_specs=[blk, blk],
        out_shape=[out, out],
        compiler_params=_cp("parallel"),
        name="filter_spectrum",
    )(fwd_b, are, aim, sumsq)


def _dft_mul_kernel(f_ref, g_ref, re_ref, im_ref, kr_ref, ki_ref, qr_ref, qi_ref):
    n2 = f_ref.shape[1] // 2
    for k in range(f_ref.shape[0]):
        rows = slice(k * n2, (k + 1) * n2)
        x = _dot(f_ref[k], jnp.concatenate([re_ref[rows], im_ref[rows]], axis=0).astype(BF16))
        xr, xi = x[:n2], x[n2:]
        kr, ki = kr_ref[rows], ki_ref[rows]
        prod = jnp.concatenate([xr * kr - xi * ki, xr * ki + xi * kr], axis=0)
        q = _dot(g_ref[k], prod.astype(BF16))
        qr_ref[rows] = q[:n2]
        qi_ref[rows] = q[n2:]


def _dft_inv_kernel(g_ref, qr_ref, qi_ref, z_ref, x2_ref, bias_ref, o_ref):
    q = jnp.concatenate([_rows(qr_ref[...]), _rows(qi_ref[...])], axis=0)
    y = _dot(g_ref[...], q.astype(BF16)).reshape(z_ref.shape)
    o_ref[...] = x2_ref[...] * (y + z_ref[...] * bias_ref[...])


def _hyena_long(z, x2, kr, ki, bias, tabs, batch, seq_len, n1, n2):
    fwd_a, inv_a, fwd_b, inv_b = tabs
    ch = z.shape[1]
    zv = z.reshape(batch, n1 // 2, n2, ch)
    are, aim = _dft_stage_a(zv, fwd_a, n1)
    are, aim = are.reshape(batch, n1 * n2, ch), aim.reshape(batch, n1 * n2, ch)
    kb = _tile(n1, DFT_BLOCKS)
    mat = pl.BlockSpec((kb, 2 * n2, 2 * n2), lambda i, b: (i, 0, 0))
    dat = pl.BlockSpec((None, kb * n2, ch), lambda i, b: (b, i, 0))
    spec = pl.BlockSpec((kb * n2, ch), lambda i, b: (i, 0))
    out = jax.ShapeDtypeStruct((batch, n1 * n2, ch), F32)
    qr, qi = pl.pallas_call(
        _dft_mul_kernel,
        grid=(n1 // kb, batch),
        in_specs=[mat, mat, dat, dat, spec, spec],
        out_specs=[dat, dat],
        out_shape=[out, out],
        compiler_params=_cp("parallel", "parallel"),
        name="dft_filter_mul",
    )(fwd_b, inv_b, are, aim, kr, ki)
    qr, qi = qr.reshape(batch, n1, n2, ch), qi.reshape(batch, n1, n2, ch)
    full = pl.BlockSpec((None, n1, SUBLANE, ch), lambda b, j: (b, 0, j, 0))
    half = pl.BlockSpec((None, n1 // 2, SUBLANE, ch), lambda b, j: (b, 0, j, 0))
    o = pl.pallas_call(
        _dft_inv_kernel,
        grid=(batch, n2 // SUBLANE),
        in_specs=[pl.BlockSpec(inv_a.shape, lambda b, j: (0, 0)), full, full, half, half,
                  pl.BlockSpec((1, ch), lambda b, j: (0, 0))],
        out_specs=half,
        out_shape=jax.ShapeDtypeStruct((batch, n1 // 2, n2, ch), F32),
        compiler_params=_cp("parallel", "parallel"),
        name="dft_inverse_gate",
    )(inv_a, qr, qi, zv, x2.reshape(batch, n1 // 2, n2, ch), bias.reshape(1, ch))
    return o.reshape(batch * seq_len, ch)


def _hyena_short_kernel(f_ref, g_ref, k_ref, ss_ref, z_ref, x2_ref, bias_ref, o_ref):
    f = f_ref[...]
    z = z_ref[...]
    seq = z.shape[0]
    ks = _dot(f, k_ref[...].astype(BF16)) * lax.rsqrt(ss_ref[...] + EPS)
    zs = _dot(f[:, :seq], z.astype(BF16))
    n = ks.shape[0] // 2
    kr, ki, zr, zi = ks[:n], ks[n:], zs[:n], zs[n:]
    prod = jnp.concatenate([zr * kr - zi * ki, zr * ki + zi * kr], axis=0)
    y = _dot(g_ref[...], prod.astype(BF16))
    o_ref[...] = x2_ref[...] * (y + z * bias_ref[...])


def _hyena_short(z, x2, kern, sumsq, bias, batch, seq_len):
    n = 2 * seq_len
    ch = z.shape[1]
    a = jnp.arange(n)
    ang = ((a[:, None] * a[None, :]) % n).astype(F32) * (2.0 * math.pi / n)
    c, s = jnp.cos(ang), jnp.sin(ang)
    f = jnp.concatenate([c, -s], axis=0).astype(BF16)
    g = (jnp.concatenate([c, -s], axis=1)[:seq_len] / n).astype(BF16)
    full = lambda shape: pl.BlockSpec(shape, lambda b: (0, 0))
    dat = pl.BlockSpec((seq_len, ch), lambda b: (b, 0))
    return pl.pallas_call(
        _hyena_short_kernel,
        grid=(batch,),
        in_specs=[full((2 * n, n)), full((seq_len, 2 * n)),
                  full((n, ch)), full((1, ch)), dat, dat, full((1, ch))],
        out_specs=dat,
        out_shape=jax.ShapeDtypeStruct((batch * seq_len, ch), F32),
        compiler_params=_cp("parallel"),
        name="hyena_short_conv",
    )(f, g, kern, sumsq, z, x2, bias.reshape(1, ch))


def _hyena_mixer(p, col0, width, batch, seq_len, conv_w, conv_b, w1, b1, w2, b2, w3, freq, bias):
    z, x2 = _hyena_pre(p, col0, width, seq_len, conv_w, conv_b)
    kern, sumsq = _hyena_filter(seq_len, w1, b1, w2, b2, w3, freq)
    n = 2 * seq_len
    if n <= 4 * DFT_RADIX:
        return _hyena_short(z, x2, kern, sumsq, bias, batch, seq_len)
    n2 = DFT_RADIX
    n1 = n // n2
    tabs = _dft_tables(n1, n2)
    kr, ki = _filter_spectrum(kern, sumsq, tabs, n1, n2)
    return _hyena_long(z, x2, kr, ki, bias, tabs, batch, seq_len, n1, n2)


def _outproj_kernel(x_ref, a_ref, b_ref, c_ref, d_ref, w_ref, g_ref, o_ref):
    acc = None
    row = 0
    for m_ref in (a_ref, b_ref, c_ref, d_ref):
        wdt = m_ref.shape[1]
        part = _dot(m_ref[...].astype(BF16), w_ref[row:row + wdt, :])
        acc = part if acc is None else acc + part
        row += wdt
    o_ref[...] = x_ref[...] + g_ref[...] * acc


def _out_projection(x2d, mixers, w, gate, rows_per_group):
    n, d = x2d.shape
    tm = _tile(rows_per_group, 512)
    tpg = rows_per_group // tm
    return pl.pallas_call(
        _outproj_kernel,
        grid=(n // tm,),
        in_specs=[pl.BlockSpec((tm, d), lambda i: (i, 0))]
        + [pl.BlockSpec((tm, m.shape[1]), lambda i: (i, 0)) for m in mixers]
        + [pl.BlockSpec(w.shape, lambda i: (0, 0)),
           pl.BlockSpec((None, 1, d), lambda i: (i // tpg, 0, 0))],
        out_specs=pl.BlockSpec((tm, d), lambda i: (i, 0)),
        out_shape=jax.ShapeDtypeStruct((n, d), F32),
        compiler_params=_cp("parallel"),
        name="out_proj",
    )(x2d, *mixers, w, gate)


def _ffn_kernel(seq_len, tile, x_ref, xp_ref, xn_ref, g_ref, sc_ref, sh_ref, gate_ref,
                wg_ref, wu_ref, cwg_ref, cwu_ref, cbg_ref, cbu_ref, wd_ref, o_ref, h_scr, acc):
    j = pl.program_id(1)

    @pl.when(j == 0)
    def _():
        ext = jnp.concatenate([xp_ref[...], x_ref[...], xn_ref[...]], axis=0)
        h = _norm_mod(ext, g_ref[...], sc_ref[...], sh_ref[...])
        tps = seq_len // tile
        outside = _outside_rows(tile, pl.program_id(0) % tps, tps)
        h_scr[...] = jnp.where(outside, 0.0, h).astype(BF16)
        acc[...] = jnp.zeros(acc.shape, F32)

    h = h_scr[...]
    gate = _conv3(_dot(h, wg_ref[...]), cwg_ref[...], cbg_ref[...], tile)
    up = _conv3(_dot(h, wu_ref[...]), cwu_ref[...], cbu_ref[...], tile)
    act = (gate * jax.nn.sigmoid(gate)) * up
    acc[...] += _dot(act.astype(BF16), wd_ref[...])

    @pl.when(j == pl.num_programs(1) - 1)
    def _():
        o_ref[...] = x_ref[...] + gate_ref[...] * acc[...]


def _conv_ffn(x2d, g, sc, sh, gate, w_up, conv_w, conv_b, w_down, seq_len, rows_per_group):
    n, d = x2d.shape
    f = w_down.shape[0]
    tile = _tile(seq_len, 1024)
    hc = _tile(f, 512)
    nh = f // hc
    once = pl.Buffered(1)
    tpg = rows_per_group // tile
    r = tile // HALO
    last = n // HALO - 1
    vec = pl.BlockSpec((None, 1, d), lambda i, j: (i // tpg, 0, 0))
    cb = conv_b.reshape(1, 2 * f)
    return pl.pallas_call(
        functools.partial(_ffn_kernel, seq_len, tile),
        grid=(n // tile, nh),
        in_specs=[
            pl.BlockSpec((tile, d), lambda i, j: (i, 0), pipeline_mode=once),
            pl.BlockSpec((HALO, d), lambda i, j: (jnp.maximum(i * r - 1, 0), 0)),
            pl.BlockSpec((HALO, d), lambda i, j: (jnp.minimum(i * r + r, last), 0)),
            pl.BlockSpec((1, d), lambda i, j: (0, 0)),
            vec, vec, vec,
            pl.BlockSpec((d, hc), lambda i, j: (0, j)),
            pl.BlockSpec((d, hc), lambda i, j: (0, nh + j)),
            pl.BlockSpec((3, hc), lambda i, j: (0, j)),
            pl.BlockSpec((3, hc), lambda i, j: (0, nh + j)),
            pl.BlockSpec((1, hc), lambda i, j: (0, j)),
            pl.BlockSpec((1, hc), lambda i, j: (0, nh + j)),
            pl.BlockSpec((hc, d), lambda i, j: (j, 0)),
        ],
        out_specs=pl.BlockSpec((tile, d), lambda i, j: (i, 0), pipeline_mode=once),
        out_shape=jax.ShapeDtypeStruct((n, d), F32),
        scratch_shapes=[pltpu.VMEM((tile + 2 * HALO, d), BF16), pltpu.VMEM((tile, d), F32)],
        compiler_params=_cp("parallel", "arbitrary"),
        name="conv_ffn",
    )(x2d, x2d, x2d, g, sc, sh, gate, w_up, w_up, conv_w, conv_w, cb, cb, w_down)


def kernel(x, c, ctx, c_ctx, w_mod, b_mod, norm1_g, norm2_g, w_in, w_out, qk_gain, diff_lam, diff_subln,
           win_sink, pool_w, pool_scale, hy_conv_w, hy_conv_b, hy_w1, hy_b1, hy_w2, hy_b2, hy_w3, hy_freq,
           hy_bias, ffn_w_in, ffn_conv_w, ffn_conv_b, ffn_w_out):
    batch, seq, d = x.shape
    clen = ctx.shape[1]
    depth = w_in.shape[0]
    a_heads = d // 512
    a_width = a_heads * 2 * HEAD_DIM
    b_heads = d // 256
    b_kv = b_heads // 4
    b_width = b_heads * HEAD_DIM
    b_kvw = b_kv * HEAD_DIM
    c_width = d // 4
    d_width = d - a_width - b_width - c_width
    off_ka = a_width
    off_va = 2 * a_width
    off_qb = 3 * a_width
    off_kb = off_qb + b_width
    off_vb = off_kb + b_kvw
    off_pool = off_vb + b_kvw
    off_hy = off_pool + c_width
    in_cols = off_hy + 3 * d_width
    order = ((off_hy, in_cols), (off_pool, off_hy), (off_va, off_qb), (0, off_ka), (off_ka, off_va),
             (off_qb, off_kb), (off_kb, off_vb), (off_vb, off_pool))
    n_hy, n_pool, n_va, n_qa, n_ka, n_qb, n_kb, n_vb = (
        sum(e - s for s, e in order[:k]) for k in range(len(order)))
    assert a_width % LANE == 0 and b_width % LANE == 0 and b_kvw == LANE and c_width // 4 == LANE
    assert seq % BLOCK == 0 and clen % BLOCK == 0 and n_pool % c_width == 0
    assert (n_ka - n_qa) == a_width and (n_qb - n_qa) % b_width == 0 and n_kb - n_qb == b_width

    rows = -(-(batch + 1) // SUBLANE) * SUBLANE
    cvec = jnp.zeros((rows, d), F32).at[:batch].set(c).at[batch].set(c_ctx)
    mods = _modulation(cvec, w_mod, b_mod)


    cos_t, sin_t = _rope_tables(seq)
    scale = HEAD_DIM ** -0.5
    tile2 = lambda v: jnp.tile(v, LANE // HEAD_DIM)

    xs = x.reshape(batch * seq, d)
    cs = ctx.reshape(batch * clen, d)
    for l in range(depth):
        last = l == depth - 1
        m = mods[l]
        sh1, sc1, g1, sh2, sc2, g2 = [m[:batch, i * d:(i + 1) * d].reshape(batch, 1, d) for i in range(6)]
        csh1, csc1, cg1, csh2, csc2, cg2 = [m[batch, i * d:(i + 1) * d].reshape(1, 1, d) for i in range(6)]
        n1g = norm1_g[l].reshape(1, d)
        n2g = norm2_g[l].reshape(1, d)
        w_in_l = jnp.concatenate([w_in[l, :, s:e] for s, e in order], axis=-1).astype(BF16)
        w_out_l = w_out[l].astype(BF16)
        ffn_in_l = ffn_w_in[l].astype(BF16)
        ffn_out_l = ffn_w_out[l].astype(BF16)
        pool_w_l = pool_w[l].astype(BF16)

        p = _in_projection(xs, n1g, sc1, sh1, w_in_l, seq)
        pc = _in_projection(cs, n1g, csc1, csh1, w_in_l, batch * clen)

        na = a_width // LANE
        nbq = b_width // LANE
        gains = jnp.stack([tile2(qk_gain[l, 0]) * (scale * LOG2E)] * na + [tile2(qk_gain[l, 1])] * na
                          + [tile2(qk_gain[l, 2]) * (scale * LOG2E)] * nbq + [tile2(qk_gain[l, 3])])
        qk = _qk_prep(p, n_qa // LANE, gains, cos_t, sin_t, seq, True)
        qk_c = _qk_prep(pc, n_qa // LANE, gains, cos_t, sin_t, clen, False)
        kb_blk = 2 * na + nbq
        qb_blk = (n_qb - n_qa) // b_width

        lam_p = diff_lam[l].astype(F32)
        lambda_init = 0.8 - 0.6 * math.exp(-0.3 * l)
        lam = jnp.exp(jnp.sum(lam_p[0] * lam_p[1])) - jnp.exp(jnp.sum(lam_p[2] * lam_p[3])) + lambda_init
        lam_v = jnp.full((1, LANE), lam, F32)
        subg = diff_subln[l].reshape(1, LANE)
        sink_tab = jnp.broadcast_to(win_sink[l].astype(F32)[:, None], (b_heads, LANE))
        hy = (hy_conv_w[l], hy_conv_b[l], hy_w1[l], hy_b1[l], hy_w2[l], hy_b2[l], hy_w3[l], hy_freq[l],
              hy_bias[l])

        vt = _transposed_values(p, n_va // LANE, a_heads, seq, _tile(seq, 512))
        vt_c = _transposed_values(pc, n_va // LANE, a_heads, clen, clen)
        o_a = _diff_attention(qk, seq, qk_c, vt_c, clen, qk, vt, seq, lam_v, subg, 1.0 - lambda_init,
                              a_heads, na)
        vtb = _transposed_values(p, n_vb // LANE, 1, seq, BLOCK)
        vtb_c = _transposed_values(pc, n_vb // LANE, 1, clen, clen)
        o_b = _window_attention(qk, seq, qk_c, vtb_c, clen, vtb, sink_tab, b_heads, b_kv,
                                qb_blk, kb_blk, True)
        o_c = _pool_mixer(p, n_pool, c_width, seq, pool_w_l, pool_scale[l].reshape(1, c_width))
        o_d = _hyena_mixer(p, n_hy, 3 * d_width, batch, seq, *hy)
        xs = _out_projection(xs, (o_a, o_b, o_c, o_d), w_out_l, g1, seq)
        xs = _conv_ffn(xs, n2g, sc2, sh2, g2, ffn_in_l, ffn_conv_w[l], ffn_conv_b[l], ffn_out_l,
                       seq, seq)

        if not last:
            o_a_c = _diff_attention(qk_c, clen, qk_c, vt_c, clen, None, None, 0, lam_v, subg,
                                    1.0 - lambda_init, a_heads, na)
            o_b_c = _window_attention(qk_c, clen, qk_c, vtb_c, clen, None, sink_tab, b_heads, b_kv,
                                      qb_blk, kb_blk, False)
            o_c_c = _pool_mixer(pc, n_pool, c_width, clen, pool_w_l, pool_scale[l].reshape(1, c_width))
            o_d_c = _hyena_mixer(pc, n_hy, 3 * d_width, batch, clen, *hy)
            cs = _out_projection(cs, (o_a_c, o_b_c, o_c_c, o_d_c), w_out_l, cg1, batch * clen)
            cs = _conv_ffn(cs, n2g, csc2, csh2, cg2, ffn_in_l, ffn_conv_w[l], ffn_conv_b[l],
                           ffn_out_l, clen, batch * clen)
    return xs.reshape(batch, seq, d)
```

```python
import functools
import math

import jax
import jax.numpy as jnp
from jax import lax
from jax.experimental import pallas as pl
from jax.experimental.pallas import tpu as pltpu

F32 = jnp.float32
BF16 = jnp.bfloat16

HEAD_DIM = 64
GRID_W = 64
ROPE_BASE = 10000.0
EPS = 1e-6
NEG_INF = -1e30
WINDOW = 128
BLOCK = 128
POOL_WINDOWS = (2, 4, 8, 16)
HY_EMB = 33
HY_HIDDEN = 64
HY_FAST_DECAY = 0.3
HY_SLOW_DECAY = 1.5
HY_TARGET = 1e-2

LANE = 128
SUBLANE = 8
HALO = SUBLANE
DFT_RADIX = 128
DFT_BLOCKS = 8
VT_ROWS = LANE + 16
LOG2E = 1.4426950408889634
HY_PACK = LANE // 2
VMEM_LIMIT = 56 * 1024 * 1024


def _cp(*sem):
    return pltpu.CompilerParams(dimension_semantics=sem, vmem_limit_bytes=VMEM_LIMIT)


def _tile(n, pref):
    if n <= pref:
        return n
    t = pref
    while n % t:
        t //= 2
    assert t >= SUBLANE, (n, pref)
    return t


def _dot(a, b):
    return jnp.dot(a, b, preferred_element_type=F32)


def _dot_nt(a, b):
    return lax.dot_general(a, b, (((1,), (1,)), ((), ())), preferred_element_type=F32)


def _mod_kernel(c_ref, w_ref, b_ref, o_ref):
    c = c_ref[...]
    a = (c * jax.nn.sigmoid(c)).astype(BF16)
    o_ref[...] = _dot(a, w_ref[...].astype(BF16)) + b_ref[...]


def _modulation(cvec, w_mod, b_mod):
    depth, d, n = w_mod.shape
    rows = cvec.shape[0]
    tn = _tile(n, 1024)
    return pl.pallas_call(
        _mod_kernel,
        grid=(depth, n // tn),
        in_specs=[
            pl.BlockSpec((rows, d), lambda l, j: (0, 0)),
            pl.BlockSpec((None, d, tn), lambda l, j: (l, 0, j)),
            pl.BlockSpec((None, 1, tn), lambda l, j: (l, 0, j)),
        ],
        out_specs=pl.BlockSpec((None, rows, tn), lambda l, j: (l, 0, j)),
        out_shape=jax.ShapeDtypeStruct((depth, rows, n), F32),
        compiler_params=_cp("parallel", "parallel"),
        name="adaln_mod",
    )(cvec, w_mod, b_mod.reshape(depth, 1, n))


def _norm_mod(x, g, sc, sh):
    ms = jnp.mean(x * x, axis=-1, keepdims=True)
    return (x * lax.rsqrt(ms + EPS) * g) * (1.0 + sc) + sh


def _inproj_kernel(x_ref, g_ref, sc_ref, sh_ref, w_ref, o_ref):
    h = _norm_mod(x_ref[...], g_ref[...], sc_ref[...], sh_ref[...])
    o_ref[...] = _dot(h.astype(BF16), w_ref[...])


def _in_projection(x2d, g, sc, sh, w, rows_per_group):
    n, d = x2d.shape
    nc = w.shape[1]
    tm = _tile(rows_per_group, 512)
    tpg = rows_per_group // tm
    ncol = 2 if (nc // 2) % LANE == 0 else 1
    tn = nc // ncol
    vec = pl.BlockSpec((None, 1, d), lambda j, i: (i // tpg, 0, 0))
    return pl.pallas_call(
        _inproj_kernel,
        grid=(ncol, n // tm),
        in_specs=[
            pl.BlockSpec((tm, d), lambda j, i: (i, 0)),
            pl.BlockSpec((1, d), lambda j, i: (0, 0)),
            vec, vec,
            pl.BlockSpec((d, tn), lambda j, i: (0, j)),
        ],
        out_specs=pl.BlockSpec((tm, tn), lambda j, i: (i, j)),
        out_shape=jax.ShapeDtypeStruct((n, nc), F32),
        compiler_params=_cp("parallel", "parallel"),
        name="in_proj",
    )(x2d, g, sc, sh, w)


def _qkprep_kernel(rope, x_ref, gain_ref, cos_ref, sin_ref, gm_ref, pm_ref, o_ref):
    x = x_ref[...]
    ms = _dot((x * x).astype(BF16), gm_ref[...])
    y = x * lax.rsqrt(ms + EPS) * gain_ref[...]
    if rope:
        cos = jnp.concatenate([cos_ref[...]] * 2, axis=1)
        sin = jnp.concatenate([sin_ref[...]] * 2, axis=1)
        y = y * cos + _dot(y.astype(BF16), pm_ref[...]) * sin
    o_ref[...] = y.astype(BF16)


def _qk_prep(p, col_block0, gains, cos_t, sin_t, seq_len, rope):
    n = p.shape[0]
    nblk = gains.shape[0]
    wide = 2 * LANE
    nw = -(-nblk // 2)
    assert col_block0 % 2 == 0 and (col_block0 + 2 * nw) * LANE <= p.shape[1]
    gains = jnp.concatenate([gains, jnp.ones((2 * nw - nblk, LANE), F32)], axis=0).reshape(nw, 1, wide)
    tm = _tile(seq_len, 2048)
    tps = seq_len // tm
    lane = jnp.arange(wide)
    gm = ((lane[:, None] // HEAD_DIM) == (lane[None, :] // HEAD_DIM)).astype(BF16) * (1.0 / HEAD_DIM)
    partner = jnp.where((lane % 32) < 16, lane + 16, lane - 16)
    pm = (lane[:, None] == partner[None, :]).astype(BF16)
    mat = pl.BlockSpec((wide, wide), lambda i, j: (0, 0))
    tab = pl.BlockSpec((tm, LANE), lambda i, j: (i % tps, 0))
    return pl.pallas_call(
        functools.partial(_qkprep_kernel, rope),
        grid=(n // tm, nw),
        in_specs=[
            pl.BlockSpec((tm, wide), lambda i, j: (i, col_block0 // 2 + j)),
            pl.BlockSpec((None, 1, wide), lambda i, j: (j, 0, 0)),
            tab, tab, mat, mat,
        ],
        out_specs=pl.BlockSpec((tm, wide), lambda i, j: (i, j)),
        out_shape=jax.ShapeDtypeStruct((n, nw * wide), BF16),
        compiler_params=_cp("parallel", "parallel"),
        name="qk_prep",
    )(p, gains, cos_t, sin_t, gm.astype(BF16), pm)


def _rope_tables(seq_len):
    rows = seq_len // GRID_W
    row = jnp.repeat(jnp.arange(rows, dtype=F32), GRID_W)
    col = jnp.tile(jnp.arange(GRID_W, dtype=F32), rows)
    half = HEAD_DIM // 2
    inv = ROPE_BASE ** (-jnp.arange(0, half, 2, dtype=F32) / half)
    ang = jnp.concatenate([row[:, None] * inv, col[:, None] * inv], axis=-1)
    lane = jnp.arange(LANE)
    within = lane % HEAD_DIM
    idx = (within // 32) * 16 + (within % 16)
    sign = jnp.where((lane % 32) < 16, -1.0, 1.0).astype(F32)
    return jnp.cos(ang)[:, idx], jnp.sin(ang)[:, idx] * sign


def _vt_kernel(x_ref, o_ref):
    chunks, _, chunk = o_ref.shape
    row = lax.broadcasted_iota(jnp.int32, (VT_ROWS - LANE, chunk), 0)
    ones_row = jnp.where(row == 0, 1.0, 0.0).astype(BF16)
    for c in range(chunks):
        x_t = x_ref[c * chunk:(c + 1) * chunk, :].T.astype(BF16)
        o_ref[c] = jnp.concatenate([x_t, ones_row], axis=0)


def _transposed_values(p, col_block0, nblk, seq_len, chunk):
    n = p.shape[0]
    rows = max(chunk, _tile(seq_len, 2048))
    return pl.pallas_call(
        _vt_kernel,
        grid=(n // rows, nblk),
        in_specs=[pl.BlockSpec((rows, LANE), lambda i, j: (i, col_block0 + j))],
        out_specs=pl.BlockSpec((None, rows // chunk, VT_ROWS, chunk), lambda i, j: (j, i, 0, 0)),
        out_shape=jax.ShapeDtypeStruct((nblk, n // chunk, VT_ROWS, chunk), BF16),
        compiler_params=_cp("parallel", "parallel"),
        name="values_t",
    )(p)


def _attn_a_kernel(n_lat, tk, sub_scale, *refs):
    if n_lat:
        (q_ref, kc_ref, vc_ref, k_ref, v_ref, lam_ref, subg_ref, o_ref, qt, m, acc,
         s0, s1, x0, x1, p0, p1, al0, al1) = refs
    else:
        q_ref, kc_ref, vc_ref, lam_ref, subg_ref, o_ref, qt, m, acc = refs
    tq = q_ref.shape[0]
    q_t = q_ref[...].astype(F32).T
    row = lax.broadcasted_iota(jnp.int32, q_t.shape, 0)
    qt[:, :tq] = jnp.where(row < HEAD_DIM, q_t, 0.0).astype(BF16)
    qt[:, tq:] = jnp.where(row >= HEAD_DIM, q_t, 0.0).astype(BF16)
    m[...] = jnp.full(m.shape, -jnp.inf, F32)
    acc[...] = jnp.zeros(acc.shape, F32)

    def softmax_update(s, s_max):
        m_prev = m[...]
        m_new = jnp.maximum(m_prev, s_max)
        m[...] = m_new
        return jnp.exp2(s - m_new).astype(BF16), jnp.exp2(m_prev - m_new)

    s_c = _dot(kc_ref[...], qt[...])

    def context_update():
        p_c, alpha_c = softmax_update(s_c, jnp.max(s_c, axis=0, keepdims=True))
        acc[...] = alpha_c * acc[...] + _dot(vc_ref[0], p_c)

    if not n_lat:
        context_update()
    else:
        def scores(c, s_buf, x_buf):
            s = _dot(k_ref[pl.ds(pl.multiple_of(c * tk, tk), tk), :], qt[...])
            s_buf[...] = s
            x_buf[...] = jnp.max(s, axis=0, keepdims=True)

        def softmax(s_buf, x_buf, p_buf, al_buf):
            p_buf[...], al_buf[...] = softmax_update(s_buf[...], x_buf[...])

        def values(c, p_buf, al_buf):
            acc[...] = al_buf[...] * acc[...] + _dot(v_ref[c], p_buf[...])

        scores(0, s0, x0)
        if n_lat == 1:
            context_update()
            softmax(s0, x0, p0, al0)
            values(0, p0, al0)
        else:
            assert n_lat % 2 == 0
            scores(1, s1, x1)
            context_update()
            softmax(s0, x0, p0, al0)

            def body(t, carry):
                c = 2 * t
                scores(c + 2, s0, x0)
                softmax(s1, x1, p1, al1)
                values(c, p0, al0)
                scores(c + 3, s1, x1)
                softmax(s0, x0, p0, al0)
                values(c + 1, p1, al1)
                return carry
            lax.fori_loop(0, n_lat // 2 - 1, body, 0)
            softmax(s1, x1, p1, al1)
            values(n_lat - 2, p0, al0)
            values(n_lat - 1, p1, al1)

    o_t = acc[:LANE] / acc[LANE:LANE + 1]
    o = (o_t[:, :tq] - lam_ref[0:1, 0:1] * o_t[:, tq:]).T
    ms = jnp.mean(o * o, axis=-1, keepdims=True)
    o_ref[...] = ((o * lax.rsqrt(ms + EPS) * subg_ref[...]) * sub_scale).astype(BF16)


def _diff_attention(q_arr, q_len, qk_ctx, vt_ctx, ctx_len, qk_lat, vt_lat, lat_len, lam, subg,
                    sub_scale, heads, k_col0):
    batch = q_arr.shape[0] // q_len
    tq = _tile(q_len, 1024)
    nq = q_len // tq
    in_specs = [
        pl.BlockSpec((tq, LANE), lambda b, h, i: (b * nq + i, h)),
        pl.BlockSpec((ctx_len, LANE), lambda b, h, i: (b, k_col0 + h)),
        pl.BlockSpec((None, 1, VT_ROWS, ctx_len), lambda b, h, i: (h, b, 0, 0)),
    ]
    args = [q_arr, qk_ctx, vt_ctx]
    n_lat, tk = 0, 0
    if qk_lat is not None:
        tk = vt_lat.shape[-1]
        n_lat = lat_len // tk
        in_specs += [
            pl.BlockSpec((lat_len, LANE), lambda b, h, i: (b, k_col0 + h)),
            pl.BlockSpec((None, n_lat, VT_ROWS, tk), lambda b, h, i: (h, b, 0, 0)),
        ]
        args += [qk_lat, vt_lat]
    in_specs += [pl.BlockSpec((1, LANE), lambda b, h, i: (0, 0))] * 2
    args += [lam, subg]
    stat = pltpu.VMEM((1, 2 * tq), F32)
    scratch = [pltpu.VMEM((LANE, 2 * tq), BF16), stat, pltpu.VMEM((VT_ROWS, 2 * tq), F32)]
    if n_lat:
        scratch += [pltpu.VMEM((tk, 2 * tq), F32)] * 2 + [stat] * 2
        scratch += [pltpu.VMEM((tk, 2 * tq), BF16)] * 2 + [stat] * 2
    return pl.pallas_call(
        functools.partial(_attn_a_kernel, n_lat, tk, sub_scale),
        grid=(batch, heads, nq),
        in_specs=in_specs,
        out_specs=pl.BlockSpec((tq, LANE), lambda b, h, i: (b * nq + i, h)),
        out_shape=jax.ShapeDtypeStruct((batch * q_len, heads * LANE), BF16),
        scratch_shapes=scratch,
        compiler_params=_cp("parallel", "parallel", "parallel"),
        name="diff_attn",
    )(*args)


def _attn_b_kernel(has_loc, seq_len, q_heads, group, *refs):
    if has_loc:
        (q_ref, kx_ref, vx_ref, kp_ref, kc_ref, kn_ref, vp_ref, vc_ref, vn_ref, sink_ref, o_ref) = refs
    else:
        (q_ref, kx_ref, vx_ref, sink_ref, o_ref) = refs
    n = pl.program_id(1)
    clen = kx_ref.shape[0]
    q_t = q_ref[...].astype(F32).T
    keys = [kx_ref[...]]
    vals = [vx_ref[0]]
    if has_loc:
        keys += [kp_ref[...], kc_ref[...], kn_ref[...]]
        vals += [vp_ref[0], vc_ref[0], vn_ref[0]]
        kpos = (n - 1) * BLOCK + lax.broadcasted_iota(jnp.int32, (3 * BLOCK, BLOCK), 0)
        qpos = n * BLOCK + lax.broadcasted_iota(jnp.int32, (3 * BLOCK, BLOCK), 1)
        valid = (kpos >= 0) & (kpos < seq_len) & (jnp.abs(kpos - qpos) <= WINDOW)
        valid = jnp.concatenate([valid] * q_heads, axis=1)
    k_all = jnp.concatenate(keys, axis=0)
    v_all = jnp.concatenate(vals, axis=1)
    zeros = jnp.zeros((HEAD_DIM, BLOCK), F32)
    cols = []
    for hq in range(q_heads):
        blk = q_t[hq * HEAD_DIM:(hq + 1) * HEAD_DIM]
        cols.append(jnp.concatenate([blk, zeros] if hq // group == 0 else [zeros, blk], axis=0))
    q_all = jnp.concatenate(cols, axis=1).astype(BF16)
    sink = jnp.concatenate([sink_ref[hq:hq + 1, :] for hq in range(q_heads)], axis=1) * LOG2E
    s = _dot(k_all, q_all)
    if has_loc:
        s = jnp.concatenate([s[:clen], jnp.where(valid, s[clen:], NEG_INF)], axis=0)
    m = jnp.maximum(jnp.max(s, axis=0, keepdims=True), sink)
    o_t = _dot(v_all, jnp.exp2(s - m).astype(BF16))
    den = o_t[LANE:LANE + 1] + jnp.exp2(sink - m)
    heads_out = []
    for hq in range(q_heads):
        kvh = hq // group
        lanes = slice(hq * BLOCK, (hq + 1) * BLOCK)
        heads_out.append(o_t[kvh * HEAD_DIM:(kvh + 1) * HEAD_DIM, lanes] / den[:, lanes])
    o_ref[...] = jnp.concatenate(heads_out, axis=0).T.astype(BF16)


def _window_attention(q_arr, q_len, qk_ctx, vt_ctx, ctx_len, vt_lat, sink_tab, q_heads, kv_heads,
                      q_blk, k_blk, local):
    assert kv_heads == 2, "key/value heads must fill exactly one 128-lane block"
    batch = q_arr.shape[0] // q_len
    nb = q_len // BLOCK
    qw = q_heads * HEAD_DIM
    in_specs = [
        pl.BlockSpec((BLOCK, qw), lambda b, n: (b * nb + n, q_blk)),
        pl.BlockSpec((ctx_len, LANE), lambda b, n: (b, k_blk)),
        pl.BlockSpec((None, 1, VT_ROWS, ctx_len), lambda b, n: (0, b, 0, 0)),
    ]
    args = [q_arr, qk_ctx, vt_ctx]
    if local:
        prev = lambda b, n: b * nb + jnp.maximum(n - 1, 0)
        nxt = lambda b, n: b * nb + jnp.minimum(n + 1, nb - 1)
        in_specs += [
            pl.BlockSpec((BLOCK, LANE), lambda b, n: (prev(b, n), k_blk)),
            pl.BlockSpec((BLOCK, LANE), lambda b, n: (b * nb + n, k_blk)),
            pl.BlockSpec((BLOCK, LANE), lambda b, n: (nxt(b, n), k_blk)),
            pl.BlockSpec((None, 1, VT_ROWS, BLOCK), lambda b, n: (0, prev(b, n), 0, 0)),
            pl.BlockSpec((None, 1, VT_ROWS, BLOCK), lambda b, n: (0, b * nb + n, 0, 0)),
            pl.BlockSpec((None, 1, VT_ROWS, BLOCK), lambda b, n: (0, nxt(b, n), 0, 0)),
        ]
        args += [q_arr] * 3 + [vt_lat] * 3
    in_specs.append(pl.BlockSpec((q_heads, LANE), lambda b, n: (0, 0)))
    args.append(sink_tab)
    return pl.pallas_call(
        functools.partial(_attn_b_kernel, local, q_len, q_heads, q_heads // kv_heads),
        grid=(batch, nb),
        in_specs=in_specs,
        out_specs=pl.BlockSpec((BLOCK, qw), lambda b, n: (b * nb + n, 0)),
        out_shape=jax.ShapeDtypeStruct((batch * q_len, qw), BF16),
        compiler_params=_cp("parallel", "parallel"),
        name="window_attn",
    )(*args)


def _halo_specs(tile, width, col_blk, n_rows):
    r = tile // HALO
    last = n_rows // HALO - 1
    return [
        pl.BlockSpec((tile, width), lambda i: (i, col_blk)),
        pl.BlockSpec((HALO, width), lambda i: (jnp.maximum(i * r - 1, 0), col_blk)),
        pl.BlockSpec((HALO, width), lambda i: (jnp.minimum(i * r + r, last), col_blk)),
    ]


def _outside_rows(tile, tile_idx, tiles_per_seq):
    row = lax.broadcasted_iota(jnp.int32, (tile + 2 * HALO, 1), 0)
    lo = jnp.where(tile_idx == 0, HALO, 0)
    hi = jnp.where(tile_idx == tiles_per_seq - 1, HALO + tile, tile + 2 * HALO)
    return (row < lo) | (row >= hi)


def _with_halo(x, xp, xn, tiles_per_seq):
    ext = jnp.concatenate([xp, x, xn], axis=0)
    outside = _outside_rows(x.shape[0], pl.program_id(0) % tiles_per_seq, tiles_per_seq)
    return jnp.where(outside, 0.0, ext)


def _conv3(ext, w, b, tile):
    rows = ext.shape[0]
    c = (pltpu.roll(ext, 1, 0) * w[0:1] + ext * w[1:2] + pltpu.roll(ext, rows - 1, 0) * w[2:3])
    return c[HALO:HALO + tile] + b


def _pool_kernel(seq_len, tile, x_ref, xp_ref, xn_ref, w_ref, ls_ref, o_ref):
    tps = seq_len // tile
    x = x_ref[...]
    ext = _with_halo(x, xp_ref[...], xn_ref[...], tps)
    rows = tile + 2 * HALO
    t = (pl.program_id(0) % tps) * tile + lax.broadcasted_iota(jnp.int32, (tile, 1), 0)
    gw = x.shape[1] // len(POOL_WINDOWS)
    outs = []
    for g, w in enumerate(POOL_WINDOWS):
        lo = w // 2
        hi = w - 1 - lo
        a = ext[:, g * gw:(g + 1) * gw]
        span = 1
        while span < w:
            a = a + pltpu.roll(a, span, 0)
            span *= 2
        win = (pltpu.roll(a, rows - hi, 0) if hi else a)[HALO:HALO + tile]
        cnt = jnp.minimum(t + hi + 1, seq_len) - jnp.maximum(t - lo, 0)
        d = win / cnt.astype(F32) - x[:, g * gw:(g + 1) * gw]
        outs.append(_dot(d.astype(BF16), w_ref[g]))
    o_ref[...] = (jnp.concatenate(outs, axis=1) * ls_ref[...]).astype(BF16)


def _pool_mixer(p, col0, width, seq_len, w_lin, ls):
    n = p.shape[0]
    tile = _tile(seq_len, 512)
    g, gw, _ = w_lin.shape
    return pl.pallas_call(
        functools.partial(_pool_kernel, seq_len, tile),
        grid=(n // tile,),
        in_specs=_halo_specs(tile, width, col0 // width, n)
        + [pl.BlockSpec((g, gw, gw), lambda i: (0, 0, 0)),
           pl.BlockSpec((1, width), lambda i: (0, 0))],
        out_specs=pl.BlockSpec((tile, width), lambda i: (i, 0)),
        out_shape=jax.ShapeDtypeStruct((n, width), BF16),
        compiler_params=_cp("parallel"),
        name="pool_mixer",
    )(p, p, p, w_lin, ls)


def _hypre_kernel(seq_len, tile, x_ref, xp_ref, xn_ref, cw_ref, cb_ref, z_ref, x2_ref):
    ext = _with_halo(x_ref[...], xp_ref[...], xn_ref[...], seq_len // tile)
    c = _conv3(ext, cw_ref[...], cb_ref[...], tile)
    dw = c.shape[1] // 3
    z_ref[...] = c[:, :dw] * c[:, dw:2 * dw]
    x2_ref[...] = c[:, 2 * dw:]


def _hyena_pre(p, col0, width, seq_len, conv_w, conv_b):
    n = p.shape[0]
    tile = _tile(seq_len, 512)
    dw = width // 3
    out = jax.ShapeDtypeStruct((n, dw), F32)
    return pl.pallas_call(
        functools.partial(_hypre_kernel, seq_len, tile),
        grid=(n // tile,),
        in_specs=_halo_specs(tile, width, col0 // width, n)
        + [pl.BlockSpec((3, width), lambda i: (0, 0)),
           pl.BlockSpec((1, width), lambda i: (0, 0))],
        out_specs=[pl.BlockSpec((tile, dw), lambda i: (i, 0))] * 2,
        out_shape=[out, out],
        compiler_params=_cp("parallel"),
        name="hyena_pre",
    )(p, p, p, conv_w, conv_b.reshape(1, width))


def _hyfilt_kernel(seq_len, tile, zf_ref, w1_ref, b1_ref, w2_ref, b2_ref, w3_ref, fr_ref, dl_ref,
                   k_ref, ss_ref):
    i = pl.program_id(0)
    half = tile // 2
    hp = functools.partial(jnp.dot, precision=lax.Precision.HIGHEST, preferred_element_type=F32)
    zf = zf_ref[...]
    fr = fr_ref[...]
    h = jnp.sin(fr * (hp(zf, w1_ref[...]) + b1_ref[...]))
    h = jnp.sin(fr * (hp(h, w2_ref[...]) + b2_ref[...]))
    w3 = w3_ref[...]
    dl = dl_ref[...]
    ss = jnp.zeros(ss_ref.shape, F32)
    for part in range(2):
        k = hp(h, w3[part]) * jnp.exp(-zf[:, part * HY_PACK:part * HY_PACK + 1] * dl)
        row = i * tile + part * half + lax.broadcasted_iota(jnp.int32, (half, 1), 0)
        k = jnp.where(row == seq_len, 0.0, k)
        k_ref[part * half:(part + 1) * half, :] = k
        ss = ss + jnp.sum(k * k, axis=0, keepdims=True)

    @pl.when(i == 0)
    def _():
        ss_ref[...] = jnp.zeros(ss_ref.shape, F32)

    ss_ref[...] += ss


def _hyena_filter(seq_len, w1, b1, w2, b2, w3, freq):
    dw = w3.shape[1] // 2
    bands = (HY_EMB - 1) // 2
    t01 = jnp.linspace(0.0, 1.0, seq_len, dtype=F32)[:, None]
    w_ang = 2.0 * math.pi * jnp.arange(seq_len, dtype=F32)[:, None] / seq_len
    f = jnp.linspace(1e-4, bands - 1, bands, dtype=F32)[None, :]
    z = jnp.concatenate([t01, jnp.cos(f * w_ang), -jnp.sin(f * w_ang)], axis=-1)
    rev = (seq_len - jnp.arange(seq_len)) % seq_len
    assert HY_HIDDEN == HY_PACK and HY_EMB <= HY_PACK
    tile = _tile(seq_len, 1024)
    tps = seq_len // tile
    half = tile // 2
    zf = jnp.pad(jnp.concatenate([z, z[rev]], axis=0), ((0, 0), (0, HY_PACK - HY_EMB)))
    zf = zf.reshape(2 * tps, 2, half, HY_PACK).transpose(0, 2, 1, 3).reshape(2 * tps * half, LANE)
    zero = jnp.zeros((HY_PACK, HY_PACK), F32)
    blockdiag = lambda w: jnp.block([[w, zero], [zero, w]])
    w1p = blockdiag(jnp.pad(w1, ((0, HY_PACK - HY_EMB), (0, 0))))
    w2p = blockdiag(w2)
    w3d = w3.reshape(HY_HIDDEN, 2, dw).transpose(1, 0, 2)
    zrows = jnp.zeros_like(w3d)
    w3p = jnp.stack([jnp.concatenate([w3d, zrows], axis=1), jnp.concatenate([zrows, w3d], axis=1)], axis=1)
    vec = lambda v: jnp.concatenate([v, v]).reshape(1, LANE)
    deltas = jnp.abs(jnp.linspace(math.log(HY_TARGET) / HY_FAST_DECAY, math.log(HY_TARGET) / HY_SLOW_DECAY,
                                  dw, dtype=F32)).reshape(1, dw)
    full = lambda shape: pl.BlockSpec(shape, lambda i: (0, 0))
    return pl.pallas_call(
        functools.partial(_hyfilt_kernel, seq_len, tile),
        grid=(2 * tps,),
        in_specs=[pl.BlockSpec((half, LANE), lambda i: (i, 0)),
                  full((LANE, LANE)), full((1, LANE)), full((LANE, LANE)), full((1, LANE)),
                  pl.BlockSpec((None, 2, LANE, dw), lambda i: (i // tps, 0, 0, 0)),
                  full((1, LANE)), full((1, dw))],
        out_specs=[pl.BlockSpec((tile, dw), lambda i: (i, 0)), full((1, dw))],
        out_shape=[jax.ShapeDtypeStruct((2 * seq_len, dw), F32), jax.ShapeDtypeStruct((1, dw), F32)],
        compiler_params=_cp("arbitrary"),
        name="hyena_filter",
    )(zf, w1p, vec(b1), w2p, vec(b2), w3p, vec(freq), deltas)


def _dft_tables(n1, n2):
    n = n1 * n2
    two_pi = 2.0 * math.pi

    def cs(num, den):
        ang = (num % den).astype(F32) * (two_pi / den)
        return jnp.cos(ang), jnp.sin(ang)

    a = jnp.arange(n1)
    c1, s1 = cs(a[:, None] * a[None, :], n1)
    kron = lambda mat: jnp.kron(mat, jnp.eye(SUBLANE, dtype=F32))
    fwd_a = jnp.concatenate([kron(c1), kron(-s1)], axis=0)
    inv_a = jnp.concatenate([kron(c1[: n1 // 2]), kron(-s1[: n1 // 2])], axis=1) / n
    k2 = jnp.arange(n2)
    num = (k2[None, :, None] * k2[None, None, :] * n1 + a[:, None, None] * k2[None, None, :])
    cb, sb = cs(num, n)
    fwd_b = jnp.concatenate([jnp.concatenate([cb, sb], axis=2),
                             jnp.concatenate([-sb, cb], axis=2)], axis=1)
    cbt, sbt = jnp.swapaxes(cb, 1, 2), jnp.swapaxes(sb, 1, 2)
    inv_b = jnp.concatenate([jnp.concatenate([cbt, -sbt], axis=2),
                             jnp.concatenate([sbt, cbt], axis=2)], axis=1)
    return tuple(m.astype(BF16) for m in (fwd_a, inv_a, fwd_b, inv_b))


def _rows(x):
    a, g, c = x.shape
    return x.reshape(a * g, c)


def _dft_a_kernel(f_ref, x_ref, re_ref, im_ref):
    y = _dot(f_ref[...], _rows(x_ref[...]).astype(BF16))
    half = y.shape[0] // 2
    re_ref[...] = y[:half].reshape(re_ref.shape)
    im_ref[...] = y[half:].reshape(im_ref.shape)


def _dft_stage_a(xv, fwd_a, n1):
    b, k, n2, ch = xv.shape
    groups = n2 // SUBLANE
    out = jax.ShapeDtypeStruct((b, n1, n2, ch), F32)
    return pl.pallas_call(
        _dft_a_kernel,
        grid=(b, groups),
        in_specs=[pl.BlockSpec((2 * n1 * SUBLANE, k * SUBLANE), lambda bi, j: (0, 0)),
                  pl.BlockSpec((None, k, SUBLANE, ch), lambda bi, j: (bi, 0, j, 0))],
        out_specs=[pl.BlockSpec((None, n1, SUBLANE, ch), lambda bi, j: (bi, 0, j, 0))] * 2,
        out_shape=[out, out],
        compiler_params=_cp("parallel", "parallel"),
        name="dft_stage_a",
    )(fwd_a[:, :k * SUBLANE], xv)


def _dft_b_kernel(m_ref, re_ref, im_ref, ss_ref, kr_ref, ki_ref):
    nrm = lax.rsqrt(ss_ref[...] + EPS)
    n2 = m_ref.shape[1] // 2
    for k in range(m_ref.shape[0]):
        rows = slice(k * n2, (k + 1) * n2)
        x = _dot(m_ref[k], jnp.concatenate([re_ref[rows], im_ref[rows]], axis=0).astype(BF16))
        kr_ref[rows] = x[:n2] * nrm
        ki_ref[rows] = x[n2:] * nrm


def _filter_spectrum(kern, sumsq, tabs, n1, n2):
    fwd_a, _, fwd_b, _ = tabs
    ch = kern.shape[1]
    are, aim = _dft_stage_a(kern.reshape(1, n1, n2, ch), fwd_a, n1)
    are, aim = are.reshape(n1 * n2, ch), aim.reshape(n1 * n2, ch)
    kb = _tile(n1, DFT_BLOCKS)
    blk = pl.BlockSpec((kb * n2, ch), lambda i: (i, 0))
    out = jax.ShapeDtypeStruct((n1 * n2, ch), F32)
    return pl.pallas_call(
        _dft_b_kernel,
        grid=(n1 // kb,),
        in_specs=[pl.BlockSpec((kb, 2 * n2, 2 * n2), lambda i: (i, 0, 0)), blk, blk,
                  pl.BlockSpec((1, ch), lambda i: (0, 0))],
        out_specs=[blk, blk],
        out_shape=[out, out],
        compiler_params=_cp("parallel"),
        name="filter_spectrum",
    )(fwd_b, are, aim, sumsq)


def _dft_mul_kernel(f_ref, g_ref, re_ref, im_ref, kr_ref, ki_ref, qr_ref, qi_ref):
    n2 = f_ref.shape[1] // 2
    for k in range(f_ref.shape[0]):
        rows = slice(k * n2, (k + 1) * n2)
        x = _dot(f_ref[k], jnp.concatenate([re_ref[rows], im_ref[rows]], axis=0).astype(BF16))
        xr, xi = x[:n2], x[n2:]
        kr, ki = kr_ref[rows], ki_ref[rows]
        prod = jnp.concatenate([xr * kr - xi * ki, xr * ki + xi * kr], axis=0)
        q = _dot(g_ref[k], prod.astype(BF16))
        qr_ref[rows] = q[:n2]
        qi_ref[rows] = q[n2:]


def _dft_inv_kernel(g_ref, qr_ref, qi_ref, z_ref, x2_ref, bias_ref, o_ref):
    q = jnp.concatenate([_rows(qr_ref[...]), _rows(qi_ref[...])], axis=0)
    y = _dot(g_ref[...], q.astype(BF16)).reshape(z_ref.shape)
    o_ref[...] = x2_ref[...] * (y + z_ref[...] * bias_ref[...])


def _hyena_long(z, x2, kr, ki, bias, tabs, batch, seq_len, n1, n2):
    fwd_a, inv_a, fwd_b, inv_b = tabs
    ch = z.shape[1]
    zv = z.reshape(batch, n1 // 2, n2, ch)
    are, aim = _dft_stage_a(zv, fwd_a, n1)
    are, aim = are.reshape(batch, n1 * n2, ch), aim.reshape(batch, n1 * n2, ch)
    kb = _tile(n1, DFT_BLOCKS)
    mat = pl.BlockSpec((kb, 2 * n2, 2 * n2), lambda i, b: (i, 0, 0))
    dat = pl.BlockSpec((None, kb * n2, ch), lambda i, b: (b, i, 0))
    spec = pl.BlockSpec((kb * n2, ch), lambda i, b: (i, 0))
    out = jax.ShapeDtypeStruct((batch, n1 * n2, ch), F32)
    qr, qi = pl.pallas_call(
        _dft_mul_kernel,
        grid=(n1 // kb, batch),
        in_specs=[mat, mat, dat, dat, spec, spec],
        out_specs=[dat, dat],
        out_shape=[out, out],
        compiler_params=_cp("parallel", "parallel"),
        name="dft_filter_mul",
    )(fwd_b, inv_b, are, aim, kr, ki)
    qr, qi = qr.reshape(batch, n1, n2, ch), qi.reshape(batch, n1, n2, ch)
    full = pl.BlockSpec((None, n1, SUBLANE, ch), lambda b, j: (b, 0, j, 0))
    half = pl.BlockSpec((None, n1 // 2, SUBLANE, ch), lambda b, j: (b, 0, j, 0))
    o = pl.pallas_call(
        _dft_inv_kernel,
        grid=(batch, n2 // SUBLANE),
        in_specs=[pl.BlockSpec(inv_a.shape, lambda b, j: (0, 0)), full, full, half, half,
                  pl.BlockSpec((1, ch), lambda b, j: (0, 0))],
        out_specs=half,
        out_shape=jax.ShapeDtypeStruct((batch, n1 // 2, n2, ch), F32),
        compiler_params=_cp("parallel", "parallel"),
        name="dft_inverse_gate",
    )(inv_a, qr, qi, zv, x2.reshape(batch, n1 // 2, n2, ch), bias.reshape(1, ch))
    return o.reshape(batch * seq_len, ch)


def _hyena_short_kernel(f_ref, g_ref, k_ref, ss_ref, z_ref, x2_ref, bias_ref, o_ref):
    f = f_ref[...]
    z = z_ref[...]
    seq = z.shape[0]
    ks = _dot(f, k_ref[...].astype(BF16)) * lax.rsqrt(ss_ref[...] + EPS)
    zs = _dot(f[:, :seq], z.astype(BF16))
    n = ks.shape[0] // 2
    kr, ki, zr, zi = ks[:n], ks[n:], zs[:n], zs[n:]
    prod = jnp.concatenate([zr * kr - zi * ki, zr * ki + zi * kr], axis=0)
    y = _dot(g_ref[...], prod.astype(BF16))
    o_ref[...] = x2_ref[...] * (y + z * bias_ref[...])


def _hyena_short(z, x2, kern, sumsq, bias, batch, seq_len):
    n = 2 * seq_len
    ch = z.shape[1]
    a = jnp.arange(n)
    ang = ((a[:, None] * a[None, :]) % n).astype(F32) * (2.0 * math.pi / n)
    c, s = jnp.cos(ang), jnp.sin(ang)
    f = jnp.concatenate([c, -s], axis=0).astype(BF16)
    g = (jnp.concatenate([c, -s], axis=1)[:seq_len] / n).astype(BF16)
    full = lambda shape: pl.BlockSpec(shape, lambda b: (0, 0))
    dat = pl.BlockSpec((seq_len, ch), lambda b: (b, 0))
    return pl.pallas_call(
        _hyena_short_kernel,
        grid=(batch,),
        in_specs=[full((2 * n, n)), full((seq_len, 2 * n)),
                  full((n, ch)), full((1, ch)), dat, dat, full((1, ch))],
        out_specs=dat,
        out_shape=jax.ShapeDtypeStruct((batch * seq_len, ch), F32),
        compiler_params=_cp("parallel"),
        name="hyena_short_conv",
    )(f, g, kern, sumsq, z, x2, bias.reshape(1, ch))


def _hyena_mixer(p, col0, width, batch, seq_len, conv_w, conv_b, w1, b1, w2, b2, w3, freq, bias):
    z, x2 = _hyena_pre(p, col0, width, seq_len, conv_w, conv_b)
    kern, sumsq = _hyena_filter(seq_len, w1, b1, w2, b2, w3, freq)
    n = 2 * seq_len
    if n <= 4 * DFT_RADIX:
        return _hyena_short(z, x2, kern, sumsq, bias, batch, seq_len)
    n2 = DFT_RADIX
    n1 = n // n2
    tabs = _dft_tables(n1, n2)
    kr, ki = _filter_spectrum(kern, sumsq, tabs, n1, n2)
    return _hyena_long(z, x2, kr, ki, bias, tabs, batch, seq_len, n1, n2)


def _outproj_kernel(x_ref, a_ref, b_ref, c_ref, d_ref, w_ref, g_ref, o_ref):
    acc = None
    row = 0
    for m_ref in (a_ref, b_ref, c_ref, d_ref):
        wdt = m_ref.shape[1]
        part = _dot(m_ref[...].astype(BF16), w_ref[row:row + wdt, :])
        acc = part if acc is None else acc + part
        row += wdt
    o_ref[...] = x_ref[...] + g_ref[...] * acc


def _out_projection(x2d, mixers, w, gate, rows_per_group):
    n, d = x2d.shape
    tm = _tile(rows_per_group, 512)
    tpg = rows_per_group // tm
    return pl.pallas_call(
        _outproj_kernel,
        grid=(n // tm,),
        in_specs=[pl.BlockSpec((tm, d), lambda i: (i, 0))]
        + [pl.BlockSpec((tm, m.shape[1]), lambda i: (i, 0)) for m in mixers]
        + [pl.BlockSpec(w.shape, lambda i: (0, 0)),
           pl.BlockSpec((None, 1, d), lambda i: (i // tpg, 0, 0))],
        out_specs=pl.BlockSpec((tm, d), lambda i: (i, 0)),
        out_shape=jax.ShapeDtypeStruct((n, d), F32),
        compiler_params=_cp("parallel"),
        name="out_proj",
    )(x2d, *mixers, w, gate)


def _ffn_kernel(seq_len, tile, x_ref, xp_ref, xn_ref, g_ref, sc_ref, sh_ref, gate_ref,
                wg_ref, wu_ref, cwg_ref, cwu_ref, cbg_ref, cbu_ref, wd_ref, o_ref, h_scr, acc):
    j = pl.program_id(1)

    @pl.when(j == 0)
    def _():
        ext = jnp.concatenate([xp_ref[...], x_ref[...], xn_ref[...]], axis=0)
        h = _norm_mod(ext, g_ref[...], sc_ref[...], sh_ref[...])
        tps = seq_len // tile
        outside = _outside_rows(tile, pl.program_id(0) % tps, tps)
        h_scr[...] = jnp.where(outside, 0.0, h).astype(BF16)
        acc[...] = jnp.zeros(acc.shape, F32)

    h = h_scr[...]
    gate = _conv3(_dot(h, wg_ref[...]), cwg_ref[...], cbg_ref[...], tile)
    up = _conv3(_dot(h, wu_ref[...]), cwu_ref[...], cbu_ref[...], tile)
    act = (gate * jax.nn.sigmoid(gate)) * up
    acc[...] += _dot(act.astype(BF16), wd_ref[...])

    @pl.when(j == pl.num_programs(1) - 1)
    def _():
        o_ref[...] = x_ref[...] + gate_ref[...] * acc[...]


def _conv_ffn(x2d, g, sc, sh, gate, w_up, conv_w, conv_b, w_down, seq_len, rows_per_group):
    n, d = x2d.shape
    f = w_down.shape[0]
    tile = _tile(seq_len, 1024)
    hc = _tile(f, 512)
    nh = f // hc
    once = pl.Buffered(1)
    tpg = rows_per_group // tile
    r = tile // HALO
    last = n // HALO - 1
    vec = pl.BlockSpec((None, 1, d), lambda i, j: (i // tpg, 0, 0))
    cb = conv_b.reshape(1, 2 * f)
    return pl.pallas_call(
        functools.partial(_ffn_kernel, seq_len, tile),
        grid=(n // tile, nh),
        in_specs=[
            pl.BlockSpec((tile, d), lambda i, j: (i, 0), pipeline_mode=once),
            pl.BlockSpec((HALO, d), lambda i, j: (jnp.maximum(i * r - 1, 0), 0)),
            pl.BlockSpec((HALO, d), lambda i, j: (jnp.minimum(i * r + r, last), 0)),
            pl.BlockSpec((1, d), lambda i, j: (0, 0)),
            vec, vec, vec,
            pl.BlockSpec((d, hc), lambda i, j: (0, j)),
            pl.BlockSpec((d, hc), lambda i, j: (0, nh + j)),
            pl.BlockSpec((3, hc), lambda i, j: (0, j)),
            pl.BlockSpec((3, hc), lambda i, j: (0, nh + j)),
            pl.BlockSpec((1, hc), lambda i, j: (0, j)),
            pl.BlockSpec((1, hc), lambda i, j: (0, nh + j)),
            pl.BlockSpec((hc, d), lambda i, j: (j, 0)),
        ],
        out_specs=pl.BlockSpec((tile, d), lambda i, j: (i, 0), pipeline_mode=once),
        out_shape=jax.ShapeDtypeStruct((n, d), F32),
        scratch_shapes=[pltpu.VMEM((tile + 2 * HALO, d), BF16), pltpu.VMEM((tile, d), F32)],
        compiler_params=_cp("parallel", "arbitrary"),
        name="conv_ffn",
    )(x2d, x2d, x2d, g, sc, sh, gate, w_up, w_up, conv_w, conv_w, cb, cb, w_down)


def kernel(x, c, ctx, c_ctx, w_mod, b_mod, norm1_g, norm2_g, w_in, w_out, qk_gain, diff_lam, diff_subln,
           win_sink, pool_w, pool_scale, hy_conv_w, hy_conv_b, hy_w1, hy_b1, hy_w2, hy_b2, hy_w3, hy_freq,
           hy_bias, ffn_w_in, ffn_conv_w, ffn_conv_b, ffn_w_out):
    batch, seq, d = x.shape
    clen = ctx.shape[1]
    depth = w_in.shape[0]
    a_heads = d // 512
    a_width = a_heads * 2 * HEAD_DIM
    b_heads = d // 256
    b_kv = b_heads // 4
    b_width = b_heads * HEAD_DIM
    b_kvw = b_kv * HEAD_DIM
    c_width = d // 4
    d_width = d - a_width - b_width - c_width
    off_ka = a_width
    off_va = 2 * a_width
    off_qb = 3 * a_width
    off_kb = off_qb + b_width
    off_vb = off_kb + b_kvw
    off_pool = off_vb + b_kvw
    off_hy = off_pool + c_width
    in_cols = off_hy + 3 * d_width
    order = ((off_hy, in_cols), (off_pool, off_hy), (off_va, off_qb), (0, off_ka), (off_ka, off_va),
             (off_qb, off_kb), (off_kb, off_vb), (off_vb, off_pool))
    n_hy, n_pool, n_va, n_qa, n_ka, n_qb, n_kb, n_vb = (
        sum(e - s for s, e in order[:k]) for k in range(len(order)))
    assert a_width % LANE == 0 and b_width % LANE == 0 and b_kvw == LANE and c_width // 4 == LANE
    assert seq % BLOCK == 0 and clen % BLOCK == 0 and n_pool % c_width == 0
    assert (n_ka - n_qa) == a_width and (n_qb - n_qa) % b_width == 0 and n_kb - n_qb == b_width

    rows = -(-(batch + 1) // SUBLANE) * SUBLANE
    cvec = jnp.zeros((rows, d), F32).at[:batch].set(c).at[batch].set(c_ctx)
    mods = _modulation(cvec, w_mod, b_mod)


    cos_t, sin_t = _rope_tables(seq)
    scale = HEAD_DIM ** -0.5
    tile2 = lambda v: jnp.tile(v, LANE // HEAD_DIM)

    xs = x.reshape(batch * seq, d)
    cs = ctx.reshape(batch * clen, d)
    for l in range(depth):
        last = l == depth - 1
        m = mods[l]
        sh1, sc1, g1, sh2, sc2, g2 = [m[:batch, i * d:(i + 1) * d].reshape(batch, 1, d) for i in range(6)]
        csh1, csc1, cg1, csh2, csc2, cg2 = [m[batch, i * d:(i + 1) * d].reshape(1, 1, d) for i in range(6)]
        n1g = norm1_g[l].reshape(1, d)
        n2g = norm2_g[l].reshape(1, d)
        w_in_l = jnp.concatenate([w_in[l, :, s:e] for s, e in order], axis=-1).astype(BF16)
        w_out_l = w_out[l].astype(BF16)
        ffn_in_l = ffn_w_in[l].astype(BF16)
        ffn_out_l = ffn_w_out[l].astype(BF16)
        pool_w_l = pool_w[l].astype(BF16)

        p = _in_projection(xs, n1g, sc1, sh1, w_in_l, seq)
        pc = _in_projection(cs, n1g, csc1, csh1, w_in_l, batch * clen)

        na = a_width // LANE
        nbq = b_width // LANE
        gains = jnp.stack([tile2(qk_gain[l, 0]) * (scale * LOG2E)] * na + [tile2(qk_gain[l, 1])] * na
                          + [tile2(qk_gain[l, 2]) * (scale * LOG2E)] * nbq + [tile2(qk_gain[l, 3])])
        qk = _qk_prep(p, n_qa // LANE, gains, cos_t, sin_t, seq, True)
        qk_c = _qk_prep(pc, n_qa // LANE, gains, cos_t, sin_t, clen, False)
        kb_blk = 2 * na + nbq
        qb_blk = (n_qb - n_qa) // b_width

        lam_p = diff_lam[l].astype(F32)
        lambda_init = 0.8 - 0.6 * math.exp(-0.3 * l)
        lam = jnp.exp(jnp.sum(lam_p[0] * lam_p[1])) - jnp.exp(jnp.sum(lam_p[2] * lam_p[3])) + lambda_init
        lam_v = jnp.full((1, LANE), lam, F32)
        subg = diff_subln[l].reshape(1, LANE)
        sink_tab = jnp.broadcast_to(win_sink[l].astype(F32)[:, None], (b_heads, LANE))
        hy = (hy_conv_w[l], hy_conv_b[l], hy_w1[l], hy_b1[l], hy_w2[l], hy_b2[l], hy_w3[l], hy_freq[l],
              hy_bias[l])

        vt = _transposed_values(p, n_va // LANE, a_heads, seq, _tile(seq, 512))
        vt_c = _transposed_values(pc, n_va // LANE, a_heads, clen, clen)
        o_a = _diff_attention(qk, seq, qk_c, vt_c, clen, qk, vt, seq, lam_v, subg, 1.0 - lambda_init,
                              a_heads, na)
        vtb = _transposed_values(p, n_vb // LANE, 1, seq, BLOCK)
        vtb_c = _transposed_values(pc, n_vb // LANE, 1, clen, clen)
        o_b = _window_attention(qk, seq, qk_c, vtb_c, clen, vtb, sink_tab, b_heads, b_kv,
                                qb_blk, kb_blk, True)
        o_c = _pool_mixer(p, n_pool, c_width, seq, pool_w_l, pool_scale[l].reshape(1, c_width))
        o_d = _hyena_mixer(p, n_hy, 3 * d_width, batch, seq, *hy)
        xs = _out_projection(xs, (o_a, o_b, o_c, o_d), w_out_l, g1, seq)
        xs = _conv_ffn(xs, n2g, sc2, sh2, g2, ffn_in_l, ffn_conv_w[l], ffn_conv_b[l], ffn_out_l,
                       seq, seq)

        if not last:
            o_a_c = _diff_attention(qk_c, clen, qk_c, vt_c, clen, None, None, 0, lam_v, subg,
                                    1.0 - lambda_init, a_heads, na)
            o_b_c = _window_attention(qk_c, clen, qk_c, vtb_c, clen, None, sink_tab, b_heads, b_kv,
                                      qb_blk, kb_blk, False)
            o_c_c = _pool_mixer(pc, n_pool, c_width, clen, pool_w_l, pool_scale[l].reshape(1, c_width))
            o_d_c = _hyena_mixer(pc, n_hy, 3 * d_width, batch, clen, *hy)
            cs = _out_projection(cs, (o_a_c, o_b_c, o_c_c, o_d_c), w_out_l, cg1, batch * clen)
            cs = _conv_ffn(cs, n2g, csc2, csh2, cg2, ffn_in_l, ffn_conv_w[l], ffn_conv_b[l],
                           ffn_out_l, clen, batch * clen)
    return xs.reshape(batch, seq, d)
```

```python
import functools
import math

import jax
import jax.numpy as jnp
from jax import lax
from jax.experimental import pallas as pl
from jax.experimental.pallas import tpu as pltpu

F32 = jnp.float32
BF16 = jnp.bfloat16

HEAD_DIM = 64
GRID_W = 64
ROPE_BASE = 10000.0
EPS = 1e-6
NEG_INF = -1e30
WINDOW = 128
BLOCK = 128
POOL_WINDOWS = (2, 4, 8, 16)
HY_EMB = 33
HY_HIDDEN = 64
HY_FAST_DECAY = 0.3
HY_SLOW_DECAY = 1.5
HY_TARGET = 1e-2

LANE = 128
SUBLANE = 8
HALO = SUBLANE
DFT_RADIX = 128
DFT_BLOCKS = 8
VT_ROWS = LANE + 16
LOG2E = 1.4426950408889634
HY_PACK = LANE // 2
VMEM_LIMIT = 56 * 1024 * 1024


def _cp(*sem):
    return pltpu.CompilerParams(dimension_semantics=sem, vmem_limit_bytes=VMEM_LIMIT)


def _tile(n, pref):
    if n <= pref:
        return n
    t = pref
    while n % t:
        t //= 2
    assert t >= SUBLANE, (n, pref)
    return t


def _dot(a, b):
    return jnp.dot(a, b, preferred_element_type=F32)


def _dot_nt(a, b):
    return lax.dot_general(a, b, (((1,), (1,)), ((), ())), preferred_element_type=F32)


def _mod_kernel(c_ref, w_ref, b_ref, o_ref):
    c = c_ref[...]
    a = (c * jax.nn.sigmoid(c)).astype(BF16)
    o_ref[...] = _dot(a, w_ref[...].astype(BF16)) + b_ref[...]


def _modulation(cvec, w_mod, b_mod):
    depth, d, n = w_mod.shape
    rows = cvec.shape[0]
    tn = _tile(n, 1024)
    return pl.pallas_call(
        _mod_kernel,
        grid=(depth, n // tn),
        in_specs=[
            pl.BlockSpec((rows, d), lambda l, j: (0, 0)),
            pl.BlockSpec((None, d, tn), lambda l, j: (l, 0, j)),
            pl.BlockSpec((None, 1, tn), lambda l, j: (l, 0, j)),
        ],
        out_specs=pl.BlockSpec((None, rows, tn), lambda l, j: (l, 0, j)),
        out_shape=jax.ShapeDtypeStruct((depth, rows, n), F32),
        compiler_params=_cp("parallel", "parallel"),
        name="adaln_mod",
    )(cvec, w_mod, b_mod.reshape(depth, 1, n))


def _norm_mod(x, g, sc, sh):
    ms = jnp.mean(x * x, axis=-1, keepdims=True)
    return (x * lax.rsqrt(ms + EPS) * g) * (1.0 + sc) + sh


def _inproj_kernel(x_ref, g_ref, sc_ref, sh_ref, w_ref, o_ref):
    h = _norm_mod(x_ref[...], g_ref[...], sc_ref[...], sh_ref[...])
    o_ref[...] = _dot(h.astype(BF16), w_ref[...])


def _in_projection(x2d, g, sc, sh, w, rows_per_group):
    n, d = x2d.shape
    nc = w.shape[1]
    tm = _tile(rows_per_group, 512)
    tpg = rows_per_group // tm
    vec = pl.BlockSpec((None, 1, d), lambda i: (i // tpg, 0, 0))
    return pl.pallas_call(
        _inproj_kernel,
        grid=(n // tm,),
        in_specs=[
            pl.BlockSpec((tm, d), lambda i: (i, 0)),
            pl.BlockSpec((1, d), lambda i: (0, 0)),
            vec, vec,
            pl.BlockSpec((d, nc), lambda i: (0, 0), pipeline_mode=pl.Buffered(1)),
        ],
        out_specs=pl.BlockSpec((tm, nc), lambda i: (i, 0)),
        out_shape=jax.ShapeDtypeStruct((n, nc), F32),
        compiler_params=_cp("parallel"),
        name="in_proj",
    )(x2d, g, sc, sh, w)


def _qkprep_kernel(rope, x_ref, gain_ref, cos_ref, sin_ref, gm_ref, pm_ref, o_ref):
    x = x_ref[...]
    ms = _dot((x * x).astype(BF16), gm_ref[...])
    y = x * lax.rsqrt(ms + EPS) * gain_ref[...]
    if rope:
        cos = jnp.concatenate([cos_ref[...]] * 2, axis=1)
        sin = jnp.concatenate([sin_ref[...]] * 2, axis=1)
        y = y * cos + _dot(y.astype(BF16), pm_ref[...]) * sin
    o_ref[...] = y.astype(BF16)


def _qk_prep(p, col_block0, gains, cos_t, sin_t, seq_len, rope):
    n = p.shape[0]
    nblk = gains.shape[0]
    wide = 2 * LANE
    nw = -(-nblk // 2)
    assert col_block0 % 2 == 0 and (col_block0 + 2 * nw) * LANE <= p.shape[1]
    gains = jnp.concatenate([gains, jnp.ones((2 * nw - nblk, LANE), F32)], axis=0).reshape(nw, 1, wide)
    tm = _tile(seq_len, 2048)
    tps = seq_len // tm
    lane = jnp.arange(wide)
    gm = ((lane[:, None] // HEAD_DIM) == (lane[None, :] // HEAD_DIM)).astype(BF16) * (1.0 / HEAD_DIM)
    partner = jnp.where((lane % 32) < 16, lane + 16, lane - 16)
    pm = (lane[:, None] == partner[None, :]).astype(BF16)
    mat = pl.BlockSpec((wide, wide), lambda i, j: (0, 0))
    tab = pl.BlockSpec((tm, LANE), lambda i, j: (i % tps, 0))
    return pl.pallas_call(
        functools.partial(_qkprep_kernel, rope),
        grid=(n // tm, nw),
        in_specs=[
            pl.BlockSpec((tm, wide), lambda i, j: (i, col_block0 // 2 + j)),
            pl.BlockSpec((None, 1, wide), lambda i, j: (j, 0, 0)),
            tab, tab, mat, mat,
        ],
        out_specs=pl.BlockSpec((tm, wide), lambda i, j: (i, j)),
        out_shape=jax.ShapeDtypeStruct((n, nw * wide), BF16),
        compiler_params=_cp("parallel", "parallel"),
        name="qk_prep",
    )(p, gains, cos_t, sin_t, gm.astype(BF16), pm)


def _rope_tables(seq_len):
    rows = seq_len // GRID_W
    row = jnp.repeat(jnp.arange(rows, dtype=F32), GRID_W)
    col = jnp.tile(jnp.arange(GRID_W, dtype=F32), rows)
    half = HEAD_DIM // 2
    inv = ROPE_BASE ** (-jnp.arange(0, half, 2, dtype=F32) / half)
    ang = jnp.concatenate([row[:, None] * inv, col[:, None] * inv], axis=-1)
    lane = jnp.arange(LANE)
    within = lane % HEAD_DIM
    idx = (within // 32) * 16 + (within % 16)
    sign = jnp.where((lane % 32) < 16, -1.0, 1.0).astype(F32)
    return jnp.cos(ang)[:, idx], jnp.sin(ang)[:, idx] * sign


def _vt_kernel(x_ref, o_ref):
    chunks, _, chunk = o_ref.shape
    row = lax.broadcasted_iota(jnp.int32, (VT_ROWS - LANE, chunk), 0)
    ones_row = jnp.where(row == 0, 1.0, 0.0).astype(BF16)
    for c in range(chunks):
        x_t = x_ref[c * chunk:(c + 1) * chunk, :].T.astype(BF16)
        o_ref[c] = jnp.concatenate([x_t, ones_row], axis=0)


def _transposed_values(p, col_block0, nblk, seq_len, chunk):
    n = p.shape[0]
    rows = max(chunk, _tile(seq_len, 2048))
    return pl.pallas_call(
        _vt_kernel,
        grid=(n // rows, nblk),
        in_specs=[pl.BlockSpec((rows, LANE), lambda i, j: (i, col_block0 + j))],
        out_specs=pl.BlockSpec((None, rows // chunk, VT_ROWS, chunk), lambda i, j: (j, i, 0, 0)),
        out_shape=jax.ShapeDtypeStruct((nblk, n // chunk, VT_ROWS, chunk), BF16),
        compiler_params=_cp("parallel", "parallel"),
        name="values_t",
    )(p)


def _attn_a_kernel(n_lat, tk, sub_scale, *refs):
    if n_lat:
        (q_ref, kc_ref, vc_ref, k_ref, v_ref, lam_ref, subg_ref, o_ref, qt, m, acc,
         s0, s1, x0, x1, p0, p1, al0, al1) = refs
    else:
        q_ref, kc_ref, vc_ref, lam_ref, subg_ref, o_ref, qt, m, acc = refs
    tq = q_ref.shape[0]
    q_t = q_ref[...].astype(F32).T
    row = lax.broadcasted_iota(jnp.int32, q_t.shape, 0)
    qt[:, :tq] = jnp.where(row < HEAD_DIM, q_t, 0.0).astype(BF16)
    qt[:, tq:] = jnp.where(row >= HEAD_DIM, q_t, 0.0).astype(BF16)
    m[...] = jnp.full(m.shape, -jnp.inf, F32)
    acc[...] = jnp.zeros(acc.shape, F32)

    def softmax_update(s, s_max):
        m_prev = m[...]
        m_new = jnp.maximum(m_prev, s_max)
        m[...] = m_new
        return jnp.exp2(s - m_new).astype(BF16), jnp.exp2(m_prev - m_new)

    s_c = _dot(kc_ref[...], qt[...])

    def context_update():
        p_c, alpha_c = softmax_update(s_c, jnp.max(s_c, axis=0, keepdims=True))
        acc[...] = alpha_c * acc[...] + _dot(vc_ref[0], p_c)

    if not n_lat:
        context_update()
    else:
        def scores(c, s_buf, x_buf):
            s = _dot(k_ref[pl.ds(pl.multiple_of(c * tk, tk), tk), :], qt[...])
            s_buf[...] = s
            x_buf[...] = jnp.max(s, axis=0, keepdims=True)

        def softmax(s_buf, x_buf, p_buf, al_buf):
            p_buf[...], al_buf[...] = softmax_update(s_buf[...], x_buf[...])

        def values(c, p_buf, al_buf):
            acc[...] = al_buf[...] * acc[...] + _dot(v_ref[c], p_buf[...])

        scores(0, s0, x0)
        if n_lat == 1:
            context_update()
            softmax(s0, x0, p0, al0)
            values(0, p0, al0)
        else:
            assert n_lat % 2 == 0
            scores(1, s1, x1)
            context_update()
            softmax(s0, x0, p0, al0)

            def body(t, carry):
                c = 2 * t
                scores(c + 2, s0, x0)
                softmax(s1, x1, p1, al1)
                values(c, p0, al0)
                scores(c + 3, s1, x1)
                softmax(s0, x0, p0, al0)
                values(c + 1, p1, al1)
                return carry
            lax.fori_loop(0, n_lat // 2 - 1, body, 0)
            softmax(s1, x1, p1, al1)
            values(n_lat - 2, p0, al0)
            values(n_lat - 1, p1, al1)

    o_t = acc[:LANE] / acc[LANE:LANE + 1]
    o = (o_t[:, :tq] - lam_ref[0:1, 0:1] * o_t[:, tq:]).T
    ms = jnp.mean(o * o, axis=-1, keepdims=True)
    o_ref[...] = ((o * lax.rsqrt(ms + EPS) * subg_ref[...]) * sub_scale).astype(BF16)


def _diff_attention(q_arr, q_len, qk_ctx, vt_ctx, ctx_len, qk_lat, vt_lat, lat_len, lam, subg,
                    sub_scale, heads, k_col0):
    batch = q_arr.shape[0] // q_len
    tq = _tile(q_len, 1024)
    nq = q_len // tq
    in_specs = [
        pl.BlockSpec((tq, LANE), lambda b, h, i: (b * nq + i, h)),
        pl.BlockSpec((ctx_len, LANE), lambda b, h, i: (b, k_col0 + h)),
        pl.BlockSpec((None, 1, VT_ROWS, ctx_len), lambda b, h, i: (h, b, 0, 0)),
    ]
    args = [q_arr, qk_ctx, vt_ctx]
    n_lat, tk = 0, 0
    if qk_lat is not None:
        tk = vt_lat.shape[-1]
        n_lat = lat_len // tk
        in_specs += [
            pl.BlockSpec((lat_len, LANE), lambda b, h, i: (b, k_col0 + h)),
            pl.BlockSpec((None, n_lat, VT_ROWS, tk), lambda b, h, i: (h, b, 0, 0)),
        ]
        args += [qk_lat, vt_lat]
    in_specs += [pl.BlockSpec((1, LANE), lambda b, h, i: (0, 0))] * 2
    args += [lam, subg]
    stat = pltpu.VMEM((1, 2 * tq), F32)
    scratch = [pltpu.VMEM((LANE, 2 * tq), BF16), stat, pltpu.VMEM((VT_ROWS, 2 * tq), F32)]
    if n_lat:
        scratch += [pltpu.VMEM((tk, 2 * tq), F32)] * 2 + [stat] * 2
        scratch += [pltpu.VMEM((tk, 2 * tq), BF16)] * 2 + [stat] * 2
    return pl.pallas_call(
        functools.partial(_attn_a_kernel, n_lat, tk, sub_scale),
        grid=(batch, heads, nq),
        in_specs=in_specs,
        out_specs=pl.BlockSpec((tq, LANE), lambda b, h, i: (b * nq + i, h)),
        out_shape=jax.ShapeDtypeStruct((batch * q_len, heads * LANE), BF16),
        scratch_shapes=scratch,
        compiler_params=_cp("parallel", "parallel", "parallel"),
        name="diff_attn",
    )(*args)


def _attn_b_kernel(has_loc, seq_len, q_heads, group, *refs):
    if has_loc:
        (q_ref, kx_ref, vx_ref, kp_ref, kc_ref, kn_ref, vp_ref, vc_ref, vn_ref, sink_ref, o_ref) = refs
    else:
        (q_ref, kx_ref, vx_ref, sink_ref, o_ref) = refs
    n = pl.program_id(1)
    clen = kx_ref.shape[0]
    q_t = q_ref[...].astype(F32).T
    keys = [kx_ref[...]]
    vals = [vx_ref[0]]
    if has_loc:
        keys += [kp_ref[...], kc_ref[...], kn_ref[...]]
        vals += [vp_ref[0], vc_ref[0], vn_ref[0]]
        kpos = (n - 1) * BLOCK + lax.broadcasted_iota(jnp.int32, (3 * BLOCK, BLOCK), 0)
        qpos = n * BLOCK + lax.broadcasted_iota(jnp.int32, (3 * BLOCK, BLOCK), 1)
        valid = (kpos >= 0) & (kpos < seq_len) & (jnp.abs(kpos - qpos) <= WINDOW)
        valid = jnp.concatenate([valid] * q_heads, axis=1)
    k_all = jnp.concatenate(keys, axis=0)
    v_all = jnp.concatenate(vals, axis=1)
    zeros = jnp.zeros((HEAD_DIM, BLOCK), F32)
    cols = []
    for hq in range(q_heads):
        blk = q_t[hq * HEAD_DIM:(hq + 1) * HEAD_DIM]
        cols.append(jnp.concatenate([blk, zeros] if hq // group == 0 else [zeros, blk], axis=0))
    q_all = jnp.concatenate(cols, axis=1).astype(BF16)
    sink = jnp.concatenate([sink_ref[hq:hq + 1, :] for hq in range(q_heads)], axis=1) * LOG2E
    s = _dot(k_all, q_all)
    if has_loc:
        s = jnp.concatenate([s[:clen], jnp.where(valid, s[clen:], NEG_INF)], axis=0)
    m = jnp.maximum(jnp.max(s, axis=0, keepdims=True), sink)
    o_t = _dot(v_all, jnp.exp2(s - m).astype(BF16))
    den = o_t[LANE:LANE + 1] + jnp.exp2(sink - m)
    heads_out = []
    for hq in range(q_heads):
        kvh = hq // group
        lanes = slice(hq * BLOCK, (hq + 1) * BLOCK)
        heads_out.append(o_t[kvh * HEAD_DIM:(kvh + 1) * HEAD_DIM, lanes] / den[:, lanes])
    o_ref[...] = jnp.concatenate(heads_out, axis=0).T.astype(BF16)


def _window_attention(q_arr, q_len, qk_ctx, vt_ctx, ctx_len, vt_lat, sink_tab, q_heads, kv_heads,
                      q_blk, k_blk, local):
    assert kv_heads == 2, "key/value heads must fill exactly one 128-lane block"
    batch = q_arr.shape[0] // q_len
    nb = q_len // BLOCK
    qw = q_heads * HEAD_DIM
    in_specs = [
        pl.BlockSpec((BLOCK, qw), lambda b, n: (b * nb + n, q_blk)),
        pl.BlockSpec((ctx_len, LANE), lambda b, n: (b, k_blk)),
        pl.BlockSpec((None, 1, VT_ROWS, ctx_len), lambda b, n: (0, b, 0, 0)),
    ]
    args = [q_arr, qk_ctx, vt_ctx]
    if local:
        prev = lambda b, n: b * nb + jnp.maximum(n - 1, 0)
        nxt = lambda b, n: b * nb + jnp.minimum(n + 1, nb - 1)
        in_specs += [
            pl.BlockSpec((BLOCK, LANE), lambda b, n: (prev(b, n), k_blk)),
            pl.BlockSpec((BLOCK, LANE), lambda b, n: (b * nb + n, k_blk)),
            pl.BlockSpec((BLOCK, LANE), lambda b, n: (nxt(b, n), k_blk)),
            pl.BlockSpec((None, 1, VT_ROWS, BLOCK), lambda b, n: (0, prev(b, n), 0, 0)),
            pl.BlockSpec((None, 1, VT_ROWS, BLOCK), lambda b, n: (0, b * nb + n, 0, 0)),
            pl.BlockSpec((None, 1, VT_ROWS, BLOCK), lambda b, n: (0, nxt(b, n), 0, 0)),
        ]
        args += [q_arr] * 3 + [vt_lat] * 3
    in_specs.append(pl.BlockSpec((q_heads, LANE), lambda b, n: (0, 0)))
    args.append(sink_tab)
    return pl.pallas_call(
        functools.partial(_attn_b_kernel, local, q_len, q_heads, q_heads // kv_heads),
        grid=(batch, nb),
        in_specs=in_specs,
        out_specs=pl.BlockSpec((BLOCK, qw), lambda b, n: (b * nb + n, 0)),
        out_shape=jax.ShapeDtypeStruct((batch * q_len, qw), BF16),
        compiler_params=_cp("parallel", "parallel"),
        name="window_attn",
    )(*args)


def _halo_specs(tile, width, col_blk, n_rows):
    r = tile // HALO
    last = n_rows // HALO - 1
    return [
        pl.BlockSpec((tile, width), lambda i: (i, col_blk)),
        pl.BlockSpec((HALO, width), lambda i: (jnp.maximum(i * r - 1, 0), col_blk)),
        pl.BlockSpec((HALO, width), lambda i: (jnp.minimum(i * r + r, last), col_blk)),
    ]


def _outside_rows(tile, tile_idx, tiles_per_seq):
    row = lax.broadcasted_iota(jnp.int32, (tile + 2 * HALO, 1), 0)
    lo = jnp.where(tile_idx == 0, HALO, 0)
    hi = jnp.where(tile_idx == tiles_per_seq - 1, HALO + tile, tile + 2 * HALO)
    return (row < lo) | (row >= hi)


def _with_halo(x, xp, xn, tiles_per_seq):
    ext = jnp.concatenate([xp, x, xn], axis=0)
    outside = _outside_rows(x.shape[0], pl.program_id(0) % tiles_per_seq, tiles_per_seq)
    return jnp.where(outside, 0.0, ext)


def _conv3(ext, w, b, tile):
    rows = ext.shape[0]
    c = (pltpu.roll(ext, 1, 0) * w[0:1] + ext * w[1:2] + pltpu.roll(ext, rows - 1, 0) * w[2:3])
    return c[HALO:HALO + tile] + b


def _pool_kernel(seq_len, tile, x_ref, xp_ref, xn_ref, w_ref, ls_ref, o_ref):
    tps = seq_len // tile
    x = x_ref[...]
    ext = _with_halo(x, xp_ref[...], xn_ref[...], tps)
    rows = tile + 2 * HALO
    t = (pl.program_id(0) % tps) * tile + lax.broadcasted_iota(jnp.int32, (tile, 1), 0)
    gw = x.shape[1] // len(POOL_WINDOWS)
    outs = []
    for g, w in enumerate(POOL_WINDOWS):
        lo = w // 2
        hi = w - 1 - lo
        a = ext[:, g * gw:(g + 1) * gw]
        span = 1
        while span < w:
            a = a + pltpu.roll(a, span, 0)
            span *= 2
        win = (pltpu.roll(a, rows - hi, 0) if hi else a)[HALO:HALO + tile]
        cnt = jnp.minimum(t + hi + 1, seq_len) - jnp.maximum(t - lo, 0)
        d = win / cnt.astype(F32) - x[:, g * gw:(g + 1) * gw]
        outs.append(_dot(d.astype(BF16), w_ref[g]))
    o_ref[...] = (jnp.concatenate(outs, axis=1) * ls_ref[...]).astype(BF16)


def _pool_mixer(p, col0, width, seq_len, w_lin, ls):
    n = p.shape[0]
    tile = _tile(seq_len, 512)
    g, gw, _ = w_lin.shape
    return pl.pallas_call(
        functools.partial(_pool_kernel, seq_len, tile),
        grid=(n // tile,),
        in_specs=_halo_specs(tile, width, col0 // width, n)
        + [pl.BlockSpec((g, gw, gw), lambda i: (0, 0, 0)),
           pl.BlockSpec((1, width), lambda i: (0, 0))],
        out_specs=pl.BlockSpec((tile, width), lambda i: (i, 0)),
        out_shape=jax.ShapeDtypeStruct((n, width), BF16),
        compiler_params=_cp("parallel"),
        name="pool_mixer",
    )(p, p, p, w_lin, ls)


def _hypre_kernel(seq_len, tile, x_ref, xp_ref, xn_ref, cw_ref, cb_ref, z_ref, x2_ref):
    ext = _with_halo(x_ref[...], xp_ref[...], xn_ref[...], seq_len // tile)
    c = _conv3(ext, cw_ref[...], cb_ref[...], tile)
    dw = c.shape[1] // 3
    z_ref[...] = c[:, :dw] * c[:, dw:2 * dw]
    x2_ref[...] = c[:, 2 * dw:]


def _hyena_pre(p, col0, width, seq_len, conv_w, conv_b):
    n = p.shape[0]
    tile = _tile(seq_len, 512)
    dw = width // 3
    out = jax.ShapeDtypeStruct((n, dw), F32)
    return pl.pallas_call(
        functools.partial(_hypre_kernel, seq_len, tile),
        grid=(n // tile,),
        in_specs=_halo_specs(tile, width, col0 // width, n)
        + [pl.BlockSpec((3, width), lambda i: (0, 0)),
           pl.BlockSpec((1, width), lambda i: (0, 0))],
        out_specs=[pl.BlockSpec((tile, dw), lambda i: (i, 0))] * 2,
        out_shape=[out, out],
        compiler_params=_cp("parallel"),
        name="hyena_pre",
    )(p, p, p, conv_w, conv_b.reshape(1, width))


def _hyfilt_kernel(seq_len, tile, zf_ref, w1_ref, b1_ref, w2_ref, b2_ref, w3_ref, fr_ref, dl_ref,
                   k_ref, ss_ref):
    i = pl.program_id(0)
    half = tile // 2
    hp = functools.partial(jnp.dot, precision=lax.Precision.HIGHEST, preferred_element_type=F32)
    zf = zf_ref[...]
    fr = fr_ref[...]
    h = jnp.sin(fr * (hp(zf, w1_ref[...]) + b1_ref[...]))
    h = jnp.sin(fr * (hp(h, w2_ref[...]) + b2_ref[...]))
    w3 = w3_ref[...]
    dl = dl_ref[...]
    ss = jnp.zeros(ss_ref.shape, F32)
    for part in range(2):
        k = hp(h, w3[part]) * jnp.exp(-zf[:, part * HY_PACK:part * HY_PACK + 1] * dl)
        row = i * tile + part * half + lax.broadcasted_iota(jnp.int32, (half, 1), 0)
        k = jnp.where(row == seq_len, 0.0, k)
        k_ref[part * half:(part + 1) * half, :] = k
        ss = ss + jnp.sum(k * k, axis=0, keepdims=True)

    @pl.when(i == 0)
    def _():
        ss_ref[...] = jnp.zeros(ss_ref.shape, F32)

    ss_ref[...] += ss


def _hyena_filter(seq_len, w1, b1, w2, b2, w3, freq):
    dw = w3.shape[1] // 2
    bands = (HY_EMB - 1) // 2
    t01 = jnp.linspace(0.0, 1.0, seq_len, dtype=F32)[:, None]
    w_ang = 2.0 * math.pi * jnp.arange(seq_len, dtype=F32)[:, None] / seq_len
    f = jnp.linspace(1e-4, bands - 1, bands, dtype=F32)[None, :]
    z = jnp.concatenate([t01, jnp.cos(f * w_ang), -jnp.sin(f * w_ang)], axis=-1)
    rev = (seq_len - jnp.arange(seq_len)) % seq_len
    assert HY_HIDDEN == HY_PACK and HY_EMB <= HY_PACK
    tile = _tile(seq_len, 1024)
    tps = seq_len // tile
    half = tile // 2
    zf = jnp.pad(jnp.concatenate([z, z[rev]], axis=0), ((0, 0), (0, HY_PACK - HY_EMB)))
    zf = zf.reshape(2 * tps, 2, half, HY_PACK).transpose(0, 2, 1, 3).reshape(2 * tps * half, LANE)
    zero = jnp.zeros((HY_PACK, HY_PACK), F32)
    blockdiag = lambda w: jnp.block([[w, zero], [zero, w]])
    w1p = blockdiag(jnp.pad(w1, ((0, HY_PACK - HY_EMB), (0, 0))))
    w2p = blockdiag(w2)
    w3d = w3.reshape(HY_HIDDEN, 2, dw).transpose(1, 0, 2)
    zrows = jnp.zeros_like(w3d)
    w3p = jnp.stack([jnp.concatenate([w3d, zrows], axis=1), jnp.concatenate([zrows, w3d], axis=1)], axis=1)
    vec = lambda v: jnp.concatenate([v, v]).reshape(1, LANE)
    deltas = jnp.abs(jnp.linspace(math.log(HY_TARGET) / HY_FAST_DECAY, math.log(HY_TARGET) / HY_SLOW_DECAY,
                                  dw, dtype=F32)).reshape(1, dw)
    full = lambda shape: pl.BlockSpec(shape, lambda i: (0, 0))
    return pl.pallas_call(
        functools.partial(_hyfilt_kernel, seq_len, tile),
        grid=(2 * tps,),
        in_specs=[pl.BlockSpec((half, LANE), lambda i: (i, 0)),
                  full((LANE, LANE)), full((1, LANE)), full((LANE, LANE)), full((1, LANE)),
                  pl.BlockSpec((None, 2, LANE, dw), lambda i: (i // tps, 0, 0, 0)),
                  full((1, LANE)), full((1, dw))],
        out_specs=[pl.BlockSpec((tile, dw), lambda i: (i, 0)), full((1, dw))],
        out_shape=[jax.ShapeDtypeStruct((2 * seq_len, dw), F32), jax.ShapeDtypeStruct((1, dw), F32)],
        compiler_params=_cp("arbitrary"),
        name="hyena_filter",
    )(zf, w1p, vec(b1), w2p, vec(b2), w3p, vec(freq), deltas)


def _dft_tables(n1, n2):
    n = n1 * n2
    nf = min(n1, -(-(n1 // 2 + 1) // DFT_BLOCKS) * DFT_BLOCKS)
    two_pi = 2.0 * math.pi

    def cs(num, den):
        ang = (num % den).astype(F32) * (two_pi / den)
        return jnp.cos(ang), jnp.sin(ang)

    a = jnp.arange(n1)
    c1, s1 = cs(a[:, None] * a[None, :], n1)
    kron = lambda mat: jnp.kron(mat, jnp.eye(SUBLANE, dtype=F32))
    fwd_a = jnp.concatenate([kron(c1[:nf]), kron(-s1[:nf])], axis=0)
    wgt = jnp.where((a == 0) | (a == n1 // 2), 1.0, jnp.where(a < n1 // 2, 2.0, 0.0))[None, :nf] / n
    inv_a = jnp.concatenate([kron(c1[: n1 // 2, :nf] * wgt), kron(-s1[: n1 // 2, :nf] * wgt)], axis=1)
    k2 = jnp.arange(n2)
    num = (k2[None, :, None] * k2[None, None, :] * n1 + a[:nf, None, None] * k2[None, None, :])
    cb, sb = cs(num, n)
    fwd_b = jnp.concatenate([jnp.concatenate([cb, sb], axis=2),
                             jnp.concatenate([-sb, cb], axis=2)], axis=1)
    cbt, sbt = jnp.swapaxes(cb, 1, 2), jnp.swapaxes(sb, 1, 2)
    inv_b = jnp.concatenate([jnp.concatenate([cbt, -sbt], axis=2),
                             jnp.concatenate([sbt, cbt], axis=2)], axis=1)
    return tuple(m.astype(BF16) for m in (fwd_a, inv_a, fwd_b, inv_b)), nf


def _rows(x):
    a, g, c = x.shape
    return x.reshape(a * g, c)


def _dft_a_kernel(f_ref, x_ref, re_ref, im_ref):
    y = _dot(f_ref[...], _rows(x_ref[...]).astype(BF16))
    half = y.shape[0] // 2
    re_ref[...] = y[:half].reshape(re_ref.shape)
    im_ref[...] = y[half:].reshape(im_ref.shape)


def _dft_stage_a(xv, fwd_a, nf):
    b, k, n2, ch = xv.shape
    groups = n2 // SUBLANE
    out = jax.ShapeDtypeStruct((b, nf, n2, ch), F32)
    return pl.pallas_call(
        _dft_a_kernel,
        grid=(b, groups),
        in_specs=[pl.BlockSpec((2 * nf * SUBLANE, k * SUBLANE), lambda bi, j: (0, 0)),
                  pl.BlockSpec((None, k, SUBLANE, ch), lambda bi, j: (bi, 0, j, 0))],
        out_specs=[pl.BlockSpec((None, nf, SUBLANE, ch), lambda bi, j: (bi, 0, j, 0))] * 2,
        out_shape=[out, out],
        compiler_params=_cp("parallel", "parallel"),
        name="dft_stage_a",
    )(fwd_a[:, :k * SUBLANE], xv)


def _dft_b_kernel(m_ref, re_ref, im_ref, ss_ref, kr_ref, ki_ref):
    nrm = lax.rsqrt(ss_ref[...] + EPS)
    n2 = m_ref.shape[1] // 2
    for k in range(m_ref.shape[0]):
        rows = slice(k * n2, (k + 1) * n2)
        x = _dot(m_ref[k], jnp.concatenate([re_ref[rows], im_ref[rows]], axis=0).astype(BF16))
        kr_ref[rows] = x[:n2] * nrm
        ki_ref[rows] = x[n2:] * nrm


def _filter_spectrum(kern, sumsq, tabs, nf, n1, n2):
    fwd_a, _, fwd_b, _ = tabs
    ch = kern.shape[1]
    are, aim = _dft_stage_a(kern.reshape(1, n1, n2, ch), fwd_a, nf)
    are, aim = are.reshape(nf * n2, ch), aim.reshape(nf * n2, ch)
    kb = _tile(nf, DFT_BLOCKS)
    blk = pl.BlockSpec((kb * n2, ch), lambda i: (i, 0))
    out = jax.ShapeDtypeStruct((nf * n2, ch), F32)
    return pl.pallas_call(
        _dft_b_kernel,
        grid=(nf // kb,),
        in_specs=[pl.BlockSpec((kb, 2 * n2, 2 * n2), lambda i: (i, 0, 0)), blk, blk,
                  pl.BlockSpec((1, ch), lambda i: (0, 0))],
        out_specs=[blk, blk],
        out_shape=[out, out],
        compiler_params=_cp("parallel"),
        name="filter_spectrum",
    )(fwd_b, are, aim, sumsq)


def _dft_mul_kernel(f_ref, g_ref, re_ref, im_ref, kr_ref, ki_ref, qr_ref, qi_ref):
    n2 = f_ref.shape[1] // 2
    for k in range(f_ref.shape[0]):
        rows = slice(k * n2, (k + 1) * n2)
        x = _dot(f_ref[k], jnp.concatenate([re_ref[rows], im_ref[rows]], axis=0).astype(BF16))
        xr, xi = x[:n2], x[n2:]
        kr, ki = kr_ref[rows], ki_ref[rows]
        prod = jnp.concatenate([xr * kr - xi * ki, xr * ki + xi * kr], axis=0)
        q = _dot(g_ref[k], prod.astype(BF16))
        qr_ref[rows] = q[:n2]
        qi_ref[rows] = q[n2:]


def _dft_inv_kernel(g_ref, qr_ref, qi_ref, z_ref, x2_ref, bias_ref, o_ref):
    q = jnp.concatenate([_rows(qr_ref[...]), _rows(qi_ref[...])], axis=0)
    y = _dot(g_ref[...], q.astype(BF16)).reshape(z_ref.shape)
    o_ref[...] = x2_ref[...] * (y + z_ref[...] * bias_ref[...])


def _hyena_long(z, x2, kr, ki, bias, tabs, batch, seq_len, nf, n1, n2):
    fwd_a, inv_a, fwd_b, inv_b = tabs
    ch = z.shape[1]
    zv = z.reshape(batch, n1 // 2, n2, ch)
    are, aim = _dft_stage_a(zv, fwd_a, nf)
    are, aim = are.reshape(batch, nf * n2, ch), aim.reshape(batch, nf * n2, ch)
    kb = _tile(nf, DFT_BLOCKS)
    mat = pl.BlockSpec((kb, 2 * n2, 2 * n2), lambda i, b: (i, 0, 0))
    dat = pl.BlockSpec((None, kb * n2, ch), lambda i, b: (b, i, 0))
    spec = pl.BlockSpec((kb * n2, ch), lambda i, b: (i, 0))
    out = jax.ShapeDtypeStruct((batch, nf * n2, ch), F32)
    qr, qi = pl.pallas_call(
        _dft_mul_kernel,
        grid=(nf // kb, batch),
        in_specs=[mat, mat, dat, dat, spec, spec],
        out_specs=[dat, dat],
        out_shape=[out, out],
        compiler_params=_cp("parallel", "parallel"),
        name="dft_filter_mul",
    )(fwd_b, inv_b, are, aim, kr, ki)
    qr, qi = qr.reshape(batch, nf, n2, ch), qi.reshape(batch, nf, n2, ch)
    full = pl.BlockSpec((None, nf, SUBLANE, ch), lambda b, j: (b, 0, j, 0))
    half = pl.BlockSpec((None, n1 // 2, SUBLANE, ch), lambda b, j: (b, 0, j, 0))
    o = pl.pallas_call(
        _dft_inv_kernel,
        grid=(batch, n2 // SUBLANE),
        in_specs=[pl.BlockSpec(inv_a.shape, lambda b, j: (0, 0)), full, full, half, half,
                  pl.BlockSpec((1, ch), lambda b, j: (0, 0))],
        out_specs=half,
        out_shape=jax.ShapeDtypeStruct((batch, n1 // 2, n2, ch), F32),
        compiler_params=_cp("parallel", "parallel"),
        name="dft_inverse_gate",
    )(inv_a, qr, qi, zv, x2.reshape(batch, n1 // 2, n2, ch), bias.reshape(1, ch))
    return o.reshape(batch * seq_len, ch)


def _hyena_short_kernel(f_ref, g_ref, k_ref, ss_ref, z_ref, x2_ref, bias_ref, o_ref):
    f = f_ref[...]
    z = z_ref[...]
    seq = z.shape[0]
    ks = _dot(f, k_ref[...].astype(BF16)) * lax.rsqrt(ss_ref[...] + EPS)
    zs = _dot(f[:, :seq], z.astype(BF16))
    n = ks.shape[0] // 2
    kr, ki, zr, zi = ks[:n], ks[n:], zs[:n], zs[n:]
    prod = jnp.concatenate([zr * kr - zi * ki, zr * ki + zi * kr], axis=0)
    y = _dot(g_ref[...], prod.astype(BF16))
    o_ref[...] = x2_ref[...] * (y + z * bias_ref[...])


def _hyena_short(z, x2, kern, sumsq, bias, batch, seq_len):
    n = 2 * seq_len
    ch = z.shape[1]
    a = jnp.arange(n)
    ang = ((a[:, None] * a[None, :]) % n).astype(F32) * (2.0 * math.pi / n)
    c, s = jnp.cos(ang), jnp.sin(ang)
    f = jnp.concatenate([c, -s], axis=0).astype(BF16)
    g = (jnp.concatenate([c, -s], axis=1)[:seq_len] / n).astype(BF16)
    full = lambda shape: pl.BlockSpec(shape, lambda b: (0, 0))
    dat = pl.BlockSpec((seq_len, ch), lambda b: (b, 0))
    return pl.pallas_call(
        _hyena_short_kernel,
        grid=(batch,),
        in_specs=[full((2 * n, n)), full((seq_len, 2 * n)),
                  full((n, ch)), full((1, ch)), dat, dat, full((1, ch))],
        out_specs=dat,
        out_shape=jax.ShapeDtypeStruct((batch * seq_len, ch), F32),
        compiler_params=_cp("parallel"),
        name="hyena_short_conv",
    )(f, g, kern, sumsq, z, x2, bias.reshape(1, ch))


def _hyena_mixer(p, col0, width, batch, seq_len, conv_w, conv_b, w1, b1, w2, b2, w3, freq, bias):
    z, x2 = _hyena_pre(p, col0, width, seq_len, conv_w, conv_b)
    kern, sumsq = _hyena_filter(seq_len, w1, b1, w2, b2, w3, freq)
    n = 2 * seq_len
    if n <= 4 * DFT_RADIX:
        return _hyena_short(z, x2, kern, sumsq, bias, batch, seq_len)
    n2 = DFT_RADIX
    n1 = n // n2
    tabs, nf = _dft_tables(n1, n2)
    kr, ki = _filter_spectrum(kern, sumsq, tabs, nf, n1, n2)
    return _hyena_long(z, x2, kr, ki, bias, tabs, batch, seq_len, nf, n1, n2)


def _outproj_kernel(x_ref, a_ref, b_ref, c_ref, d_ref, w_ref, g_ref, o_ref):
    acc = None
    row = 0
    for m_ref in (a_ref, b_ref, c_ref, d_ref):
        wdt = m_ref.shape[1]
        part = _dot(m_ref[...].astype(BF16), w_ref[row:row + wdt, :])
        acc = part if acc is None else acc + part
        row += wdt
    o_ref[...] = x_ref[...] + g_ref[...] * acc


def _out_projection(x2d, mixers, w, gate, rows_per_group):
    n, d = x2d.shape
    tm = _tile(rows_per_group, 512)
    tpg = rows_per_group // tm
    return pl.pallas_call(
        _outproj_kernel,
        grid=(n // tm,),
        in_specs=[pl.BlockSpec((tm, d), lambda i: (i, 0))]
        + [pl.BlockSpec((tm, m.shape[1]), lambda i: (i, 0)) for m in mixers]
        + [pl.BlockSpec(w.shape, lambda i: (0, 0)),
           pl.BlockSpec((None, 1, d), lambda i: (i // tpg, 0, 0))],
        out_specs=pl.BlockSpec((tm, d), lambda i: (i, 0)),
        out_shape=jax.ShapeDtypeStruct((n, d), F32),
        compiler_params=_cp("parallel"),
        name="out_proj",
    )(x2d, *mixers, w, gate)


def _ffn_kernel(seq_len, tile, x_ref, xp_ref, xn_ref, g_ref, sc_ref, sh_ref, gate_ref,
                wg_ref, wu_ref, cwg_ref, cwu_ref, cbg_ref, cbu_ref, wd_ref, o_ref, h_scr, acc):
    j = pl.program_id(1)

    @pl.when(j == 0)
    def _():
        ext = jnp.concatenate([xp_ref[...], x_ref[...], xn_ref[...]], axis=0)
        h = _norm_mod(ext, g_ref[...], sc_ref[...], sh_ref[...])
        tps = seq_len // tile
        outside = _outside_rows(tile, pl.program_id(0) % tps, tps)
        h_scr[...] = jnp.where(outside, 0.0, h).astype(BF16)
        acc[...] = jnp.zeros(acc.shape, F32)

    h = h_scr[...]
    gate = _conv3(_dot(h, wg_ref[...]), cwg_ref[...], cbg_ref[...], tile)
    up = _conv3(_dot(h, wu_ref[...]), cwu_ref[...], cbu_ref[...], tile)
    act = (gate * jax.nn.sigmoid(gate)) * up
    acc[...] += _dot(act.astype(BF16), wd_ref[...])

    @pl.when(j == pl.num_programs(1) - 1)
    def _():
        o_ref[...] = x_ref[...] + gate_ref[...] * acc[...]


def _conv_ffn(x2d, g, sc, sh, gate, w_up, conv_w, conv_b, w_down, seq_len, rows_per_group):
    n, d = x2d.shape
    f = w_down.shape[0]
    tile = _tile(seq_len, 1024)
    hc = _tile(f, 512)
    nh = f // hc
    once = pl.Buffered(1)
    tpg = rows_per_group // tile
    r = tile // HALO
    last = n // HALO - 1
    vec = pl.BlockSpec((None, 1, d), lambda i, j: (i // tpg, 0, 0))
    cb = conv_b.reshape(1, 2 * f)
    return pl.pallas_call(
        functools.partial(_ffn_kernel, seq_len, tile),
        grid=(n // tile, nh),
        in_specs=[
            pl.BlockSpec((tile, d), lambda i, j: (i, 0), pipeline_mode=once),
            pl.BlockSpec((HALO, d), lambda i, j: (jnp.maximum(i * r - 1, 0), 0)),
            pl.BlockSpec((HALO, d), lambda i, j: (jnp.minimum(i * r + r, last), 0)),
            pl.BlockSpec((1, d), lambda i, j: (0, 0)),
            vec, vec, vec,
            pl.BlockSpec((d, hc), lambda i, j: (0, j)),
            pl.BlockSpec((d, hc), lambda i, j: (0, nh + j)),
            pl.BlockSpec((3, hc), lambda i, j: (0, j)),
            pl.BlockSpec((3, hc), lambda i, j: (0, nh + j)),
            pl.BlockSpec((1, hc), lambda i, j: (0, j)),
            pl.BlockSpec((1, hc), lambda i, j: (0, nh + j)),
            pl.BlockSpec((hc, d), lambda i, j: (j, 0)),
        ],
        out_specs=pl.BlockSpec((tile, d), lambda i, j: (i, 0), pipeline_mode=once),
        out_shape=jax.ShapeDtypeStruct((n, d), F32),
        scratch_shapes=[pltpu.VMEM((tile + 2 * HALO, d), BF16), pltpu.VMEM((tile, d), F32)],
        compiler_params=_cp("parallel", "arbitrary"),
        name="conv_ffn",
    )(x2d, x2d, x2d, g, sc, sh, gate, w_up, w_up, conv_w, conv_w, cb, cb, w_down)


def kernel(x, c, ctx, c_ctx, w_mod, b_mod, norm1_g, norm2_g, w_in, w_out, qk_gain, diff_lam, diff_subln,
           win_sink, pool_w, pool_scale, hy_conv_w, hy_conv_b, hy_w1, hy_b1, hy_w2, hy_b2, hy_w3, hy_freq,
           hy_bias, ffn_w_in, ffn_conv_w, ffn_conv_b, ffn_w_out):
    batch, seq, d = x.shape
    clen = ctx.shape[1]
    depth = w_in.shape[0]
    a_heads = d // 512
    a_width = a_heads * 2 * HEAD_DIM
    b_heads = d // 256
    b_kv = b_heads // 4
    b_width = b_heads * HEAD_DIM
    b_kvw = b_kv * HEAD_DIM
    c_width = d // 4
    d_width = d - a_width - b_width - c_width
    off_ka = a_width
    off_va = 2 * a_width
    off_qb = 3 * a_width
    off_kb = off_qb + b_width
    off_vb = off_kb + b_kvw
    off_pool = off_vb + b_kvw
    off_hy = off_pool + c_width
    in_cols = off_hy + 3 * d_width
    order = ((off_hy, in_cols), (off_pool, off_hy), (off_va, off_qb), (0, off_ka), (off_ka, off_va),
             (off_qb, off_kb), (off_kb, off_vb), (off_vb, off_pool))
    n_hy, n_pool, n_va, n_qa, n_ka, n_qb, n_kb, n_vb = (
        sum(e - s for s, e in order[:k]) for k in range(len(order)))
    assert a_width % LANE == 0 and b_width % LANE == 0 and b_kvw == LANE and c_width // 4 == LANE
    assert seq % BLOCK == 0 and clen % BLOCK == 0 and n_pool % c_width == 0
    assert (n_ka - n_qa) == a_width and (n_qb - n_qa) % b_width == 0 and n_kb - n_qb == b_width

    rows = -(-(batch + 1) // SUBLANE) * SUBLANE
    cvec = jnp.zeros((rows, d), F32).at[:batch].set(c).at[batch].set(c_ctx)
    mods = _modulation(cvec, w_mod, b_mod)


    cos_t, sin_t = _rope_tables(seq)
    scale = HEAD_DIM ** -0.5
    tile2 = lambda v: jnp.tile(v, LANE // HEAD_DIM)

    xs = x.reshape(batch * seq, d)
    cs = ctx.reshape(batch * clen, d)
    for l in range(depth):
        last = l == depth - 1
        m = mods[l]
        sh1, sc1, g1, sh2, sc2, g2 = [m[:batch, i * d:(i + 1) * d].reshape(batch, 1, d) for i in range(6)]
        csh1, csc1, cg1, csh2, csc2, cg2 = [m[batch, i * d:(i + 1) * d].reshape(1, 1, d) for i in range(6)]
        n1g = norm1_g[l].reshape(1, d)
        n2g = norm2_g[l].reshape(1, d)
        w_in_l = jnp.concatenate([w_in[l, :, s:e] for s, e in order], axis=-1).astype(BF16)
        w_out_l = w_out[l].astype(BF16)
        ffn_in_l = ffn_w_in[l].astype(BF16)
        ffn_out_l = ffn_w_out[l].astype(BF16)
        pool_w_l = pool_w[l].astype(BF16)

        p = _in_projection(xs, n1g, sc1, sh1, w_in_l, seq)
        pc = _in_projection(cs, n1g, csc1, csh1, w_in_l, batch * clen)

        na = a_width // LANE
        nbq = b_width // LANE
        gains = jnp.stack([tile2(qk_gain[l, 0]) * (scale * LOG2E)] * na + [tile2(qk_gain[l, 1])] * na
                          + [tile2(qk_gain[l, 2]) * (scale * LOG2E)] * nbq + [tile2(qk_gain[l, 3])])
        qk = _qk_prep(p, n_qa // LANE, gains, cos_t, sin_t, seq, True)
        qk_c = _qk_prep(pc, n_qa // LANE, gains, cos_t, sin_t, clen, False)
        kb_blk = 2 * na + nbq
        qb_blk = (n_qb - n_qa) // b_width

        lam_p = diff_lam[l].astype(F32)
        lambda_init = 0.8 - 0.6 * math.exp(-0.3 * l)
        lam = jnp.exp(jnp.sum(lam_p[0] * lam_p[1])) - jnp.exp(jnp.sum(lam_p[2] * lam_p[3])) + lambda_init
        lam_v = jnp.full((1, LANE), lam, F32)
        subg = diff_subln[l].reshape(1, LANE)
        sink_tab = jnp.broadcast_to(win_sink[l].astype(F32)[:, None], (b_heads, LANE))
        hy = (hy_conv_w[l], hy_conv_b[l], hy_w1[l], hy_b1[l], hy_w2[l], hy_b2[l], hy_w3[l], hy_freq[l],
              hy_bias[l])

        vt = _transposed_values(p, n_va // LANE, a_heads, seq, _tile(seq, 512))
        vt_c = _transposed_values(pc, n_va // LANE, a_heads, clen, clen)
        o_a = _diff_attention(qk, seq, qk_c, vt_c, clen, qk, vt, seq, lam_v, subg, 1.0 - lambda_init,
                              a_heads, na)
        vtb = _transposed_values(p, n_vb // LANE, 1, seq, BLOCK)
        vtb_c = _transposed_values(pc, n_vb // LANE, 1, clen, clen)
        o_b = _window_attention(qk, seq, qk_c, vtb_c, clen, vtb, sink_tab, b_heads, b_kv,
                                qb_blk, kb_blk, True)
        o_c = _pool_mixer(p, n_pool, c_width, seq, pool_w_l, pool_scale[l].reshape(1, c_width))
        o_d = _hyena_mixer(p, n_hy, 3 * d_width, batch, seq, *hy)
        xs = _out_projection(xs, (o_a, o_b, o_c, o_d), w_out_l, g1, seq)
        xs = _conv_ffn(xs, n2g, sc2, sh2, g2, ffn_in_l, ffn_conv_w[l], ffn_conv_b[l], ffn_out_l,
                       seq, seq)

        if not last:
            o_a_c = _diff_attention(qk_c, clen, qk_c, vt_c, clen, None, None, 0, lam_v, subg,
                                    1.0 - lambda_init, a_heads, na)
            o_b_c = _window_attention(qk_c, clen, qk_c, vtb_c, clen, None, sink_tab, b_heads, b_kv,
                                      qb_blk, kb_blk, False)
            o_c_c = _pool_mixer(pc, n_pool, c_width, clen, pool_w_l, pool_scale[l].reshape(1, c_width))
            o_d_c = _hyena_mixer(pc, n_hy, 3 * d_width, batch, clen, *hy)
            cs = _out_projection(cs, (o_a_c, o_b_c, o_c_c, o_d_c), w_out_l, cg1, batch * clen)
            cs = _conv_ffn(cs, n2g, csc2, csh2, cg2, ffn_in_l, ffn_conv_w[l], ffn_conv_b[l],
                           ffn_out_l, clen, batch * clen)
    return xs.reshape(batch, seq, d)
```

```python
import functools
import math

import jax
import jax.numpy as jnp
from jax import lax
from jax.experimental import pallas as pl
from jax.experimental.pallas import tpu as pltpu

F32 = jnp.float32
BF16 = jnp.bfloat16

HEAD_DIM = 64
GRID_W = 64
ROPE_BASE = 10000.0
EPS = 1e-6
NEG_INF = -1e30
WINDOW = 128
BLOCK = 128
POOL_WINDOWS = (2, 4, 8, 16)
HY_EMB = 33
HY_HIDDEN = 64
HY_FAST_DECAY = 0.3
HY_SLOW_DECAY = 1.5
HY_TARGET = 1e-2

LANE = 128
SUBLANE = 8
HALO = SUBLANE
DFT_RADIX = 128
DFT_BLOCKS = 8
VT_ROWS = LANE + 16
LOG2E = 1.4426950408889634
HY_PACK = LANE // 2
VMEM_LIMIT = 56 * 1024 * 1024


def _cp(*sem):
    return pltpu.CompilerParams(dimension_semantics=sem, vmem_limit_bytes=VMEM_LIMIT)


def _tile(n, pref):
    if n <= pref:
        return n
    t = pref
    while n % t:
        t //= 2
    assert t >= SUBLANE, (n, pref)
    return t


def _dot(a, b):
    return jnp.dot(a, b, preferred_element_type=F32)


def _dot_nt(a, b):
    return lax.dot_general(a, b, (((1,), (1,)), ((), ())), preferred_element_type=F32)


def _mod_kernel(c_ref, w_ref, b_ref, o_ref):
    c = c_ref[...]
    a = (c * jax.nn.sigmoid(c)).astype(BF16)
    o_ref[...] = _dot(a, w_ref[...].astype(BF16)) + b_ref[...]


def _modulation(cvec, w_mod, b_mod):
    depth, d, n = w_mod.shape
    rows = cvec.shape[0]
    tn = _tile(n, 1024)
    return pl.pallas_call(
        _mod_kernel,
        grid=(depth, n // tn),
        in_specs=[
            pl.BlockSpec((rows, d), lambda l, j: (0, 0)),
            pl.BlockSpec((None, d, tn), lambda l, j: (l, 0, j)),
            pl.BlockSpec((None, 1, tn), lambda l, j: (l, 0, j)),
        ],
        out_specs=pl.BlockSpec((None, rows, tn), lambda l, j: (l, 0, j)),
        out_shape=jax.ShapeDtypeStruct((depth, rows, n), F32),
        compiler_params=_cp("parallel", "parallel"),
        name="adaln_mod",
    )(cvec, w_mod, b_mod.reshape(depth, 1, n))


def _norm_mod(x, g, sc, sh):
    ms = jnp.mean(x * x, axis=-1, keepdims=True)
    return (x * lax.rsqrt(ms + EPS) * g) * (1.0 + sc) + sh


def _inproj_kernel(x_ref, g_ref, sc_ref, sh_ref, w_ref, o_ref):
    h = _norm_mod(x_ref[...], g_ref[...], sc_ref[...], sh_ref[...])
    o_ref[...] = _dot(h.astype(BF16), w_ref[...])


def _in_projection(x2d, g, sc, sh, w, layer, rows_per_group):
    n, d = x2d.shape
    nc = w.shape[2]
    tm = _tile(rows_per_group, 512)
    tpg = rows_per_group // tm
    vec = pl.BlockSpec((None, 1, d), lambda i: (i // tpg, 0, 0))
    return pl.pallas_call(
        _inproj_kernel,
        grid=(n // tm,),
        in_specs=[
            pl.BlockSpec((tm, d), lambda i: (i, 0)),
            pl.BlockSpec((1, d), lambda i: (0, 0)),
            vec, vec,
            pl.BlockSpec((None, d, nc), lambda i: (layer, 0, 0), pipeline_mode=pl.Buffered(1)),
        ],
        out_specs=pl.BlockSpec((tm, nc), lambda i: (i, 0)),
        out_shape=jax.ShapeDtypeStruct((n, nc), F32),
        compiler_params=_cp("parallel"),
        name="in_proj",
    )(x2d, g, sc, sh, w)


def _qkprep_kernel(rope, x_ref, gain_ref, cos_ref, sin_ref, gm_ref, pm_ref, o_ref):
    x = x_ref[...]
    ms = _dot((x * x).astype(BF16), gm_ref[...])
    y = x * lax.rsqrt(ms + EPS) * gain_ref[...]
    if rope:
        cos = jnp.concatenate([cos_ref[...]] * 2, axis=1)
        sin = jnp.concatenate([sin_ref[...]] * 2, axis=1)
        y = y * cos + _dot(y.astype(BF16), pm_ref[...]) * sin
    o_ref[...] = y.astype(BF16)


def _qk_prep(p, col_block0, gains, cos_t, sin_t, seq_len, rope):
    n = p.shape[0]
    nblk = gains.shape[0]
    wide = 2 * LANE
    nw = -(-nblk // 2)
    assert col_block0 % 2 == 0 and (col_block0 + 2 * nw) * LANE <= p.shape[1]
    gains = jnp.concatenate([gains, jnp.ones((2 * nw - nblk, LANE), F32)], axis=0).reshape(nw, 1, wide)
    tm = _tile(seq_len, 2048)
    tps = seq_len // tm
    lane = jnp.arange(wide)
    gm = ((lane[:, None] // HEAD_DIM) == (lane[None, :] // HEAD_DIM)).astype(BF16) * (1.0 / HEAD_DIM)
    partner = jnp.where((lane % 32) < 16, lane + 16, lane - 16)
    pm = (lane[:, None] == partner[None, :]).astype(BF16)
    mat = pl.BlockSpec((wide, wide), lambda i, j: (0, 0))
    tab = pl.BlockSpec((tm, LANE), lambda i, j: (i % tps, 0))
    return pl.pallas_call(
        functools.partial(_qkprep_kernel, rope),
        grid=(n // tm, nw),
        in_specs=[
            pl.BlockSpec((tm, wide), lambda i, j: (i, col_block0 // 2 + j)),
            pl.BlockSpec((None, 1, wide), lambda i, j: (j, 0, 0)),
            tab, tab, mat, mat,
        ],
        out_specs=pl.BlockSpec((tm, wide), lambda i, j: (i, j)),
        out_shape=jax.ShapeDtypeStruct((n, nw * wide), BF16),
        compiler_params=_cp("parallel", "parallel"),
        name="qk_prep",
    )(p, gains, cos_t, sin_t, gm.astype(BF16), pm)


def _rope_tables(seq_len):
    rows = seq_len // GRID_W
    row = jnp.repeat(jnp.arange(rows, dtype=F32), GRID_W)
    col = jnp.tile(jnp.arange(GRID_W, dtype=F32), rows)
    half = HEAD_DIM // 2
    inv = ROPE_BASE ** (-jnp.arange(0, half, 2, dtype=F32) / half)
    ang = jnp.concatenate([row[:, None] * inv, col[:, None] * inv], axis=-1)
    lane = jnp.arange(LANE)
    within = lane % HEAD_DIM
    idx = (within // 32) * 16 + (within % 16)
    sign = jnp.where((lane % 32) < 16, -1.0, 1.0).astype(F32)
    return jnp.cos(ang)[:, idx], jnp.sin(ang)[:, idx] * sign


def _vt_kernel(x_ref, o_ref):
    chunks, _, chunk = o_ref.shape
    row = lax.broadcasted_iota(jnp.int32, (VT_ROWS - LANE, chunk), 0)
    ones_row = jnp.where(row == 0, 1.0, 0.0).astype(BF16)
    for c in range(chunks):
        x_t = x_ref[c * chunk:(c + 1) * chunk, :].T.astype(BF16)
        o_ref[c] = jnp.concatenate([x_t, ones_row], axis=0)


def _transposed_values(p, col_block0, nblk, seq_len, chunk):
    n = p.shape[0]
    rows = max(chunk, _tile(seq_len, 2048))
    return pl.pallas_call(
        _vt_kernel,
        grid=(n // rows, nblk),
        in_specs=[pl.BlockSpec((rows, LANE), lambda i, j: (i, col_block0 + j))],
        out_specs=pl.BlockSpec((None, rows // chunk, VT_ROWS, chunk), lambda i, j: (j, i, 0, 0)),
        out_shape=jax.ShapeDtypeStruct((nblk, n // chunk, VT_ROWS, chunk), BF16),
        compiler_params=_cp("parallel", "parallel"),
        name="values_t",
    )(p)


def _attn_a_kernel(n_lat, tk, sub_scale, *refs):
    if n_lat:
        (q_ref, kc_ref, vc_ref, k_ref, v_ref, lam_ref, subg_ref, o_ref, qt, m, acc,
         s0, s1, x0, x1, p0, p1, al0, al1) = refs
    else:
        q_ref, kc_ref, vc_ref, lam_ref, subg_ref, o_ref, qt, m, acc = refs
    tq = q_ref.shape[0]
    q_t = q_ref[...].astype(F32).T
    row = lax.broadcasted_iota(jnp.int32, q_t.shape, 0)
    qt[:, :tq] = jnp.where(row < HEAD_DIM, q_t, 0.0).astype(BF16)
    qt[:, tq:] = jnp.where(row >= HEAD_DIM, q_t, 0.0).astype(BF16)
    m[...] = jnp.full(m.shape, -jnp.inf, F32)
    acc[...] = jnp.zeros(acc.shape, F32)

    def softmax_update(s, s_max):
        m_prev = m[...]
        m_new = jnp.maximum(m_prev, s_max)
        m[...] = m_new
        return jnp.exp2(s - m_new).astype(BF16), jnp.exp2(m_prev - m_new)

    s_c = _dot(kc_ref[...], qt[...])

    def context_update():
        p_c, alpha_c = softmax_update(s_c, jnp.max(s_c, axis=0, keepdims=True))
        acc[...] = alpha_c * acc[...] + _dot(vc_ref[0], p_c)

    if not n_lat:
        context_update()
    else:
        def scores(c, s_buf, x_buf):
            s = _dot(k_ref[pl.ds(pl.multiple_of(c * tk, tk), tk), :], qt[...])
            s_buf[...] = s
            x_buf[...] = jnp.max(s, axis=0, keepdims=True)

        def softmax(s_buf, x_buf, p_buf, al_buf):
            p_buf[...], al_buf[...] = softmax_update(s_buf[...], x_buf[...])

        def values(c, p_buf, al_buf):
            acc[...] = al_buf[...] * acc[...] + _dot(v_ref[c], p_buf[...])

        scores(0, s0, x0)
        if n_lat == 1:
            context_update()
            softmax(s0, x0, p0, al0)
            values(0, p0, al0)
        else:
            assert n_lat % 2 == 0
            scores(1, s1, x1)
            context_update()
            softmax(s0, x0, p0, al0)

            def body(t, carry):
                c = 2 * t
                scores(c + 2, s0, x0)
                softmax(s1, x1, p1, al1)
                values(c, p0, al0)
                scores(c + 3, s1, x1)
                softmax(s0, x0, p0, al0)
                values(c + 1, p1, al1)
                return carry
            lax.fori_loop(0, n_lat // 2 - 1, body, 0)
            softmax(s1, x1, p1, al1)
            values(n_lat - 2, p0, al0)
            values(n_lat - 1, p1, al1)

    o_t = acc[:LANE] / acc[LANE:LANE + 1]
    o = (o_t[:, :tq] - lam_ref[0:1, 0:1] * o_t[:, tq:]).T
    ms = jnp.mean(o * o, axis=-1, keepdims=True)
    o_ref[...] = ((o * lax.rsqrt(ms + EPS) * subg_ref[...]) * sub_scale).astype(BF16)


def _diff_attention(q_arr, q_len, qk_ctx, vt_ctx, ctx_len, qk_lat, vt_lat, lat_len, lam, subg,
                    sub_scale, heads, k_col0):
    batch = q_arr.shape[0] // q_len
    tq = _tile(q_len, 2048)
    nq = q_len // tq
    in_specs = [
        pl.BlockSpec((tq, LANE), lambda b, h, i: (b * nq + i, h)),
        pl.BlockSpec((ctx_len, LANE), lambda b, h, i: (b, k_col0 + h)),
        pl.BlockSpec((None, 1, VT_ROWS, ctx_len), lambda b, h, i: (h, b, 0, 0)),
    ]
    args = [q_arr, qk_ctx, vt_ctx]
    n_lat, tk = 0, 0
    if qk_lat is not None:
        tk = vt_lat.shape[-1]
        n_lat = lat_len // tk
        in_specs += [
            pl.BlockSpec((lat_len, LANE), lambda b, h, i: (b, k_col0 + h)),
            pl.BlockSpec((None, n_lat, VT_ROWS, tk), lambda b, h, i: (h, b, 0, 0)),
        ]
        args += [qk_lat, vt_lat]
    in_specs += [pl.BlockSpec((1, LANE), lambda b, h, i: (0, 0))] * 2
    args += [lam, subg]
    stat = pltpu.VMEM((1, 2 * tq), F32)
    scratch = [pltpu.VMEM((LANE, 2 * tq), BF16), stat, pltpu.VMEM((VT_ROWS, 2 * tq), F32)]
    if n_lat:
        scratch += [pltpu.VMEM((tk, 2 * tq), F32)] * 2 + [stat] * 2
        scratch += [pltpu.VMEM((tk, 2 * tq), BF16)] * 2 + [stat] * 2
    return pl.pallas_call(
        functools.partial(_attn_a_kernel, n_lat, tk, sub_scale),
        grid=(batch, heads, nq),
        in_specs=in_specs,
        out_specs=pl.BlockSpec((tq, LANE), lambda b, h, i: (b * nq + i, h)),
        out_shape=jax.ShapeDtypeStruct((batch * q_len, heads * LANE), BF16),
        scratch_shapes=scratch,
        compiler_params=_cp("parallel", "parallel", "parallel"),
        name="diff_attn",
    )(*args)


def _attn_b_kernel(has_loc, seq_len, q_heads, group, *refs):
    if has_loc:
        (q_ref, kx_ref, vx_ref, kp_ref, kc_ref, kn_ref, vp_ref, vc_ref, vn_ref, sink_ref, o_ref) = refs
    else:
        (q_ref, kx_ref, vx_ref, sink_ref, o_ref) = refs
    n = pl.program_id(1)
    clen = kx_ref.shape[0]
    q_t = q_ref[...].astype(F32).T
    keys = [kx_ref[...]]
    vals = [vx_ref[0]]
    if has_loc:
        keys += [kp_ref[...], kc_ref[...], kn_ref[...]]
        vals += [vp_ref[0], vc_ref[0], vn_ref[0]]
        kpos = (n - 1) * BLOCK + lax.broadcasted_iota(jnp.int32, (3 * BLOCK, BLOCK), 0)
        qpos = n * BLOCK + lax.broadcasted_iota(jnp.int32, (3 * BLOCK, BLOCK), 1)
        valid = (kpos >= 0) & (kpos < seq_len) & (jnp.abs(kpos - qpos) <= WINDOW)
        valid = jnp.concatenate([valid] * q_heads, axis=1)
    k_all = jnp.concatenate(keys, axis=0)
    v_all = jnp.concatenate(vals, axis=1)
    zeros = jnp.zeros((HEAD_DIM, BLOCK), F32)
    cols = []
    for hq in range(q_heads):
        blk = q_t[hq * HEAD_DIM:(hq + 1) * HEAD_DIM]
        cols.append(jnp.concatenate([blk, zeros] if hq // group == 0 else [zeros, blk], axis=0))
    q_all = jnp.concatenate(cols, axis=1).astype(BF16)
    sink = jnp.concatenate([sink_ref[hq:hq + 1, :] for hq in range(q_heads)], axis=1) * LOG2E
    s = _dot(k_all, q_all)
    if has_loc:
        s = jnp.concatenate([s[:clen], jnp.where(valid, s[clen:], NEG_INF)], axis=0)
    m = jnp.maximum(jnp.max(s, axis=0, keepdims=True), sink)
    o_t = _dot(v_all, jnp.exp2(s - m).astype(BF16))
    den = o_t[LANE:LANE + 1] + jnp.exp2(sink - m)
    heads_out = []
    for hq in range(q_heads):
        kvh = hq // group
        lanes = slice(hq * BLOCK, (hq + 1) * BLOCK)
        heads_out.append(o_t[kvh * HEAD_DIM:(kvh + 1) * HEAD_DIM, lanes] / den[:, lanes])
    o_ref[...] = jnp.concatenate(heads_out, axis=0).T.astype(BF16)


def _window_attention(q_arr, q_len, qk_ctx, vt_ctx, ctx_len, vt_lat, sink_tab, q_heads, kv_heads,
                      q_blk, k_blk, local):
    assert kv_heads == 2, "key/value heads must fill exactly one 128-lane block"
    batch = q_arr.shape[0] // q_len
    nb = q_len // BLOCK
    qw = q_heads * HEAD_DIM
    in_specs = [
        pl.BlockSpec((BLOCK, qw), lambda b, n: (b * nb + n, q_blk)),
        pl.BlockSpec((ctx_len, LANE), lambda b, n: (b, k_blk)),
        pl.BlockSpec((None, 1, VT_ROWS, ctx_len), lambda b, n: (0, b, 0, 0)),
    ]
    args = [q_arr, qk_ctx, vt_ctx]
    if local:
        prev = lambda b, n: b * nb + jnp.maximum(n - 1, 0)
        nxt = lambda b, n: b * nb + jnp.minimum(n + 1, nb - 1)
        in_specs += [
            pl.BlockSpec((BLOCK, LANE), lambda b, n: (prev(b, n), k_blk)),
            pl.BlockSpec((BLOCK, LANE), lambda b, n: (b * nb + n, k_blk)),
            pl.BlockSpec((BLOCK, LANE), lambda b, n: (nxt(b, n), k_blk)),
            pl.BlockSpec((None, 1, VT_ROWS, BLOCK), lambda b, n: (0, prev(b, n), 0, 0)),
            pl.BlockSpec((None, 1, VT_ROWS, BLOCK), lambda b, n: (0, b * nb + n, 0, 0)),
            pl.BlockSpec((None, 1, VT_ROWS, BLOCK), lambda b, n: (0, nxt(b, n), 0, 0)),
        ]
        args += [q_arr] * 3 + [vt_lat] * 3
    in_specs.append(pl.BlockSpec((q_heads, LANE), lambda b, n: (0, 0)))
    args.append(sink_tab)
    return pl.pallas_call(
        functools.partial(_attn_b_kernel, local, q_len, q_heads, q_heads // kv_heads),
        grid=(batch, nb),
        in_specs=in_specs,
        out_specs=pl.BlockSpec((BLOCK, qw), lambda b, n: (b * nb + n, 0)),
        out_shape=jax.ShapeDtypeStruct((batch * q_len, qw), BF16),
        compiler_params=_cp("parallel", "parallel"),
        name="window_attn",
    )(*args)


def _halo_specs(tile, width, col_blk, n_rows):
    r = tile // HALO
    last = n_rows // HALO - 1
    return [
        pl.BlockSpec((tile, width), lambda i: (i, col_blk)),
        pl.BlockSpec((HALO, width), lambda i: (jnp.maximum(i * r - 1, 0), col_blk)),
        pl.BlockSpec((HALO, width), lambda i: (jnp.minimum(i * r + r, last), col_blk)),
    ]


def _outside_rows(tile, tile_idx, tiles_per_seq):
    row = lax.broadcasted_iota(jnp.int32, (tile + 2 * HALO, 1), 0)
    lo = jnp.where(tile_idx == 0, HALO, 0)
    hi = jnp.where(tile_idx == tiles_per_seq - 1, HALO + tile, tile + 2 * HALO)
    return (row < lo) | (row >= hi)


def _with_halo(x, xp, xn, tiles_per_seq):
    ext = jnp.concatenate([xp, x, xn], axis=0)
    outside = _outside_rows(x.shape[0], pl.program_id(0) % tiles_per_seq, tiles_per_seq)
    return jnp.where(outside, 0.0, ext)


def _conv3(ext, w, b, tile):
    rows = ext.shape[0]
    c = (pltpu.roll(ext, 1, 0) * w[0:1] + ext * w[1:2] + pltpu.roll(ext, rows - 1, 0) * w[2:3])
    return c[HALO:HALO + tile] + b


def _pool_kernel(seq_len, tile, x_ref, xp_ref, xn_ref, w_ref, ls_ref, o_ref):
    tps = seq_len // tile
    x = x_ref[...]
    ext = _with_halo(x, xp_ref[...], xn_ref[...], tps)
    rows = tile + 2 * HALO
    t = (pl.program_id(0) % tps) * tile + lax.broadcasted_iota(jnp.int32, (tile, 1), 0)
    gw = x.shape[1] // len(POOL_WINDOWS)
    outs = []
    for g, w in enumerate(POOL_WINDOWS):
        lo = w // 2
        hi = w - 1 - lo
        a = ext[:, g * gw:(g + 1) * gw]
        span = 1
        while span < w:
            a = a + pltpu.roll(a, span, 0)
            span *= 2
        win = (pltpu.roll(a, rows - hi, 0) if hi else a)[HALO:HALO + tile]
        cnt = jnp.minimum(t + hi + 1, seq_len) - jnp.maximum(t - lo, 0)
        d = win / cnt.astype(F32) - x[:, g * gw:(g + 1) * gw]
        outs.append(_dot(d.astype(BF16), w_ref[g]))
    o_ref[...] = (jnp.concatenate(outs, axis=1) * ls_ref[...]).astype(BF16)


def _pool_mixer(p, col0, width, seq_len, w_lin, ls):
    n = p.shape[0]
    tile = _tile(seq_len, 512)
    g, gw, _ = w_lin.shape
    return pl.pallas_call(
        functools.partial(_pool_kernel, seq_len, tile),
        grid=(n // tile,),
        in_specs=_halo_specs(tile, width, col0 // width, n)
        + [pl.BlockSpec((g, gw, gw), lambda i: (0, 0, 0)),
           pl.BlockSpec((1, width), lambda i: (0, 0))],
        out_specs=pl.BlockSpec((tile, width), lambda i: (i, 0)),
        out_shape=jax.ShapeDtypeStruct((n, width), BF16),
        compiler_params=_cp("parallel"),
        name="pool_mixer",
    )(p, p, p, w_lin, ls)


def _hypre_kernel(seq_len, tile, x_ref, xp_ref, xn_ref, cw_ref, cb_ref, z_ref, x2_ref):
    ext = _with_halo(x_ref[...], xp_ref[...], xn_ref[...], seq_len // tile)
    c = _conv3(ext, cw_ref[...], cb_ref[...], tile)
    dw = c.shape[1] // 3
    z_ref[...] = c[:, :dw] * c[:, dw:2 * dw]
    x2_ref[...] = c[:, 2 * dw:]


def _hyena_pre(p, col0, width, seq_len, conv_w, conv_b):
    n = p.shape[0]
    tile = _tile(seq_len, 512)
    dw = width // 3
    out = jax.ShapeDtypeStruct((n, dw), F32)
    return pl.pallas_call(
        functools.partial(_hypre_kernel, seq_len, tile),
        grid=(n // tile,),
        in_specs=_halo_specs(tile, width, col0 // width, n)
        + [pl.BlockSpec((3, width), lambda i: (0, 0)),
           pl.BlockSpec((1, width), lambda i: (0, 0))],
        out_specs=[pl.BlockSpec((tile, dw), lambda i: (i, 0))] * 2,
        out_shape=[out, out],
        compiler_params=_cp("parallel"),
        name="hyena_pre",
    )(p, p, p, conv_w, conv_b.reshape(1, width))


def _hyfilt_kernel(seq_len, tile, zf_ref, w1_ref, b1_ref, w2_ref, b2_ref, w3_ref, fr_ref, dl_ref,
                   k_ref, ss_ref):
    i = pl.program_id(0)
    half = tile // 2
    hp = functools.partial(jnp.dot, precision=lax.Precision.HIGHEST, preferred_element_type=F32)
    zf = zf_ref[...]
    fr = fr_ref[...]
    h = jnp.sin(fr * (hp(zf, w1_ref[...]) + b1_ref[...]))
    h = jnp.sin(fr * (hp(h, w2_ref[...]) + b2_ref[...]))
    w3 = w3_ref[...]
    dl = dl_ref[...]
    ss = jnp.zeros(ss_ref.shape, F32)
    for part in range(2):
        k = hp(h, w3[part]) * jnp.exp(-zf[:, part * HY_PACK:part * HY_PACK + 1] * dl)
        row = i * tile + part * half + lax.broadcasted_iota(jnp.int32, (half, 1), 0)
        k = jnp.where(row == seq_len, 0.0, k)
        k_ref[part * half:(part + 1) * half, :] = k
        ss = ss + jnp.sum(k * k, axis=0, keepdims=True)

    @pl.when(i == 0)
    def _():
        ss_ref[...] = jnp.zeros(ss_ref.shape, F32)

    ss_ref[...] += ss


def _hyena_filter(seq_len, w1, b1, w2, b2, w3, freq):
    dw = w3.shape[1] // 2
    bands = (HY_EMB - 1) // 2
    t01 = jnp.linspace(0.0, 1.0, seq_len, dtype=F32)[:, None]
    w_ang = 2.0 * math.pi * jnp.arange(seq_len, dtype=F32)[:, None] / seq_len
    f = jnp.linspace(1e-4, bands - 1, bands, dtype=F32)[None, :]
    z = jnp.concatenate([t01, jnp.cos(f * w_ang), -jnp.sin(f * w_ang)], axis=-1)
    rev = (seq_len - jnp.arange(seq_len)) % seq_len
    assert HY_HIDDEN == HY_PACK and HY_EMB <= HY_PACK
    tile = _tile(seq_len, 1024)
    tps = seq_len // tile
    half = tile // 2
    zf = jnp.pad(jnp.concatenate([z, z[rev]], axis=0), ((0, 0), (0, HY_PACK - HY_EMB)))
    zf = zf.reshape(2 * tps, 2, half, HY_PACK).transpose(0, 2, 1, 3).reshape(2 * tps * half, LANE)
    zero = jnp.zeros((HY_PACK, HY_PACK), F32)
    blockdiag = lambda w: jnp.block([[w, zero], [zero, w]])
    w1p = blockdiag(jnp.pad(w1, ((0, HY_PACK - HY_EMB), (0, 0))))
    w2p = blockdiag(w2)
    w3d = w3.reshape(HY_HIDDEN, 2, dw).transpose(1, 0, 2)
    zrows = jnp.zeros_like(w3d)
    w3p = jnp.stack([jnp.concatenate([w3d, zrows], axis=1), jnp.concatenate([zrows, w3d], axis=1)], axis=1)
    vec = lambda v: jnp.concatenate([v, v]).reshape(1, LANE)
    deltas = jnp.abs(jnp.linspace(math.log(HY_TARGET) / HY_FAST_DECAY, math.log(HY_TARGET) / HY_SLOW_DECAY,
                                  dw, dtype=F32)).reshape(1, dw)
    full = lambda shape: pl.BlockSpec(shape, lambda i: (0, 0))
    return pl.pallas_call(
        functools.partial(_hyfilt_kernel, seq_len, tile),
        grid=(2 * tps,),
        in_specs=[pl.BlockSpec((half, LANE), lambda i: (i, 0)),
                  full((LANE, LANE)), full((1, LANE)), full((LANE, LANE)), full((1, LANE)),
                  pl.BlockSpec((None, 2, LANE, dw), lambda i: (i // tps, 0, 0, 0)),
                  full((1, LANE)), full((1, dw))],
        out_specs=[pl.BlockSpec((tile, dw), lambda i: (i, 0)), full((1, dw))],
        out_shape=[jax.ShapeDtypeStruct((2 * seq_len, dw), F32), jax.ShapeDtypeStruct((1, dw), F32)],
        compiler_params=_cp("arbitrary"),
        name="hyena_filter",
    )(zf, w1p, vec(b1), w2p, vec(b2), w3p, vec(freq), deltas)


def _dft_tables(n1, n2):
    n = n1 * n2
    nf = min(n1, -(-(n1 // 2 + 1) // DFT_BLOCKS) * DFT_BLOCKS)
    two_pi = 2.0 * math.pi

    def cs(num, den):
        ang = (num % den).astype(F32) * (two_pi / den)
        return jnp.cos(ang), jnp.sin(ang)

    a = jnp.arange(n1)
    c1, s1 = cs(a[:, None] * a[None, :], n1)
    kron = lambda mat: jnp.kron(mat, jnp.eye(SUBLANE, dtype=F32))
    fwd_a = jnp.concatenate([kron(c1[:nf]), kron(-s1[:nf])], axis=0)
    wgt = jnp.where((a == 0) | (a == n1 // 2), 1.0, jnp.where(a < n1 // 2, 2.0, 0.0))[None, :nf] / n
    inv_a = jnp.concatenate([kron(c1[: n1 // 2, :nf] * wgt), kron(-s1[: n1 // 2, :nf] * wgt)], axis=1)
    k2 = jnp.arange(n2)
    num = (k2[None, :, None] * k2[None, None, :] * n1 + a[:nf, None, None] * k2[None, None, :])
    cb, sb = cs(num, n)
    fwd_b = jnp.concatenate([jnp.concatenate([cb, sb], axis=2),
                             jnp.concatenate([-sb, cb], axis=2)], axis=1)
    cbt, sbt = jnp.swapaxes(cb, 1, 2), jnp.swapaxes(sb, 1, 2)
    inv_b = jnp.concatenate([jnp.concatenate([cbt, -sbt], axis=2),
                             jnp.concatenate([sbt, cbt], axis=2)], axis=1)
    return tuple(m.astype(BF16) for m in (fwd_a, inv_a, fwd_b, inv_b)), nf


def _rows(x):
    a, g, c = x.shape
    return x.reshape(a * g, c)


def _dft_a_kernel(f_ref, x_ref, re_ref, im_ref):
    y = _dot(f_ref[...], _rows(x_ref[...]).astype(BF16))
    half = y.shape[0] // 2
    re_ref[...] = y[:half].reshape(re_ref.shape)
    im_ref[...] = y[half:].reshape(im_ref.shape)


def _dft_stage_a(xv, fwd_a, nf):
    b, k, n2, ch = xv.shape
    groups = n2 // SUBLANE
    out = jax.ShapeDtypeStruct((b, nf, n2, ch), F32)
    return pl.pallas_call(
        _dft_a_kernel,
        grid=(b, groups),
        in_specs=[pl.BlockSpec((2 * nf * SUBLANE, k * SUBLANE), lambda bi, j: (0, 0)),
                  pl.BlockSpec((None, k, SUBLANE, ch), lambda bi, j: (bi, 0, j, 0))],
        out_specs=[pl.BlockSpec((None, nf, SUBLANE, ch), lambda bi, j: (bi, 0, j, 0))] * 2,
        out_shape=[out, out],
        compiler_params=_cp("parallel", "parallel"),
        name="dft_stage_a",
    )(fwd_a[:, :k * SUBLANE], xv)


def _dft_b_kernel(m_ref, re_ref, im_ref, ss_ref, kr_ref, ki_ref):
    nrm = lax.rsqrt(ss_ref[...] + EPS)
    n2 = m_ref.shape[1] // 2
    for k in range(m_ref.shape[0]):
        rows = slice(k * n2, (k + 1) * n2)
        x = _dot(m_ref[k], jnp.concatenate([re_ref[rows], im_ref[rows]], axis=0).astype(BF16))
        kr_ref[rows] = x[:n2] * nrm
        ki_ref[rows] = x[n2:] * nrm


def _filter_spectrum(kern, sumsq, tabs, nf, n1, n2):
    fwd_a, _, fwd_b, _ = tabs
    ch = kern.shape[1]
    are, aim = _dft_stage_a(kern.reshape(1, n1, n2, ch), fwd_a, nf)
    are, aim = are.reshape(nf * n2, ch), aim.reshape(nf * n2, ch)
    kb = _tile(nf, DFT_BLOCKS)
    blk = pl.BlockSpec((kb * n2, ch), lambda i: (i, 0))
    out = jax.ShapeDtypeStruct((nf * n2, ch), F32)
    return pl.pallas_call(
        _dft_b_kernel,
        grid=(nf // kb,),
        in_specs=[pl.BlockSpec((kb, 2 * n2, 2 * n2), lambda i: (i, 0, 0)), blk, blk,
                  pl.BlockSpec((1, ch), lambda i: (0, 0))],
        out_specs=[blk, blk],
        out_shape=[out, out],
        compiler_params=_cp("parallel"),
        name="filter_spectrum",
    )(fwd_b, are, aim, sumsq)


def _dft_mul_kernel(f_ref, g_ref, re_ref, im_ref, kr_ref, ki_ref, qr_ref, qi_ref):
    n2 = f_ref.shape[1] // 2
    for k in range(f_ref.shape[0]):
        rows = slice(k * n2, (k + 1) * n2)
        x = _dot(f_ref[k], jnp.concatenate([re_ref[rows], im_ref[rows]], axis=0).astype(BF16))
        xr, xi = x[:n2], x[n2:]
        kr, ki = kr_ref[rows], ki_ref[rows]
        prod = jnp.concatenate([xr * kr - xi * ki, xr * ki + xi * kr], axis=0)
        q = _dot(g_ref[k], prod.astype(BF16))
        qr_ref[rows] = q[:n2]
        qi_ref[rows] = q[n2:]


def _dft_inv_kernel(g_ref, qr_ref, qi_ref, z_ref, x2_ref, bias_ref, o_ref):
    q = jnp.concatenate([_rows(qr_ref[...]), _rows(qi_ref[...])], axis=0)
    y = _dot(g_ref[...], q.astype(BF16)).reshape(z_ref.shape)
    o_ref[...] = x2_ref[...] * (y + z_ref[...] * bias_ref[...])


def _hyena_long(z, x2, kr, ki, bias, tabs, batch, seq_len, nf, n1, n2):
    fwd_a, inv_a, fwd_b, inv_b = tabs
    ch = z.shape[1]
    zv = z.reshape(batch, n1 // 2, n2, ch)
    are, aim = _dft_stage_a(zv, fwd_a, nf)
    are, aim = are.reshape(batch, nf * n2, ch), aim.reshape(batch, nf * n2, ch)
    kb = _tile(nf, DFT_BLOCKS)
    mat = pl.BlockSpec((kb, 2 * n2, 2 * n2), lambda i, b: (i, 0, 0))
    dat = pl.BlockSpec((None, kb * n2, ch), lambda i, b: (b, i, 0))
    spec = pl.BlockSpec((kb * n2, ch), lambda i, b: (i, 0))
    out = jax.ShapeDtypeStruct((batch, nf * n2, ch), F32)
    qr, qi = pl.pallas_call(
        _dft_mul_kernel,
        grid=(nf // kb, batch),
        in_specs=[mat, mat, dat, dat, spec, spec],
        out_specs=[dat, dat],
        out_shape=[out, out],
        compiler_params=_cp("parallel", "parallel"),
        name="dft_filter_mul",
    )(fwd_b, inv_b, are, aim, kr, ki)
    qr, qi = qr.reshape(batch, nf, n2, ch), qi.reshape(batch, nf, n2, ch)
    full = pl.BlockSpec((None, nf, SUBLANE, ch), lambda b, j: (b, 0, j, 0))
    half = pl.BlockSpec((None, n1 // 2, SUBLANE, ch), lambda b, j: (b, 0, j, 0))
    o = pl.pallas_call(
        _dft_inv_kernel,
        grid=(batch, n2 // SUBLANE),
        in_specs=[pl.BlockSpec(inv_a.shape, lambda b, j: (0, 0)), full, full, half, half,
                  pl.BlockSpec((1, ch), lambda b, j: (0, 0))],
        out_specs=half,
        out_shape=jax.ShapeDtypeStruct((batch, n1 // 2, n2, ch), F32),
        compiler_params=_cp("parallel", "parallel"),
        name="dft_inverse_gate",
    )(inv_a, qr, qi, zv, x2.reshape(batch, n1 // 2, n2, ch), bias.reshape(1, ch))
    return o.reshape(batch * seq_len, ch)


def _hyena_short_kernel(f_ref, g_ref, k_ref, ss_ref, z_ref, x2_ref, bias_ref, o_ref):
    f = f_ref[...]
    z = z_ref[...]
    seq = z.shape[0]
    ks = _dot(f, k_ref[...].astype(BF16)) * lax.rsqrt(ss_ref[...] + EPS)
    zs = _dot(f[:, :seq], z.astype(BF16))
    n = ks.shape[0] // 2
    kr, ki, zr, zi = ks[:n], ks[n:], zs[:n], zs[n:]
    prod = jnp.concatenate([zr * kr - zi * ki, zr * ki + zi * kr], axis=0)
    y = _dot(g_ref[...], prod.astype(BF16))
    o_ref[...] = x2_ref[...] * (y + z * bias_ref[...])


def _hyena_short(z, x2, kern, sumsq, bias, batch, seq_len):
    n = 2 * seq_len
    ch = z.shape[1]
    a = jnp.arange(n)
    ang = ((a[:, None] * a[None, :]) % n).astype(F32) * (2.0 * math.pi / n)
    c, s = jnp.cos(ang), jnp.sin(ang)
    f = jnp.concatenate([c, -s], axis=0).astype(BF16)
    g = (jnp.concatenate([c, -s], axis=1)[:seq_len] / n).astype(BF16)
    full = lambda shape: pl.BlockSpec(shape, lambda b: (0, 0))
    dat = pl.BlockSpec((seq_len, ch), lambda b: (b, 0))
    return pl.pallas_call(
        _hyena_short_kernel,
        grid=(batch,),
        in_specs=[full((2 * n, n)), full((seq_len, 2 * n)),
                  full((n, ch)), full((1, ch)), dat, dat, full((1, ch))],
        out_specs=dat,
        out_shape=jax.ShapeDtypeStruct((batch * seq_len, ch), F32),
        compiler_params=_cp("parallel"),
        name="hyena_short_conv",
    )(f, g, kern, sumsq, z, x2, bias.reshape(1, ch))


def _hyena_mixer(p, col0, width, batch, seq_len, conv_w, conv_b, w1, b1, w2, b2, w3, freq, bias):
    z, x2 = _hyena_pre(p, col0, width, seq_len, conv_w, conv_b)
    kern, sumsq = _hyena_filter(seq_len, w1, b1, w2, b2, w3, freq)
    n = 2 * seq_len
    if n <= 4 * DFT_RADIX:
        return _hyena_short(z, x2, kern, sumsq, bias, batch, seq_len)
    n2 = DFT_RADIX
    n1 = n // n2
    tabs, nf = _dft_tables(n1, n2)
    kr, ki = _filter_spectrum(kern, sumsq, tabs, nf, n1, n2)
    return _hyena_long(z, x2, kr, ki, bias, tabs, batch, seq_len, nf, n1, n2)


def _outproj_kernel(x_ref, a_ref, b_ref, c_ref, d_ref, w_ref, g_ref, o_ref):
    acc = None
    row = 0
    for m_ref in (a_ref, b_ref, c_ref, d_ref):
        wdt = m_ref.shape[1]
        part = _dot(m_ref[...].astype(BF16), w_ref[row:row + wdt, :])
        acc = part if acc is None else acc + part
        row += wdt
    o_ref[...] = x_ref[...] + g_ref[...] * acc


def _out_projection(x2d, mixers, w, layer, gate, rows_per_group):
    n, d = x2d.shape
    tm = _tile(rows_per_group, 512)
    tpg = rows_per_group // tm
    return pl.pallas_call(
        _outproj_kernel,
        grid=(n // tm,),
        in_specs=[pl.BlockSpec((tm, d), lambda i: (i, 0))]
        + [pl.BlockSpec((tm, m.shape[1]), lambda i: (i, 0)) for m in mixers]
        + [pl.BlockSpec((None,) + w.shape[1:], lambda i: (layer, 0, 0)),
           pl.BlockSpec((None, 1, d), lambda i: (i // tpg, 0, 0))],
        out_specs=pl.BlockSpec((tm, d), lambda i: (i, 0)),
        out_shape=jax.ShapeDtypeStruct((n, d), F32),
        compiler_params=_cp("parallel"),
        name="out_proj",
    )(x2d, *mixers, w, gate)


def _ffn_kernel(seq_len, tile, x_ref, xp_ref, xn_ref, g_ref, sc_ref, sh_ref, gate_ref,
                wg_ref, wu_ref, cwg_ref, cwu_ref, cbg_ref, cbu_ref, wd_ref, o_ref, h_scr, acc):
    j = pl.program_id(1)

    @pl.when(j == 0)
    def _():
        ext = jnp.concatenate([xp_ref[...], x_ref[...], xn_ref[...]], axis=0)
        h = _norm_mod(ext, g_ref[...], sc_ref[...], sh_ref[...])
        tps = seq_len // tile
        outside = _outside_rows(tile, pl.program_id(0) % tps, tps)
        h_scr[...] = jnp.where(outside, 0.0, h).astype(BF16)
        acc[...] = jnp.zeros(acc.shape, F32)

    h = h_scr[...]
    gate = _conv3(_dot(h, wg_ref[...]), cwg_ref[...], cbg_ref[...], tile)
    up = _conv3(_dot(h, wu_ref[...]), cwu_ref[...], cbu_ref[...], tile)
    act = (gate * jax.nn.sigmoid(gate)) * up
    acc[...] += _dot(act.astype(BF16), wd_ref[...])

    @pl.when(j == pl.num_programs(1) - 1)
    def _():
        o_ref[...] = x_ref[...] + gate_ref[...] * acc[...]


def _conv_ffn(x2d, g, sc, sh, gate, w_up, conv_w, conv_b, w_down, layer, seq_len, rows_per_group):
    n, d = x2d.shape
    f = w_down.shape[1]
    tile = _tile(seq_len, 1024)
    hc = _tile(f, 512)
    nh = f // hc
    once = pl.Buffered(1)
    tpg = rows_per_group // tile
    r = tile // HALO
    last = n // HALO - 1
    vec = pl.BlockSpec((None, 1, d), lambda i, j: (i // tpg, 0, 0))
    cb = conv_b.reshape(1, 2 * f)
    return pl.pallas_call(
        functools.partial(_ffn_kernel, seq_len, tile),
        grid=(n // tile, nh),
        in_specs=[
            pl.BlockSpec((tile, d), lambda i, j: (i, 0), pipeline_mode=once),
            pl.BlockSpec((HALO, d), lambda i, j: (jnp.maximum(i * r - 1, 0), 0)),
            pl.BlockSpec((HALO, d), lambda i, j: (jnp.minimum(i * r + r, last), 0)),
            pl.BlockSpec((1, d), lambda i, j: (0, 0)),
            vec, vec, vec,
            pl.BlockSpec((None, d, hc), lambda i, j: (layer, 0, j)),
            pl.BlockSpec((None, d, hc), lambda i, j: (layer, 0, nh + j)),
            pl.BlockSpec((3, hc), lambda i, j: (0, j)),
            pl.BlockSpec((3, hc), lambda i, j: (0, nh + j)),
            pl.BlockSpec((1, hc), lambda i, j: (0, j)),
            pl.BlockSpec((1, hc), lambda i, j: (0, nh + j)),
            pl.BlockSpec((None, hc, d), lambda i, j: (layer, j, 0)),
        ],
        out_specs=pl.BlockSpec((tile, d), lambda i, j: (i, 0), pipeline_mode=once),
        out_shape=jax.ShapeDtypeStruct((n, d), F32),
        scratch_shapes=[pltpu.VMEM((tile + 2 * HALO, d), BF16), pltpu.VMEM((tile, d), F32)],
        compiler_params=_cp("parallel", "arbitrary"),
        name="conv_ffn",
    )(x2d, x2d, x2d, g, sc, sh, gate, w_up, w_up, conv_w, conv_w, cb, cb, w_down)


def kernel(x, c, ctx, c_ctx, w_mod, b_mod, norm1_g, norm2_g, w_in, w_out, qk_gain, diff_lam, diff_subln,
           win_sink, pool_w, pool_scale, hy_conv_w, hy_conv_b, hy_w1, hy_b1, hy_w2, hy_b2, hy_w3, hy_freq,
           hy_bias, ffn_w_in, ffn_conv_w, ffn_conv_b, ffn_w_out):
    batch, seq, d = x.shape
    clen = ctx.shape[1]
    depth = w_in.shape[0]
    a_heads = d // 512
    a_width = a_heads * 2 * HEAD_DIM
    b_heads = d // 256
    b_kv = b_heads // 4
    b_width = b_heads * HEAD_DIM
    b_kvw = b_kv * HEAD_DIM
    c_width = d // 4
    d_width = d - a_width - b_width - c_width
    off_ka = a_width
    off_va = 2 * a_width
    off_qb = 3 * a_width
    off_kb = off_qb + b_width
    off_vb = off_kb + b_kvw
    off_pool = off_vb + b_kvw
    off_hy = off_pool + c_width
    in_cols = off_hy + 3 * d_width
    order = ((off_hy, in_cols), (off_pool, off_hy), (off_va, off_qb), (0, off_ka), (off_ka, off_va),
             (off_qb, off_kb), (off_kb, off_vb), (off_vb, off_pool))
    n_hy, n_pool, n_va, n_qa, n_ka, n_qb, n_kb, n_vb = (
        sum(e - s for s, e in order[:k]) for k in range(len(order)))
    assert a_width % LANE == 0 and b_width % LANE == 0 and b_kvw == LANE and c_width // 4 == LANE
    assert seq % BLOCK == 0 and clen % BLOCK == 0 and n_pool % c_width == 0
    assert (n_ka - n_qa) == a_width and (n_qb - n_qa) % b_width == 0 and n_kb - n_qb == b_width

    rows = -(-(batch + 1) // SUBLANE) * SUBLANE
    cvec = jnp.zeros((rows, d), F32).at[:batch].set(c).at[batch].set(c_ctx)
    mods = _modulation(cvec, w_mod, b_mod)

    w_in_b = jnp.concatenate([w_in[:, :, s:e] for s, e in order], axis=-1).astype(BF16)
    w_out_b = w_out.astype(BF16)
    ffn_in_b = ffn_w_in.astype(BF16)
    ffn_out_b = ffn_w_out.astype(BF16)

    cos_t, sin_t = _rope_tables(seq)
    scale = HEAD_DIM ** -0.5
    tile2 = lambda v: jnp.tile(v, LANE // HEAD_DIM)

    xs = x.reshape(batch * seq, d)
    cs = ctx.reshape(batch * clen, d)
    for l in range(depth):
        last = l == depth - 1
        m = mods[l]
        sh1, sc1, g1, sh2, sc2, g2 = [m[:batch, i * d:(i + 1) * d].reshape(batch, 1, d) for i in range(6)]
        csh1, csc1, cg1, csh2, csc2, cg2 = [m[batch, i * d:(i + 1) * d].reshape(1, 1, d) for i in range(6)]
        n1g = norm1_g[l].reshape(1, d)
        n2g = norm2_g[l].reshape(1, d)
        pool_w_l = pool_w[l].astype(BF16)

        p = _in_projection(xs, n1g, sc1, sh1, w_in_b, l, seq)
        pc = _in_projection(cs, n1g, csc1, csh1, w_in_b, l, batch * clen)

        na = a_width // LANE
        nbq = b_width // LANE
        gains = jnp.stack([tile2(qk_gain[l, 0]) * (scale * LOG2E)] * na + [tile2(qk_gain[l, 1])] * na
                          + [tile2(qk_gain[l, 2]) * (scale * LOG2E)] * nbq + [tile2(qk_gain[l, 3])])
        qk = _qk_prep(p, n_qa // LANE, gains, cos_t, sin_t, seq, True)
        qk_c = _qk_prep(pc, n_qa // LANE, gains, cos_t, sin_t, clen, False)
        kb_blk = 2 * na + nbq
        qb_blk = (n_qb - n_qa) // b_width

        lam_p = diff_lam[l].astype(F32)
        lambda_init = 0.8 - 0.6 * math.exp(-0.3 * l)
        lam = jnp.exp(jnp.sum(lam_p[0] * lam_p[1])) - jnp.exp(jnp.sum(lam_p[2] * lam_p[3])) + lambda_init
        lam_v = jnp.full((1, LANE), lam, F32)
        subg = diff_subln[l].reshape(1, LANE)
        sink_tab = jnp.broadcast_to(win_sink[l].astype(F32)[:, None], (b_heads, LANE))
        hy = (hy_conv_w[l], hy_conv_b[l], hy_w1[l], hy_b1[l], hy_w2[l], hy_b2[l], hy_w3[l], hy_freq[l],
              hy_bias[l])

        vt = _transposed_values(p, n_va // LANE, a_heads, seq, _tile(seq, 512))
        vt_c = _transposed_values(pc, n_va // LANE, a_heads, clen, clen)
        o_a = _diff_attention(qk, seq, qk_c, vt_c, clen, qk, vt, seq, lam_v, subg, 1.0 - lambda_init,
                              a_heads, na)
        vtb = _transposed_values(p, n_vb // LANE, 1, seq, BLOCK)
        vtb_c = _transposed_values(pc, n_vb // LANE, 1, clen, clen)
        o_b = _window_attention(qk, seq, qk_c, vtb_c, clen, vtb, sink_tab, b_heads, b_kv,
                                qb_blk, kb_blk, True)
        o_c = _pool_mixer(p, n_pool, c_width, seq, pool_w_l, pool_scale[l].reshape(1, c_width))
        o_d = _hyena_mixer(p, n_hy, 3 * d_width, batch, seq, *hy)
        xs = _out_projection(xs, (o_a, o_b, o_c, o_d), w_out_b, l, g1, seq)
        xs = _conv_ffn(xs, n2g, sc2, sh2, g2, ffn_in_b, ffn_conv_w[l], ffn_conv_b[l], ffn_out_b, l,
                       seq, seq)

        if not last:
            o_a_c = _diff_attention(qk_c, clen, qk_c, vt_c, clen, None, None, 0, lam_v, subg,
                                    1.0 - lambda_init, a_heads, na)
            o_b_c = _window_attention(qk_c, clen, qk_c, vtb_c, clen, None, sink_tab, b_heads, b_kv,
                                      qb_blk, kb_blk, False)
            o_c_c = _pool_mixer(pc, n_pool, c_width, clen, pool_w_l, pool_scale[l].reshape(1, c_width))
            o_d_c = _hyena_mixer(pc, n_hy, 3 * d_width, batch, clen, *hy)
            cs = _out_projection(cs, (o_a_c, o_b_c, o_c_c, o_d_c), w_out_b, l, cg1, batch * clen)
            cs = _conv_ffn(cs, n2g, csc2, csh2, cg2, ffn_in_b, ffn_conv_w[l], ffn_conv_b[l],
                           ffn_out_b, l, clen, batch * clen)
    return xs.reshape(batch, seq, d)
```

```python
import functools
import math

import jax
import jax.numpy as jnp
from jax import lax
from jax.experimental import pallas as pl
from jax.experimental.pallas import tpu as pltpu

F32 = jnp.float32
BF16 = jnp.bfloat16

HEAD_DIM = 64
GRID_W = 64
ROPE_BASE = 10000.0
EPS = 1e-6
NEG_INF = -1e30
WINDOW = 128
BLOCK = 128
POOL_WINDOWS = (2, 4, 8, 16)
HY_EMB = 33
HY_HIDDEN = 64
HY_FAST_DECAY = 0.3
HY_SLOW_DECAY = 1.5
HY_TARGET = 1e-2

LANE = 128
SUBLANE = 8
HALO = SUBLANE
DFT_RADIX = 128
DFT_BLOCKS = 8
BF16_ROWS = 2 * SUBLANE
VT_ROWS = LANE + BF16_ROWS
LOG2E = 1.4426950408889634
HY_PACK = LANE // 2
VMEM_LIMIT = 56 * 1024 * 1024


def _cp(*sem):
    return pltpu.CompilerParams(dimension_semantics=sem, vmem_limit_bytes=VMEM_LIMIT)


def _tile(n, pref):
    if n <= pref:
        return n
    t = pref
    while n % t:
        t //= 2
    assert t >= SUBLANE, (n, pref)
    return t


def _dot(a, b):
    return jnp.dot(a, b, preferred_element_type=F32)


def _mod_kernel(c_ref, w_ref, b_ref, o_ref):
    c = c_ref[...]
    a = (c * jax.nn.sigmoid(c)).astype(BF16)
    o_ref[...] = _dot(a, w_ref[...].astype(BF16)) + b_ref[...]


def _modulation(cvec, w_mod, b_mod):
    depth, d, n = w_mod.shape
    rows = cvec.shape[0]
    tn = _tile(n, 1024)
    return pl.pallas_call(
        _mod_kernel,
        grid=(depth, n // tn),
        in_specs=[
            pl.BlockSpec((rows, d), lambda l, j: (0, 0)),
            pl.BlockSpec((None, d, tn), lambda l, j: (l, 0, j)),
            pl.BlockSpec((None, 1, tn), lambda l, j: (l, 0, j)),
        ],
        out_specs=pl.BlockSpec((None, rows, tn), lambda l, j: (l, 0, j)),
        out_shape=jax.ShapeDtypeStruct((depth, rows, n), F32),
        compiler_params=_cp("parallel", "parallel"),
        name="adaln_mod",
    )(cvec, w_mod, b_mod.reshape(depth, 1, n))


def _norm_mod(x, g, sc, sh):
    ms = jnp.mean(x * x, axis=-1, keepdims=True)
    return (x * lax.rsqrt(ms + EPS) * g) * (1.0 + sc) + sh


def _inproj_kernel(x_ref, g_ref, sc_ref, sh_ref, w_ref, o_ref):
    h = _norm_mod(x_ref[...], g_ref[...], sc_ref[...], sh_ref[...])
    o_ref[...] = _dot(h.astype(BF16), w_ref[...])


def _in_projection(x2d, g, sc, sh, w, layer, rows_per_group):
    n, d = x2d.shape
    nc = w.shape[2]
    tm = _tile(rows_per_group, 512)
    tpg = rows_per_group // tm
    vec = pl.BlockSpec((None, 1, d), lambda i: (i // tpg, 0, 0))
    return pl.pallas_call(
        _inproj_kernel,
        grid=(n // tm,),
        in_specs=[
            pl.BlockSpec((tm, d), lambda i: (i, 0)),
            pl.BlockSpec((1, d), lambda i: (0, 0)),
            vec, vec,
            pl.BlockSpec((None, d, nc), lambda i: (layer, 0, 0), pipeline_mode=pl.Buffered(1)),
        ],
        out_specs=pl.BlockSpec((tm, nc), lambda i: (i, 0)),
        out_shape=jax.ShapeDtypeStruct((n, nc), F32),
        compiler_params=_cp("parallel"),
        name="in_proj",
    )(x2d, g, sc, sh, w)


def _qkprep_kernel(rope, x_ref, gain_ref, cos_ref, sin_ref, gm_ref, pm_ref, o_ref):
    x = x_ref[...]
    ms = _dot((x * x).astype(BF16), gm_ref[...])
    y = x * lax.rsqrt(ms + EPS) * gain_ref[...]
    if rope:
        cos = jnp.concatenate([cos_ref[...]] * 2, axis=1)
        sin = jnp.concatenate([sin_ref[...]] * 2, axis=1)
        y = y * cos + _dot(y.astype(BF16), pm_ref[...]) * sin
    o_ref[...] = y.astype(BF16)


def _qk_prep(p, col_block0, gains, cos_t, sin_t, seq_len, rope):
    n = p.shape[0]
    nblk = gains.shape[0]
    wide = 2 * LANE
    nw = -(-nblk // 2)
    assert col_block0 % 2 == 0 and (col_block0 + 2 * nw) * LANE <= p.shape[1]
    gains = jnp.concatenate([gains, jnp.ones((2 * nw - nblk, LANE), F32)], axis=0).reshape(nw, 1, wide)
    tm = _tile(seq_len, 2048)
    tps = seq_len // tm
    lane = jnp.arange(wide)
    gm = ((lane[:, None] // HEAD_DIM) == (lane[None, :] // HEAD_DIM)).astype(BF16) * (1.0 / HEAD_DIM)
    partner = jnp.where((lane % 32) < 16, lane + 16, lane - 16)
    pm = (lane[:, None] == partner[None, :]).astype(BF16)
    mat = pl.BlockSpec((wide, wide), lambda i, j: (0, 0))
    tab = pl.BlockSpec((tm, LANE), lambda i, j: (i % tps, 0))
    return pl.pallas_call(
        functools.partial(_qkprep_kernel, rope),
        grid=(n // tm, nw),
        in_specs=[
            pl.BlockSpec((tm, wide), lambda i, j: (i, col_block0 // 2 + j)),
            pl.BlockSpec((None, 1, wide), lambda i, j: (j, 0, 0)),
            tab, tab, mat, mat,
        ],
        out_specs=pl.BlockSpec((tm, wide), lambda i, j: (i, j)),
        out_shape=jax.ShapeDtypeStruct((n, nw * wide), BF16),
        compiler_params=_cp("parallel", "parallel"),
        name="qk_prep",
    )(p, gains, cos_t, sin_t, gm.astype(BF16), pm)


def _rope_tables(seq_len):
    rows = seq_len // GRID_W
    row = jnp.repeat(jnp.arange(rows, dtype=F32), GRID_W)
    col = jnp.tile(jnp.arange(GRID_W, dtype=F32), rows)
    half = HEAD_DIM // 2
    inv = ROPE_BASE ** (-jnp.arange(0, half, 2, dtype=F32) / half)
    ang = jnp.concatenate([row[:, None] * inv, col[:, None] * inv], axis=-1)
    lane = jnp.arange(LANE)
    within = lane % HEAD_DIM
    idx = (within // 32) * 16 + (within % 16)
    sign = jnp.where((lane % 32) < 16, -1.0, 1.0).astype(F32)
    return jnp.cos(ang)[:, idx], jnp.sin(ang)[:, idx] * sign


def _vt_kernel(x_ref, o_ref):
    chunks, _, chunk = o_ref.shape
    row = lax.broadcasted_iota(jnp.int32, (VT_ROWS - LANE, chunk), 0)
    ones_row = jnp.where(row == 0, 1.0, 0.0).astype(BF16)
    for c in range(chunks):
        x_t = x_ref[c * chunk:(c + 1) * chunk, :].T.astype(BF16)
        o_ref[c] = jnp.concatenate([x_t, ones_row], axis=0)


def _transposed_values(p, col_block0, nblk, seq_len, chunk):
    n = p.shape[0]
    rows = max(chunk, _tile(seq_len, 2048))
    return pl.pallas_call(
        _vt_kernel,
        grid=(n // rows, nblk),
        in_specs=[pl.BlockSpec((rows, LANE), lambda i, j: (i, col_block0 + j))],
        out_specs=pl.BlockSpec((None, rows // chunk, VT_ROWS, chunk), lambda i, j: (j, i, 0, 0)),
        out_shape=jax.ShapeDtypeStruct((nblk, n // chunk, VT_ROWS, chunk), BF16),
        compiler_params=_cp("parallel", "parallel"),
        name="values_t",
    )(p)


def _attn_a_kernel(n_lat, tk, sub_scale, *refs):
    if n_lat:
        (q_ref, kc_ref, vc_ref, k_ref, v_ref, lam_ref, subg_ref, o_ref, qt, m, acc,
         s0, s1, x0, x1, p0, p1, al0, al1) = refs
    else:
        q_ref, kc_ref, vc_ref, lam_ref, subg_ref, o_ref, qt, m, acc = refs
    tq = q_ref.shape[0]
    q_t = q_ref[...].astype(F32).T
    row = lax.broadcasted_iota(jnp.int32, q_t.shape, 0)
    qt[:, :tq] = jnp.where(row < HEAD_DIM, q_t, 0.0).astype(BF16)
    qt[:, tq:] = jnp.where(row >= HEAD_DIM, q_t, 0.0).astype(BF16)
    m[...] = jnp.full(m.shape, -jnp.inf, F32)
    acc[...] = jnp.zeros(acc.shape, F32)

    def softmax_update(s, s_max):
        m_prev = m[...]
        m_new = jnp.maximum(m_prev, s_max)
        m[...] = m_new
        return jnp.exp2(s - m_new).astype(BF16), jnp.exp2(m_prev - m_new)

    s_c = _dot(kc_ref[...], qt[...])

    def context_update():
        p_c, alpha_c = softmax_update(s_c, jnp.max(s_c, axis=0, keepdims=True))
        acc[...] = alpha_c * acc[...] + _dot(vc_ref[0], p_c)

    if not n_lat:
        context_update()
    else:
        def scores(c, s_buf, x_buf):
            s = _dot(k_ref[pl.ds(pl.multiple_of(c * tk, tk), tk), :], qt[...])
            s_buf[...] = s
            x_buf[...] = jnp.max(s, axis=0, keepdims=True)

        def softmax(s_buf, x_buf, p_buf, al_buf):
            p_buf[...], al_buf[...] = softmax_update(s_buf[...], x_buf[...])

        def values(c, p_buf, al_buf):
            acc[...] = al_buf[...] * acc[...] + _dot(v_ref[c], p_buf[...])

        scores(0, s0, x0)
        if n_lat == 1:
            context_update()
            softmax(s0, x0, p0, al0)
            values(0, p0, al0)
        else:
            assert n_lat % 2 == 0
            scores(1, s1, x1)
            context_update()
            softmax(s0, x0, p0, al0)

            def body(t, carry):
                c = 2 * t
                scores(c + 2, s0, x0)
                softmax(s1, x1, p1, al1)
                values(c, p0, al0)
                scores(c + 3, s1, x1)
                softmax(s0, x0, p0, al0)
                values(c + 1, p1, al1)
                return carry
            lax.fori_loop(0, n_lat // 2 - 1, body, 0)
            softmax(s1, x1, p1, al1)
            values(n_lat - 2, p0, al0)
            values(n_lat - 1, p1, al1)

    o_t = acc[:LANE] / acc[LANE:LANE + 1]
    o = (o_t[:, :tq] - lam_ref[0:1, 0:1] * o_t[:, tq:]).T
    ms = jnp.mean(o * o, axis=-1, keepdims=True)
    o_ref[...] = ((o * lax.rsqrt(ms + EPS) * subg_ref[...]) * sub_scale).astype(BF16)


def _diff_attention(q_arr, q_len, qk_ctx, vt_ctx, ctx_len, qk_lat, vt_lat, lat_len, lam, subg,
                    sub_scale, heads, k_col0):
    batch = q_arr.shape[0] // q_len
    tq = _tile(q_len, 2048)
    nq = q_len // tq
    in_specs = [
        pl.BlockSpec((tq, LANE), lambda b, h, i: (b * nq + i, h)),
        pl.BlockSpec((ctx_len, LANE), lambda b, h, i: (b, k_col0 + h)),
        pl.BlockSpec((None, 1, VT_ROWS, ctx_len), lambda b, h, i: (h, b, 0, 0)),
    ]
    args = [q_arr, qk_ctx, vt_ctx]
    n_lat, tk = 0, 0
    if qk_lat is not None:
        tk = vt_lat.shape[-1]
        n_lat = lat_len // tk
        in_specs += [
            pl.BlockSpec((lat_len, LANE), lambda b, h, i: (b, k_col0 + h)),
            pl.BlockSpec((None, n_lat, VT_ROWS, tk), lambda b, h, i: (h, b, 0, 0)),
        ]
        args += [qk_lat, vt_lat]
    in_specs += [pl.BlockSpec((1, LANE), lambda b, h, i: (0, 0))] * 2
    args += [lam, subg]
    stat = pltpu.VMEM((1, 2 * tq), F32)
    scratch = [pltpu.VMEM((LANE, 2 * tq), BF16), stat, pltpu.VMEM((VT_ROWS, 2 * tq), F32)]
    if n_lat:
        scratch += [pltpu.VMEM((tk, 2 * tq), F32)] * 2 + [stat] * 2
        scratch += [pltpu.VMEM((tk, 2 * tq), BF16)] * 2 + [stat] * 2
    return pl.pallas_call(
        functools.partial(_attn_a_kernel, n_lat, tk, sub_scale),
        grid=(batch, heads, nq),
        in_specs=in_specs,
        out_specs=pl.BlockSpec((tq, LANE), lambda b, h, i: (b * nq + i, h)),
        out_shape=jax.ShapeDtypeStruct((batch * q_len, heads * LANE), BF16),
        scratch_shapes=scratch,
        compiler_params=_cp("parallel", "parallel", "parallel"),
        name="diff_attn",
    )(*args)


def _attn_b_kernel(has_loc, seq_len, q_heads, group, *refs):
    if has_loc:
        (q_ref, kx_ref, vx_ref, kp_ref, kc_ref, kn_ref, vp_ref, vc_ref, vn_ref, sink_ref, o_ref) = refs
    else:
        (q_ref, kx_ref, vx_ref, sink_ref, o_ref) = refs
    n = pl.program_id(1)
    clen = kx_ref.shape[0]
    q_t = q_ref[...].astype(F32).T
    keys = [kx_ref[...]]
    vals = [vx_ref[0]]
    if has_loc:
        keys += [kp_ref[...], kc_ref[...], kn_ref[...]]
        vals += [vp_ref[0], vc_ref[0], vn_ref[0]]
        kpos = (n - 1) * BLOCK + lax.broadcasted_iota(jnp.int32, (3 * BLOCK, BLOCK), 0)
        qpos = n * BLOCK + lax.broadcasted_iota(jnp.int32, (3 * BLOCK, BLOCK), 1)
        valid = (kpos >= 0) & (kpos < seq_len) & (jnp.abs(kpos - qpos) <= WINDOW)
        valid = jnp.concatenate([valid] * q_heads, axis=1)
    k_all = jnp.concatenate(keys, axis=0)
    v_all = jnp.concatenate(vals, axis=1)
    zeros = jnp.zeros((HEAD_DIM, BLOCK), F32)
    cols = []
    for hq in range(q_heads):
        blk = q_t[hq * HEAD_DIM:(hq + 1) * HEAD_DIM]
        cols.append(jnp.concatenate([blk, zeros] if hq // group == 0 else [zeros, blk], axis=0))
    q_all = jnp.concatenate(cols, axis=1).astype(BF16)
    sink = jnp.concatenate([sink_ref[hq:hq + 1, :] for hq in range(q_heads)], axis=1) * LOG2E
    s = _dot(k_all, q_all)
    if has_loc:
        s = jnp.concatenate([s[:clen], jnp.where(valid, s[clen:], NEG_INF)], axis=0)
    m = jnp.maximum(jnp.max(s, axis=0, keepdims=True), sink)
    o_t = _dot(v_all, jnp.exp2(s - m).astype(BF16))
    den = o_t[LANE:LANE + 1] + jnp.exp2(sink - m)
    heads_out = []
    for hq in range(q_heads):
        kvh = hq // group
        lanes = slice(hq * BLOCK, (hq + 1) * BLOCK)
        heads_out.append(o_t[kvh * HEAD_DIM:(kvh + 1) * HEAD_DIM, lanes] / den[:, lanes])
    o_ref[...] = jnp.concatenate(heads_out, axis=0).T.astype(BF16)


def _window_attention(q_arr, q_len, qk_ctx, vt_ctx, ctx_len, vt_lat, sink_tab, q_heads, kv_heads,
                      q_blk, k_blk, local):
    assert kv_heads == 2, "key/value heads must fill exactly one 128-lane block"
    batch = q_arr.shape[0] // q_len
    nb = q_len // BLOCK
    qw = q_heads * HEAD_DIM
    in_specs = [
        pl.BlockSpec((BLOCK, qw), lambda b, n: (b * nb + n, q_blk)),
        pl.BlockSpec((ctx_len, LANE), lambda b, n: (b, k_blk)),
        pl.BlockSpec((None, 1, VT_ROWS, ctx_len), lambda b, n: (0, b, 0, 0)),
    ]
    args = [q_arr, qk_ctx, vt_ctx]
    if local:
        prev = lambda b, n: b * nb + jnp.maximum(n - 1, 0)
        nxt = lambda b, n: b * nb + jnp.minimum(n + 1, nb - 1)
        in_specs += [
            pl.BlockSpec((BLOCK, LANE), lambda b, n: (prev(b, n), k_blk)),
            pl.BlockSpec((BLOCK, LANE), lambda b, n: (b * nb + n, k_blk)),
            pl.BlockSpec((BLOCK, LANE), lambda b, n: (nxt(b, n), k_blk)),
            pl.BlockSpec((None, 1, VT_ROWS, BLOCK), lambda b, n: (0, prev(b, n), 0, 0)),
            pl.BlockSpec((None, 1, VT_ROWS, BLOCK), lambda b, n: (0, b * nb + n, 0, 0)),
            pl.BlockSpec((None, 1, VT_ROWS, BLOCK), lambda b, n: (0, nxt(b, n), 0, 0)),
        ]
        args += [q_arr] * 3 + [vt_lat] * 3
    in_specs.append(pl.BlockSpec((q_heads, LANE), lambda b, n: (0, 0)))
    args.append(sink_tab)
    return pl.pallas_call(
        functools.partial(_attn_b_kernel, local, q_len, q_heads, q_heads // kv_heads),
        grid=(batch, nb),
        in_specs=in_specs,
        out_specs=pl.BlockSpec((BLOCK, qw), lambda b, n: (b * nb + n, 0)),
        out_shape=jax.ShapeDtypeStruct((batch * q_len, qw), BF16),
        compiler_params=_cp("parallel", "parallel"),
        name="window_attn",
    )(*args)


def _halo_specs(tile, width, col_blk, n_rows):
    r = tile // HALO
    last = n_rows // HALO - 1
    return [
        pl.BlockSpec((tile, width), lambda i: (i, col_blk)),
        pl.BlockSpec((HALO, width), lambda i: (jnp.maximum(i * r - 1, 0), col_blk)),
        pl.BlockSpec((HALO, width), lambda i: (jnp.minimum(i * r + r, last), col_blk)),
    ]


def _outside_rows(tile, tile_idx, tiles_per_seq):
    row = lax.broadcasted_iota(jnp.int32, (tile + 2 * HALO, 1), 0)
    lo = jnp.where(tile_idx == 0, HALO, 0)
    hi = jnp.where(tile_idx == tiles_per_seq - 1, HALO + tile, tile + 2 * HALO)
    return (row < lo) | (row >= hi)


def _with_halo(x, xp, xn, tiles_per_seq):
    ext = jnp.concatenate([xp, x, xn], axis=0)
    outside = _outside_rows(x.shape[0], pl.program_id(0) % tiles_per_seq, tiles_per_seq)
    return jnp.where(outside, 0.0, ext)


def _conv3(ext, w, b, tile):
    rows = ext.shape[0]
    c = (pltpu.roll(ext, 1, 0) * w[0:1] + ext * w[1:2] + pltpu.roll(ext, rows - 1, 0) * w[2:3])
    return c[HALO:HALO + tile] + b


def _pool_kernel(seq_len, tile, x_ref, xp_ref, xn_ref, w_ref, ls_ref, o_ref):
    tps = seq_len // tile
    x = x_ref[...]
    ext = _with_halo(x, xp_ref[...], xn_ref[...], tps)
    rows = tile + 2 * HALO
    t = (pl.program_id(0) % tps) * tile + lax.broadcasted_iota(jnp.int32, (tile, 1), 0)
    gw = x.shape[1] // len(POOL_WINDOWS)
    outs = []
    for g, w in enumerate(POOL_WINDOWS):
        lo = w // 2
        hi = w - 1 - lo
        a = ext[:, g * gw:(g + 1) * gw]
        span = 1
        while span < w:
            a = a + pltpu.roll(a, span, 0)
            span *= 2
        win = (pltpu.roll(a, rows - hi, 0) if hi else a)[HALO:HALO + tile]
        cnt = jnp.minimum(t + hi + 1, seq_len) - jnp.maximum(t - lo, 0)
        d = win / cnt.astype(F32) - x[:, g * gw:(g + 1) * gw]
        outs.append(_dot(d.astype(BF16), w_ref[g]))
    o_ref[...] = (jnp.concatenate(outs, axis=1) * ls_ref[...]).astype(BF16)


def _pool_mixer(p, col0, width, seq_len, w_lin, ls):
    n = p.shape[0]
    tile = _tile(seq_len, 512)
    g, gw, _ = w_lin.shape
    return pl.pallas_call(
        functools.partial(_pool_kernel, seq_len, tile),
        grid=(n // tile,),
        in_specs=_halo_specs(tile, width, col0 // width, n)
        + [pl.BlockSpec((g, gw, gw), lambda i: (0, 0, 0)),
           pl.BlockSpec((1, width), lambda i: (0, 0))],
        out_specs=pl.BlockSpec((tile, width), lambda i: (i, 0)),
        out_shape=jax.ShapeDtypeStruct((n, width), BF16),
        compiler_params=_cp("parallel"),
        name="pool_mixer",
    )(p, p, p, w_lin, ls)


def _hypre_kernel(seq_len, tile, x_ref, xp_ref, xn_ref, cw_ref, cb_ref, z_ref, x2_ref):
    ext = _with_halo(x_ref[...], xp_ref[...], xn_ref[...], seq_len // tile)
    c = _conv3(ext, cw_ref[...], cb_ref[...], tile)
    dw = c.shape[1] // 3
    z_ref[...] = c[:, :dw] * c[:, dw:2 * dw]
    x2_ref[...] = c[:, 2 * dw:]


def _hyena_pre(p, col0, width, seq_len, conv_w, conv_b):
    n = p.shape[0]
    tile = _tile(seq_len, 512)
    dw = width // 3
    out = jax.ShapeDtypeStruct((n, dw), F32)
    return pl.pallas_call(
        functools.partial(_hypre_kernel, seq_len, tile),
        grid=(n // tile,),
        in_specs=_halo_specs(tile, width, col0 // width, n)
        + [pl.BlockSpec((3, width), lambda i: (0, 0)),
           pl.BlockSpec((1, width), lambda i: (0, 0))],
        out_specs=[pl.BlockSpec((tile, dw), lambda i: (i, 0))] * 2,
        out_shape=[out, out],
        compiler_params=_cp("parallel"),
        name="hyena_pre",
    )(p, p, p, conv_w, conv_b.reshape(1, width))


def _hyfilt_kernel(seq_len, tile, zf_ref, w1_ref, b1_ref, w2_ref, b2_ref, w3_ref, fr_ref, dl_ref,
                   k_ref, ss_ref):
    i = pl.program_id(0)
    half = tile // 2
    hp = functools.partial(jnp.dot, precision=lax.Precision.HIGHEST, preferred_element_type=F32)
    zf = zf_ref[...]
    fr = fr_ref[...]
    h = jnp.sin(fr * (hp(zf, w1_ref[...]) + b1_ref[...]))
    h = jnp.sin(fr * (hp(h, w2_ref[...]) + b2_ref[...]))
    w3 = w3_ref[...]
    dl = dl_ref[...]
    ss = jnp.zeros(ss_ref.shape, F32)
    for part in range(2):
        k = hp(h, w3[part]) * jnp.exp(-zf[:, part * HY_PACK:part * HY_PACK + 1] * dl)
        row = i * tile + part * half + lax.broadcasted_iota(jnp.int32, (half, 1), 0)
        k = jnp.where(row == seq_len, 0.0, k)
        k_ref[part * half:(part + 1) * half, :] = k
        ss = ss + jnp.sum(k * k, axis=0, keepdims=True)

    @pl.when(i == 0)
    def _():
        ss_ref[...] = jnp.zeros(ss_ref.shape, F32)

    ss_ref[...] += ss


def _hyena_filter(seq_len, w1, b1, w2, b2, w3, freq):
    dw = w3.shape[1] // 2
    bands = (HY_EMB - 1) // 2
    t01 = jnp.linspace(0.0, 1.0, seq_len, dtype=F32)[:, None]
    w_ang = 2.0 * math.pi * jnp.arange(seq_len, dtype=F32)[:, None] / seq_len
    f = jnp.linspace(1e-4, bands - 1, bands, dtype=F32)[None, :]
    z = jnp.concatenate([t01, jnp.cos(f * w_ang), -jnp.sin(f * w_ang)], axis=-1)
    rev = (seq_len - jnp.arange(seq_len)) % seq_len
    assert HY_HIDDEN == HY_PACK and HY_EMB <= HY_PACK
    tile = _tile(seq_len, 1024)
    tps = seq_len // tile
    half = tile // 2
    zf = jnp.pad(jnp.concatenate([z, z[rev]], axis=0), ((0, 0), (0, HY_PACK - HY_EMB)))
    zf = zf.reshape(2 * tps, 2, half, HY_PACK).transpose(0, 2, 1, 3).reshape(2 * tps * half, LANE)
    zero = jnp.zeros((HY_PACK, HY_PACK), F32)
    blockdiag = lambda w: jnp.block([[w, zero], [zero, w]])
    w1p = blockdiag(jnp.pad(w1, ((0, HY_PACK - HY_EMB), (0, 0))))
    w2p = blockdiag(w2)
    w3d = w3.reshape(HY_HIDDEN, 2, dw).transpose(1, 0, 2)
    zrows = jnp.zeros_like(w3d)
    w3p = jnp.stack([jnp.concatenate([w3d, zrows], axis=1), jnp.concatenate([zrows, w3d], axis=1)], axis=1)
    vec = lambda v: jnp.concatenate([v, v]).reshape(1, LANE)
    deltas = jnp.abs(jnp.linspace(math.log(HY_TARGET) / HY_FAST_DECAY, math.log(HY_TARGET) / HY_SLOW_DECAY,
                                  dw, dtype=F32)).reshape(1, dw)
    full = lambda shape: pl.BlockSpec(shape, lambda i: (0, 0))
    return pl.pallas_call(
        functools.partial(_hyfilt_kernel, seq_len, tile),
        grid=(2 * tps,),
        in_specs=[pl.BlockSpec((half, LANE), lambda i: (i, 0)),
                  full((LANE, LANE)), full((1, LANE)), full((LANE, LANE)), full((1, LANE)),
                  pl.BlockSpec((None, 2, LANE, dw), lambda i: (i // tps, 0, 0, 0)),
                  full((1, LANE)), full((1, dw))],
        out_specs=[pl.BlockSpec((tile, dw), lambda i: (i, 0)), full((1, dw))],
        out_shape=[jax.ShapeDtypeStruct((2 * seq_len, dw), F32), jax.ShapeDtypeStruct((1, dw), F32)],
        compiler_params=_cp("arbitrary"),
        name="hyena_filter",
    )(zf, w1p, vec(b1), w2p, vec(b2), w3p, vec(freq), deltas)


def _dft_tables(n1, n2):
    n = n1 * n2
    nf = min(n1, -(-(n1 // 2 + 1) // DFT_BLOCKS) * DFT_BLOCKS)
    two_pi = 2.0 * math.pi

    def cs(num, den):
        ang = (num % den).astype(F32) * (two_pi / den)
        return jnp.cos(ang), jnp.sin(ang)

    a = jnp.arange(n1)
    c1, s1 = cs(a[:, None] * a[None, :], n1)
    kron = lambda mat: jnp.kron(mat, jnp.eye(SUBLANE, dtype=F32))
    fwd_a = jnp.concatenate([kron(c1[:nf]), kron(-s1[:nf])], axis=0)
    wgt = jnp.where((a == 0) | (a == n1 // 2), 1.0, jnp.where(a < n1 // 2, 2.0, 0.0))[None, :nf] / n
    inv_a = jnp.concatenate([kron(c1[: n1 // 2, :nf] * wgt), kron(-s1[: n1 // 2, :nf] * wgt)], axis=1)
    k2 = jnp.arange(n2)
    num = (k2[None, :, None] * k2[None, None, :] * n1 + a[:nf, None, None] * k2[None, None, :])
    cb, sb = cs(num, n)
    fwd_b = jnp.concatenate([jnp.concatenate([cb, sb], axis=2),
                             jnp.concatenate([-sb, cb], axis=2)], axis=1)
    cbt, sbt = jnp.swapaxes(cb, 1, 2), jnp.swapaxes(sb, 1, 2)
    inv_b = jnp.concatenate([jnp.concatenate([cbt, -sbt], axis=2),
                             jnp.concatenate([sbt, cbt], axis=2)], axis=1)
    return tuple(m.astype(BF16) for m in (fwd_a, inv_a, fwd_b, inv_b)), nf


def _rows(x):
    a, g, c = x.shape
    return x.reshape(a * g, c)


def _dft_a_kernel(f_ref, x_ref, re_ref, im_ref):
    y = _dot(f_ref[...], _rows(x_ref[...]).astype(BF16))
    half = y.shape[0] // 2
    re_ref[...] = y[:half].reshape(re_ref.shape)
    im_ref[...] = y[half:].reshape(im_ref.shape)


def _dft_stage_a(xv, fwd_a, nf):
    b, k, n2, ch = xv.shape
    groups = n2 // SUBLANE
    out = jax.ShapeDtypeStruct((b, nf, n2, ch), F32)
    return pl.pallas_call(
        _dft_a_kernel,
        grid=(b, groups),
        in_specs=[pl.BlockSpec((2 * nf * SUBLANE, k * SUBLANE), lambda bi, j: (0, 0)),
                  pl.BlockSpec((None, k, SUBLANE, ch), lambda bi, j: (bi, 0, j, 0))],
        out_specs=[pl.BlockSpec((None, nf, SUBLANE, ch), lambda bi, j: (bi, 0, j, 0))] * 2,
        out_shape=[out, out],
        compiler_params=_cp("parallel", "parallel"),
        name="dft_stage_a",
    )(fwd_a[:, :k * SUBLANE], xv)


def _dft_b_kernel(m_ref, re_ref, im_ref, ss_ref, kr_ref, ki_ref):
    nrm = lax.rsqrt(ss_ref[...] + EPS)
    n2 = m_ref.shape[1] // 2
    for k in range(m_ref.shape[0]):
        rows = slice(k * n2, (k + 1) * n2)
        x = _dot(m_ref[k], jnp.concatenate([re_ref[rows], im_ref[rows]], axis=0).astype(BF16))
        kr_ref[rows] = x[:n2] * nrm
        ki_ref[rows] = x[n2:] * nrm


def _filter_spectrum(kern, sumsq, tabs, nf, n1, n2):
    fwd_a, _, fwd_b, _ = tabs
    ch = kern.shape[1]
    are, aim = _dft_stage_a(kern.reshape(1, n1, n2, ch), fwd_a, nf)
    are, aim = are.reshape(nf * n2, ch), aim.reshape(nf * n2, ch)
    kb = _tile(nf, DFT_BLOCKS)
    blk = pl.BlockSpec((kb * n2, ch), lambda i: (i, 0))
    out = jax.ShapeDtypeStruct((nf * n2, ch), F32)
    return pl.pallas_call(
        _dft_b_kernel,
        grid=(nf // kb,),
        in_specs=[pl.BlockSpec((kb, 2 * n2, 2 * n2), lambda i: (i, 0, 0)), blk, blk,
                  pl.BlockSpec((1, ch), lambda i: (0, 0))],
        out_specs=[blk, blk],
        out_shape=[out, out],
        compiler_params=_cp("parallel"),
        name="filter_spectrum",
    )(fwd_b, are, aim, sumsq)


def _dft_mul_kernel(f_ref, g_ref, re_ref, im_ref, kr_ref, ki_ref, qr_ref, qi_ref):
    n2 = f_ref.shape[1] // 2
    for k in range(f_ref.shape[0]):
        rows = slice(k * n2, (k + 1) * n2)
        x = _dot(f_ref[k], jnp.concatenate([re_ref[rows], im_ref[rows]], axis=0).astype(BF16))
        xr, xi = x[:n2], x[n2:]
        kr, ki = kr_ref[rows], ki_ref[rows]
        prod = jnp.concatenate([xr * kr - xi * ki, xr * ki + xi * kr], axis=0)
        q = _dot(g_ref[k], prod.astype(BF16))
        qr_ref[rows] = q[:n2]
        qi_ref[rows] = q[n2:]


def _dft_inv_kernel(g_ref, qr_ref, qi_ref, z_ref, x2_ref, bias_ref, o_ref):
    q = jnp.concatenate([_rows(qr_ref[...]), _rows(qi_ref[...])], axis=0)
    y = _dot(g_ref[...], q.astype(BF16)).reshape(z_ref.shape)
    o_ref[...] = x2_ref[...] * (y + z_ref[...] * bias_ref[...])


def _hyena_long(z, x2, kr, ki, bias, tabs, batch, seq_len, nf, n1, n2):
    fwd_a, inv_a, fwd_b, inv_b = tabs
    ch = z.shape[1]
    zv = z.reshape(batch, n1 // 2, n2, ch)
    are, aim = _dft_stage_a(zv, fwd_a, nf)
    are, aim = are.reshape(batch, nf * n2, ch), aim.reshape(batch, nf * n2, ch)
    kb = _tile(nf, DFT_BLOCKS)
    mat = pl.BlockSpec((kb, 2 * n2, 2 * n2), lambda i, b: (i, 0, 0))
    dat = pl.BlockSpec((None, kb * n2, ch), lambda i, b: (b, i, 0))
    spec = pl.BlockSpec((kb * n2, ch), lambda i, b: (i, 0))
    out = jax.ShapeDtypeStruct((batch, nf * n2, ch), F32)
    qr, qi = pl.pallas_call(
        _dft_mul_kernel,
        grid=(nf // kb, batch),
        in_specs=[mat, mat, dat, dat, spec, spec],
        out_specs=[dat, dat],
        out_shape=[out, out],
        compiler_params=_cp("parallel", "parallel"),
        name="dft_filter_mul",
    )(fwd_b, inv_b, are, aim, kr, ki)
    qr, qi = qr.reshape(batch, nf, n2, ch), qi.reshape(batch, nf, n2, ch)
    full = pl.BlockSpec((None, nf, SUBLANE, ch), lambda b, j: (b, 0, j, 0))
    half = pl.BlockSpec((None, n1 // 2, SUBLANE, ch), lambda b, j: (b, 0, j, 0))
    o = pl.pallas_call(
        _dft_inv_kernel,
        grid=(batch, n2 // SUBLANE),
        in_specs=[pl.BlockSpec(inv_a.shape, lambda b, j: (0, 0)), full, full, half, half,
                  pl.BlockSpec((1, ch), lambda b, j: (0, 0))],
        out_specs=half,
        out_shape=jax.ShapeDtypeStruct((batch, n1 // 2, n2, ch), F32),
        compiler_params=_cp("parallel", "parallel"),
        name="dft_inverse_gate",
    )(inv_a, qr, qi, zv, x2.reshape(batch, n1 // 2, n2, ch), bias.reshape(1, ch))
    return o.reshape(batch * seq_len, ch)


def _hyena_short_kernel(f_ref, g_ref, k_ref, ss_ref, z_ref, x2_ref, bias_ref, o_ref):
    f = f_ref[...]
    z = z_ref[...]
    seq = z.shape[0]
    ks = _dot(f, k_ref[...].astype(BF16)) * lax.rsqrt(ss_ref[...] + EPS)
    zs = _dot(f[:, :seq], z.astype(BF16))
    n = ks.shape[0] // 2
    kr, ki, zr, zi = ks[:n], ks[n:], zs[:n], zs[n:]
    prod = jnp.concatenate([zr * kr - zi * ki, zr * ki + zi * kr], axis=0)
    y = _dot(g_ref[...], prod.astype(BF16))
    o_ref[...] = x2_ref[...] * (y + z * bias_ref[...])


def _hyena_short(z, x2, kern, sumsq, bias, batch, seq_len):
    n = 2 * seq_len
    ch = z.shape[1]
    a = jnp.arange(n)
    ang = ((a[:, None] * a[None, :]) % n).astype(F32) * (2.0 * math.pi / n)
    c, s = jnp.cos(ang), jnp.sin(ang)
    f = jnp.concatenate([c, -s], axis=0).astype(BF16)
    g = (jnp.concatenate([c, -s], axis=1)[:seq_len] / n).astype(BF16)
    full = lambda shape: pl.BlockSpec(shape, lambda b: (0, 0))
    dat = pl.BlockSpec((seq_len, ch), lambda b: (b, 0))
    return pl.pallas_call(
        _hyena_short_kernel,
        grid=(batch,),
        in_specs=[full((2 * n, n)), full((seq_len, 2 * n)),
                  full((n, ch)), full((1, ch)), dat, dat, full((1, ch))],
        out_specs=dat,
        out_shape=jax.ShapeDtypeStruct((batch * seq_len, ch), F32),
        compiler_params=_cp("parallel"),
        name="hyena_short_conv",
    )(f, g, kern, sumsq, z, x2, bias.reshape(1, ch))


def _hyena_mixer(p, col0, width, batch, seq_len, conv_w, conv_b, w1, b1, w2, b2, w3, freq, bias):
    z, x2 = _hyena_pre(p, col0, width, seq_len, conv_w, conv_b)
    kern, sumsq = _hyena_filter(seq_len, w1, b1, w2, b2, w3, freq)
    n = 2 * seq_len
    if n <= 4 * DFT_RADIX:
        return _hyena_short(z, x2, kern, sumsq, bias, batch, seq_len)
    n2 = DFT_RADIX
    n1 = n // n2
    tabs, nf = _dft_tables(n1, n2)
    kr, ki = _filter_spectrum(kern, sumsq, tabs, nf, n1, n2)
    return _hyena_long(z, x2, kr, ki, bias, tabs, batch, seq_len, nf, n1, n2)


def _outproj_kernel(x_ref, a_ref, b_ref, c_ref, d_ref, w_ref, g_ref, o_ref):
    acc = None
    row = 0
    for m_ref in (a_ref, b_ref, c_ref, d_ref):
        wdt = m_ref.shape[1]
        part = _dot(m_ref[...].astype(BF16), w_ref[row:row + wdt, :])
        acc = part if acc is None else acc + part
        row += wdt
    o_ref[...] = x_ref[...] + g_ref[...] * acc


def _out_projection(x2d, mixers, w, layer, gate, rows_per_group):
    n, d = x2d.shape
    tm = _tile(rows_per_group, 512)
    tpg = rows_per_group // tm
    return pl.pallas_call(
        _outproj_kernel,
        grid=(n // tm,),
        in_specs=[pl.BlockSpec((tm, d), lambda i: (i, 0))]
        + [pl.BlockSpec((tm, m.shape[1]), lambda i: (i, 0)) for m in mixers]
        + [pl.BlockSpec((None,) + w.shape[1:], lambda i: (layer, 0, 0)),
           pl.BlockSpec((None, 1, d), lambda i: (i // tpg, 0, 0))],
        out_specs=pl.BlockSpec((tm, d), lambda i: (i, 0)),
        out_shape=jax.ShapeDtypeStruct((n, d), F32),
        compiler_params=_cp("parallel"),
        name="out_proj",
    )(x2d, *mixers, w, gate)


def _ffn_kernel(seq_len, tile, x_ref, xp_ref, xn_ref, g_ref, sc_ref, sh_ref, gate_ref,
                wg_ref, wu_ref, cwg_ref, cwu_ref, cbg_ref, cbu_ref, wd_ref, o_ref, h_scr, acc):
    j = pl.program_id(1)

    @pl.when(j == 0)
    def _():
        g, sc, sh = g_ref[...], sc_ref[...], sh_ref[...]
        tps = seq_len // tile
        ti = pl.program_id(0) % tps
        halo = jnp.concatenate([xn_ref[...], xp_ref[...]], axis=0)
        row = lax.broadcasted_iota(jnp.int32, (2 * HALO, 1), 0)
        outside = (row < jnp.where(ti == tps - 1, HALO, 0)) | (row >= jnp.where(ti == 0, HALO, 2 * HALO))
        h_scr[tile:, :] = jnp.where(outside, 0.0, _norm_mod(halo, g, sc, sh)).astype(BF16)
        chunk = min(tile, 4 * LANE)

        def body(c, carry):
            rows = pl.ds(pl.multiple_of(c * chunk, chunk), chunk)
            h_scr[rows, :] = _norm_mod(x_ref[rows, :], g, sc, sh).astype(BF16)
            return carry
        lax.fori_loop(0, tile // chunk, body, 0)
        acc[...] = jnp.zeros(acc.shape, F32)

    def conv3(u, w, b):
        rows = u.shape[0]
        c = pltpu.roll(u, 1, 0) * w[0:1] + u * w[1:2] + pltpu.roll(u, rows - 1, 0) * w[2:3]
        return c[:tile] + b

    h = h_scr[...]
    gate = conv3(_dot(h, wg_ref[...]), cwg_ref[...], cbg_ref[...])
    up = conv3(_dot(h, wu_ref[...]), cwu_ref[...], cbu_ref[...])
    act = (gate * jax.nn.sigmoid(gate)) * up
    acc[...] += _dot(act.astype(BF16), wd_ref[...])

    @pl.when(j == pl.num_programs(1) - 1)
    def _():
        o_ref[...] = x_ref[...] + gate_ref[...] * acc[...]


def _conv_ffn(x2d, g, sc, sh, gate, w_up, conv_w, conv_b, w_down, layer, seq_len, rows_per_group):
    n, d = x2d.shape
    f = w_down.shape[1]
    tile = _tile(seq_len, 1024)
    hc = _tile(f, 512)
    nh = f // hc
    once = pl.Buffered(1)
    tpg = rows_per_group // tile
    r = tile // HALO
    last = n // HALO - 1
    vec = pl.BlockSpec((None, 1, d), lambda i, j: (i // tpg, 0, 0))
    cb = conv_b.reshape(1, 2 * f)
    return pl.pallas_call(
        functools.partial(_ffn_kernel, seq_len, tile),
        grid=(n // tile, nh),
        in_specs=[
            pl.BlockSpec((tile, d), lambda i, j: (i, 0), pipeline_mode=once),
            pl.BlockSpec((HALO, d), lambda i, j: (jnp.maximum(i * r - 1, 0), 0)),
            pl.BlockSpec((HALO, d), lambda i, j: (jnp.minimum(i * r + r, last), 0)),
            pl.BlockSpec((1, d), lambda i, j: (0, 0)),
            vec, vec, vec,
            pl.BlockSpec((None, d, hc), lambda i, j: (layer, 0, j)),
            pl.BlockSpec((None, d, hc), lambda i, j: (layer, 0, nh + j)),
            pl.BlockSpec((3, hc), lambda i, j: (0, j)),
            pl.BlockSpec((3, hc), lambda i, j: (0, nh + j)),
            pl.BlockSpec((1, hc), lambda i, j: (0, j)),
            pl.BlockSpec((1, hc), lambda i, j: (0, nh + j)),
            pl.BlockSpec((None, hc, d), lambda i, j: (layer, j, 0)),
        ],
        out_specs=pl.BlockSpec((tile, d), lambda i, j: (i, 0), pipeline_mode=once),
        out_shape=jax.ShapeDtypeStruct((n, d), F32),
        scratch_shapes=[pltpu.VMEM((tile + 2 * HALO, d), BF16), pltpu.VMEM((tile, d), F32)],
        compiler_params=_cp("parallel", "arbitrary"),
        name="conv_ffn",
    )(x2d, x2d, x2d, g, sc, sh, gate, w_up, w_up, conv_w, conv_w, cb, cb, w_down)


def kernel(x, c, ctx, c_ctx, w_mod, b_mod, norm1_g, norm2_g, w_in, w_out, qk_gain, diff_lam, diff_subln,
           win_sink, pool_w, pool_scale, hy_conv_w, hy_conv_b, hy_w1, hy_b1, hy_w2, hy_b2, hy_w3, hy_freq,
           hy_bias, ffn_w_in, ffn_conv_w, ffn_conv_b, ffn_w_out):
    batch, seq, d = x.shape
    clen = ctx.shape[1]
    depth = w_in.shape[0]
    a_heads = d // 512
    a_width = a_heads * 2 * HEAD_DIM
    b_heads = d // 256
    b_kv = b_heads // 4
    b_width = b_heads * HEAD_DIM
    b_kvw = b_kv * HEAD_DIM
    c_width = d // 4
    d_width = d - a_width - b_width - c_width
    off_ka = a_width
    off_va = 2 * a_width
    off_qb = 3 * a_width
    off_kb = off_qb + b_width
    off_vb = off_kb + b_kvw
    off_pool = off_vb + b_kvw
    off_hy = off_pool + c_width
    in_cols = off_hy + 3 * d_width
    order = ((off_hy, in_cols), (off_pool, off_hy), (off_va, off_qb), (0, off_ka), (off_ka, off_va),
             (off_qb, off_kb), (off_kb, off_vb), (off_vb, off_pool))
    n_hy, n_pool, n_va, n_qa, n_ka, n_qb, n_kb, n_vb = (
        sum(e - s for s, e in order[:k]) for k in range(len(order)))
    assert a_width % LANE == 0 and b_width % LANE == 0 and b_kvw == LANE and c_width // 4 == LANE
    assert seq % BLOCK == 0 and clen % BLOCK == 0 and n_pool % c_width == 0
    assert (n_ka - n_qa) == a_width and (n_qb - n_qa) % b_width == 0 and n_kb - n_qb == b_width

    rows = -(-(batch + 1) // SUBLANE) * SUBLANE
    cvec = jnp.zeros((rows, d), F32).at[:batch].set(c).at[batch].set(c_ctx)
    mods = _modulation(cvec, w_mod, b_mod)

    w_in_b = jnp.concatenate([w_in[:, :, s:e] for s, e in order], axis=-1).astype(BF16)
    w_out_b = w_out.astype(BF16)
    ffn_in_b = ffn_w_in.astype(BF16)
    ffn_out_b = ffn_w_out.astype(BF16)

    cos_t, sin_t = _rope_tables(seq)
    scale = HEAD_DIM ** -0.5
    tile2 = lambda v: jnp.tile(v, LANE // HEAD_DIM)

    xs = x.reshape(batch * seq, d)
    cs = ctx.reshape(batch * clen, d)
    for l in range(depth):
        last = l == depth - 1
        m = mods[l]
        sh1, sc1, g1, sh2, sc2, g2 = [m[:batch, i * d:(i + 1) * d].reshape(batch, 1, d) for i in range(6)]
        csh1, csc1, cg1, csh2, csc2, cg2 = [m[batch, i * d:(i + 1) * d].reshape(1, 1, d) for i in range(6)]
        n1g = norm1_g[l].reshape(1, d)
        n2g = norm2_g[l].reshape(1, d)
        pool_w_l = pool_w[l].astype(BF16)

        p = _in_projection(xs, n1g, sc1, sh1, w_in_b, l, seq)
        pc = _in_projection(cs, n1g, csc1, csh1, w_in_b, l, batch * clen)

        na = a_width // LANE
        nbq = b_width // LANE
        gains = jnp.stack([tile2(qk_gain[l, 0]) * (scale * LOG2E)] * na + [tile2(qk_gain[l, 1])] * na
                          + [tile2(qk_gain[l, 2]) * (scale * LOG2E)] * nbq + [tile2(qk_gain[l, 3])])
        qk = _qk_prep(p, n_qa // LANE, gains, cos_t, sin_t, seq, True)
        qk_c = _qk_prep(pc, n_qa // LANE, gains, cos_t, sin_t, clen, False)
        kb_blk = 2 * na + nbq
        qb_blk = (n_qb - n_qa) // b_width

        lam_p = diff_lam[l].astype(F32)
        lambda_init = 0.8 - 0.6 * math.exp(-0.3 * l)
        lam = jnp.exp(jnp.sum(lam_p[0] * lam_p[1])) - jnp.exp(jnp.sum(lam_p[2] * lam_p[3])) + lambda_init
        lam_v = jnp.full((1, LANE), lam, F32)
        subg = diff_subln[l].reshape(1, LANE)
        sink_tab = jnp.broadcast_to(win_sink[l].astype(F32)[:, None], (b_heads, LANE))
        hy = (hy_conv_w[l], hy_conv_b[l], hy_w1[l], hy_b1[l], hy_w2[l], hy_b2[l], hy_w3[l], hy_freq[l],
              hy_bias[l])

        vt = _transposed_values(p, n_va // LANE, a_heads, seq, _tile(seq, 512))
        vt_c = _transposed_values(pc, n_va // LANE, a_heads, clen, clen)
        o_a = _diff_attention(qk, seq, qk_c, vt_c, clen, qk, vt, seq, lam_v, subg, 1.0 - lambda_init,
                              a_heads, na)
        vtb = _transposed_values(p, n_vb // LANE, 1, seq, BLOCK)
        vtb_c = _transposed_values(pc, n_vb // LANE, 1, clen, clen)
        o_b = _window_attention(qk, seq, qk_c, vtb_c, clen, vtb, sink_tab, b_heads, b_kv,
                                qb_blk, kb_blk, True)
        o_c = _pool_mixer(p, n_pool, c_width, seq, pool_w_l, pool_scale[l].reshape(1, c_width))
        o_d = _hyena_mixer(p, n_hy, 3 * d_width, batch, seq, *hy)
        xs = _out_projection(xs, (o_a, o_b, o_c, o_d), w_out_b, l, g1, seq)
        xs = _conv_ffn(xs, n2g, sc2, sh2, g2, ffn_in_b, ffn_conv_w[l], ffn_conv_b[l], ffn_out_b, l,
                       seq, seq)

        if not last:
            o_a_c = _diff_attention(qk_c, clen, qk_c, vt_c, clen, None, None, 0, lam_v, subg,
                                    1.0 - lambda_init, a_heads, na)
            o_b_c = _window_attention(qk_c, clen, qk_c, vtb_c, clen, None, sink_tab, b_heads, b_kv,
                                      qb_blk, kb_blk, False)
            o_c_c = _pool_mixer(pc, n_pool, c_width, clen, pool_w_l, pool_scale[l].reshape(1, c_width))
            o_d_c = _hyena_mixer(pc, n_hy, 3 * d_width, batch, clen, *hy)
            cs = _out_projection(cs, (o_a_c, o_b_c, o_c_c, o_d_c), w_out_b, l, cg1, batch * clen)
            cs = _conv_ffn(cs, n2g, csc2, csh2, cg2, ffn_in_b, ffn_conv_w[l], ffn_conv_b[l],
                           ffn_out_b, l, clen, batch * clen)
    return xs.reshape(batch, seq, d)
```

```python
import functools
import math

import jax
import jax.numpy as jnp
from jax import lax
from jax.experimental import pallas as pl
from jax.experimental.pallas import tpu as pltpu

F32 = jnp.float32
BF16 = jnp.bfloat16

HEAD_DIM = 64
GRID_W = 64
ROPE_BASE = 10000.0
EPS = 1e-6
NEG_INF = -1e30
WINDOW = 128
BLOCK = 128
POOL_WINDOWS = (2, 4, 8, 16)
HY_EMB = 33
HY_HIDDEN = 64
HY_FAST_DECAY = 0.3
HY_SLOW_DECAY = 1.5
HY_TARGET = 1e-2

LANE = 128
SUBLANE = 8
HALO = SUBLANE
DFT_RADIX = 128
DFT_BLOCKS = 8
BF16_ROWS = 2 * SUBLANE
VT_ROWS = LANE + BF16_ROWS
LOG2E = 1.4426950408889634
HY_PACK = LANE // 2
VMEM_LIMIT = 56 * 1024 * 1024


def _cp(*sem):
    return pltpu.CompilerParams(dimension_semantics=sem, vmem_limit_bytes=VMEM_LIMIT)


def _tile(n, pref):
    if n <= pref:
        return n
    t = pref
    while n % t:
        t //= 2
    assert t >= SUBLANE, (n, pref)
    return t


def _dot(a, b):
    return jnp.dot(a, b, preferred_element_type=F32)


def _mod_kernel(c_ref, w_ref, b_ref, o_ref):
    c = c_ref[...]
    a = (c * jax.nn.sigmoid(c)).astype(BF16)
    o_ref[...] = _dot(a, w_ref[...].astype(BF16)) + b_ref[...]


def _modulation(cvec, w_mod, b_mod):
    depth, d, n = w_mod.shape
    rows = cvec.shape[0]
    tn = _tile(n, 1024)
    return pl.pallas_call(
        _mod_kernel,
        grid=(depth, n // tn),
        in_specs=[
            pl.BlockSpec((rows, d), lambda l, j: (0, 0)),
            pl.BlockSpec((None, d, tn), lambda l, j: (l, 0, j)),
            pl.BlockSpec((None, 1, tn), lambda l, j: (l, 0, j)),
        ],
        out_specs=pl.BlockSpec((None, rows, tn), lambda l, j: (l, 0, j)),
        out_shape=jax.ShapeDtypeStruct((depth, rows, n), F32),
        compiler_params=_cp("parallel", "parallel"),
        name="adaln_mod",
    )(cvec, w_mod, b_mod.reshape(depth, 1, n))


def _norm_mod(x, g, sc, sh):
    ms = jnp.mean(x * x, axis=-1, keepdims=True)
    return (x * lax.rsqrt(ms + EPS) * g) * (1.0 + sc) + sh


def _inproj_kernel(x_ref, g_ref, sc_ref, sh_ref, w_ref, o_ref):
    h = _norm_mod(x_ref[...], g_ref[...], sc_ref[...], sh_ref[...])
    o_ref[...] = _dot(h.astype(BF16), w_ref[...])


def _in_projection(x2d, g, sc, sh, w, layer, rows_per_group):
    n, d = x2d.shape
    nc = w.shape[2]
    tm = _tile(rows_per_group, 512)
    tpg = rows_per_group // tm
    vec = pl.BlockSpec((None, 1, d), lambda i: (i // tpg, 0, 0))
    return pl.pallas_call(
        _inproj_kernel,
        grid=(n // tm,),
        in_specs=[
            pl.BlockSpec((tm, d), lambda i: (i, 0)),
            pl.BlockSpec((1, d), lambda i: (0, 0)),
            vec, vec,
            pl.BlockSpec((None, d, nc), lambda i: (layer, 0, 0), pipeline_mode=pl.Buffered(1)),
        ],
        out_specs=pl.BlockSpec((tm, nc), lambda i: (i, 0)),
        out_shape=jax.ShapeDtypeStruct((n, nc), F32),
        compiler_params=_cp("parallel"),
        name="in_proj",
    )(x2d, g, sc, sh, w)


def _qkprep_kernel(rope, x_ref, gain_ref, cos_ref, sin_ref, gm_ref, pm_ref, o_ref):
    x = x_ref[...]
    ms = _dot((x * x).astype(BF16), gm_ref[...])
    y = x * lax.rsqrt(ms + EPS) * gain_ref[...]
    if rope:
        cos = jnp.concatenate([cos_ref[...]] * 2, axis=1)
        sin = jnp.concatenate([sin_ref[...]] * 2, axis=1)
        y = y * cos + _dot(y.astype(BF16), pm_ref[...]) * sin
    o_ref[...] = y.astype(BF16)


def _qk_prep(p, col_block0, gains, cos_t, sin_t, seq_len, rope):
    n = p.shape[0]
    nblk = gains.shape[0]
    wide = 2 * LANE
    nw = -(-nblk // 2)
    assert col_block0 % 2 == 0 and (col_block0 + 2 * nw) * LANE <= p.shape[1]
    gains = jnp.concatenate([gains, jnp.ones((2 * nw - nblk, LANE), F32)], axis=0).reshape(nw, 1, wide)
    tm = _tile(seq_len, 2048)
    tps = seq_len // tm
    lane = jnp.arange(wide)
    gm = ((lane[:, None] // HEAD_DIM) == (lane[None, :] // HEAD_DIM)).astype(BF16) * (1.0 / HEAD_DIM)
    partner = jnp.where((lane % 32) < 16, lane + 16, lane - 16)
    pm = (lane[:, None] == partner[None, :]).astype(BF16)
    mat = pl.BlockSpec((wide, wide), lambda i, j: (0, 0))
    tab = pl.BlockSpec((tm, LANE), lambda i, j: (i % tps, 0))
    return pl.pallas_call(
        functools.partial(_qkprep_kernel, rope),
        grid=(n // tm, nw),
        in_specs=[
            pl.BlockSpec((tm, wide), lambda i, j: (i, col_block0 // 2 + j)),
            pl.BlockSpec((None, 1, wide), lambda i, j: (j, 0, 0)),
            tab, tab, mat, mat,
        ],
        out_specs=pl.BlockSpec((tm, wide), lambda i, j: (i, j)),
        out_shape=jax.ShapeDtypeStruct((n, nw * wide), BF16),
        compiler_params=_cp("parallel", "parallel"),
        name="qk_prep",
    )(p, gains, cos_t, sin_t, gm.astype(BF16), pm)


def _rope_tables(seq_len):
    rows = seq_len // GRID_W
    row = jnp.repeat(jnp.arange(rows, dtype=F32), GRID_W)
    col = jnp.tile(jnp.arange(GRID_W, dtype=F32), rows)
    half = HEAD_DIM // 2
    inv = ROPE_BASE ** (-jnp.arange(0, half, 2, dtype=F32) / half)
    ang = jnp.concatenate([row[:, None] * inv, col[:, None] * inv], axis=-1)
    lane = jnp.arange(LANE)
    within = lane % HEAD_DIM
    idx = (within // 32) * 16 + (within % 16)
    sign = jnp.where((lane % 32) < 16, -1.0, 1.0).astype(F32)
    return jnp.cos(ang)[:, idx], jnp.sin(ang)[:, idx] * sign


def _vt_kernel(x_ref, o_ref):
    chunks, _, chunk = o_ref.shape
    row = lax.broadcasted_iota(jnp.int32, (VT_ROWS - LANE, chunk), 0)
    ones_row = jnp.where(row == 0, 1.0, 0.0).astype(BF16)
    for c in range(chunks):
        x_t = x_ref[c * chunk:(c + 1) * chunk, :].T.astype(BF16)
        o_ref[c] = jnp.concatenate([x_t, ones_row], axis=0)


def _transposed_values(p, col_block0, nblk, seq_len, chunk):
    n = p.shape[0]
    rows = max(chunk, _tile(seq_len, 2048))
    return pl.pallas_call(
        _vt_kernel,
        grid=(n // rows, nblk),
        in_specs=[pl.BlockSpec((rows, LANE), lambda i, j: (i, col_block0 + j))],
        out_specs=pl.BlockSpec((None, rows // chunk, VT_ROWS, chunk), lambda i, j: (j, i, 0, 0)),
        out_shape=jax.ShapeDtypeStruct((nblk, n // chunk, VT_ROWS, chunk), BF16),
        compiler_params=_cp("parallel", "parallel"),
        name="values_t",
    )(p)


def _attn_a_kernel(n_lat, tk, sub_scale, *refs):
    if n_lat:
        (q_ref, kc_ref, vc_ref, k_ref, v_ref, lam_ref, subg_ref, o_ref, qt, m, acc,
         s0, s1, x0, x1, p0, p1, al0, al1) = refs
    else:
        q_ref, kc_ref, vc_ref, lam_ref, subg_ref, o_ref, qt, m, acc = refs
    tq = q_ref.shape[0]
    q_t = q_ref[...].astype(F32).T
    row = lax.broadcasted_iota(jnp.int32, q_t.shape, 0)
    qt[:, :tq] = jnp.where(row < HEAD_DIM, q_t, 0.0).astype(BF16)
    qt[:, tq:] = jnp.where(row >= HEAD_DIM, q_t, 0.0).astype(BF16)
    m[...] = jnp.full(m.shape, -jnp.inf, F32)
    acc[...] = jnp.zeros(acc.shape, F32)

    def softmax_update(s, s_max):
        m_prev = m[...]
        m_new = jnp.maximum(m_prev, s_max)
        m[...] = m_new
        return jnp.exp2(s - m_new).astype(BF16), jnp.exp2(m_prev - m_new)

    s_c = _dot(kc_ref[...], qt[...])

    def context_update():
        p_c, alpha_c = softmax_update(s_c, jnp.max(s_c, axis=0, keepdims=True))
        acc[...] = alpha_c * acc[...] + _dot(vc_ref[0], p_c)

    if not n_lat:
        context_update()
    else:
        def scores(c, s_buf, x_buf):
            s = _dot(k_ref[pl.ds(pl.multiple_of(c * tk, tk), tk), :], qt[...])
            s_buf[...] = s
            x_buf[...] = jnp.max(s, axis=0, keepdims=True)

        def softmax(s_buf, x_buf, p_buf, al_buf):
            p_buf[...], al_buf[...] = softmax_update(s_buf[...], x_buf[...])

        def values(c, p_buf, al_buf):
            acc[...] = al_buf[...] * acc[...] + _dot(v_ref[c], p_buf[...])

        scores(0, s0, x0)
        if n_lat == 1:
            context_update()
            softmax(s0, x0, p0, al0)
            values(0, p0, al0)
        else:
            assert n_lat % 2 == 0
            scores(1, s1, x1)
            context_update()
            softmax(s0, x0, p0, al0)

            def body(t, carry):
                c = 2 * t
                scores(c + 2, s0, x0)
                softmax(s1, x1, p1, al1)
                values(c, p0, al0)
                scores(c + 3, s1, x1)
                softmax(s0, x0, p0, al0)
                values(c + 1, p1, al1)
                return carry
            lax.fori_loop(0, n_lat // 2 - 1, body, 0)
            softmax(s1, x1, p1, al1)
            values(n_lat - 2, p0, al0)
            values(n_lat - 1, p1, al1)

    o_t = acc[:LANE] / acc[LANE:LANE + 1]
    o = (o_t[:, :tq] - lam_ref[0:1, 0:1] * o_t[:, tq:]).T
    ms = jnp.mean(o * o, axis=-1, keepdims=True)
    o_ref[...] = ((o * lax.rsqrt(ms + EPS) * subg_ref[...]) * sub_scale).astype(BF16)


def _diff_attention(q_arr, q_len, qk_ctx, vt_ctx, ctx_len, qk_lat, vt_lat, lat_len, lam, subg,
                    sub_scale, heads, k_col0):
    batch = q_arr.shape[0] // q_len
    tq = _tile(q_len, 2048)
    nq = q_len // tq
    in_specs = [
        pl.BlockSpec((tq, LANE), lambda b, h, i: (b * nq + i, h)),
        pl.BlockSpec((ctx_len, LANE), lambda b, h, i: (b, k_col0 + h)),
        pl.BlockSpec((None, 1, VT_ROWS, ctx_len), lambda b, h, i: (h, b, 0, 0)),
    ]
    args = [q_arr, qk_ctx, vt_ctx]
    n_lat, tk = 0, 0
    if qk_lat is not None:
        tk = vt_lat.shape[-1]
        n_lat = lat_len // tk
        in_specs += [
            pl.BlockSpec((lat_len, LANE), lambda b, h, i: (b, k_col0 + h)),
            pl.BlockSpec((None, n_lat, VT_ROWS, tk), lambda b, h, i: (h, b, 0, 0)),
        ]
        args += [qk_lat, vt_lat]
    in_specs += [pl.BlockSpec((1, LANE), lambda b, h, i: (0, 0))] * 2
    args += [lam, subg]
    stat = pltpu.VMEM((1, 2 * tq), F32)
    scratch = [pltpu.VMEM((LANE, 2 * tq), BF16), stat, pltpu.VMEM((VT_ROWS, 2 * tq), F32)]
    if n_lat:
        scratch += [pltpu.VMEM((tk, 2 * tq), F32)] * 2 + [stat] * 2
        scratch += [pltpu.VMEM((tk, 2 * tq), BF16)] * 2 + [stat] * 2
    return pl.pallas_call(
        functools.partial(_attn_a_kernel, n_lat, tk, sub_scale),
        grid=(batch, heads, nq),
        in_specs=in_specs,
        out_specs=pl.BlockSpec((tq, LANE), lambda b, h, i: (b * nq + i, h)),
        out_shape=jax.ShapeDtypeStruct((batch * q_len, heads * LANE), BF16),
        scratch_shapes=scratch,
        compiler_params=_cp("parallel", "parallel", "parallel"),
        name="diff_attn",
    )(*args)


def _attn_b_kernel(has_loc, seq_len, q_heads, group, *refs):
    if has_loc:
        (q_ref, kx_ref, vx_ref, kp_ref, kc_ref, kn_ref, vp_ref, vc_ref, vn_ref, sink_ref, o_ref) = refs
    else:
        (q_ref, kx_ref, vx_ref, sink_ref, o_ref) = refs
    n = pl.program_id(1)
    clen = kx_ref.shape[0]
    q_t = q_ref[...].astype(F32).T
    keys = [kx_ref[...]]
    vals = [vx_ref[0]]
    if has_loc:
        keys += [kp_ref[...], kc_ref[...], kn_ref[...]]
        vals += [vp_ref[0], vc_ref[0], vn_ref[0]]
        kpos = (n - 1) * BLOCK + lax.broadcasted_iota(jnp.int32, (3 * BLOCK, BLOCK), 0)
        qpos = n * BLOCK + lax.broadcasted_iota(jnp.int32, (3 * BLOCK, BLOCK), 1)
        valid = (kpos >= 0) & (kpos < seq_len) & (jnp.abs(kpos - qpos) <= WINDOW)
        valid = jnp.concatenate([valid] * q_heads, axis=1)
    k_all = jnp.concatenate(keys, axis=0)
    v_all = jnp.concatenate(vals, axis=1)
    zeros = jnp.zeros((HEAD_DIM, BLOCK), F32)
    cols = []
    for hq in range(q_heads):
        blk = q_t[hq * HEAD_DIM:(hq + 1) * HEAD_DIM]
        cols.append(jnp.concatenate([blk, zeros] if hq // group == 0 else [zeros, blk], axis=0))
    q_all = jnp.concatenate(cols, axis=1).astype(BF16)
    sink = jnp.concatenate([sink_ref[hq:hq + 1, :] for hq in range(q_heads)], axis=1) * LOG2E
    s = _dot(k_all, q_all)
    if has_loc:
        s = jnp.concatenate([s[:clen], jnp.where(valid, s[clen:], NEG_INF)], axis=0)
    m = jnp.maximum(jnp.max(s, axis=0, keepdims=True), sink)
    o_t = _dot(v_all, jnp.exp2(s - m).astype(BF16))
    den = o_t[LANE:LANE + 1] + jnp.exp2(sink - m)
    heads_out = []
    for hq in range(q_heads):
        kvh = hq // group
        lanes = slice(hq * BLOCK, (hq + 1) * BLOCK)
        heads_out.append(o_t[kvh * HEAD_DIM:(kvh + 1) * HEAD_DIM, lanes] / den[:, lanes])
    o_ref[...] = jnp.concatenate(heads_out, axis=0).T.astype(BF16)


def _window_attention(q_arr, q_len, qk_ctx, vt_ctx, ctx_len, vt_lat, sink_tab, q_heads, kv_heads,
                      q_blk, k_blk, local):
    assert kv_heads == 2, "key/value heads must fill exactly one 128-lane block"
    batch = q_arr.shape[0] // q_len
    nb = q_len // BLOCK
    qw = q_heads * HEAD_DIM
    in_specs = [
        pl.BlockSpec((BLOCK, qw), lambda b, n: (b * nb + n, q_blk)),
        pl.BlockSpec((ctx_len, LANE), lambda b, n: (b, k_blk)),
        pl.BlockSpec((None, 1, VT_ROWS, ctx_len), lambda b, n: (0, b, 0, 0)),
    ]
    args = [q_arr, qk_ctx, vt_ctx]
    if local:
        prev = lambda b, n: b * nb + jnp.maximum(n - 1, 0)
        nxt = lambda b, n: b * nb + jnp.minimum(n + 1, nb - 1)
        in_specs += [
            pl.BlockSpec((BLOCK, LANE), lambda b, n: (prev(b, n), k_blk)),
            pl.BlockSpec((BLOCK, LANE), lambda b, n: (b * nb + n, k_blk)),
            pl.BlockSpec((BLOCK, LANE), lambda b, n: (nxt(b, n), k_blk)),
            pl.BlockSpec((None, 1, VT_ROWS, BLOCK), lambda b, n: (0, prev(b, n), 0, 0)),
            pl.BlockSpec((None, 1, VT_ROWS, BLOCK), lambda b, n: (0, b * nb + n, 0, 0)),
            pl.BlockSpec((None, 1, VT_ROWS, BLOCK), lambda b, n: (0, nxt(b, n), 0, 0)),
        ]
        args += [q_arr] * 3 + [vt_lat] * 3
    in_specs.append(pl.BlockSpec((q_heads, LANE), lambda b, n: (0, 0)))
    args.append(sink_tab)
    return pl.pallas_call(
        functools.partial(_attn_b_kernel, local, q_len, q_heads, q_heads // kv_heads),
        grid=(batch, nb),
        in_specs=in_specs,
        out_specs=pl.BlockSpec((BLOCK, qw), lambda b, n: (b * nb + n, 0)),
        out_shape=jax.ShapeDtypeStruct((batch * q_len, qw), BF16),
        compiler_params=_cp("parallel", "parallel"),
        name="window_attn",
    )(*args)


def _halo_specs(tile, width, col_blk, n_rows):
    r = tile // HALO
    last = n_rows // HALO - 1
    return [
        pl.BlockSpec((tile, width), lambda i: (i, col_blk)),
        pl.BlockSpec((HALO, width), lambda i: (jnp.maximum(i * r - 1, 0), col_blk)),
        pl.BlockSpec((HALO, width), lambda i: (jnp.minimum(i * r + r, last), col_blk)),
    ]


def _outside_rows(tile, tile_idx, tiles_per_seq):
    row = lax.broadcasted_iota(jnp.int32, (tile + 2 * HALO, 1), 0)
    lo = jnp.where(tile_idx == 0, HALO, 0)
    hi = jnp.where(tile_idx == tiles_per_seq - 1, HALO + tile, tile + 2 * HALO)
    return (row < lo) | (row >= hi)


def _with_halo(x, xp, xn, tiles_per_seq):
    ext = jnp.concatenate([xp, x, xn], axis=0)
    outside = _outside_rows(x.shape[0], pl.program_id(0) % tiles_per_seq, tiles_per_seq)
    return jnp.where(outside, 0.0, ext)


def _conv3(ext, w, b, tile):
    rows = ext.shape[0]
    c = (pltpu.roll(ext, 1, 0) * w[0:1] + ext * w[1:2] + pltpu.roll(ext, rows - 1, 0) * w[2:3])
    return c[HALO:HALO + tile] + b


def _pool_kernel(seq_len, tile, x_ref, xp_ref, xn_ref, w_ref, ls_ref, o_ref):
    tps = seq_len // tile
    x = x_ref[...]
    ext = _with_halo(x, xp_ref[...], xn_ref[...], tps)
    rows = tile + 2 * HALO
    t = (pl.program_id(0) % tps) * tile + lax.broadcasted_iota(jnp.int32, (tile, 1), 0)
    gw = x.shape[1] // len(POOL_WINDOWS)
    outs = []
    for g, w in enumerate(POOL_WINDOWS):
        lo = w // 2
        hi = w - 1 - lo
        a = ext[:, g * gw:(g + 1) * gw]
        span = 1
        while span < w:
            a = a + pltpu.roll(a, span, 0)
            span *= 2
        win = (pltpu.roll(a, rows - hi, 0) if hi else a)[HALO:HALO + tile]
        cnt = jnp.minimum(t + hi + 1, seq_len) - jnp.maximum(t - lo, 0)
        d = win / cnt.astype(F32) - x[:, g * gw:(g + 1) * gw]
        outs.append(_dot(d.astype(BF16), w_ref[g]))
    o_ref[...] = (jnp.concatenate(outs, axis=1) * ls_ref[...]).astype(BF16)


def _pool_mixer(p, col0, width, seq_len, w_lin, ls):
    n = p.shape[0]
    tile = _tile(seq_len, 512)
    g, gw, _ = w_lin.shape
    return pl.pallas_call(
        functools.partial(_pool_kernel, seq_len, tile),
        grid=(n // tile,),
        in_specs=_halo_specs(tile, width, col0 // width, n)
        + [pl.BlockSpec((g, gw, gw), lambda i: (0, 0, 0)),
           pl.BlockSpec((1, width), lambda i: (0, 0))],
        out_specs=pl.BlockSpec((tile, width), lambda i: (i, 0)),
        out_shape=jax.ShapeDtypeStruct((n, width), BF16),
        compiler_params=_cp("parallel"),
        name="pool_mixer",
    )(p, p, p, w_lin, ls)


def _hypre_kernel(seq_len, tile, x_ref, xp_ref, xn_ref, cw_ref, cb_ref, z_ref, x2_ref):
    ext = _with_halo(x_ref[...], xp_ref[...], xn_ref[...], seq_len // tile)
    c = _conv3(ext, cw_ref[...], cb_ref[...], tile)
    dw = c.shape[1] // 3
    z_ref[...] = c[:, :dw] * c[:, dw:2 * dw]
    x2_ref[...] = c[:, 2 * dw:]


def _hyena_pre(p, col0, width, seq_len, conv_w, conv_b):
    n = p.shape[0]
    tile = _tile(seq_len, 512)
    dw = width // 3
    out = jax.ShapeDtypeStruct((n, dw), F32)
    return pl.pallas_call(
        functools.partial(_hypre_kernel, seq_len, tile),
        grid=(n // tile,),
        in_specs=_halo_specs(tile, width, col0 // width, n)
        + [pl.BlockSpec((3, width), lambda i: (0, 0)),
           pl.BlockSpec((1, width), lambda i: (0, 0))],
        out_specs=[pl.BlockSpec((tile, dw), lambda i: (i, 0))] * 2,
        out_shape=[out, out],
        compiler_params=_cp("parallel"),
        name="hyena_pre",
    )(p, p, p, conv_w, conv_b.reshape(1, width))


def _hyfilt_kernel(seq_len, tile, zf_ref, w1_ref, b1_ref, w2_ref, b2_ref, w3_ref, fr_ref, dl_ref,
                   k_ref, ss_ref):
    i = pl.program_id(0)
    half = tile // 2
    hp = functools.partial(jnp.dot, precision=lax.Precision.HIGHEST, preferred_element_type=F32)
    zf = zf_ref[...]
    fr = fr_ref[...]
    h = jnp.sin(fr * (hp(zf, w1_ref[...]) + b1_ref[...]))
    h = jnp.sin(fr * (hp(h, w2_ref[...]) + b2_ref[...]))
    w3 = w3_ref[...]
    dl = dl_ref[...]
    ss = jnp.zeros(ss_ref.shape, F32)
    for part in range(2):
        k = hp(h, w3[part]) * jnp.exp(-zf[:, part * HY_PACK:part * HY_PACK + 1] * dl)
        row = i * tile + part * half + lax.broadcasted_iota(jnp.int32, (half, 1), 0)
        k = jnp.where(row == seq_len, 0.0, k)
        k_ref[part * half:(part + 1) * half, :] = k
        ss = ss + jnp.sum(k * k, axis=0, keepdims=True)

    @pl.when(i == 0)
    def _():
        ss_ref[...] = jnp.zeros(ss_ref.shape, F32)

    ss_ref[...] += ss


def _hyena_filter(seq_len, w1, b1, w2, b2, w3, freq):
    dw = w3.shape[1] // 2
    bands = (HY_EMB - 1) // 2
    t01 = jnp.linspace(0.0, 1.0, seq_len, dtype=F32)[:, None]
    w_ang = 2.0 * math.pi * jnp.arange(seq_len, dtype=F32)[:, None] / seq_len
    f = jnp.linspace(1e-4, bands - 1, bands, dtype=F32)[None, :]
    z = jnp.concatenate([t01, jnp.cos(f * w_ang), -jnp.sin(f * w_ang)], axis=-1)
    rev = (seq_len - jnp.arange(seq_len)) % seq_len
    assert HY_HIDDEN == HY_PACK and HY_EMB <= HY_PACK
    tile = _tile(seq_len, 1024)
    tps = seq_len // tile
    half = tile // 2
    zf = jnp.pad(jnp.concatenate([z, z[rev]], axis=0), ((0, 0), (0, HY_PACK - HY_EMB)))
    zf = zf.reshape(2 * tps, 2, half, HY_PACK).transpose(0, 2, 1, 3).reshape(2 * tps * half, LANE)
    zero = jnp.zeros((HY_PACK, HY_PACK), F32)
    blockdiag = lambda w: jnp.block([[w, zero], [zero, w]])
    w1p = blockdiag(jnp.pad(w1, ((0, HY_PACK - HY_EMB), (0, 0))))
    w2p = blockdiag(w2)
    w3d = w3.reshape(HY_HIDDEN, 2, dw).transpose(1, 0, 2)
    zrows = jnp.zeros_like(w3d)
    w3p = jnp.stack([jnp.concatenate([w3d, zrows], axis=1), jnp.concatenate([zrows, w3d], axis=1)], axis=1)
    vec = lambda v: jnp.concatenate([v, v]).reshape(1, LANE)
    deltas = jnp.abs(jnp.linspace(math.log(HY_TARGET) / HY_FAST_DECAY, math.log(HY_TARGET) / HY_SLOW_DECAY,
                                  dw, dtype=F32)).reshape(1, dw)
    full = lambda shape: pl.BlockSpec(shape, lambda i: (0, 0))
    return pl.pallas_call(
        functools.partial(_hyfilt_kernel, seq_len, tile),
        grid=(2 * tps,),
        in_specs=[pl.BlockSpec((half, LANE), lambda i: (i, 0)),
                  full((LANE, LANE)), full((1, LANE)), full((LANE, LANE)), full((1, LANE)),
                  pl.BlockSpec((None, 2, LANE, dw), lambda i: (i // tps, 0, 0, 0)),
                  full((1, LANE)), full((1, dw))],
        out_specs=[pl.BlockSpec((tile, dw), lambda i: (i, 0)), full((1, dw))],
        out_shape=[jax.ShapeDtypeStruct((2 * seq_len, dw), F32), jax.ShapeDtypeStruct((1, dw), F32)],
        compiler_params=_cp("arbitrary"),
        name="hyena_filter",
    )(zf, w1p, vec(b1), w2p, vec(b2), w3p, vec(freq), deltas)


def _dft_tables(n1, n2):
    n = n1 * n2
    nf = min(n1, -(-(n1 // 2 + 1) // DFT_BLOCKS) * DFT_BLOCKS)
    two_pi = 2.0 * math.pi

    def cs(num, den):
        ang = (num % den).astype(F32) * (two_pi / den)
        return jnp.cos(ang), jnp.sin(ang)

    a = jnp.arange(n1)
    c1, s1 = cs(a[:, None] * a[None, :], n1)
    kron = lambda mat: jnp.kron(mat, jnp.eye(SUBLANE, dtype=F32))
    fwd_a = jnp.concatenate([kron(c1[:nf]), kron(-s1[:nf])], axis=0)
    wgt = jnp.where((a == 0) | (a == n1 // 2), 1.0, jnp.where(a < n1 // 2, 2.0, 0.0))[None, :nf] / n
    inv_a = jnp.concatenate([kron(c1[: n1 // 2, :nf] * wgt), kron(-s1[: n1 // 2, :nf] * wgt)], axis=1)
    k2 = jnp.arange(n2)
    num = (k2[None, :, None] * k2[None, None, :] * n1 + a[:nf, None, None] * k2[None, None, :])
    cb, sb = cs(num, n)
    fwd_b = jnp.concatenate([jnp.concatenate([cb, sb], axis=2),
                             jnp.concatenate([-sb, cb], axis=2)], axis=1)
    cbt, sbt = jnp.swapaxes(cb, 1, 2), jnp.swapaxes(sb, 1, 2)
    inv_b = jnp.concatenate([jnp.concatenate([cbt, -sbt], axis=2),
                             jnp.concatenate([sbt, cbt], axis=2)], axis=1)
    return tuple(m.astype(BF16) for m in (fwd_a, inv_a, fwd_b, inv_b)), nf


def _rows(x):
    a, g, c = x.shape
    return x.reshape(a * g, c)


def _dft_a_kernel(f_ref, x_ref, re_ref, im_ref):
    y = _dot(f_ref[...], _rows(x_ref[...]).astype(BF16))
    half = y.shape[0] // 2
    re_ref[...] = y[:half].reshape(re_ref.shape)
    im_ref[...] = y[half:].reshape(im_ref.shape)


def _dft_stage_a(xv, fwd_a, nf):
    b, k, n2, ch = xv.shape
    groups = n2 // SUBLANE
    out = jax.ShapeDtypeStruct((b, nf, n2, ch), F32)
    return pl.pallas_call(
        _dft_a_kernel,
        grid=(b, groups),
        in_specs=[pl.BlockSpec((2 * nf * SUBLANE, k * SUBLANE), lambda bi, j: (0, 0)),
                  pl.BlockSpec((None, k, SUBLANE, ch), lambda bi, j: (bi, 0, j, 0))],
        out_specs=[pl.BlockSpec((None, nf, SUBLANE, ch), lambda bi, j: (bi, 0, j, 0))] * 2,
        out_shape=[out, out],
        compiler_params=_cp("parallel", "parallel"),
        name="dft_stage_a",
    )(fwd_a[:, :k * SUBLANE], xv)


def _dft_b_kernel(m_ref, re_ref, im_ref, ss_ref, kr_ref, ki_ref):
    nrm = lax.rsqrt(ss_ref[...] + EPS)
    n2 = m_ref.shape[1] // 2
    for k in range(m_ref.shape[0]):
        rows = slice(k * n2, (k + 1) * n2)
        x = _dot(m_ref[k], jnp.concatenate([re_ref[rows], im_ref[rows]], axis=0).astype(BF16))
        kr_ref[rows] = x[:n2] * nrm
        ki_ref[rows] = x[n2:] * nrm


def _filter_spectrum(kern, sumsq, tabs, nf, n1, n2):
    fwd_a, _, fwd_b, _ = tabs
    ch = kern.shape[1]
    are, aim = _dft_stage_a(kern.reshape(1, n1, n2, ch), fwd_a, nf)
    are, aim = are.reshape(nf * n2, ch), aim.reshape(nf * n2, ch)
    kb = _tile(nf, DFT_BLOCKS)
    blk = pl.BlockSpec((kb * n2, ch), lambda i: (i, 0))
    out = jax.ShapeDtypeStruct((nf * n2, ch), F32)
    return pl.pallas_call(
        _dft_b_kernel,
        grid=(nf // kb,),
        in_specs=[pl.BlockSpec((kb, 2 * n2, 2 * n2), lambda i: (i, 0, 0)), blk, blk,
                  pl.BlockSpec((1, ch), lambda i: (0, 0))],
        out_specs=[blk, blk],
        out_shape=[out, out],
        compiler_params=_cp("parallel"),
        name="filter_spectrum",
    )(fwd_b, are, aim, sumsq)


def _dft_mul_kernel(f_ref, g_ref, re_ref, im_ref, kr_ref, ki_ref, qr_ref, qi_ref):
    n2 = f_ref.shape[1] // 2
    for k in range(f_ref.shape[0]):
        rows = slice(k * n2, (k + 1) * n2)
        x = _dot(f_ref[k], jnp.concatenate([re_ref[rows], im_ref[rows]], axis=0).astype(BF16))
        xr, xi = x[:n2], x[n2:]
        kr, ki = kr_ref[rows], ki_ref[rows]
        prod = jnp.concatenate([xr * kr - xi * ki, xr * ki + xi * kr], axis=0)
        q = _dot(g_ref[k], prod.astype(BF16))
        qr_ref[rows] = q[:n2]
        qi_ref[rows] = q[n2:]


def _dft_inv_kernel(g_ref, qr_ref, qi_ref, z_ref, x2_ref, bias_ref, o_ref):
    q = jnp.concatenate([_rows(qr_ref[...]), _rows(qi_ref[...])], axis=0)
    y = _dot(g_ref[...], q.astype(BF16)).reshape(z_ref.shape)
    o_ref[...] = x2_ref[...] * (y + z_ref[...] * bias_ref[...])


def _hyena_long(z, x2, kr, ki, bias, tabs, batch, seq_len, nf, n1, n2):
    fwd_a, inv_a, fwd_b, inv_b = tabs
    ch = z.shape[1]
    zv = z.reshape(batch, n1 // 2, n2, ch)
    are, aim = _dft_stage_a(zv, fwd_a, nf)
    are, aim = are.reshape(batch, nf * n2, ch), aim.reshape(batch, nf * n2, ch)
    kb = _tile(nf, DFT_BLOCKS)
    mat = pl.BlockSpec((kb, 2 * n2, 2 * n2), lambda i, b: (i, 0, 0))
    dat = pl.BlockSpec((None, kb * n2, ch), lambda i, b: (b, i, 0))
    spec = pl.BlockSpec((kb * n2, ch), lambda i, b: (i, 0))
    out = jax.ShapeDtypeStruct((batch, nf * n2, ch), F32)
    qr, qi = pl.pallas_call(
        _dft_mul_kernel,
        grid=(nf // kb, batch),
        in_specs=[mat, mat, dat, dat, spec, spec],
        out_specs=[dat, dat],
        out_shape=[out, out],
        compiler_params=_cp("parallel", "parallel"),
        name="dft_filter_mul",
    )(fwd_b, inv_b, are, aim, kr, ki)
    qr, qi = qr.reshape(batch, nf, n2, ch), qi.reshape(batch, nf, n2, ch)
    full = pl.BlockSpec((None, nf, SUBLANE, ch), lambda b, j: (b, 0, j, 0))
    half = pl.BlockSpec((None, n1 // 2, SUBLANE, ch), lambda b, j: (b, 0, j, 0))
    o = pl.pallas_call(
        _dft_inv_kernel,
        grid=(batch, n2 // SUBLANE),
        in_specs=[pl.BlockSpec(inv_a.shape, lambda b, j: (0, 0)), full, full, half, half,
                  pl.BlockSpec((1, ch), lambda b, j: (0, 0))],
        out_specs=half,
        out_shape=jax.ShapeDtypeStruct((batch, n1 // 2, n2, ch), F32),
        compiler_params=_cp("parallel", "parallel"),
        name="dft_inverse_gate",
    )(inv_a, qr, qi, zv, x2.reshape(batch, n1 // 2, n2, ch), bias.reshape(1, ch))
    return o.reshape(batch * seq_len, ch)


def _hyena_short_kernel(f_ref, g_ref, k_ref, ss_ref, z_ref, x2_ref, bias_ref, o_ref):
    f = f_ref[...]
    z = z_ref[...]
    seq = z.shape[0]
    ks = _dot(f, k_ref[...].astype(BF16)) * lax.rsqrt(ss_ref[...] + EPS)
    zs = _dot(f[:, :seq], z.astype(BF16))
    n = ks.shape[0] // 2
    kr, ki, zr, zi = ks[:n], ks[n:], zs[:n], zs[n:]
    prod = jnp.concatenate([zr * kr - zi * ki, zr * ki + zi * kr], axis=0)
    y = _dot(g_ref[...], prod.astype(BF16))
    o_ref[...] = x2_ref[...] * (y + z * bias_ref[...])


def _hyena_short(z, x2, kern, sumsq, bias, batch, seq_len):
    n = 2 * seq_len
    ch = z.shape[1]
    a = jnp.arange(n)
    ang = ((a[:, None] * a[None, :]) % n).astype(F32) * (2.0 * math.pi / n)
    c, s = jnp.cos(ang), jnp.sin(ang)
    f = jnp.concatenate([c, -s], axis=0).astype(BF16)
    g = (jnp.concatenate([c, -s], axis=1)[:seq_len] / n).astype(BF16)
    full = lambda shape: pl.BlockSpec(shape, lambda b: (0, 0))
    dat = pl.BlockSpec((seq_len, ch), lambda b: (b, 0))
    return pl.pallas_call(
        _hyena_short_kernel,
        grid=(batch,),
        in_specs=[full((2 * n, n)), full((seq_len, 2 * n)),
                  full((n, ch)), full((1, ch)), dat, dat, full((1, ch))],
        out_specs=dat,
        out_shape=jax.ShapeDtypeStruct((batch * seq_len, ch), F32),
        compiler_params=_cp("parallel"),
        name="hyena_short_conv",
    )(f, g, kern, sumsq, z, x2, bias.reshape(1, ch))


def _hyena_mixer(p, col0, width, batch, seq_len, conv_w, conv_b, w1, b1, w2, b2, w3, freq, bias):
    z, x2 = _hyena_pre(p, col0, width, seq_len, conv_w, conv_b)
    kern, sumsq = _hyena_filter(seq_len, w1, b1, w2, b2, w3, freq)
    n = 2 * seq_len
    if n <= 4 * DFT_RADIX:
        return _hyena_short(z, x2, kern, sumsq, bias, batch, seq_len)
    n2 = DFT_RADIX
    n1 = n // n2
    tabs, nf = _dft_tables(n1, n2)
    kr, ki = _filter_spectrum(kern, sumsq, tabs, nf, n1, n2)
    return _hyena_long(z, x2, kr, ki, bias, tabs, batch, seq_len, nf, n1, n2)


def _outproj_kernel(x_ref, a_ref, b_ref, c_ref, d_ref, w_ref, g_ref, o_ref):
    acc = None
    row = 0
    for m_ref in (a_ref, b_ref, c_ref, d_ref):
        wdt = m_ref.shape[1]
        part = _dot(m_ref[...].astype(BF16), w_ref[row:row + wdt, :])
        acc = part if acc is None else acc + part
        row += wdt
    o_ref[...] = x_ref[...] + g_ref[...] * acc


def _out_projection(x2d, mixers, w, layer, gate, rows_per_group):
    n, d = x2d.shape
    tm = _tile(rows_per_group, 512)
    tpg = rows_per_group // tm
    return pl.pallas_call(
        _outproj_kernel,
        grid=(n // tm,),
        in_specs=[pl.BlockSpec((tm, d), lambda i: (i, 0))]
        + [pl.BlockSpec((tm, m.shape[1]), lambda i: (i, 0)) for m in mixers]
        + [pl.BlockSpec((None,) + w.shape[1:], lambda i: (layer, 0, 0)),
           pl.BlockSpec((None, 1, d), lambda i: (i // tpg, 0, 0))],
        out_specs=pl.BlockSpec((tm, d), lambda i: (i, 0)),
        out_shape=jax.ShapeDtypeStruct((n, d), F32),
        compiler_params=_cp("parallel"),
        name="out_proj",
    )(x2d, *mixers, w, gate)


def _ffn_kernel(seq_len, tile, x_ref, xp_ref, xn_ref, g_ref, sc_ref, sh_ref, gate_ref,
                wg_ref, wu_ref, cwg_ref, cwu_ref, cbg_ref, cbu_ref, wd_ref, o_ref, h_scr):
    j = pl.program_id(1)

    @pl.when(j == 0)
    def _():
        g, sc, sh = g_ref[...], sc_ref[...], sh_ref[...]
        tps = seq_len // tile
        ti = pl.program_id(0) % tps
        halo = jnp.concatenate([xn_ref[...], xp_ref[...]], axis=0)
        row = lax.broadcasted_iota(jnp.int32, (2 * HALO, 1), 0)
        outside = (row < jnp.where(ti == tps - 1, HALO, 0)) | (row >= jnp.where(ti == 0, HALO, 2 * HALO))
        h_scr[tile:, :] = jnp.where(outside, 0.0, _norm_mod(halo, g, sc, sh)).astype(BF16)
        chunk = min(tile, 4 * LANE)

        def body(c, carry):
            rows = pl.ds(pl.multiple_of(c * chunk, chunk), chunk)
            h_scr[rows, :] = _norm_mod(x_ref[rows, :], g, sc, sh).astype(BF16)
            return carry
        lax.fori_loop(0, tile // chunk, body, 0)
        o_ref[...] = jnp.zeros(o_ref.shape, F32)

    def conv3(u, w, b):
        rows = u.shape[0]
        c = pltpu.roll(u, 1, 0) * w[0:1] + u * w[1:2] + pltpu.roll(u, rows - 1, 0) * w[2:3]
        return c[:tile] + b

    h = h_scr[...]
    gate = conv3(_dot(h, wg_ref[...]), cwg_ref[...], cbg_ref[...])
    up = conv3(_dot(h, wu_ref[...]), cwu_ref[...], cbu_ref[...])
    act = (gate * jax.nn.sigmoid(gate)) * up
    o_ref[...] += _dot(act.astype(BF16), wd_ref[...])

    @pl.when(j == pl.num_programs(1) - 1)
    def _():
        o_ref[...] = x_ref[...] + gate_ref[...] * o_ref[...]


def _conv_ffn(x2d, g, sc, sh, gate, w_up, conv_w, conv_b, w_down, layer, seq_len, rows_per_group):
    n, d = x2d.shape
    f = w_down.shape[1]
    tile = _tile(seq_len, 1024)
    hc = _tile(f, 512)
    nh = f // hc
    once = pl.Buffered(1)
    tpg = rows_per_group // tile
    r = tile // HALO
    last = n // HALO - 1
    vec = pl.BlockSpec((None, 1, d), lambda i, j: (i // tpg, 0, 0))
    cb = conv_b.reshape(1, 2 * f)
    return pl.pallas_call(
        functools.partial(_ffn_kernel, seq_len, tile),
        grid=(n // tile, nh),
        in_specs=[
            pl.BlockSpec((tile, d), lambda i, j: (i, 0)),
            pl.BlockSpec((HALO, d), lambda i, j: (jnp.maximum(i * r - 1, 0), 0)),
            pl.BlockSpec((HALO, d), lambda i, j: (jnp.minimum(i * r + r, last), 0)),
            pl.BlockSpec((1, d), lambda i, j: (0, 0)),
            vec, vec, vec,
            pl.BlockSpec((None, d, hc), lambda i, j: (layer, 0, j)),
            pl.BlockSpec((None, d, hc), lambda i, j: (layer, 0, nh + j)),
            pl.BlockSpec((3, hc), lambda i, j: (0, j)),
            pl.BlockSpec((3, hc), lambda i, j: (0, nh + j)),
            pl.BlockSpec((1, hc), lambda i, j: (0, j)),
            pl.BlockSpec((1, hc), lambda i, j: (0, nh + j)),
            pl.BlockSpec((None, hc, d), lambda i, j: (layer, j, 0)),
        ],
        out_specs=pl.BlockSpec((tile, d), lambda i, j: (i, 0), pipeline_mode=once),
        out_shape=jax.ShapeDtypeStruct((n, d), F32),
        scratch_shapes=[pltpu.VMEM((tile + 2 * HALO, d), BF16)],
        compiler_params=_cp("parallel", "arbitrary"),
        name="conv_ffn",
    )(x2d, x2d, x2d, g, sc, sh, gate, w_up, w_up, conv_w, conv_w, cb, cb, w_down)


def kernel(x, c, ctx, c_ctx, w_mod, b_mod, norm1_g, norm2_g, w_in, w_out, qk_gain, diff_lam, diff_subln,
           win_sink, pool_w, pool_scale, hy_conv_w, hy_conv_b, hy_w1, hy_b1, hy_w2, hy_b2, hy_w3, hy_freq,
           hy_bias, ffn_w_in, ffn_conv_w, ffn_conv_b, ffn_w_out):
    batch, seq, d = x.shape
    clen = ctx.shape[1]
    depth = w_in.shape[0]
    a_heads = d // 512
    a_width = a_heads * 2 * HEAD_DIM
    b_heads = d // 256
    b_kv = b_heads // 4
    b_width = b_heads * HEAD_DIM
    b_kvw = b_kv * HEAD_DIM
    c_width = d // 4
    d_width = d - a_width - b_width - c_width
    off_ka = a_width
    off_va = 2 * a_width
    off_qb = 3 * a_width
    off_kb = off_qb + b_width
    off_vb = off_kb + b_kvw
    off_pool = off_vb + b_kvw
    off_hy = off_pool + c_width
    in_cols = off_hy + 3 * d_width
    order = ((off_hy, in_cols), (off_pool, off_hy), (off_va, off_qb), (0, off_ka), (off_ka, off_va),
             (off_qb, off_kb), (off_kb, off_vb), (off_vb, off_pool))
    n_hy, n_pool, n_va, n_qa, n_ka, n_qb, n_kb, n_vb = (
        sum(e - s for s, e in order[:k]) for k in range(len(order)))
    assert a_width % LANE == 0 and b_width % LANE == 0 and b_kvw == LANE and c_width // 4 == LANE
    assert seq % BLOCK == 0 and clen % BLOCK == 0 and n_pool % c_width == 0
    assert (n_ka - n_qa) == a_width and (n_qb - n_qa) % b_width == 0 and n_kb - n_qb == b_width

    rows = -(-(batch + 1) // SUBLANE) * SUBLANE
    cvec = jnp.zeros((rows, d), F32).at[:batch].set(c).at[batch].set(c_ctx)
    mods = _modulation(cvec, w_mod, b_mod)

    w_in_b = jnp.concatenate([w_in[:, :, s:e] for s, e in order], axis=-1).astype(BF16)
    w_out_b = w_out.astype(BF16)
    ffn_in_b = ffn_w_in.astype(BF16)
    ffn_out_b = ffn_w_out.astype(BF16)

    cos_t, sin_t = _rope_tables(seq)
    scale = HEAD_DIM ** -0.5
    tile2 = lambda v: jnp.tile(v, LANE // HEAD_DIM)

    xs = x.reshape(batch * seq, d)
    cs = ctx.reshape(batch * clen, d)
    for l in range(depth):
        last = l == depth - 1
        m = mods[l]
        sh1, sc1, g1, sh2, sc2, g2 = [m[:batch, i * d:(i + 1) * d].reshape(batch, 1, d) for i in range(6)]
        csh1, csc1, cg1, csh2, csc2, cg2 = [m[batch, i * d:(i + 1) * d].reshape(1, 1, d) for i in range(6)]
        n1g = norm1_g[l].reshape(1, d)
        n2g = norm2_g[l].reshape(1, d)
        pool_w_l = pool_w[l].astype(BF16)

        p = _in_projection(xs, n1g, sc1, sh1, w_in_b, l, seq)
        pc = _in_projection(cs, n1g, csc1, csh1, w_in_b, l, batch * clen)

        na = a_width // LANE
        nbq = b_width // LANE
        gains = jnp.stack([tile2(qk_gain[l, 0]) * (scale * LOG2E)] * na + [tile2(qk_gain[l, 1])] * na
                          + [tile2(qk_gain[l, 2]) * (scale * LOG2E)] * nbq + [tile2(qk_gain[l, 3])])
        qk = _qk_prep(p, n_qa // LANE, gains, cos_t, sin_t, seq, True)
        qk_c = _qk_prep(pc, n_qa // LANE, gains, cos_t, sin_t, clen, False)
        kb_blk = 2 * na + nbq
        qb_blk = (n_qb - n_qa) // b_width

        lam_p = diff_lam[l].astype(F32)
        lambda_init = 0.8 - 0.6 * math.exp(-0.3 * l)
        lam = jnp.exp(jnp.sum(lam_p[0] * lam_p[1])) - jnp.exp(jnp.sum(lam_p[2] * lam_p[3])) + lambda_init
        lam_v = jnp.full((1, LANE), lam, F32)
        subg = diff_subln[l].reshape(1, LANE)
        sink_tab = jnp.broadcast_to(win_sink[l].astype(F32)[:, None], (b_heads, LANE))
        hy = (hy_conv_w[l], hy_conv_b[l], hy_w1[l], hy_b1[l], hy_w2[l], hy_b2[l], hy_w3[l], hy_freq[l],
              hy_bias[l])

        vt = _transposed_values(p, n_va // LANE, a_heads, seq, _tile(seq, 512))
        vt_c = _transposed_values(pc, n_va // LANE, a_heads, clen, clen)
        o_a = _diff_attention(qk, seq, qk_c, vt_c, clen, qk, vt, seq, lam_v, subg, 1.0 - lambda_init,
                              a_heads, na)
        vtb = _transposed_values(p, n_vb // LANE, 1, seq, BLOCK)
        vtb_c = _transposed_values(pc, n_vb // LANE, 1, clen, clen)
        o_b = _window_attention(qk, seq, qk_c, vtb_c, clen, vtb, sink_tab, b_heads, b_kv,
                                qb_blk, kb_blk, True)
        o_c = _pool_mixer(p, n_pool, c_width, seq, pool_w_l, pool_scale[l].reshape(1, c_width))
        o_d = _hyena_mixer(p, n_hy, 3 * d_width, batch, seq, *hy)
        xs = _out_projection(xs, (o_a, o_b, o_c, o_d), w_out_b, l, g1, seq)
        xs = _conv_ffn(xs, n2g, sc2, sh2, g2, ffn_in_b, ffn_conv_w[l], ffn_conv_b[l], ffn_out_b, l,
                       seq, seq)

        if not last:
            o_a_c = _diff_attention(qk_c, clen, qk_c, vt_c, clen, None, None, 0, lam_v, subg,
                                    1.0 - lambda_init, a_heads, na)
            o_b_c = _window_attention(qk_c, clen, qk_c, vtb_c, clen, None, sink_tab, b_heads, b_kv,
                                      qb_blk, kb_blk, False)
            o_c_c = _pool_mixer(pc, n_pool, c_width, clen, pool_w_l, pool_scale[l].reshape(1, c_width))
            o_d_c = _hyena_mixer(pc, n_hy, 3 * d_width, batch, clen, *hy)
            cs = _out_projection(cs, (o_a_c, o_b_c, o_c_c, o_d_c), w_out_b, l, cg1, batch * clen)
            cs = _conv_ffn(cs, n2g, csc2, csh2, cg2, ffn_in_b, ffn_conv_w[l], ffn_conv_b[l],
                           ffn_out_b, l, clen, batch * clen)
    return xs.reshape(batch, seq, d)
```

```python
import functools
import math

import jax
import jax.numpy as jnp
from jax import lax
from jax.experimental import pallas as pl
from jax.experimental.pallas import tpu as pltpu

F32 = jnp.float32
BF16 = jnp.bfloat16

HEAD_DIM = 64
GRID_W = 64
ROPE_BASE = 10000.0
EPS = 1e-6
NEG_INF = -1e30
WINDOW = 128
BLOCK = 128
POOL_WINDOWS = (2, 4, 8, 16)
HY_EMB = 33
HY_HIDDEN = 64
HY_FAST_DECAY = 0.3
HY_SLOW_DECAY = 1.5
HY_TARGET = 1e-2

LANE = 128
SUBLANE = 8
HALO = SUBLANE
DFT_RADIX = 128
DFT_BLOCKS = 8
BF16_ROWS = 2 * SUBLANE
VT_ROWS = LANE + BF16_ROWS
LOG2E = 1.4426950408889634
HY_PACK = LANE // 2
VMEM_LIMIT = 56 * 1024 * 1024


def _cp(*sem):
    return pltpu.CompilerParams(dimension_semantics=sem, vmem_limit_bytes=VMEM_LIMIT)


def _tile(n, pref):
    if n <= pref:
        return n
    t = pref
    while n % t:
        t //= 2
    assert t >= SUBLANE, (n, pref)
    return t


def _dot(a, b):
    return jnp.dot(a, b, preferred_element_type=F32)


def _mod_kernel(c_ref, w_ref, b_ref, o_ref):
    c = c_ref[...]
    a = (c * jax.nn.sigmoid(c)).astype(BF16)
    o_ref[...] = _dot(a, w_ref[...].astype(BF16)) + b_ref[...]


def _modulation(cvec, w_mod, b_mod):
    depth, d, n = w_mod.shape
    rows = cvec.shape[0]
    tn = _tile(n, 1024)
    return pl.pallas_call(
        _mod_kernel,
        grid=(depth, n // tn),
        in_specs=[
            pl.BlockSpec((rows, d), lambda l, j: (0, 0)),
            pl.BlockSpec((None, d, tn), lambda l, j: (l, 0, j)),
            pl.BlockSpec((None, 1, tn), lambda l, j: (l, 0, j)),
        ],
        out_specs=pl.BlockSpec((None, rows, tn), lambda l, j: (l, 0, j)),
        out_shape=jax.ShapeDtypeStruct((depth, rows, n), F32),
        compiler_params=_cp("parallel", "parallel"),
        name="adaln_mod",
    )(cvec, w_mod, b_mod.reshape(depth, 1, n))


def _norm_mod(x, g, sc, sh):
    ms = jnp.mean(x * x, axis=-1, keepdims=True)
    return (x * lax.rsqrt(ms + EPS) * g) * (1.0 + sc) + sh


def _inproj_kernel(x_ref, g_ref, sc_ref, sh_ref, w_ref, o_ref):
    h = _norm_mod(x_ref[...], g_ref[...], sc_ref[...], sh_ref[...])
    o_ref[...] = _dot(h.astype(BF16), w_ref[...])


def _in_projection(x2d, g, sc, sh, w, layer, rows_per_group):
    n, d = x2d.shape
    nc = w.shape[2]
    tm = _tile(rows_per_group, 512)
    tpg = rows_per_group // tm
    vec = pl.BlockSpec((None, 1, d), lambda i: (i // tpg, 0, 0))
    return pl.pallas_call(
        _inproj_kernel,
        grid=(n // tm,),
        in_specs=[
            pl.BlockSpec((tm, d), lambda i: (i, 0)),
            pl.BlockSpec((1, d), lambda i: (0, 0)),
            vec, vec,
            pl.BlockSpec((None, d, nc), lambda i: (layer, 0, 0), pipeline_mode=pl.Buffered(1)),
        ],
        out_specs=pl.BlockSpec((tm, nc), lambda i: (i, 0)),
        out_shape=jax.ShapeDtypeStruct((n, nc), F32),
        compiler_params=_cp("parallel"),
        name="in_proj",
    )(x2d, g, sc, sh, w)


def _qkprep_kernel(rope, x_ref, gain_ref, cos_ref, sin_ref, gm_ref, pm_ref, o_ref):
    x = x_ref[...]
    ms = _dot((x * x).astype(BF16), gm_ref[...])
    y = x * lax.rsqrt(ms + EPS) * gain_ref[...]
    if rope:
        cos = jnp.concatenate([cos_ref[...]] * 2, axis=1)
        sin = jnp.concatenate([sin_ref[...]] * 2, axis=1)
        y = y * cos + _dot(y.astype(BF16), pm_ref[...]) * sin
    o_ref[...] = y.astype(BF16)


def _qk_prep(p, col_block0, gains, cos_t, sin_t, seq_len, rope):
    n = p.shape[0]
    nblk = gains.shape[0]
    wide = 2 * LANE
    nw = -(-nblk // 2)
    assert col_block0 % 2 == 0 and (col_block0 + 2 * nw) * LANE <= p.shape[1]
    gains = jnp.concatenate([gains, jnp.ones((2 * nw - nblk, LANE), F32)], axis=0).reshape(nw, 1, wide)
    tm = _tile(seq_len, 2048)
    tps = seq_len // tm
    lane = jnp.arange(wide)
    gm = ((lane[:, None] // HEAD_DIM) == (lane[None, :] // HEAD_DIM)).astype(BF16) * (1.0 / HEAD_DIM)
    partner = jnp.where((lane % 32) < 16, lane + 16, lane - 16)
    pm = (lane[:, None] == partner[None, :]).astype(BF16)
    mat = pl.BlockSpec((wide, wide), lambda i, j: (0, 0))
    tab = pl.BlockSpec((tm, LANE), lambda i, j: (i % tps, 0))
    return pl.pallas_call(
        functools.partial(_qkprep_kernel, rope),
        grid=(n // tm, nw),
        in_specs=[
            pl.BlockSpec((tm, wide), lambda i, j: (i, col_block0 // 2 + j)),
            pl.BlockSpec((None, 1, wide), lambda i, j: (j, 0, 0)),
            tab, tab, mat, mat,
        ],
        out_specs=pl.BlockSpec((tm, wide), lambda i, j: (i, j)),
        out_shape=jax.ShapeDtypeStruct((n, nw * wide), BF16),
        compiler_params=_cp("parallel", "parallel"),
        name="qk_prep",
    )(p, gains, cos_t, sin_t, gm.astype(BF16), pm)


def _rope_tables(seq_len):
    rows = seq_len // GRID_W
    row = jnp.repeat(jnp.arange(rows, dtype=F32), GRID_W)
    col = jnp.tile(jnp.arange(GRID_W, dtype=F32), rows)
    half = HEAD_DIM // 2
    inv = ROPE_BASE ** (-jnp.arange(0, half, 2, dtype=F32) / half)
    ang = jnp.concatenate([row[:, None] * inv, col[:, None] * inv], axis=-1)
    lane = jnp.arange(LANE)
    within = lane % HEAD_DIM
    idx = (within // 32) * 16 + (within % 16)
    sign = jnp.where((lane % 32) < 16, -1.0, 1.0).astype(F32)
    return jnp.cos(ang)[:, idx], jnp.sin(ang)[:, idx] * sign


def _vt_kernel(x_ref, o_ref):
    chunks, _, chunk = o_ref.shape
    row = lax.broadcasted_iota(jnp.int32, (VT_ROWS - LANE, chunk), 0)
    ones_row = jnp.where(row == 0, 1.0, 0.0).astype(BF16)
    for c in range(chunks):
        x_t = x_ref[c * chunk:(c + 1) * chunk, :].T.astype(BF16)
        o_ref[c] = jnp.concatenate([x_t, ones_row], axis=0)


def _transposed_values(p, col_block0, nblk, seq_len, chunk):
    n = p.shape[0]
    rows = max(chunk, _tile(seq_len, 2048))
    return pl.pallas_call(
        _vt_kernel,
        grid=(n // rows, nblk),
        in_specs=[pl.BlockSpec((rows, LANE), lambda i, j: (i, col_block0 + j))],
        out_specs=pl.BlockSpec((None, rows // chunk, VT_ROWS, chunk), lambda i, j: (j, i, 0, 0)),
        out_shape=jax.ShapeDtypeStruct((nblk, n // chunk, VT_ROWS, chunk), BF16),
        compiler_params=_cp("parallel", "parallel"),
        name="values_t",
    )(p)


def _attn_a_kernel(n_lat, tk, sub_scale, *refs):
    if n_lat:
        (q_ref, kc_ref, vc_ref, k_ref, v_ref, lam_ref, subg_ref, o_ref, qt, m, acc,
         s0, s1, x0, x1, p0, p1, al0, al1) = refs
    else:
        q_ref, kc_ref, vc_ref, lam_ref, subg_ref, o_ref, qt, m, acc = refs
    tq = q_ref.shape[0]
    q_t = q_ref[...].astype(F32).T
    row = lax.broadcasted_iota(jnp.int32, q_t.shape, 0)
    qt[:, :tq] = jnp.where(row < HEAD_DIM, q_t, 0.0).astype(BF16)
    qt[:, tq:] = jnp.where(row >= HEAD_DIM, q_t, 0.0).astype(BF16)
    m[...] = jnp.full(m.shape, -jnp.inf, F32)
    acc[...] = jnp.zeros(acc.shape, F32)

    def softmax_update(s, s_max):
        m_prev = m[...]
        m_new = jnp.maximum(m_prev, s_max)
        m[...] = m_new
        return jnp.exp2(s - m_new).astype(BF16), jnp.exp2(m_prev - m_new)

    s_c = _dot(kc_ref[...], qt[...])

    def context_update():
        p_c, alpha_c = softmax_update(s_c, jnp.max(s_c, axis=0, keepdims=True))
        acc[...] = alpha_c * acc[...] + _dot(vc_ref[0], p_c)

    if not n_lat:
        context_update()
    else:
        def scores(c, s_buf, x_buf):
            s = _dot(k_ref[pl.ds(pl.multiple_of(c * tk, tk), tk), :], qt[...])
            s_buf[...] = s
            x_buf[...] = jnp.max(s, axis=0, keepdims=True)

        def softmax(s_buf, x_buf, p_buf, al_buf):
            p_buf[...], al_buf[...] = softmax_update(s_buf[...], x_buf[...])

        def values(c, p_buf, al_buf):
            acc[...] = al_buf[...] * acc[...] + _dot(v_ref[c], p_buf[...])

        scores(0, s0, x0)
        if n_lat == 1:
            context_update()
            softmax(s0, x0, p0, al0)
            values(0, p0, al0)
        else:
            assert n_lat % 2 == 0
            scores(1, s1, x1)
            context_update()
            softmax(s0, x0, p0, al0)

            def body(t, carry):
                c = 2 * t
                scores(c + 2, s0, x0)
                softmax(s1, x1, p1, al1)
                values(c, p0, al0)
                scores(c + 3, s1, x1)
                softmax(s0, x0, p0, al0)
                values(c + 1, p1, al1)
                return carry
            lax.fori_loop(0, n_lat // 2 - 1, body, 0)
            softmax(s1, x1, p1, al1)
            values(n_lat - 2, p0, al0)
            values(n_lat - 1, p1, al1)

    o_t = acc[:LANE] / acc[LANE:LANE + 1]
    o = (o_t[:, :tq] - lam_ref[0:1, 0:1] * o_t[:, tq:]).T
    ms = jnp.mean(o * o, axis=-1, keepdims=True)
    o_ref[...] = ((o * lax.rsqrt(ms + EPS) * subg_ref[...]) * sub_scale).astype(BF16)


def _diff_attention(q_arr, q_len, qk_ctx, vt_ctx, ctx_len, qk_lat, vt_lat, lat_len, lam, subg,
                    sub_scale, heads, k_col0):
    batch = q_arr.shape[0] // q_len
    tq = _tile(q_len, 2048)
    nq = q_len // tq
    in_specs = [
        pl.BlockSpec((tq, LANE), lambda b, h, i: (b * nq + i, h)),
        pl.BlockSpec((ctx_len, LANE), lambda b, h, i: (b, k_col0 + h)),
        pl.BlockSpec((None, 1, VT_ROWS, ctx_len), lambda b, h, i: (h, b, 0, 0)),
    ]
    args = [q_arr, qk_ctx, vt_ctx]
    n_lat, tk = 0, 0
    if qk_lat is not None:
        tk = vt_lat.shape[-1]
        n_lat = lat_len // tk
        in_specs += [
            pl.BlockSpec((lat_len, LANE), lambda b, h, i: (b, k_col0 + h)),
            pl.BlockSpec((None, n_lat, VT_ROWS, tk), lambda b, h, i: (h, b, 0, 0)),
        ]
        args += [qk_lat, vt_lat]
    in_specs += [pl.BlockSpec((1, LANE), lambda b, h, i: (0, 0))] * 2
    args += [lam, subg]
    stat = pltpu.VMEM((1, 2 * tq), F32)
    scratch = [pltpu.VMEM((LANE, 2 * tq), BF16), stat, pltpu.VMEM((VT_ROWS, 2 * tq), F32)]
    if n_lat:
        scratch += [pltpu.VMEM((tk, 2 * tq), F32)] * 2 + [stat] * 2
        scratch += [pltpu.VMEM((tk, 2 * tq), BF16)] * 2 + [stat] * 2
    return pl.pallas_call(
        functools.partial(_attn_a_kernel, n_lat, tk, sub_scale),
        grid=(batch, heads, nq),
        in_specs=in_specs,
        out_specs=pl.BlockSpec((tq, LANE), lambda b, h, i: (b * nq + i, h)),
        out_shape=jax.ShapeDtypeStruct((batch * q_len, heads * LANE), BF16),
        scratch_shapes=scratch,
        compiler_params=_cp("parallel", "parallel", "parallel"),
        name="diff_attn",
    )(*args)


def _attn_b_kernel(has_loc, seq_len, q_heads, group, *refs):
    if has_loc:
        (q_ref, kx_ref, vx_ref, kp_ref, kc_ref, kn_ref, vp_ref, vc_ref, vn_ref, sink_ref, o_ref) = refs
    else:
        (q_ref, kx_ref, vx_ref, sink_ref, o_ref) = refs
    n = pl.program_id(1)
    clen = kx_ref.shape[0]
    q_t = q_ref[...].astype(F32).T
    keys = [kx_ref[...]]
    vals = [vx_ref[0]]
    if has_loc:
        keys += [kp_ref[...], kc_ref[...], kn_ref[...]]
        vals += [vp_ref[0], vc_ref[0], vn_ref[0]]
        kpos = (n - 1) * BLOCK + lax.broadcasted_iota(jnp.int32, (3 * BLOCK, BLOCK), 0)
        qpos = n * BLOCK + lax.broadcasted_iota(jnp.int32, (3 * BLOCK, BLOCK), 1)
        valid = (kpos >= 0) & (kpos < seq_len) & (jnp.abs(kpos - qpos) <= WINDOW)
        valid = jnp.concatenate([valid] * q_heads, axis=1)
    k_all = jnp.concatenate(keys, axis=0)
    v_all = jnp.concatenate(vals, axis=1)
    zeros = jnp.zeros((HEAD_DIM, BLOCK), F32)
    cols = []
    for hq in range(q_heads):
        blk = q_t[hq * HEAD_DIM:(hq + 1) * HEAD_DIM]
        cols.append(jnp.concatenate([blk, zeros] if hq // group == 0 else [zeros, blk], axis=0))
    q_all = jnp.concatenate(cols, axis=1).astype(BF16)
    sink = jnp.concatenate([sink_ref[hq:hq + 1, :] for hq in range(q_heads)], axis=1) * LOG2E
    s = _dot(k_all, q_all)
    if has_loc:
        s = jnp.concatenate([s[:clen], jnp.where(valid, s[clen:], NEG_INF)], axis=0)
    m = jnp.maximum(jnp.max(s, axis=0, keepdims=True), sink)
    o_t = _dot(v_all, jnp.exp2(s - m).astype(BF16))
    den = o_t[LANE:LANE + 1] + jnp.exp2(sink - m)
    heads_out = []
    for hq in range(q_heads):
        kvh = hq // group
        lanes = slice(hq * BLOCK, (hq + 1) * BLOCK)
        heads_out.append(o_t[kvh * HEAD_DIM:(kvh + 1) * HEAD_DIM, lanes] / den[:, lanes])
    o_ref[...] = jnp.concatenate(heads_out, axis=0).T.astype(BF16)


def _window_attention(q_arr, q_len, qk_ctx, vt_ctx, ctx_len, vt_lat, sink_tab, q_heads, kv_heads,
                      q_blk, k_blk, local):
    assert kv_heads == 2, "key/value heads must fill exactly one 128-lane block"
    batch = q_arr.shape[0] // q_len
    nb = q_len // BLOCK
    qw = q_heads * HEAD_DIM
    in_specs = [
        pl.BlockSpec((BLOCK, qw), lambda b, n: (b * nb + n, q_blk)),
        pl.BlockSpec((ctx_len, LANE), lambda b, n: (b, k_blk)),
        pl.BlockSpec((None, 1, VT_ROWS, ctx_len), lambda b, n: (0, b, 0, 0)),
    ]
    args = [q_arr, qk_ctx, vt_ctx]
    if local:
        prev = lambda b, n: b * nb + jnp.maximum(n - 1, 0)
        nxt = lambda b, n: b * nb + jnp.minimum(n + 1, nb - 1)
        in_specs += [
            pl.BlockSpec((BLOCK, LANE), lambda b, n: (prev(b, n), k_blk)),
            pl.BlockSpec((BLOCK, LANE), lambda b, n: (b * nb + n, k_blk)),
            pl.BlockSpec((BLOCK, LANE), lambda b, n: (nxt(b, n), k_blk)),
            pl.BlockSpec((None, 1, VT_ROWS, BLOCK), lambda b, n: (0, prev(b, n), 0, 0)),
            pl.BlockSpec((None, 1, VT_ROWS, BLOCK), lambda b, n: (0, b * nb + n, 0, 0)),
            pl.BlockSpec((None, 1, VT_ROWS, BLOCK), lambda b, n: (0, nxt(b, n), 0, 0)),
        ]
        args += [q_arr] * 3 + [vt_lat] * 3
    in_specs.append(pl.BlockSpec((q_heads, LANE), lambda b, n: (0, 0)))
    args.append(sink_tab)
    return pl.pallas_call(
        functools.partial(_attn_b_kernel, local, q_len, q_heads, q_heads // kv_heads),
        grid=(batch, nb),
        in_specs=in_specs,
        out_specs=pl.BlockSpec((BLOCK, qw), lambda b, n: (b * nb + n, 0)),
        out_shape=jax.ShapeDtypeStruct((batch * q_len, qw), BF16),
        compiler_params=_cp("parallel", "parallel"),
        name="window_attn",
    )(*args)


def _halo_specs(tile, width, col_blk, n_rows):
    r = tile // HALO
    last = n_rows // HALO - 1
    return [
        pl.BlockSpec((tile, width), lambda i: (i, col_blk)),
        pl.BlockSpec((HALO, width), lambda i: (jnp.maximum(i * r - 1, 0), col_blk)),
        pl.BlockSpec((HALO, width), lambda i: (jnp.minimum(i * r + r, last), col_blk)),
    ]


def _outside_rows(tile, tile_idx, tiles_per_seq):
    row = lax.broadcasted_iota(jnp.int32, (tile + 2 * HALO, 1), 0)
    lo = jnp.where(tile_idx == 0, HALO, 0)
    hi = jnp.where(tile_idx == tiles_per_seq - 1, HALO + tile, tile + 2 * HALO)
    return (row < lo) | (row >= hi)


def _with_halo(x, xp, xn, tiles_per_seq):
    ext = jnp.concatenate([xp, x, xn], axis=0)
    outside = _outside_rows(x.shape[0], pl.program_id(0) % tiles_per_seq, tiles_per_seq)
    return jnp.where(outside, 0.0, ext)


def _conv3(ext, w, b, tile):
    rows = ext.shape[0]
    c = (pltpu.roll(ext, 1, 0) * w[0:1] + ext * w[1:2] + pltpu.roll(ext, rows - 1, 0) * w[2:3])
    return c[HALO:HALO + tile] + b


def _pool_kernel(seq_len, tile, x_ref, xp_ref, xn_ref, w_ref, ls_ref, o_ref):
    tps = seq_len // tile
    x = x_ref[...]
    ext = _with_halo(x, xp_ref[...], xn_ref[...], tps)
    rows = tile + 2 * HALO
    t = (pl.program_id(0) % tps) * tile + lax.broadcasted_iota(jnp.int32, (tile, 1), 0)
    gw = x.shape[1] // len(POOL_WINDOWS)
    outs = []
    for g, w in enumerate(POOL_WINDOWS):
        lo = w // 2
        hi = w - 1 - lo
        a = ext[:, g * gw:(g + 1) * gw]
        span = 1
        while span < w:
            a = a + pltpu.roll(a, span, 0)
            span *= 2
        win = (pltpu.roll(a, rows - hi, 0) if hi else a)[HALO:HALO + tile]
        cnt = jnp.minimum(t + hi + 1, seq_len) - jnp.maximum(t - lo, 0)
        d = win / cnt.astype(F32) - x[:, g * gw:(g + 1) * gw]
        outs.append(_dot(d.astype(BF16), w_ref[g]))
    o_ref[...] = (jnp.concatenate(outs, axis=1) * ls_ref[...]).astype(BF16)


def _pool_mixer(p, col0, width, seq_len, w_lin, ls):
    n = p.shape[0]
    tile = _tile(seq_len, 1024)
    g, gw, _ = w_lin.shape
    return pl.pallas_call(
        functools.partial(_pool_kernel, seq_len, tile),
        grid=(n // tile,),
        in_specs=_halo_specs(tile, width, col0 // width, n)
        + [pl.BlockSpec((g, gw, gw), lambda i: (0, 0, 0)),
           pl.BlockSpec((1, width), lambda i: (0, 0))],
        out_specs=pl.BlockSpec((tile, width), lambda i: (i, 0)),
        out_shape=jax.ShapeDtypeStruct((n, width), BF16),
        compiler_params=_cp("parallel"),
        name="pool_mixer",
    )(p, p, p, w_lin, ls)


def _hypre_kernel(seq_len, tile, x_ref, xp_ref, xn_ref, cw_ref, cb_ref, z_ref, x2_ref):
    ext = _with_halo(x_ref[...], xp_ref[...], xn_ref[...], seq_len // tile)
    c = _conv3(ext, cw_ref[...], cb_ref[...], tile)
    dw = c.shape[1] // 3
    z_ref[...] = c[:, :dw] * c[:, dw:2 * dw]
    x2_ref[...] = c[:, 2 * dw:]


def _hyena_pre(p, col0, width, seq_len, conv_w, conv_b):
    n = p.shape[0]
    tile = _tile(seq_len, 1024)
    dw = width // 3
    out = jax.ShapeDtypeStruct((n, dw), F32)
    return pl.pallas_call(
        functools.partial(_hypre_kernel, seq_len, tile),
        grid=(n // tile,),
        in_specs=_halo_specs(tile, width, col0 // width, n)
        + [pl.BlockSpec((3, width), lambda i: (0, 0)),
           pl.BlockSpec((1, width), lambda i: (0, 0))],
        out_specs=[pl.BlockSpec((tile, dw), lambda i: (i, 0))] * 2,
        out_shape=[out, out],
        compiler_params=_cp("parallel"),
        name="hyena_pre",
    )(p, p, p, conv_w, conv_b.reshape(1, width))


def _hyfilt_kernel(seq_len, tile, zf_ref, w1_ref, b1_ref, w2_ref, b2_ref, w3_ref, fr_ref, dl_ref,
                   k_ref, ss_ref):
    i = pl.program_id(0)
    half = tile // 2
    hp = functools.partial(jnp.dot, precision=lax.Precision.HIGHEST, preferred_element_type=F32)
    zf = zf_ref[...]
    fr = fr_ref[...]
    h = jnp.sin(fr * (hp(zf, w1_ref[...]) + b1_ref[...]))
    h = jnp.sin(fr * (hp(h, w2_ref[...]) + b2_ref[...]))
    w3 = w3_ref[...]
    dl = dl_ref[...]
    ss = jnp.zeros(ss_ref.shape, F32)
    for part in range(2):
        k = hp(h, w3[part]) * jnp.exp(-zf[:, part * HY_PACK:part * HY_PACK + 1] * dl)
        row = i * tile + part * half + lax.broadcasted_iota(jnp.int32, (half, 1), 0)
        k = jnp.where(row == seq_len, 0.0, k)
        k_ref[part * half:(part + 1) * half, :] = k
        ss = ss + jnp.sum(k * k, axis=0, keepdims=True)

    @pl.when(i == 0)
    def _():
        ss_ref[...] = jnp.zeros(ss_ref.shape, F32)

    ss_ref[...] += ss


def _hyena_filter(seq_len, w1, b1, w2, b2, w3, freq):
    dw = w3.shape[1] // 2
    bands = (HY_EMB - 1) // 2
    t01 = jnp.linspace(0.0, 1.0, seq_len, dtype=F32)[:, None]
    w_ang = 2.0 * math.pi * jnp.arange(seq_len, dtype=F32)[:, None] / seq_len
    f = jnp.linspace(1e-4, bands - 1, bands, dtype=F32)[None, :]
    z = jnp.concatenate([t01, jnp.cos(f * w_ang), -jnp.sin(f * w_ang)], axis=-1)
    rev = (seq_len - jnp.arange(seq_len)) % seq_len
    assert HY_HIDDEN == HY_PACK and HY_EMB <= HY_PACK
    tile = _tile(seq_len, 1024)
    tps = seq_len // tile
    half = tile // 2
    zf = jnp.pad(jnp.concatenate([z, z[rev]], axis=0), ((0, 0), (0, HY_PACK - HY_EMB)))
    zf = zf.reshape(2 * tps, 2, half, HY_PACK).transpose(0, 2, 1, 3).reshape(2 * tps * half, LANE)
    zero = jnp.zeros((HY_PACK, HY_PACK), F32)
    blockdiag = lambda w: jnp.block([[w, zero], [zero, w]])
    w1p = blockdiag(jnp.pad(w1, ((0, HY_PACK - HY_EMB), (0, 0))))
    w2p = blockdiag(w2)
    w3d = w3.reshape(HY_HIDDEN, 2, dw).transpose(1, 0, 2)
    zrows = jnp.zeros_like(w3d)
    w3p = jnp.stack([jnp.concatenate([w3d, zrows], axis=1), jnp.concatenate([zrows, w3d], axis=1)], axis=1)
    vec = lambda v: jnp.concatenate([v, v]).reshape(1, LANE)
    deltas = jnp.abs(jnp.linspace(math.log(HY_TARGET) / HY_FAST_DECAY, math.log(HY_TARGET) / HY_SLOW_DECAY,
                                  dw, dtype=F32)).reshape(1, dw)
    full = lambda shape: pl.BlockSpec(shape, lambda i: (0, 0))
    return pl.pallas_call(
        functools.partial(_hyfilt_kernel, seq_len, tile),
        grid=(2 * tps,),
        in_specs=[pl.BlockSpec((half, LANE), lambda i: (i, 0)),
                  full((LANE, LANE)), full((1, LANE)), full((LANE, LANE)), full((1, LANE)),
                  pl.BlockSpec((None, 2, LANE, dw), lambda i: (i // tps, 0, 0, 0)),
                  full((1, LANE)), full((1, dw))],
        out_specs=[pl.BlockSpec((tile, dw), lambda i: (i, 0)), full((1, dw))],
        out_shape=[jax.ShapeDtypeStruct((2 * seq_len, dw), F32), jax.ShapeDtypeStruct((1, dw), F32)],
        compiler_params=_cp("arbitrary"),
        name="hyena_filter",
    )(zf, w1p, vec(b1), w2p, vec(b2), w3p, vec(freq), deltas)


def _dft_tables(n1, n2):
    n = n1 * n2
    nf = min(n1, -(-(n1 // 2 + 1) // DFT_BLOCKS) * DFT_BLOCKS)
    two_pi = 2.0 * math.pi

    def cs(num, den):
        ang = (num % den).astype(F32) * (two_pi / den)
        return jnp.cos(ang), jnp.sin(ang)

    a = jnp.arange(n1)
    c1, s1 = cs(a[:, None] * a[None, :], n1)
    kron = lambda mat: jnp.kron(mat, jnp.eye(SUBLANE, dtype=F32))
    fwd_a = jnp.concatenate([kron(c1[:nf]), kron(-s1[:nf])], axis=0)
    wgt = jnp.where((a == 0) | (a == n1 // 2), 1.0, jnp.where(a < n1 // 2, 2.0, 0.0))[None, :nf] / n
    inv_a = jnp.concatenate([kron(c1[: n1 // 2, :nf] * wgt), kron(-s1[: n1 // 2, :nf] * wgt)], axis=1)
    k2 = jnp.arange(n2)
    num = (k2[None, :, None] * k2[None, None, :] * n1 + a[:nf, None, None] * k2[None, None, :])
    cb, sb = cs(num, n)
    fwd_b = jnp.concatenate([jnp.concatenate([cb, sb], axis=2),
                             jnp.concatenate([-sb, cb], axis=2)], axis=1)
    cbt, sbt = jnp.swapaxes(cb, 1, 2), jnp.swapaxes(sb, 1, 2)
    inv_b = jnp.concatenate([jnp.concatenate([cbt, -sbt], axis=2),
                             jnp.concatenate([sbt, cbt], axis=2)], axis=1)
    return tuple(m.astype(BF16) for m in (fwd_a, inv_a, fwd_b, inv_b)), nf


def _rows(x):
    a, g, c = x.shape
    return x.reshape(a * g, c)


def _dft_a_kernel(f_ref, x_ref, re_ref, im_ref):
    y = _dot(f_ref[...], _rows(x_ref[...]).astype(BF16))
    half = y.shape[0] // 2
    re_ref[...] = y[:half].reshape(re_ref.shape)
    im_ref[...] = y[half:].reshape(im_ref.shape)


def _dft_stage_a(xv, fwd_a, nf):
    b, k, n2, ch = xv.shape
    groups = n2 // SUBLANE
    out = jax.ShapeDtypeStruct((b, nf, n2, ch), F32)
    return pl.pallas_call(
        _dft_a_kernel,
        grid=(b, groups),
        in_specs=[pl.BlockSpec((2 * nf * SUBLANE, k * SUBLANE), lambda bi, j: (0, 0)),
                  pl.BlockSpec((None, k, SUBLANE, ch), lambda bi, j: (bi, 0, j, 0))],
        out_specs=[pl.BlockSpec((None, nf, SUBLANE, ch), lambda bi, j: (bi, 0, j, 0))] * 2,
        out_shape=[out, out],
        compiler_params=_cp("parallel", "parallel"),
        name="dft_stage_a",
    )(fwd_a[:, :k * SUBLANE], xv)


def _dft_b_kernel(m_ref, re_ref, im_ref, ss_ref, kr_ref, ki_ref):
    nrm = lax.rsqrt(ss_ref[...] + EPS)
    n2 = m_ref.shape[1] // 2
    for k in range(m_ref.shape[0]):
        rows = slice(k * n2, (k + 1) * n2)
        x = _dot(m_ref[k], jnp.concatenate([re_ref[rows], im_ref[rows]], axis=0).astype(BF16))
        kr_ref[rows] = x[:n2] * nrm
        ki_ref[rows] = x[n2:] * nrm


def _filter_spectrum(kern, sumsq, tabs, nf, n1, n2):
    fwd_a, _, fwd_b, _ = tabs
    ch = kern.shape[1]
    are, aim = _dft_stage_a(kern.reshape(1, n1, n2, ch), fwd_a, nf)
    are, aim = are.reshape(nf * n2, ch), aim.reshape(nf * n2, ch)
    kb = _tile(nf, DFT_BLOCKS)
    blk = pl.BlockSpec((kb * n2, ch), lambda i: (i, 0))
    out = jax.ShapeDtypeStruct((nf * n2, ch), F32)
    return pl.pallas_call(
        _dft_b_kernel,
        grid=(nf // kb,),
        in_specs=[pl.BlockSpec((kb, 2 * n2, 2 * n2), lambda i: (i, 0, 0)), blk, blk,
                  pl.BlockSpec((1, ch), lambda i: (0, 0))],
        out_specs=[blk, blk],
        out_shape=[out, out],
        compiler_params=_cp("parallel"),
        name="filter_spectrum",
    )(fwd_b, are, aim, sumsq)


def _dft_mul_kernel(f_ref, g_ref, re_ref, im_ref, kr_ref, ki_ref, qr_ref, qi_ref):
    n2 = f_ref.shape[1] // 2
    for k in range(f_ref.shape[0]):
        rows = slice(k * n2, (k + 1) * n2)
        x = _dot(f_ref[k], jnp.concatenate([re_ref[rows], im_ref[rows]], axis=0).astype(BF16))
        xr, xi = x[:n2], x[n2:]
        kr, ki = kr_ref[rows], ki_ref[rows]
        prod = jnp.concatenate([xr * kr - xi * ki, xr * ki + xi * kr], axis=0)
        q = _dot(g_ref[k], prod.astype(BF16))
        qr_ref[rows] = q[:n2]
        qi_ref[rows] = q[n2:]


def _dft_inv_kernel(g_ref, qr_ref, qi_ref, z_ref, x2_ref, bias_ref, o_ref):
    q = jnp.concatenate([_rows(qr_ref[...]), _rows(qi_ref[...])], axis=0)
    y = _dot(g_ref[...], q.astype(BF16)).reshape(z_ref.shape)
    o_ref[...] = x2_ref[...] * (y + z_ref[...] * bias_ref[...])


def _hyena_long(z, x2, kr, ki, bias, tabs, batch, seq_len, nf, n1, n2):
    fwd_a, inv_a, fwd_b, inv_b = tabs
    ch = z.shape[1]
    zv = z.reshape(batch, n1 // 2, n2, ch)
    are, aim = _dft_stage_a(zv, fwd_a, nf)
    are, aim = are.reshape(batch, nf * n2, ch), aim.reshape(batch, nf * n2, ch)
    kb = _tile(nf, DFT_BLOCKS)
    mat = pl.BlockSpec((kb, 2 * n2, 2 * n2), lambda i, b: (i, 0, 0))
    dat = pl.BlockSpec((None, kb * n2, ch), lambda i, b: (b, i, 0))
    spec = pl.BlockSpec((kb * n2, ch), lambda i, b: (i, 0))
    out = jax.ShapeDtypeStruct((batch, nf * n2, ch), F32)
    qr, qi = pl.pallas_call(
        _dft_mul_kernel,
        grid=(nf // kb, batch),
        in_specs=[mat, mat, dat, dat, spec, spec],
        out_specs=[dat, dat],
        out_shape=[out, out],
        compiler_params=_cp("parallel", "parallel"),
        name="dft_filter_mul",
    )(fwd_b, inv_b, are, aim, kr, ki)
    qr, qi = qr.reshape(batch, nf, n2, ch), qi.reshape(batch, nf, n2, ch)
    full = pl.BlockSpec((None, nf, SUBLANE, ch), lambda b, j: (b, 0, j, 0))
    half = pl.BlockSpec((None, n1 // 2, SUBLANE, ch), lambda b, j: (b, 0, j, 0))
    o = pl.pallas_call(
        _dft_inv_kernel,
        grid=(batch, n2 // SUBLANE),
        in_specs=[pl.BlockSpec(inv_a.shape, lambda b, j: (0, 0)), full, full, half, half,
                  pl.BlockSpec((1, ch), lambda b, j: (0, 0))],
        out_specs=half,
        out_shape=jax.ShapeDtypeStruct((batch, n1 // 2, n2, ch), F32),
        compiler_params=_cp("parallel", "parallel"),
        name="dft_inverse_gate",
    )(inv_a, qr, qi, zv, x2.reshape(batch, n1 // 2, n2, ch), bias.reshape(1, ch))
    return o.reshape(batch * seq_len, ch)


def _hyena_short_kernel(f_ref, g_ref, k_ref, ss_ref, z_ref, x2_ref, bias_ref, o_ref):
    f = f_ref[...]
    z = z_ref[...]
    seq = z.shape[0]
    ks = _dot(f, k_ref[...].astype(BF16)) * lax.rsqrt(ss_ref[...] + EPS)
    zs = _dot(f[:, :seq], z.astype(BF16))
    n = ks.shape[0] // 2
    kr, ki, zr, zi = ks[:n], ks[n:], zs[:n], zs[n:]
    prod = jnp.concatenate([zr * kr - zi * ki, zr * ki + zi * kr], axis=0)
    y = _dot(g_ref[...], prod.astype(BF16))
    o_ref[...] = x2_ref[...] * (y + z * bias_ref[...])


def _hyena_short(z, x2, kern, sumsq, bias, batch, seq_len):
    n = 2 * seq_len
    ch = z.shape[1]
    a = jnp.arange(n)
    ang = ((a[:, None] * a[None, :]) % n).astype(F32) * (2.0 * math.pi / n)
    c, s = jnp.cos(ang), jnp.sin(ang)
    f = jnp.concatenate([c, -s], axis=0).astype(BF16)
    g = (jnp.concatenate([c, -s], axis=1)[:seq_len] / n).astype(BF16)
    full = lambda shape: pl.BlockSpec(shape, lambda b: (0, 0))
    dat = pl.BlockSpec((seq_len, ch), lambda b: (b, 0))
    return pl.pallas_call(
        _hyena_short_kernel,
        grid=(batch,),
        in_specs=[full((2 * n, n)), full((seq_len, 2 * n)),
                  full((n, ch)), full((1, ch)), dat, dat, full((1, ch))],
        out_specs=dat,
        out_shape=jax.ShapeDtypeStruct((batch * seq_len, ch), F32),
        compiler_params=_cp("parallel"),
        name="hyena_short_conv",
    )(f, g, kern, sumsq, z, x2, bias.reshape(1, ch))


def _hyena_mixer(p, col0, width, batch, seq_len, conv_w, conv_b, w1, b1, w2, b2, w3, freq, bias):
    z, x2 = _hyena_pre(p, col0, width, seq_len, conv_w, conv_b)
    kern, sumsq = _hyena_filter(seq_len, w1, b1, w2, b2, w3, freq)
    n = 2 * seq_len
    if n <= 4 * DFT_RADIX:
        return _hyena_short(z, x2, kern, sumsq, bias, batch, seq_len)
    n2 = DFT_RADIX
    n1 = n // n2
    tabs, nf = _dft_tables(n1, n2)
    kr, ki = _filter_spectrum(kern, sumsq, tabs, nf, n1, n2)
    return _hyena_long(z, x2, kr, ki, bias, tabs, batch, seq_len, nf, n1, n2)


def _outproj_kernel(x_ref, a_ref, b_ref, c_ref, d_ref, w_ref, g_ref, o_ref):
    acc = None
    row = 0
    for m_ref in (a_ref, b_ref, c_ref, d_ref):
        wdt = m_ref.shape[1]
        part = _dot(m_ref[...].astype(BF16), w_ref[row:row + wdt, :])
        acc = part if acc is None else acc + part
        row += wdt
    o_ref[...] = x_ref[...] + g_ref[...] * acc


def _out_projection(x2d, mixers, w, layer, gate, rows_per_group):
    n, d = x2d.shape
    tm = _tile(rows_per_group, 512)
    tpg = rows_per_group // tm
    return pl.pallas_call(
        _outproj_kernel,
        grid=(n // tm,),
        in_specs=[pl.BlockSpec((tm, d), lambda i: (i, 0))]
        + [pl.BlockSpec((tm, m.shape[1]), lambda i: (i, 0)) for m in mixers]
        + [pl.BlockSpec((None,) + w.shape[1:], lambda i: (layer, 0, 0)),
           pl.BlockSpec((None, 1, d), lambda i: (i // tpg, 0, 0))],
        out_specs=pl.BlockSpec((tm, d), lambda i: (i, 0)),
        out_shape=jax.ShapeDtypeStruct((n, d), F32),
        compiler_params=_cp("parallel"),
        name="out_proj",
    )(x2d, *mixers, w, gate)


def _ffn_kernel(seq_len, tile, x_ref, xp_ref, xn_ref, g_ref, sc_ref, sh_ref, gate_ref,
                wg_ref, wu_ref, cwg_ref, cwu_ref, cbg_ref, cbu_ref, wd_ref, o_ref, h_scr):
    j = pl.program_id(1)

    @pl.when(j == 0)
    def _():
        g, sc, sh = g_ref[...], sc_ref[...], sh_ref[...]
        tps = seq_len // tile
        ti = pl.program_id(0) % tps
        halo = jnp.concatenate([xn_ref[...], xp_ref[...]], axis=0)
        row = lax.broadcasted_iota(jnp.int32, (2 * HALO, 1), 0)
        outside = (row < jnp.where(ti == tps - 1, HALO, 0)) | (row >= jnp.where(ti == 0, HALO, 2 * HALO))
        h_scr[tile:, :] = jnp.where(outside, 0.0, _norm_mod(halo, g, sc, sh)).astype(BF16)
        chunk = min(tile, 4 * LANE)

        def body(c, carry):
            rows = pl.ds(pl.multiple_of(c * chunk, chunk), chunk)
            h_scr[rows, :] = _norm_mod(x_ref[rows, :], g, sc, sh).astype(BF16)
            return carry
        lax.fori_loop(0, tile // chunk, body, 0)
        o_ref[...] = jnp.zeros(o_ref.shape, F32)

    def conv3(u, w, b):
        rows = u.shape[0]
        c = pltpu.roll(u, 1, 0) * w[0:1] + u * w[1:2] + pltpu.roll(u, rows - 1, 0) * w[2:3]
        return c[:tile] + b

    h = h_scr[...]
    gate = conv3(_dot(h, wg_ref[...]), cwg_ref[...], cbg_ref[...])
    up = conv3(_dot(h, wu_ref[...]), cwu_ref[...], cbu_ref[...])
    act = (gate * jax.nn.sigmoid(gate)) * up
    o_ref[...] += _dot(act.astype(BF16), wd_ref[...])

    @pl.when(j == pl.num_programs(1) - 1)
    def _():
        o_ref[...] = x_ref[...] + gate_ref[...] * o_ref[...]


def _conv_ffn(x2d, g, sc, sh, gate, w_up, conv_w, conv_b, w_down, layer, seq_len, rows_per_group):
    n, d = x2d.shape
    f = w_down.shape[1]
    tile = _tile(seq_len, 1024)
    hc = _tile(f, 512)
    nh = f // hc
    once = pl.Buffered(1)
    tpg = rows_per_group // tile
    r = tile // HALO
    last = n // HALO - 1
    vec = pl.BlockSpec((None, 1, d), lambda i, j: (i // tpg, 0, 0))
    cb = conv_b.reshape(1, 2 * f)
    return pl.pallas_call(
        functools.partial(_ffn_kernel, seq_len, tile),
        grid=(n // tile, nh),
        in_specs=[
            pl.BlockSpec((tile, d), lambda i, j: (i, 0)),
            pl.BlockSpec((HALO, d), lambda i, j: (jnp.maximum(i * r - 1, 0), 0)),
            pl.BlockSpec((HALO, d), lambda i, j: (jnp.minimum(i * r + r, last), 0)),
            pl.BlockSpec((1, d), lambda i, j: (0, 0)),
            vec, vec, vec,
            pl.BlockSpec((None, d, hc), lambda i, j: (layer, 0, j)),
            pl.BlockSpec((None, d, hc), lambda i, j: (layer, 0, nh + j)),
            pl.BlockSpec((3, hc), lambda i, j: (0, j)),
            pl.BlockSpec((3, hc), lambda i, j: (0, nh + j)),
            pl.BlockSpec((1, hc), lambda i, j: (0, j)),
            pl.BlockSpec((1, hc), lambda i, j: (0, nh + j)),
            pl.BlockSpec((None, hc, d), lambda i, j: (layer, j, 0)),
        ],
        out_specs=pl.BlockSpec((tile, d), lambda i, j: (i, 0), pipeline_mode=once),
        out_shape=jax.ShapeDtypeStruct((n, d), F32),
        scratch_shapes=[pltpu.VMEM((tile + 2 * HALO, d), BF16)],
        compiler_params=_cp("parallel", "arbitrary"),
        name="conv_ffn",
    )(x2d, x2d, x2d, g, sc, sh, gate, w_up, w_up, conv_w, conv_w, cb, cb, w_down)


def kernel(x, c, ctx, c_ctx, w_mod, b_mod, norm1_g, norm2_g, w_in, w_out, qk_gain, diff_lam, diff_subln,
           win_sink, pool_w, pool_scale, hy_conv_w, hy_conv_b, hy_w1, hy_b1, hy_w2, hy_b2, hy_w3, hy_freq,
           hy_bias, ffn_w_in, ffn_conv_w, ffn_conv_b, ffn_w_out):
    batch, seq, d = x.shape
    clen = ctx.shape[1]
    depth = w_in.shape[0]
    a_heads = d // 512
    a_width = a_heads * 2 * HEAD_DIM
    b_heads = d // 256
    b_kv = b_heads // 4
    b_width = b_heads * HEAD_DIM
    b_kvw = b_kv * HEAD_DIM
    c_width = d // 4
    d_width = d - a_width - b_width - c_width
    off_ka = a_width
    off_va = 2 * a_width
    off_qb = 3 * a_width
    off_kb = off_qb + b_width
    off_vb = off_kb + b_kvw
    off_pool = off_vb + b_kvw
    off_hy = off_pool + c_width
    in_cols = off_hy + 3 * d_width
    order = ((off_hy, in_cols), (off_pool, off_hy), (off_va, off_qb), (0, off_ka), (off_ka, off_va),
             (off_qb, off_kb), (off_kb, off_vb), (off_vb, off_pool))
    n_hy, n_pool, n_va, n_qa, n_ka, n_qb, n_kb, n_vb = (
        sum(e - s for s, e in order[:k]) for k in range(len(order)))
    assert a_width % LANE == 0 and b_width % LANE == 0 and b_kvw == LANE and c_width // 4 == LANE
    assert seq % BLOCK == 0 and clen % BLOCK == 0 and n_pool % c_width == 0
    assert (n_ka - n_qa) == a_width and (n_qb - n_qa) % b_width == 0 and n_kb - n_qb == b_width

    rows = -(-(batch + 1) // SUBLANE) * SUBLANE
    cvec = jnp.zeros((rows, d), F32).at[:batch].set(c).at[batch].set(c_ctx)
    mods = _modulation(cvec, w_mod, b_mod)

    w_in_b = jnp.concatenate([w_in[:, :, s:e] for s, e in order], axis=-1).astype(BF16)
    w_out_b = w_out.astype(BF16)
    ffn_in_b = ffn_w_in.astype(BF16)
    ffn_out_b = ffn_w_out.astype(BF16)

    cos_t, sin_t = _rope_tables(seq)
    scale = HEAD_DIM ** -0.5
    tile2 = lambda v: jnp.tile(v, LANE // HEAD_DIM)

    xs = x.reshape(batch * seq, d)
    cs = ctx.reshape(batch * clen, d)
    for l in range(depth):
        last = l == depth - 1
        m = mods[l]
        sh1, sc1, g1, sh2, sc2, g2 = [m[:batch, i * d:(i + 1) * d].reshape(batch, 1, d) for i in range(6)]
        csh1, csc1, cg1, csh2, csc2, cg2 = [m[batch, i * d:(i + 1) * d].reshape(1, 1, d) for i in range(6)]
        n1g = norm1_g[l].reshape(1, d)
        n2g = norm2_g[l].reshape(1, d)
        pool_w_l = pool_w[l].astype(BF16)

        p = _in_projection(xs, n1g, sc1, sh1, w_in_b, l, seq)
        pc = _in_projection(cs, n1g, csc1, csh1, w_in_b, l, batch * clen)

        na = a_width // LANE
        nbq = b_width // LANE
        gains = jnp.stack([tile2(qk_gain[l, 0]) * (scale * LOG2E)] * na + [tile2(qk_gain[l, 1])] * na
                          + [tile2(qk_gain[l, 2]) * (scale * LOG2E)] * nbq + [tile2(qk_gain[l, 3])])
        qk = _qk_prep(p, n_qa // LANE, gains, cos_t, sin_t, seq, True)
        qk_c = _qk_prep(pc, n_qa // LANE, gains, cos_t, sin_t, clen, False)
        kb_blk = 2 * na + nbq
        qb_blk = (n_qb - n_qa) // b_width

        lam_p = diff_lam[l].astype(F32)
        lambda_init = 0.8 - 0.6 * math.exp(-0.3 * l)
        lam = jnp.exp(jnp.sum(lam_p[0] * lam_p[1])) - jnp.exp(jnp.sum(lam_p[2] * lam_p[3])) + lambda_init
        lam_v = jnp.full((1, LANE), lam, F32)
        subg = diff_subln[l].reshape(1, LANE)
        sink_tab = jnp.broadcast_to(win_sink[l].astype(F32)[:, None], (b_heads, LANE))
        hy = (hy_conv_w[l], hy_conv_b[l], hy_w1[l], hy_b1[l], hy_w2[l], hy_b2[l], hy_w3[l], hy_freq[l],
              hy_bias[l])

        vt = _transposed_values(p, n_va // LANE, a_heads, seq, _tile(seq, 512))
        vt_c = _transposed_values(pc, n_va // LANE, a_heads, clen, clen)
        o_a = _diff_attention(qk, seq, qk_c, vt_c, clen, qk, vt, seq, lam_v, subg, 1.0 - lambda_init,
                              a_heads, na)
        vtb = _transposed_values(p, n_vb // LANE, 1, seq, BLOCK)
        vtb_c = _transposed_values(pc, n_vb // LANE, 1, clen, clen)
        o_b = _window_attention(qk, seq, qk_c, vtb_c, clen, vtb, sink_tab, b_heads, b_kv,
                                qb_blk, kb_blk, True)
        o_c = _pool_mixer(p, n_pool, c_width, seq, pool_w_l, pool_scale[l].reshape(1, c_width))
        o_d = _hyena_mixer(p, n_hy, 3 * d_width, batch, seq, *hy)
        xs = _out_projection(xs, (o_a, o_b, o_c, o_d), w_out_b, l, g1, seq)
        xs = _conv_ffn(xs, n2g, sc2, sh2, g2, ffn_in_b, ffn_conv_w[l], ffn_conv_b[l], ffn_out_b, l,
                       seq, seq)

        if not last:
            o_a_c = _diff_attention(qk_c, clen, qk_c, vt_c, clen, None, None, 0, lam_v, subg,
                                    1.0 - lambda_init, a_heads, na)
            o_b_c = _window_attention(qk_c, clen, qk_c, vtb_c, clen, None, sink_tab, b_heads, b_kv,
                                      qb_blk, kb_blk, False)
            o_c_c = _pool_mixer(pc, n_pool, c_width, clen, pool_w_l, pool_scale[l].reshape(1, c_width))
            o_d_c = _hyena_mixer(pc, n_hy, 3 * d_width, batch, clen, *hy)
            cs = _out_projection(cs, (o_a_c, o_b_c, o_c_c, o_d_c), w_out_b, l, cg1, batch * clen)
            cs = _conv_ffn(cs, n2g, csc2, csh2, cg2, ffn_in_b, ffn_conv_w[l], ffn_conv_b[l],
                           ffn_out_b, l, clen, batch * clen)
    return xs.reshape(batch, seq, d)
```

```python
import functools
import math

import jax
import jax.numpy as jnp
from jax import lax
from jax.experimental import pallas as pl
from jax.experimental.pallas import tpu as pltpu

F32 = jnp.float32
BF16 = jnp.bfloat16

HEAD_DIM = 64
GRID_W = 64
ROPE_BASE = 10000.0
EPS = 1e-6
NEG_INF = -1e30
WINDOW = 128
BLOCK = 128
POOL_WINDOWS = (2, 4, 8, 16)
HY_EMB = 33
HY_HIDDEN = 64
HY_FAST_DECAY = 0.3
HY_SLOW_DECAY = 1.5
HY_TARGET = 1e-2

LANE = 128
SUBLANE = 8
HALO = SUBLANE
DFT_RADIX = 128
DFT_BLOCKS = 8
DFT_STEP_ROWS = 32
BF16_ROWS = 2 * SUBLANE
VT_ROWS = LANE + BF16_ROWS
LOG2E = 1.4426950408889634
HY_PACK = LANE // 2
VMEM_LIMIT = 56 * 1024 * 1024


def _cp(*sem):
    return pltpu.CompilerParams(dimension_semantics=sem, vmem_limit_bytes=VMEM_LIMIT)


def _tile(n, pref):
    if n <= pref:
        return n
    t = pref
    while n % t:
        t //= 2
    assert t >= SUBLANE, (n, pref)
    return t


def _dot(a, b):
    return jnp.dot(a, b, preferred_element_type=F32)


def _mod_kernel(c_ref, w_ref, b_ref, o_ref):
    c = c_ref[...]
    a = (c * jax.nn.sigmoid(c)).astype(BF16)
    o_ref[...] = _dot(a, w_ref[...].astype(BF16)) + b_ref[...]


def _modulation(cvec, w_mod, b_mod):
    depth, d, n = w_mod.shape
    rows = cvec.shape[0]
    tn = _tile(n, 1024)
    return pl.pallas_call(
        _mod_kernel,
        grid=(depth, n // tn),
        in_specs=[
            pl.BlockSpec((rows, d), lambda l, j: (0, 0)),
            pl.BlockSpec((None, d, tn), lambda l, j: (l, 0, j)),
            pl.BlockSpec((None, 1, tn), lambda l, j: (l, 0, j)),
        ],
        out_specs=pl.BlockSpec((None, rows, tn), lambda l, j: (l, 0, j)),
        out_shape=jax.ShapeDtypeStruct((depth, rows, n), F32),
        compiler_params=_cp("parallel", "parallel"),
        name="adaln_mod",
    )(cvec, w_mod, b_mod.reshape(depth, 1, n))


def _norm_mod(x, g, sc, sh):
    ms = jnp.mean(x * x, axis=-1, keepdims=True)
    return (x * lax.rsqrt(ms + EPS) * g) * (1.0 + sc) + sh


def _inproj_kernel(x_ref, g_ref, sc_ref, sh_ref, w_ref, o_ref):
    h = _norm_mod(x_ref[...], g_ref[...], sc_ref[...], sh_ref[...])
    o_ref[...] = _dot(h.astype(BF16), w_ref[...])


def _in_projection(x2d, g, sc, sh, w, layer, rows_per_group):
    n, d = x2d.shape
    nc = w.shape[2]
    tm = _tile(rows_per_group, 512)
    tpg = rows_per_group // tm
    vec = pl.BlockSpec((None, 1, d), lambda i: (i // tpg, 0, 0))
    return pl.pallas_call(
        _inproj_kernel,
        grid=(n // tm,),
        in_specs=[
            pl.BlockSpec((tm, d), lambda i: (i, 0)),
            pl.BlockSpec((1, d), lambda i: (0, 0)),
            vec, vec,
            pl.BlockSpec((None, d, nc), lambda i: (layer, 0, 0), pipeline_mode=pl.Buffered(1)),
        ],
        out_specs=pl.BlockSpec((tm, nc), lambda i: (i, 0)),
        out_shape=jax.ShapeDtypeStruct((n, nc), F32),
        compiler_params=_cp("parallel"),
        name="in_proj",
    )(x2d, g, sc, sh, w)


def _qkprep_kernel(rope, x_ref, gain_ref, cos_ref, sin_ref, gm_ref, pm_ref, o_ref):
    x = x_ref[...]
    ms = _dot((x * x).astype(BF16), gm_ref[...])
    y = x * lax.rsqrt(ms + EPS) * gain_ref[...]
    if rope:
        cos = jnp.concatenate([cos_ref[...]] * 2, axis=1)
        sin = jnp.concatenate([sin_ref[...]] * 2, axis=1)
        y = y * cos + _dot(y.astype(BF16), pm_ref[...]) * sin
    o_ref[...] = y.astype(BF16)


def _qk_prep(p, col_block0, gains, cos_t, sin_t, seq_len, rope):
    n = p.shape[0]
    nblk = gains.shape[0]
    wide = 2 * LANE
    nw = -(-nblk // 2)
    assert col_block0 % 2 == 0 and (col_block0 + 2 * nw) * LANE <= p.shape[1]
    gains = jnp.concatenate([gains, jnp.ones((2 * nw - nblk, LANE), F32)], axis=0).reshape(nw, 1, wide)
    tm = _tile(seq_len, 2048)
    tps = seq_len // tm
    lane = jnp.arange(wide)
    gm = ((lane[:, None] // HEAD_DIM) == (lane[None, :] // HEAD_DIM)).astype(BF16) * (1.0 / HEAD_DIM)
    partner = jnp.where((lane % 32) < 16, lane + 16, lane - 16)
    pm = (lane[:, None] == partner[None, :]).astype(BF16)
    mat = pl.BlockSpec((wide, wide), lambda i, j: (0, 0))
    tab = pl.BlockSpec((tm, LANE), lambda i, j: (i % tps, 0))
    return pl.pallas_call(
        functools.partial(_qkprep_kernel, rope),
        grid=(n // tm, nw),
        in_specs=[
            pl.BlockSpec((tm, wide), lambda i, j: (i, col_block0 // 2 + j)),
            pl.BlockSpec((None, 1, wide), lambda i, j: (j, 0, 0)),
            tab, tab, mat, mat,
        ],
        out_specs=pl.BlockSpec((tm, wide), lambda i, j: (i, j)),
        out_shape=jax.ShapeDtypeStruct((n, nw * wide), BF16),
        compiler_params=_cp("parallel", "parallel"),
        name="qk_prep",
    )(p, gains, cos_t, sin_t, gm.astype(BF16), pm)


def _rope_tables(seq_len):
    rows = seq_len // GRID_W
    row = jnp.repeat(jnp.arange(rows, dtype=F32), GRID_W)
    col = jnp.tile(jnp.arange(GRID_W, dtype=F32), rows)
    half = HEAD_DIM // 2
    inv = ROPE_BASE ** (-jnp.arange(0, half, 2, dtype=F32) / half)
    ang = jnp.concatenate([row[:, None] * inv, col[:, None] * inv], axis=-1)
    lane = jnp.arange(LANE)
    within = lane % HEAD_DIM
    idx = (within // 32) * 16 + (within % 16)
    sign = jnp.where((lane % 32) < 16, -1.0, 1.0).astype(F32)
    return jnp.cos(ang)[:, idx], jnp.sin(ang)[:, idx] * sign


def _vt_kernel(x_ref, o_ref):
    chunks, _, chunk = o_ref.shape
    row = lax.broadcasted_iota(jnp.int32, (VT_ROWS - LANE, chunk), 0)
    ones_row = jnp.where(row == 0, 1.0, 0.0).astype(BF16)
    for c in range(chunks):
        x_t = x_ref[c * chunk:(c + 1) * chunk, :].T.astype(BF16)
        o_ref[c] = jnp.concatenate([x_t, ones_row], axis=0)


def _transposed_values(p, col_block0, nblk, seq_len, chunk):
    n = p.shape[0]
    rows = max(chunk, _tile(seq_len, 2048))
    return pl.pallas_call(
        _vt_kernel,
        grid=(n // rows, nblk),
        in_specs=[pl.BlockSpec((rows, LANE), lambda i, j: (i, col_block0 + j))],
        out_specs=pl.BlockSpec((None, rows // chunk, VT_ROWS, chunk), lambda i, j: (j, i, 0, 0)),
        out_shape=jax.ShapeDtypeStruct((nblk, n // chunk, VT_ROWS, chunk), BF16),
        compiler_params=_cp("parallel", "parallel"),
        name="values_t",
    )(p)


def _attn_a_kernel(n_lat, tk, sub_scale, *refs):
    if n_lat:
        (q_ref, kc_ref, vc_ref, k_ref, v_ref, lam_ref, subg_ref, o_ref, qt, m, acc,
         s0, s1, x0, x1, p0, p1, al0, al1) = refs
    else:
        q_ref, kc_ref, vc_ref, lam_ref, subg_ref, o_ref, qt, m, acc = refs
    tq = q_ref.shape[0]
    q_t = q_ref[...].astype(F32).T
    row = lax.broadcasted_iota(jnp.int32, q_t.shape, 0)
    qt[:, :tq] = jnp.where(row < HEAD_DIM, q_t, 0.0).astype(BF16)
    qt[:, tq:] = jnp.where(row >= HEAD_DIM, q_t, 0.0).astype(BF16)
    m[...] = jnp.full(m.shape, -jnp.inf, F32)
    acc[...] = jnp.zeros(acc.shape, F32)

    def softmax_update(s, s_max):
        m_prev = m[...]
        m_new = jnp.maximum(m_prev, s_max)
        m[...] = m_new
        return jnp.exp2(s - m_new).astype(BF16), jnp.exp2(m_prev - m_new)

    s_c = _dot(kc_ref[...], qt[...])

    def context_update():
        p_c, alpha_c = softmax_update(s_c, jnp.max(s_c, axis=0, keepdims=True))
        acc[...] = alpha_c * acc[...] + _dot(vc_ref[0], p_c)

    if not n_lat:
        context_update()
    else:
        def scores(c, s_buf, x_buf):
            s = _dot(k_ref[pl.ds(pl.multiple_of(c * tk, tk), tk), :], qt[...])
            s_buf[...] = s
            x_buf[...] = jnp.max(s, axis=0, keepdims=True)

        def softmax(s_buf, x_buf, p_buf, al_buf):
            p_buf[...], al_buf[...] = softmax_update(s_buf[...], x_buf[...])

        def values(c, p_buf, al_buf):
            acc[...] = al_buf[...] * acc[...] + _dot(v_ref[c], p_buf[...])

        scores(0, s0, x0)
        if n_lat == 1:
            context_update()
            softmax(s0, x0, p0, al0)
            values(0, p0, al0)
        else:
            assert n_lat % 2 == 0
            scores(1, s1, x1)
            context_update()
            softmax(s0, x0, p0, al0)

            def body(t, carry):
                c = 2 * t
                scores(c + 2, s0, x0)
                softmax(s1, x1, p1, al1)
                values(c, p0, al0)
                scores(c + 3, s1, x1)
                softmax(s0, x0, p0, al0)
                values(c + 1, p1, al1)
                return carry
            lax.fori_loop(0, n_lat // 2 - 1, body, 0)
            softmax(s1, x1, p1, al1)
            values(n_lat - 2, p0, al0)
            values(n_lat - 1, p1, al1)

    o_t = acc[:LANE] / acc[LANE:LANE + 1]
    o = (o_t[:, :tq] - lam_ref[0:1, 0:1] * o_t[:, tq:]).T
    ms = jnp.mean(o * o, axis=-1, keepdims=True)
    o_ref[...] = ((o * lax.rsqrt(ms + EPS) * subg_ref[...]) * sub_scale).astype(BF16)


def _diff_attention(q_arr, q_len, qk_ctx, vt_ctx, ctx_len, qk_lat, vt_lat, lat_len, lam, subg,
                    sub_scale, heads, k_col0):
    batch = q_arr.shape[0] // q_len
    tq = _tile(q_len, 2048)
    nq = q_len // tq
    in_specs = [
        pl.BlockSpec((tq, LANE), lambda b, h, i: (b * nq + i, h)),
        pl.BlockSpec((ctx_len, LANE), lambda b, h, i: (b, k_col0 + h)),
        pl.BlockSpec((None, 1, VT_ROWS, ctx_len), lambda b, h, i: (h, b, 0, 0)),
    ]
    args = [q_arr, qk_ctx, vt_ctx]
    n_lat, tk = 0, 0
    if qk_lat is not None:
        tk = vt_lat.shape[-1]
        n_lat = lat_len // tk
        in_specs += [
            pl.BlockSpec((lat_len, LANE), lambda b, h, i: (b, k_col0 + h)),
            pl.BlockSpec((None, n_lat, VT_ROWS, tk), lambda b, h, i: (h, b, 0, 0)),
        ]
        args += [qk_lat, vt_lat]
    in_specs += [pl.BlockSpec((1, LANE), lambda b, h, i: (0, 0))] * 2
    args += [lam, subg]
    stat = pltpu.VMEM((1, 2 * tq), F32)
    scratch = [pltpu.VMEM((LANE, 2 * tq), BF16), stat, pltpu.VMEM((VT_ROWS, 2 * tq), F32)]
    if n_lat:
        scratch += [pltpu.VMEM((tk, 2 * tq), F32)] * 2 + [stat] * 2
        scratch += [pltpu.VMEM((tk, 2 * tq), BF16)] * 2 + [stat] * 2
    return pl.pallas_call(
        functools.partial(_attn_a_kernel, n_lat, tk, sub_scale),
        grid=(batch, heads, nq),
        in_specs=in_specs,
        out_specs=pl.BlockSpec((tq, LANE), lambda b, h, i: (b * nq + i, h)),
        out_shape=jax.ShapeDtypeStruct((batch * q_len, heads * LANE), BF16),
        scratch_shapes=scratch,
        compiler_params=_cp("parallel", "parallel", "parallel"),
        name="diff_attn",
    )(*args)


def _attn_b_kernel(has_loc, seq_len, q_heads, group, *refs):
    if has_loc:
        (q_ref, kx_ref, vx_ref, kp_ref, kc_ref, kn_ref, vp_ref, vc_ref, vn_ref, sink_ref, o_ref) = refs
    else:
        (q_ref, kx_ref, vx_ref, sink_ref, o_ref) = refs
    n = pl.program_id(1)
    clen = kx_ref.shape[0]
    q_t = q_ref[...].astype(F32).T
    keys = [kx_ref[...]]
    vals = [vx_ref[0]]
    if has_loc:
        keys += [kp_ref[...], kc_ref[...], kn_ref[...]]
        vals += [vp_ref[0], vc_ref[0], vn_ref[0]]
        kpos = (n - 1) * BLOCK + lax.broadcasted_iota(jnp.int32, (3 * BLOCK, BLOCK), 0)
        qpos = n * BLOCK + lax.broadcasted_iota(jnp.int32, (3 * BLOCK, BLOCK), 1)
        valid = (kpos >= 0) & (kpos < seq_len) & (jnp.abs(kpos - qpos) <= WINDOW)
        valid = jnp.concatenate([valid] * q_heads, axis=1)
    k_all = jnp.concatenate(keys, axis=0)
    v_all = jnp.concatenate(vals, axis=1)
    zeros = jnp.zeros((HEAD_DIM, BLOCK), F32)
    cols = []
    for hq in range(q_heads):
        blk = q_t[hq * HEAD_DIM:(hq + 1) * HEAD_DIM]
        cols.append(jnp.concatenate([blk, zeros] if hq // group == 0 else [zeros, blk], axis=0))
    q_all = jnp.concatenate(cols, axis=1).astype(BF16)
    sink = jnp.concatenate([sink_ref[hq:hq + 1, :] for hq in range(q_heads)], axis=1) * LOG2E
    s = _dot(k_all, q_all)
    if has_loc:
        s = jnp.concatenate([s[:clen], jnp.where(valid, s[clen:], NEG_INF)], axis=0)
    m = jnp.maximum(jnp.max(s, axis=0, keepdims=True), sink)
    o_t = _dot(v_all, jnp.exp2(s - m).astype(BF16))
    den = o_t[LANE:LANE + 1] + jnp.exp2(sink - m)
    heads_out = []
    for hq in range(q_heads):
        kvh = hq // group
        lanes = slice(hq * BLOCK, (hq + 1) * BLOCK)
        heads_out.append(o_t[kvh * HEAD_DIM:(kvh + 1) * HEAD_DIM, lanes] / den[:, lanes])
    o_ref[...] = jnp.concatenate(heads_out, axis=0).T.astype(BF16)


def _window_attention(q_arr, q_len, qk_ctx, vt_ctx, ctx_len, vt_lat, sink_tab, q_heads, kv_heads,
                      q_blk, k_blk, local):
    assert kv_heads == 2, "key/value heads must fill exactly one 128-lane block"
    batch = q_arr.shape[0] // q_len
    nb = q_len // BLOCK
    qw = q_heads * HEAD_DIM
    in_specs = [
        pl.BlockSpec((BLOCK, qw), lambda b, n: (b * nb + n, q_blk)),
        pl.BlockSpec((ctx_len, LANE), lambda b, n: (b, k_blk)),
        pl.BlockSpec((None, 1, VT_ROWS, ctx_len), lambda b, n: (0, b, 0, 0)),
    ]
    args = [q_arr, qk_ctx, vt_ctx]
    if local:
        prev = lambda b, n: b * nb + jnp.maximum(n - 1, 0)
        nxt = lambda b, n: b * nb + jnp.minimum(n + 1, nb - 1)
        in_specs += [
            pl.BlockSpec((BLOCK, LANE), lambda b, n: (prev(b, n), k_blk)),
            pl.BlockSpec((BLOCK, LANE), lambda b, n: (b * nb + n, k_blk)),
            pl.BlockSpec((BLOCK, LANE), lambda b, n: (nxt(b, n), k_blk)),
            pl.BlockSpec((None, 1, VT_ROWS, BLOCK), lambda b, n: (0, prev(b, n), 0, 0)),
            pl.BlockSpec((None, 1, VT_ROWS, BLOCK), lambda b, n: (0, b * nb + n, 0, 0)),
            pl.BlockSpec((None, 1, VT_ROWS, BLOCK), lambda b, n: (0, nxt(b, n), 0, 0)),
        ]
        args += [q_arr] * 3 + [vt_lat] * 3
    in_specs.append(pl.BlockSpec((q_heads, LANE), lambda b, n: (0, 0)))
    args.append(sink_tab)
    return pl.pallas_call(
        functools.partial(_attn_b_kernel, local, q_len, q_heads, q_heads // kv_heads),
        grid=(batch, nb),
        in_specs=in_specs,
        out_specs=pl.BlockSpec((BLOCK, qw), lambda b, n: (b * nb + n, 0)),
        out_shape=jax.ShapeDtypeStruct((batch * q_len, qw), BF16),
        compiler_params=_cp("parallel", "parallel"),
        name="window_attn",
    )(*args)


def _halo_specs(tile, width, col_blk, n_rows):
    r = tile // HALO
    last = n_rows // HALO - 1
    return [
        pl.BlockSpec((tile, width), lambda i: (i, col_blk)),
        pl.BlockSpec((HALO, width), lambda i: (jnp.maximum(i * r - 1, 0), col_blk)),
        pl.BlockSpec((HALO, width), lambda i: (jnp.minimum(i * r + r, last), col_blk)),
    ]


def _outside_rows(tile, tile_idx, tiles_per_seq):
    row = lax.broadcasted_iota(jnp.int32, (tile + 2 * HALO, 1), 0)
    lo = jnp.where(tile_idx == 0, HALO, 0)
    hi = jnp.where(tile_idx == tiles_per_seq - 1, HALO + tile, tile + 2 * HALO)
    return (row < lo) | (row >= hi)


def _with_halo(x, xp, xn, tiles_per_seq):
    ext = jnp.concatenate([xp, x, xn], axis=0)
    outside = _outside_rows(x.shape[0], pl.program_id(0) % tiles_per_seq, tiles_per_seq)
    return jnp.where(outside, 0.0, ext)


def _conv3(ext, w, b, tile):
    rows = ext.shape[0]
    c = (pltpu.roll(ext, 1, 0) * w[0:1] + ext * w[1:2] + pltpu.roll(ext, rows - 1, 0) * w[2:3])
    return c[HALO:HALO + tile] + b


def _pool_kernel(seq_len, tile, x_ref, xp_ref, xn_ref, w_ref, ls_ref, o_ref):
    tps = seq_len // tile
    x = x_ref[...]
    ext = _with_halo(x, xp_ref[...], xn_ref[...], tps)
    rows = tile + 2 * HALO
    t = (pl.program_id(0) % tps) * tile + lax.broadcasted_iota(jnp.int32, (tile, 1), 0)
    gw = x.shape[1] // len(POOL_WINDOWS)
    outs = []
    for g, w in enumerate(POOL_WINDOWS):
        lo = w // 2
        hi = w - 1 - lo
        a = ext[:, g * gw:(g + 1) * gw]
        span = 1
        while span < w:
            a = a + pltpu.roll(a, span, 0)
            span *= 2
        win = (pltpu.roll(a, rows - hi, 0) if hi else a)[HALO:HALO + tile]
        cnt = jnp.minimum(t + hi + 1, seq_len) - jnp.maximum(t - lo, 0)
        d = win / cnt.astype(F32) - x[:, g * gw:(g + 1) * gw]
        outs.append(_dot(d.astype(BF16), w_ref[g]))
    o_ref[...] = (jnp.concatenate(outs, axis=1) * ls_ref[...]).astype(BF16)


def _pool_mixer(p, col0, width, seq_len, w_lin, ls):
    n = p.shape[0]
    tile = _tile(seq_len, 1024)
    g, gw, _ = w_lin.shape
    return pl.pallas_call(
        functools.partial(_pool_kernel, seq_len, tile),
        grid=(n // tile,),
        in_specs=_halo_specs(tile, width, col0 // width, n)
        + [pl.BlockSpec((g, gw, gw), lambda i: (0, 0, 0)),
           pl.BlockSpec((1, width), lambda i: (0, 0))],
        out_specs=pl.BlockSpec((tile, width), lambda i: (i, 0)),
        out_shape=jax.ShapeDtypeStruct((n, width), BF16),
        compiler_params=_cp("parallel"),
        name="pool_mixer",
    )(p, p, p, w_lin, ls)


def _hypre_kernel(seq_len, tile, x_ref, xp_ref, xn_ref, cw_ref, cb_ref, z_ref, x2_ref):
    ext = _with_halo(x_ref[...], xp_ref[...], xn_ref[...], seq_len // tile)
    c = _conv3(ext, cw_ref[...], cb_ref[...], tile)
    dw = c.shape[1] // 3
    z_ref[...] = c[:, :dw] * c[:, dw:2 * dw]
    x2_ref[...] = c[:, 2 * dw:]


def _hyena_pre(p, col0, width, seq_len, conv_w, conv_b):
    n = p.shape[0]
    tile = _tile(seq_len, 1024)
    dw = width // 3
    out = jax.ShapeDtypeStruct((n, dw), F32)
    return pl.pallas_call(
        functools.partial(_hypre_kernel, seq_len, tile),
        grid=(n // tile,),
        in_specs=_halo_specs(tile, width, col0 // width, n)
        + [pl.BlockSpec((3, width), lambda i: (0, 0)),
           pl.BlockSpec((1, width), lambda i: (0, 0))],
        out_specs=[pl.BlockSpec((tile, dw), lambda i: (i, 0))] * 2,
        out_shape=[out, out],
        compiler_params=_cp("parallel"),
        name="hyena_pre",
    )(p, p, p, conv_w, conv_b.reshape(1, width))


def _hyfilt_kernel(seq_len, tile, zf_ref, w1_ref, b1_ref, w2_ref, b2_ref, w3_ref, fr_ref, dl_ref,
                   k_ref, ss_ref):
    i = pl.program_id(0)
    half = tile // 2
    hp = functools.partial(jnp.dot, precision=lax.Precision.HIGHEST, preferred_element_type=F32)
    zf = zf_ref[...]
    fr = fr_ref[...]
    h = jnp.sin(fr * (hp(zf, w1_ref[...]) + b1_ref[...]))
    h = jnp.sin(fr * (hp(h, w2_ref[...]) + b2_ref[...]))
    w3 = w3_ref[...]
    dl = dl_ref[...]
    ss = jnp.zeros(ss_ref.shape, F32)
    for part in range(2):
        k = hp(h, w3[part]) * jnp.exp(-zf[:, part * HY_PACK:part * HY_PACK + 1] * dl)
        row = i * tile + part * half + lax.broadcasted_iota(jnp.int32, (half, 1), 0)
        k = jnp.where(row == seq_len, 0.0, k)
        k_ref[part * half:(part + 1) * half, :] = k
        ss = ss + jnp.sum(k * k, axis=0, keepdims=True)

    @pl.when(i == 0)
    def _():
        ss_ref[...] = jnp.zeros(ss_ref.shape, F32)

    ss_ref[...] += ss


def _hyena_filter(seq_len, w1, b1, w2, b2, w3, freq):
    dw = w3.shape[1] // 2
    bands = (HY_EMB - 1) // 2
    t01 = jnp.linspace(0.0, 1.0, seq_len, dtype=F32)[:, None]
    w_ang = 2.0 * math.pi * jnp.arange(seq_len, dtype=F32)[:, None] / seq_len
    f = jnp.linspace(1e-4, bands - 1, bands, dtype=F32)[None, :]
    z = jnp.concatenate([t01, jnp.cos(f * w_ang), -jnp.sin(f * w_ang)], axis=-1)
    rev = (seq_len - jnp.arange(seq_len)) % seq_len
    assert HY_HIDDEN == HY_PACK and HY_EMB <= HY_PACK
    tile = _tile(seq_len, 1024)
    tps = seq_len // tile
    half = tile // 2
    zf = jnp.pad(jnp.concatenate([z, z[rev]], axis=0), ((0, 0), (0, HY_PACK - HY_EMB)))
    zf = zf.reshape(2 * tps, 2, half, HY_PACK).transpose(0, 2, 1, 3).reshape(2 * tps * half, LANE)
    zero = jnp.zeros((HY_PACK, HY_PACK), F32)
    blockdiag = lambda w: jnp.block([[w, zero], [zero, w]])
    w1p = blockdiag(jnp.pad(w1, ((0, HY_PACK - HY_EMB), (0, 0))))
    w2p = blockdiag(w2)
    w3d = w3.reshape(HY_HIDDEN, 2, dw).transpose(1, 0, 2)
    zrows = jnp.zeros_like(w3d)
    w3p = jnp.stack([jnp.concatenate([w3d, zrows], axis=1), jnp.concatenate([zrows, w3d], axis=1)], axis=1)
    vec = lambda v: jnp.concatenate([v, v]).reshape(1, LANE)
    deltas = jnp.abs(jnp.linspace(math.log(HY_TARGET) / HY_FAST_DECAY, math.log(HY_TARGET) / HY_SLOW_DECAY,
                                  dw, dtype=F32)).reshape(1, dw)
    full = lambda shape: pl.BlockSpec(shape, lambda i: (0, 0))
    return pl.pallas_call(
        functools.partial(_hyfilt_kernel, seq_len, tile),
        grid=(2 * tps,),
        in_specs=[pl.BlockSpec((half, LANE), lambda i: (i, 0)),
                  full((LANE, LANE)), full((1, LANE)), full((LANE, LANE)), full((1, LANE)),
                  pl.BlockSpec((None, 2, LANE, dw), lambda i: (i // tps, 0, 0, 0)),
                  full((1, LANE)), full((1, dw))],
        out_specs=[pl.BlockSpec((tile, dw), lambda i: (i, 0)), full((1, dw))],
        out_shape=[jax.ShapeDtypeStruct((2 * seq_len, dw), F32), jax.ShapeDtypeStruct((1, dw), F32)],
        compiler_params=_cp("arbitrary"),
        name="hyena_filter",
    )(zf, w1p, vec(b1), w2p, vec(b2), w3p, vec(freq), deltas)


def _dft_tables(n1, n2):
    n = n1 * n2
    nf = min(n1, -(-(n1 // 2 + 1) // DFT_BLOCKS) * DFT_BLOCKS)
    two_pi = 2.0 * math.pi

    def cs(num, den):
        ang = (num % den).astype(F32) * (two_pi / den)
        return jnp.cos(ang), jnp.sin(ang)

    a = jnp.arange(n1)
    c1, s1 = cs(a[:, None] * a[None, :], n1)
    kron = lambda mat: jnp.kron(mat, jnp.eye(SUBLANE, dtype=F32))
    fwd_a = jnp.concatenate([kron(c1[:nf]), kron(-s1[:nf])], axis=0)
    wgt = jnp.where((a == 0) | (a == n1 // 2), 1.0, jnp.where(a < n1 // 2, 2.0, 0.0))[None, :nf] / n
    inv_a = jnp.concatenate([kron(c1[: n1 // 2, :nf] * wgt), kron(-s1[: n1 // 2, :nf] * wgt)], axis=1)
    k2 = jnp.arange(n2)
    num = (k2[None, :, None] * k2[None, None, :] * n1 + a[:nf, None, None] * k2[None, None, :])
    cb, sb = cs(num, n)
    fwd_b = jnp.concatenate([jnp.concatenate([cb, sb], axis=2),
                             jnp.concatenate([-sb, cb], axis=2)], axis=1)
    cbt, sbt = jnp.swapaxes(cb, 1, 2), jnp.swapaxes(sb, 1, 2)
    inv_b = jnp.concatenate([jnp.concatenate([cbt, -sbt], axis=2),
                             jnp.concatenate([sbt, cbt], axis=2)], axis=1)
    return tuple(m.astype(BF16) for m in (fwd_a, inv_a, fwd_b, inv_b)), nf


def _rows(x):
    a, g, c = x.shape
    return x.reshape(a * g, c)


def _dft_a_kernel(f_ref, x_ref, re_ref, im_ref):
    nf = re_ref.shape[0]
    for r0 in range(0, x_ref.shape[1], SUBLANE):
        rows = slice(r0, r0 + SUBLANE)
        y = _dot(f_ref[...], _rows(x_ref[:, rows, :]).astype(BF16))
        half = y.shape[0] // 2
        re_ref[:, rows, :] = y[:half].reshape(nf, SUBLANE, -1)
        im_ref[:, rows, :] = y[half:].reshape(nf, SUBLANE, -1)


def _dft_stage_a(xv, fwd_a, nf):
    b, k, n2, ch = xv.shape
    gr = _tile(n2, DFT_STEP_ROWS)
    out = jax.ShapeDtypeStruct((b, nf, n2, ch), F32)
    return pl.pallas_call(
        _dft_a_kernel,
        grid=(b, n2 // gr),
        in_specs=[pl.BlockSpec((2 * nf * SUBLANE, k * SUBLANE), lambda bi, j: (0, 0)),
                  pl.BlockSpec((None, k, gr, ch), lambda bi, j: (bi, 0, j, 0))],
        out_specs=[pl.BlockSpec((None, nf, gr, ch), lambda bi, j: (bi, 0, j, 0))] * 2,
        out_shape=[out, out],
        compiler_params=_cp("parallel", "parallel"),
        name="dft_stage_a",
    )(fwd_a[:, :k * SUBLANE], xv)


def _dft_b_kernel(m_ref, re_ref, im_ref, ss_ref, kr_ref, ki_ref):
    nrm = lax.rsqrt(ss_ref[...] + EPS)
    n2 = m_ref.shape[1] // 2
    for k in range(m_ref.shape[0]):
        rows = slice(k * n2, (k + 1) * n2)
        x = _dot(m_ref[k], jnp.concatenate([re_ref[rows], im_ref[rows]], axis=0).astype(BF16))
        kr_ref[rows] = x[:n2] * nrm
        ki_ref[rows] = x[n2:] * nrm


def _filter_spectrum(kern, sumsq, tabs, nf, n1, n2):
    fwd_a, _, fwd_b, _ = tabs
    ch = kern.shape[1]
    are, aim = _dft_stage_a(kern.reshape(1, n1, n2, ch), fwd_a, nf)
    are, aim = are.reshape(nf * n2, ch), aim.reshape(nf * n2, ch)
    kb = _tile(nf, DFT_BLOCKS)
    blk = pl.BlockSpec((kb * n2, ch), lambda i: (i, 0))
    out = jax.ShapeDtypeStruct((nf * n2, ch), F32)
    return pl.pallas_call(
        _dft_b_kernel,
        grid=(nf // kb,),
        in_specs=[pl.BlockSpec((kb, 2 * n2, 2 * n2), lambda i: (i, 0, 0)), blk, blk,
                  pl.BlockSpec((1, ch), lambda i: (0, 0))],
        out_specs=[blk, blk],
        out_shape=[out, out],
        compiler_params=_cp("parallel"),
        name="filter_spectrum",
    )(fwd_b, are, aim, sumsq)


def _dft_mul_kernel(f_ref, g_ref, re_ref, im_ref, kr_ref, ki_ref, qr_ref, qi_ref):
    n2 = f_ref.shape[1] // 2
    for k in range(f_ref.shape[0]):
        rows = slice(k * n2, (k + 1) * n2)
        x = _dot(f_ref[k], jnp.concatenate([re_ref[rows], im_ref[rows]], axis=0).astype(BF16))
        xr, xi = x[:n2], x[n2:]
        kr, ki = kr_ref[rows], ki_ref[rows]
        prod = jnp.concatenate([xr * kr - xi * ki, xr * ki + xi * kr], axis=0)
        q = _dot(g_ref[k], prod.astype(BF16))
        qr_ref[rows] = q[:n2]
        qi_ref[rows] = q[n2:]


def _dft_inv_kernel(g_ref, qr_ref, qi_ref, z_ref, x2_ref, bias_ref, o_ref):
    for r0 in range(0, z_ref.shape[1], SUBLANE):
        rows = slice(r0, r0 + SUBLANE)
        q = jnp.concatenate([_rows(qr_ref[:, rows, :]), _rows(qi_ref[:, rows, :])], axis=0)
        z = z_ref[:, rows, :]
        y = _dot(g_ref[...], q.astype(BF16)).reshape(z.shape)
        o_ref[:, rows, :] = x2_ref[:, rows, :] * (y + z * bias_ref[...])


def _hyena_long(z, x2, kr, ki, bias, tabs, batch, seq_len, nf, n1, n2):
    fwd_a, inv_a, fwd_b, inv_b = tabs
    ch = z.shape[1]
    zv = z.reshape(batch, n1 // 2, n2, ch)
    are, aim = _dft_stage_a(zv, fwd_a, nf)
    are, aim = are.reshape(batch, nf * n2, ch), aim.reshape(batch, nf * n2, ch)
    kb = _tile(nf, DFT_BLOCKS)
    mat = pl.BlockSpec((kb, 2 * n2, 2 * n2), lambda i, b: (i, 0, 0))
    dat = pl.BlockSpec((None, kb * n2, ch), lambda i, b: (b, i, 0))
    spec = pl.BlockSpec((kb * n2, ch), lambda i, b: (i, 0))
    out = jax.ShapeDtypeStruct((batch, nf * n2, ch), F32)
    qr, qi = pl.pallas_call(
        _dft_mul_kernel,
        grid=(nf // kb, batch),
        in_specs=[mat, mat, dat, dat, spec, spec],
        out_specs=[dat, dat],
        out_shape=[out, out],
        compiler_params=_cp("parallel", "parallel"),
        name="dft_filter_mul",
    )(fwd_b, inv_b, are, aim, kr, ki)
    qr, qi = qr.reshape(batch, nf, n2, ch), qi.reshape(batch, nf, n2, ch)
    gr = _tile(n2, DFT_STEP_ROWS // 2)
    full = pl.BlockSpec((None, nf, gr, ch), lambda b, j: (b, 0, j, 0))
    half = pl.BlockSpec((None, n1 // 2, gr, ch), lambda b, j: (b, 0, j, 0))
    o = pl.pallas_call(
        _dft_inv_kernel,
        grid=(batch, n2 // gr),
        in_specs=[pl.BlockSpec(inv_a.shape, lambda b, j: (0, 0)), full, full, half, half,
                  pl.BlockSpec((1, ch), lambda b, j: (0, 0))],
        out_specs=half,
        out_shape=jax.ShapeDtypeStruct((batch, n1 // 2, n2, ch), F32),
        compiler_params=_cp("parallel", "parallel"),
        name="dft_inverse_gate",
    )(inv_a, qr, qi, zv, x2.reshape(batch, n1 // 2, n2, ch), bias.reshape(1, ch))
    return o.reshape(batch * seq_len, ch)


def _hyena_short_kernel(f_ref, g_ref, k_ref, ss_ref, z_ref, x2_ref, bias_ref, o_ref):
    f = f_ref[...]
    z = z_ref[...]
    seq = z.shape[0]
    ks = _dot(f, k_ref[...].astype(BF16)) * lax.rsqrt(ss_ref[...] + EPS)
    zs = _dot(f[:, :seq], z.astype(BF16))
    n = ks.shape[0] // 2
    kr, ki, zr, zi = ks[:n], ks[n:], zs[:n], zs[n:]
    prod = jnp.concatenate([zr * kr - zi * ki, zr * ki + zi * kr], axis=0)
    y = _dot(g_ref[...], prod.astype(BF16))
    o_ref[...] = x2_ref[...] * (y + z * bias_ref[...])


def _hyena_short(z, x2, kern, sumsq, bias, batch, seq_len):
    n = 2 * seq_len
    ch = z.shape[1]
    a = jnp.arange(n)
    ang = ((a[:, None] * a[None, :]) % n).astype(F32) * (2.0 * math.pi / n)
    c, s = jnp.cos(ang), jnp.sin(ang)
    f = jnp.concatenate([c, -s], axis=0).astype(BF16)
    g = (jnp.concatenate([c, -s], axis=1)[:seq_len] / n).astype(BF16)
    full = lambda shape: pl.BlockSpec(shape, lambda b: (0, 0))
    dat = pl.BlockSpec((seq_len, ch), lambda b: (b, 0))
    return pl.pallas_call(
        _hyena_short_kernel,
        grid=(batch,),
        in_specs=[full((2 * n, n)), full((seq_len, 2 * n)),
                  full((n, ch)), full((1, ch)), dat, dat, full((1, ch))],
        out_specs=dat,
        out_shape=jax.ShapeDtypeStruct((batch * seq_len, ch), F32),
        compiler_params=_cp("parallel"),
        name="hyena_short_conv",
    )(f, g, kern, sumsq, z, x2, bias.reshape(1, ch))


def _hyena_mixer(p, col0, width, batch, seq_len, conv_w, conv_b, w1, b1, w2, b2, w3, freq, bias):
    z, x2 = _hyena_pre(p, col0, width, seq_len, conv_w, conv_b)
    kern, sumsq = _hyena_filter(seq_len, w1, b1, w2, b2, w3, freq)
    n = 2 * seq_len
    if n <= 4 * DFT_RADIX:
        return _hyena_short(z, x2, kern, sumsq, bias, batch, seq_len)
    n2 = DFT_RADIX
    n1 = n // n2
    tabs, nf = _dft_tables(n1, n2)
    kr, ki = _filter_spectrum(kern, sumsq, tabs, nf, n1, n2)
    return _hyena_long(z, x2, kr, ki, bias, tabs, batch, seq_len, nf, n1, n2)


def _outproj_kernel(x_ref, a_ref, b_ref, c_ref, d_ref, w_ref, g_ref, o_ref):
    acc = None
    row = 0
    for m_ref in (a_ref, b_ref, c_ref, d_ref):
        wdt = m_ref.shape[1]
        part = _dot(m_ref[...].astype(BF16), w_ref[row:row + wdt, :])
        acc = part if acc is None else acc + part
        row += wdt
    o_ref[...] = x_ref[...] + g_ref[...] * acc


def _out_projection(x2d, mixers, w, layer, gate, rows_per_group):
    n, d = x2d.shape
    tm = _tile(rows_per_group, 512)
    tpg = rows_per_group // tm
    return pl.pallas_call(
        _outproj_kernel,
        grid=(n // tm,),
        in_specs=[pl.BlockSpec((tm, d), lambda i: (i, 0))]
        + [pl.BlockSpec((tm, m.shape[1]), lambda i: (i, 0)) for m in mixers]
        + [pl.BlockSpec((None,) + w.shape[1:], lambda i: (layer, 0, 0)),
           pl.BlockSpec((None, 1, d), lambda i: (i // tpg, 0, 0))],
        out_specs=pl.BlockSpec((tm, d), lambda i: (i, 0)),
        out_shape=jax.ShapeDtypeStruct((n, d), F32),
        compiler_params=_cp("parallel"),
        name="out_proj",
    )(x2d, *mixers, w, gate)


def _ffn_kernel(seq_len, tile, x_ref, xp_ref, xn_ref, g_ref, sc_ref, sh_ref, gate_ref,
                wg_ref, wu_ref, cwg_ref, cwu_ref, cbg_ref, cbu_ref, wd_ref, o_ref, h_scr):
    j = pl.program_id(1)

    @pl.when(j == 0)
    def _():
        g, sc, sh = g_ref[...], sc_ref[...], sh_ref[...]
        tps = seq_len // tile
        ti = pl.program_id(0) % tps
        halo = jnp.concatenate([xn_ref[...], xp_ref[...]], axis=0)
        row = lax.broadcasted_iota(jnp.int32, (2 * HALO, 1), 0)
        outside = (row < jnp.where(ti == tps - 1, HALO, 0)) | (row >= jnp.where(ti == 0, HALO, 2 * HALO))
        h_scr[tile:, :] = jnp.where(outside, 0.0, _norm_mod(halo, g, sc, sh)).astype(BF16)
        chunk = min(tile, 4 * LANE)

        def body(c, carry):
            rows = pl.ds(pl.multiple_of(c * chunk, chunk), chunk)
            h_scr[rows, :] = _norm_mod(x_ref[rows, :], g, sc, sh).astype(BF16)
            return carry
        lax.fori_loop(0, tile // chunk, body, 0)
        o_ref[...] = jnp.zeros(o_ref.shape, F32)

    def conv3(u, w, b):
        rows = u.shape[0]
        c = pltpu.roll(u, 1, 0) * w[0:1] + u * w[1:2] + pltpu.roll(u, rows - 1, 0) * w[2:3]
        return c[:tile] + b

    h = h_scr[...]
    gate = conv3(_dot(h, wg_ref[...]), cwg_ref[...], cbg_ref[...])
    up = conv3(_dot(h, wu_ref[...]), cwu_ref[...], cbu_ref[...])
    act = (gate * jax.nn.sigmoid(gate)) * up
    o_ref[...] += _dot(act.astype(BF16), wd_ref[...])

    @pl.when(j == pl.num_programs(1) - 1)
    def _():
        o_ref[...] = x_ref[...] + gate_ref[...] * o_ref[...]


def _conv_ffn(x2d, g, sc, sh, gate, w_up, conv_w, conv_b, w_down, layer, seq_len, rows_per_group):
    n, d = x2d.shape
    f = w_down.shape[1]
    tile = _tile(seq_len, 1024)
    hc = _tile(f, 512)
    nh = f // hc
    once = pl.Buffered(1)
    tpg = rows_per_group // tile
    r = tile // HALO
    last = n // HALO - 1
    vec = pl.BlockSpec((None, 1, d), lambda i, j: (i // tpg, 0, 0))
    cb = conv_b.reshape(1, 2 * f)
    return pl.pallas_call(
        functools.partial(_ffn_kernel, seq_len, tile),
        grid=(n // tile, nh),
        in_specs=[
            pl.BlockSpec((tile, d), lambda i, j: (i, 0)),
            pl.BlockSpec((HALO, d), lambda i, j: (jnp.maximum(i * r - 1, 0), 0)),
            pl.BlockSpec((HALO, d), lambda i, j: (jnp.minimum(i * r + r, last), 0)),
            pl.BlockSpec((1, d), lambda i, j: (0, 0)),
            vec, vec, vec,
            pl.BlockSpec((None, d, hc), lambda i, j: (layer, 0, j)),
            pl.BlockSpec((None, d, hc), lambda i, j: (layer, 0, nh + j)),
            pl.BlockSpec((3, hc), lambda i, j: (0, j)),
            pl.BlockSpec((3, hc), lambda i, j: (0, nh + j)),
            pl.BlockSpec((1, hc), lambda i, j: (0, j)),
            pl.BlockSpec((1, hc), lambda i, j: (0, nh + j)),
            pl.BlockSpec((None, hc, d), lambda i, j: (layer, j, 0)),
        ],
        out_specs=pl.BlockSpec((tile, d), lambda i, j: (i, 0), pipeline_mode=once),
        out_shape=jax.ShapeDtypeStruct((n, d), F32),
        scratch_shapes=[pltpu.VMEM((tile + 2 * HALO, d), BF16)],
        compiler_params=_cp("parallel", "arbitrary"),
        name="conv_ffn",
    )(x2d, x2d, x2d, g, sc, sh, gate, w_up, w_up, conv_w, conv_w, cb, cb, w_down)


def kernel(x, c, ctx, c_ctx, w_mod, b_mod, norm1_g, norm2_g, w_in, w_out, qk_gain, diff_lam, diff_subln,
           win_sink, pool_w, pool_scale, hy_conv_w, hy_conv_b, hy_w1, hy_b1, hy_w2, hy_b2, hy_w3, hy_freq,
           hy_bias, ffn_w_in, ffn_conv_w, ffn_conv_b, ffn_w_out):
    batch, seq, d = x.shape
    clen = ctx.shape[1]
    depth = w_in.shape[0]
    a_heads = d // 512
    a_width = a_heads * 2 * HEAD_DIM
    b_heads = d // 256
    b_kv = b_heads // 4
    b_width = b_heads * HEAD_DIM
    b_kvw = b_kv * HEAD_DIM
    c_width = d // 4
    d_width = d - a_width - b_width - c_width
    off_ka = a_width
    off_va = 2 * a_width
    off_qb = 3 * a_width
    off_kb = off_qb + b_width
    off_vb = off_kb + b_kvw
    off_pool = off_vb + b_kvw
    off_hy = off_pool + c_width
    in_cols = off_hy + 3 * d_width
    order = ((off_hy, in_cols), (off_pool, off_hy), (off_va, off_qb), (0, off_ka), (off_ka, off_va),
             (off_qb, off_kb), (off_kb, off_vb), (off_vb, off_pool))
    n_hy, n_pool, n_va, n_qa, n_ka, n_qb, n_kb, n_vb = (
        sum(e - s for s, e in order[:k]) for k in range(len(order)))
    assert a_width % LANE == 0 and b_width % LANE == 0 and b_kvw == LANE and c_width // 4 == LANE
    assert seq % BLOCK == 0 and clen % BLOCK == 0 and n_pool % c_width == 0
    assert (n_ka - n_qa) == a_width and (n_qb - n_qa) % b_width == 0 and n_kb - n_qb == b_width

    rows = -(-(batch + 1) // SUBLANE) * SUBLANE
    cvec = jnp.zeros((rows, d), F32).at[:batch].set(c).at[batch].set(c_ctx)
    mods = _modulation(cvec, w_mod, b_mod)

    w_in_b = jnp.concatenate([w_in[:, :, s:e] for s, e in order], axis=-1).astype(BF16)
    w_out_b = w_out.astype(BF16)
    ffn_in_b = ffn_w_in.astype(BF16)
    ffn_out_b = ffn_w_out.astype(BF16)

    cos_t, sin_t = _rope_tables(seq)
    scale = HEAD_DIM ** -0.5
    tile2 = lambda v: jnp.tile(v, LANE // HEAD_DIM)

    xs = x.reshape(batch * seq, d)
    cs = ctx.reshape(batch * clen, d)
    for l in range(depth):
        last = l == depth - 1
        m = mods[l]
        sh1, sc1, g1, sh2, sc2, g2 = [m[:batch, i * d:(i + 1) * d].reshape(batch, 1, d) for i in range(6)]
        csh1, csc1, cg1, csh2, csc2, cg2 = [m[batch, i * d:(i + 1) * d].reshape(1, 1, d) for i in range(6)]
        n1g = norm1_g[l].reshape(1, d)
        n2g = norm2_g[l].reshape(1, d)
        pool_w_l = pool_w[l].astype(BF16)

        p = _in_projection(xs, n1g, sc1, sh1, w_in_b, l, seq)
        pc = _in_projection(cs, n1g, csc1, csh1, w_in_b, l, batch * clen)

        na = a_width // LANE
        nbq = b_width // LANE
        gains = jnp.stack([tile2(qk_gain[l, 0]) * (scale * LOG2E)] * na + [tile2(qk_gain[l, 1])] * na
                          + [tile2(qk_gain[l, 2]) * (scale * LOG2E)] * nbq + [tile2(qk_gain[l, 3])])
        qk = _qk_prep(p, n_qa // LANE, gains, cos_t, sin_t, seq, True)
        qk_c = _qk_prep(pc, n_qa // LANE, gains, cos_t, sin_t, clen, False)
        kb_blk = 2 * na + nbq
        qb_blk = (n_qb - n_qa) // b_width

        lam_p = diff_lam[l].astype(F32)
        lambda_init = 0.8 - 0.6 * math.exp(-0.3 * l)
        lam = jnp.exp(jnp.sum(lam_p[0] * lam_p[1])) - jnp.exp(jnp.sum(lam_p[2] * lam_p[3])) + lambda_init
        lam_v = jnp.full((1, LANE), lam, F32)
        subg = diff_subln[l].reshape(1, LANE)
        sink_tab = jnp.broadcast_to(win_sink[l].astype(F32)[:, None], (b_heads, LANE))
        hy = (hy_conv_w[l], hy_conv_b[l], hy_w1[l], hy_b1[l], hy_w2[l], hy_b2[l], hy_w3[l], hy_freq[l],
              hy_bias[l])

        vt = _transposed_values(p, n_va // LANE, a_heads, seq, _tile(seq, 512))
        vt_c = _transposed_values(pc, n_va // LANE, a_heads, clen, clen)
        o_a = _diff_attention(qk, seq, qk_c, vt_c, clen, qk, vt, seq, lam_v, subg, 1.0 - lambda_init,
                              a_heads, na)
        vtb = _transposed_values(p, n_vb // LANE, 1, seq, BLOCK)
        vtb_c = _transposed_values(pc, n_vb // LANE, 1, clen, clen)
        o_b = _window_attention(qk, seq, qk_c, vtb_c, clen, vtb, sink_tab, b_heads, b_kv,
                                qb_blk, kb_blk, True)
        o_c = _pool_mixer(p, n_pool, c_width, seq, pool_w_l, pool_scale[l].reshape(1, c_width))
        o_d = _hyena_mixer(p, n_hy, 3 * d_width, batch, seq, *hy)
        xs = _out_projection(xs, (o_a, o_b, o_c, o_d), w_out_b, l, g1, seq)
        xs = _conv_ffn(xs, n2g, sc2, sh2, g2, ffn_in_b, ffn_conv_w[l], ffn_conv_b[l], ffn_out_b, l,
                       seq, seq)

        if not last:
            o_a_c = _diff_attention(qk_c, clen, qk_c, vt_c, clen, None, None, 0, lam_v, subg,
                                    1.0 - lambda_init, a_heads, na)
            o_b_c = _window_attention(qk_c, clen, qk_c, vtb_c, clen, None, sink_tab, b_heads, b_kv,
                                      qb_blk, kb_blk, False)
            o_c_c = _pool_mixer(pc, n_pool, c_width, clen, pool_w_l, pool_scale[l].reshape(1, c_width))
            o_d_c = _hyena_mixer(pc, n_hy, 3 * d_width, batch, clen, *hy)
            cs = _out_projection(cs, (o_a_c, o_b_c, o_c_c, o_d_c), w_out_b, l, cg1, batch * clen)
            cs = _conv_ffn(cs, n2g, csc2, csh2, cg2, ffn_in_b, ffn_conv_w[l], ffn_conv_b[l],
                           ffn_out_b, l, clen, batch * clen)
    return xs.reshape(batch, seq, d)
```
